```python
import math
import jax, jax.numpy as jnp
from jax import lax
import numpy as np

D_MODEL = 1024
BATCH = 16
SEQ = 2048
DEPTH = 2

GRID_W = 64
CTX_LEN = 256
EPS = 1e-6
ROPE_THETA = 10000.0
Q_BLOCK = 128
CHUNK = 128
A_GROUPS = 4
A_GROUP_DIM = 128
A_WIDTH = A_GROUPS * A_GROUP_DIM
B_HEADS = 4
B_HEAD_DIM = 64
B_WIDTH = B_HEADS * 2 * B_HEAD_DIM
EVEN_IN = 2 * A_WIDTH + 3 * B_WIDTH
EVEN_OUT = A_WIDTH + B_WIDTH
C_HEADS = 8
C_KV_HEADS = 2
C_GROUP = C_HEADS // C_KV_HEADS
C_HEAD_DIM = 128
ODD_IN = (C_HEADS + 2 * C_KV_HEADS) * C_HEAD_DIM
ODD_OUT = C_HEADS * C_HEAD_DIM
N_EXPERTS = 64
TOP_K = 6
N_GROUPS = 8
TOPK_GROUPS = 4
D_EXPERT = 256
D_SHARED = 256
ROUTE_SCALE = 2.5
EXPERT_BLOCK = 512

kernel_name = 'hybrid_gmlp_diffattn_gqa_moe_prefix_dit'


def rms_norm(x, g):
    xf = x.astype(jnp.float32)
    y = xf * lax.rsqrt(jnp.mean(xf * xf, axis=-1, keepdims=True) + EPS)
    return (y * g.astype(jnp.float32)).astype(x.dtype)


def layer_norm(x, g, b):
    xf = x.astype(jnp.float32)
    mu = jnp.mean(xf, axis=-1, keepdims=True)
    var = jnp.mean(jnp.square(xf - mu), axis=-1, keepdims=True)
    y = (xf - mu) * lax.rsqrt(var + EPS) * g.astype(jnp.float32) + b.astype(jnp.float32)
    return y.astype(x.dtype)


def adaln(cond, w, b):
    return jnp.split(jax.nn.silu(cond) @ w + b, 6, axis=-1)


def modulate(n, shift, scale):
    return n * (1 + scale) + shift


def axial_rope(n_tokens, head_dim):
    rows = n_tokens // GRID_W
    row = jnp.repeat(jnp.arange(rows, dtype=jnp.float32), GRID_W)
    col = jnp.tile(jnp.arange(GRID_W, dtype=jnp.float32), rows)
    n_freq = head_dim // 4
    inv = ROPE_THETA ** (-jnp.arange(n_freq, dtype=jnp.float32) / n_freq)
    ang = jnp.concatenate([row[:, None] * inv, col[:, None] * inv], axis=-1)
    return jnp.cos(ang), jnp.sin(ang)


def apply_rope(x, cos, sin):
    xf = x.astype(jnp.float32)
    x1, x2 = xf[..., 0::2], xf[..., 1::2]
    c, s = cos[None, :, None, :], sin[None, :, None, :]
    y = jnp.stack([x1 * c - x2 * s, x1 * s + x2 * c], axis=-1).reshape(x.shape)
    return y.astype(x.dtype)


def sweep_query_blocks(fn, q):
    b, s = q.shape[:2]
    nb = s // Q_BLOCK
    qb = jnp.moveaxis(q.reshape((b, nb, Q_BLOCK) + q.shape[2:]), 1, 0)
    out = lax.map(fn, qb)
    return jnp.moveaxis(out, 0, 1).reshape((b, s) + out.shape[3:])


def chunk_gmlp(u, v, ln_g, ln_b, ws, bs):
    b, n, _ = u.shape
    v = layer_norm(v, ln_g, ln_b)
    vc = v.reshape(b, n // CHUNK, CHUNK, A_GROUPS, A_GROUP_DIM)
    mixed = jnp.einsum('gpq,bcqgd->bcpgd', ws, vc) + bs.T[:, :, None]
    return u * mixed.reshape(b, n, A_WIDTH)


def gmlp_diff_mixer(nx, nc, w_in, w_out, ln_g, ln_b, ws, bs, q_norm, k_norm,
                    lam_q1, lam_k1, lam_q2, lam_k2, subln_g, layer, cos, sin, need_ctx):
    lam_init = 0.8 - 0.6 * math.exp(-0.3 * layer)
    lam = (jnp.exp(jnp.sum(lam_q1.astype(jnp.float32) * lam_k1.astype(jnp.float32)))
           - jnp.exp(jnp.sum(lam_q2.astype(jnp.float32) * lam_k2.astype(jnp.float32))) + lam_init)
    scale = B_HEAD_DIM ** -0.5
    splits = [A_WIDTH, 2 * A_WIDTH, 2 * A_WIDTH + B_WIDTH, 2 * A_WIDTH + 2 * B_WIDTH]

    def project(n, rope):
        b, s, _ = n.shape
        u, v, q, k, vv = jnp.split(n @ w_in, splits, axis=-1)
        q = rms_norm(q.reshape(b, s, 2 * B_HEADS, B_HEAD_DIM), q_norm)
        k = rms_norm(k.reshape(b, s, 2 * B_HEADS, B_HEAD_DIM), k_norm)
        if rope is not None:
            q = apply_rope(q, *rope)
            k = apply_rope(k, *rope)
        return (jax.nn.gelu(u, approximate=False), jax.nn.gelu(v, approximate=False),
                q.reshape(b, s, B_HEADS, 2, B_HEAD_DIM), k.reshape(b, s, B_HEADS, 2, B_HEAD_DIM),
                vv.reshape(b, s, B_HEADS, 2 * B_HEAD_DIM))

    def attend(qb, kk, vv):
        sc = jnp.einsum('bqhmd,bkhmd->bhmqk', qb, kk, preferred_element_type=jnp.float32) * scale
        p = jax.nn.softmax(sc, axis=-1)
        a = p[:, :, 0] - lam * p[:, :, 1]
        return jnp.einsum('bhqk,bkhe->bqhe', a.astype(vv.dtype), vv)

    def finish(o, u, v):
        b, s = u.shape[:2]
        o = rms_norm(o, subln_g) * (1 - lam_init)
        a = chunk_gmlp(u, v, ln_g, ln_b, ws, bs)
        return jnp.concatenate([a, o.reshape(b, s, B_WIDTH)], axis=-1) @ w_out

    ux, vx, qx, kx, vvx = project(nx, (cos, sin))
    uc, vc, qc, kc, vvc = project(nc, None)
    k_all = jnp.concatenate([kc, kx], axis=1)
    v_all = jnp.concatenate([vvc, vvx], axis=1)
    yx = finish(sweep_query_blocks(lambda qb: attend(qb, k_all, v_all), qx), ux, vx)
    yc = finish(attend(qc, kc, vvc), uc, vc) if need_ctx else None
    return yx, yc


def axial_gqa_mixer(nx, nc, w_qkv, w_out, q_norm, k_norm, cos, sin, need_ctx):
    scale = C_HEAD_DIM ** -0.5
    splits = [C_HEADS * C_HEAD_DIM, (C_HEADS + C_KV_HEADS) * C_HEAD_DIM]

    def project(n, rope):
        b, s, _ = n.shape
        q, k, v = jnp.split(n @ w_qkv, splits, axis=-1)
        q = rms_norm(q.reshape(b, s, C_HEADS, C_HEAD_DIM), q_norm)
        k = rms_norm(k.reshape(b, s, C_KV_HEADS, C_HEAD_DIM), k_norm)
        if rope is not None:
            q = apply_rope(q, *rope)
            k = apply_rope(k, *rope)
        return q.reshape(b, s, C_KV_HEADS, C_GROUP, C_HEAD_DIM), k, v.reshape(b, s, C_KV_HEADS, C_HEAD_DIM)

    def attend(qb, kk, vv):
        sc = jnp.einsum('bqngd,bknd->bngqk', qb, kk, preferred_element_type=jnp.float32) * scale
        p = jax.nn.softmax(sc, axis=-1)
        return jnp.einsum('bngqk,bknd->bqngd', p.astype(vv.dtype), vv)

    def finish(o):
        b, s = o.shape[:2]
        return o.reshape(b, s, ODD_OUT) @ w_out

    qx, kx, vx = project(nx, (cos, sin))
    qc, kc, vc = project(nc, None)
    k_all = jnp.concatenate([kc, kx], axis=1)
    v_all = jnp.concatenate([vc, vx], axis=1)
    yx = finish(sweep_query_blocks(lambda qb: attend(qb, k_all, v_all), qx))
    yc = finish(attend(qc, kc, vc)) if need_ctx else None
    return yx, yc


def swiglu(x, wg, wu, wd):
    return (jax.nn.silu(x @ wg) * (x @ wu)) @ wd


def routed_experts(xf, eidx, w, w_gate, w_up, w_down):
    t = xf.shape[0]
    n_slots = t * TOP_K
    n_blocks = -(-(n_slots + N_EXPERTS * (EXPERT_BLOCK - 1)) // EXPERT_BLOCK)
    n_rows = n_blocks * EXPERT_BLOCK
    e_flat = eidx.reshape(-1)
    order = jnp.argsort(e_flat)
    e_sorted = e_flat[order]
    tok_sorted = (order // TOP_K).astype(jnp.int32)
    w_sorted = w.reshape(-1)[order]
    counts = jnp.bincount(e_flat, length=N_EXPERTS)
    padded = (counts + EXPERT_BLOCK - 1) // EXPERT_BLOCK * EXPERT_BLOCK
    start = jnp.cumsum(counts) - counts
    pend = jnp.cumsum(padded)
    pstart = pend - padded
    dest = pstart[e_sorted] + jnp.arange(n_slots) - start[e_sorted]
    tok_rows = jnp.zeros((n_rows,), jnp.int32).at[dest].set(tok_sorted)
    w_rows = jnp.zeros((n_rows,), w.dtype).at[dest].set(w_sorted)
    block_expert = jnp.clip(jnp.searchsorted(pend, jnp.arange(n_blocks) * EXPERT_BLOCK, side='right'),
                            0, N_EXPERTS - 1)

    def block_ffn(args):
        tok, e = args
        return swiglu(xf[tok], w_gate[e], w_up[e], w_down[e])

    y = lax.map(block_ffn, (tok_rows.reshape(n_blocks, EXPERT_BLOCK), block_expert))
    y = y.reshape(n_rows, -1) * w_rows[:, None].astype(xf.dtype)
    return jnp.zeros_like(xf).at[tok_rows].add(y)


def moe_ffn(x, w_router, router_bias, w_gate, w_up, w_down, sw_gate, sw_up, sw_down):
    shp = x.shape
    xf = x.reshape(-1, shp[-1])
    t = xf.shape[0]
    scores = jax.nn.sigmoid((xf @ w_router).astype(jnp.float32))
    sel = scores + router_bias.astype(jnp.float32)
    grp_score = jnp.sum(lax.top_k(sel.reshape(t, N_GROUPS, N_EXPERTS // N_GROUPS), 2)[0], axis=-1)
    _, gidx = lax.top_k(grp_score, TOPK_GROUPS)
    gmask = jnp.any(gidx[..., None] == jnp.arange(N_GROUPS), axis=-2)
    emask = jnp.repeat(gmask, N_EXPERTS // N_GROUPS, axis=-1)
    _, eidx = lax.top_k(jnp.where(emask, sel, -jnp.inf), TOP_K)
    w = jnp.take_along_axis(scores, eidx, axis=-1)
    w = w / jnp.sum(w, axis=-1, keepdims=True) * ROUTE_SCALE
    out = routed_experts(xf, eidx, w, w_gate, w_up, w_down) + swiglu(xf, sw_gate, sw_up, sw_down)
    return out.reshape(shp)


def setup_inputs(seed: int = 0) -> dict:
    key = jax.random.key(seed)
    ks = iter(jax.random.split(key, 48))
    n_even = (DEPTH + 1) // 2
    n_odd = DEPTH // 2
    D = D_MODEL

    def nrm(shape, scale):
        return jax.random.normal(next(ks), shape, jnp.float32) * scale

    def gain(shape):
        return 1.0 + nrm(shape, 0.05)

    return {
        'x': nrm((BATCH, SEQ, D), 1.0),
        'c': nrm((BATCH, D), 1.0),
        'ctx': nrm((BATCH, CTX_LEN, D), 1.0),
        'c_ctx': nrm((D,), 1.0),
        'mod_w': nrm((DEPTH, D, 6 * D), 0.5 * D ** -0.5),
        'mod_b': nrm((DEPTH, 6 * D), 0.02),
        'norm1_g': gain((DEPTH, D)),
        'norm2_g': gain((DEPTH, D)),
        'ev_w_in': nrm((n_even, D, EVEN_IN), D ** -0.5),
        'ev_w_out': nrm((n_even, EVEN_OUT, D), EVEN_OUT ** -0.5),
        'a_ln_g': gain((n_even, A_WIDTH)),
        'a_ln_b': nrm((n_even, A_WIDTH), 0.02),
        'a_ws': nrm((n_even, A_GROUPS, CHUNK, CHUNK), CHUNK ** -0.5),
        'a_bs': gain((n_even, A_GROUPS, CHUNK)),
        'b_q_norm': gain((n_even, B_HEAD_DIM)),
        'b_k_norm': gain((n_even, B_HEAD_DIM)),
        'b_lam_q1': nrm((n_even, B_HEAD_DIM), 0.1),
        'b_lam_k1': nrm((n_even, B_HEAD_DIM), 0.1),
        'b_lam_q2': nrm((n_even, B_HEAD_DIM), 0.1),
        'b_lam_k2': nrm((n_even, B_HEAD_DIM), 0.1),
        'b_subln': gain((n_even, 2 * B_HEAD_DIM)),
        'od_w_qkv': nrm((n_odd, D, ODD_IN), D ** -0.5),
        'od_w_out': nrm((n_odd, ODD_OUT, D), ODD_OUT ** -0.5),
        'c_q_norm': gain((n_odd, C_HEAD_DIM)),
        'c_k_norm': gain((n_odd, C_HEAD_DIM)),
        'moe_router': nrm((DEPTH, D, N_EXPERTS), D ** -0.5),
        'moe_bias': nrm((DEPTH, N_EXPERTS), 0.01),
        'moe_w_gate': nrm((DEPTH, N_EXPERTS, D, D_EXPERT), D ** -0.5),
        'moe_w_up': nrm((DEPTH, N_EXPERTS, D, D_EXPERT), D ** -0.5),
        'moe_w_down': nrm((DEPTH, N_EXPERTS, D_EXPERT, D), D_EXPERT ** -0.5),
        'sh_w_gate': nrm((DEPTH, D, D_SHARED), D ** -0.5),
        'sh_w_up': nrm((DEPTH, D, D_SHARED), D ** -0.5),
        'sh_w_down': nrm((DEPTH, D_SHARED, D), D_SHARED ** -0.5),
    }


def reference(x, c, ctx, c_ctx, mod_w, mod_b, norm1_g, norm2_g, ev_w_in, ev_w_out,
              a_ln_g, a_ln_b, a_ws, a_bs, b_q_norm, b_k_norm, b_lam_q1, b_lam_k1,
              b_lam_q2, b_lam_k2, b_subln, od_w_qkv, od_w_out, c_q_norm, c_k_norm,
              moe_router, moe_bias, moe_w_gate, moe_w_up, moe_w_down,
              sh_w_gate, sh_w_up, sh_w_down):
    n_lat = x.shape[1]
    n_ctx = ctx.shape[1]
    cos_b, sin_b = axial_rope(n_lat, B_HEAD_DIM)
    cos_c, sin_c = axial_rope(n_lat, C_HEAD_DIM)
    hx, hc = x, ctx
    for layer in range(DEPTH):
        last = layer == DEPTH - 1
        i = layer // 2
        mx = [m[:, None, :] for m in adaln(c, mod_w[layer], mod_b[layer])]
        mc = adaln(c_ctx, mod_w[layer], mod_b[layer])
        nx = modulate(rms_norm(hx, norm1_g[layer]), mx[0], mx[1])
        nc = modulate(rms_norm(hc, norm1_g[layer]), mc[0], mc[1])
        if layer % 2 == 0:
            yx, yc = gmlp_diff_mixer(nx, nc, ev_w_in[i], ev_w_out[i], a_ln_g[i], a_ln_b[i], a_ws[i], a_bs[i],
                                     b_q_norm[i], b_k_norm[i], b_lam_q1[i], b_lam_k1[i], b_lam_q2[i],
                                     b_lam_k2[i], b_subln[i], layer, cos_b, sin_b, not last)
        else:
            yx, yc = axial_gqa_mixer(nx, nc, od_w_qkv[i], od_w_out[i], c_q_norm[i], c_k_norm[i],
                                     cos_c, sin_c, not last)
        hx = hx + mx[2] * yx
        moe_args = (moe_router[layer], moe_bias[layer], moe_w_gate[layer], moe_w_up[layer],
                    moe_w_down[layer], sh_w_gate[layer], sh_w_up[layer], sh_w_down[layer])
        nx2 = modulate(rms_norm(hx, norm2_g[layer]), mx[3], mx[4])
        if last:
            hx = hx + mx[5] * moe_ffn(nx2, *moe_args)
        else:
            hc = hc + mc[2] * yc
            nc2 = modulate(rms_norm(hc, norm2_g[layer]), mc[3], mc[4])
            y_all = moe_ffn(jnp.concatenate([nc2, nx2], axis=1), *moe_args)
            hc = hc + mc[5] * y_all[:, :n_ctx]
            hx = hx + mx[5] * y_all[:, n_ctx:]
    return hx
```

```python
import functools
import math

import numpy as np
import jax
import jax.numpy as jnp
from jax import lax
from jax.experimental import pallas as pl
from jax.experimental.pallas import tpu as pltpu

F32 = jnp.float32
BF16 = jnp.bfloat16
I32 = jnp.int32

GRID_W = 64
EPS = 1e-6
ROPE_THETA = 10000.0
A_GROUPS = 4
A_GROUP_DIM = 128
A_WIDTH = A_GROUPS * A_GROUP_DIM
GMLP_CHUNK = 128
B_HEADS = 4
B_HEAD_DIM = 64
B_WIDTH = B_HEADS * 2 * B_HEAD_DIM
C_HEADS = 8
C_KV_HEADS = 2
C_HEAD_DIM = 128
N_EXPERTS = 64
TOP_K = 6
N_GROUPS = 8
TOPK_GROUPS = 4
ROUTE_SCALE = 2.5

LANES = 128
SUBLANES = 8
ROW_TILE = 256
ROUTER_TILE = 512
EXPERT_TILE = 512
MOD_ROWS = 24
VMEM_LIMIT = 56 * 1024 * 1024

NT_DIMS = (((1,), (1,)), ((), ()))


def _cparams(*sem):
    return pltpu.CompilerParams(dimension_semantics=sem, vmem_limit_bytes=VMEM_LIMIT)


def _rms(x, g):
    return x * lax.rsqrt(jnp.mean(x * x, axis=-1, keepdims=True) + EPS) * g


def _norm_mod(h, g, shift, scale):
    return _rms(h, g) * (1.0 + scale) + shift


def _gelu(x):
    return 0.5 * x * (1.0 + lax.erf(x * np.float32(math.sqrt(0.5))))


def _silu(x):
    return x * jax.nn.sigmoid(x)


def _bdot(a, b):
    return jnp.dot(a.astype(BF16), b.astype(BF16), preferred_element_type=F32)


def _adaln_kernel(c_ref, w_ref, b_ref, o_ref):
    o_ref[0] = _bdot(_silu(c_ref[...]), w_ref[0]) + b_ref[0]


def _adaln(cond, mod_w, mod_b):
    depth, d, d6 = mod_w.shape
    tn = d6 // 4
    return pl.pallas_call(
        _adaln_kernel,
        grid=(depth, d6 // tn),
        in_specs=[
            pl.BlockSpec((MOD_ROWS, d), lambda l, j: (0, 0)),
            pl.BlockSpec((1, d, tn), lambda l, j: (l, 0, j)),
            pl.BlockSpec((1, 1, tn), lambda l, j: (l, 0, j)),
        ],
        out_specs=pl.BlockSpec((1, MOD_ROWS, tn), lambda l, j: (l, 0, j)),
        out_shape=jax.ShapeDtypeStruct((depth, MOD_ROWS, d6), F32),
        compiler_params=_cparams("parallel", "parallel"),
        name="adaln",
    )(cond, mod_w, mod_b.reshape(depth, 1, d6))


def _even_in_kernel(h_ref, sh_ref, sc_ref, g_ref, w_ref, qg_ref, kg_ref, c_ref, sa_ref, sb_ref,
                    uv_ref, q_ref, k_ref, v_ref):
    n = _norm_mod(h_ref[...], g_ref[...], sh_ref[0], sc_ref[0])
    p = jnp.dot(n.astype(BF16), w_ref[...], preferred_element_type=F32)
    uv_ref[...] = _gelu(p[:, :2 * A_WIDTH])
    cos, sa, sb = c_ref[...], sa_ref[...], sb_ref[...]
    lane = lax.broadcasted_iota(I32, cos.shape, 1)
    low = lane < B_HEAD_DIM

    def head_pair(x, gain, scale):
        sq = x * x
        s_lo = jnp.sum(jnp.where(low, sq, 0.0), axis=-1, keepdims=True)
        s_hi = jnp.sum(jnp.where(low, 0.0, sq), axis=-1, keepdims=True)
        ms = jnp.where(low, s_lo, s_hi) * np.float32(1.0 / B_HEAD_DIM)
        y = x * lax.rsqrt(ms + EPS) * gain
        y = (y * cos + pltpu.roll(y, LANES - B_HEAD_DIM // 2, 1) * sa
             + pltpu.roll(y, B_HEAD_DIM // 2, 1) * sb)
        if scale is not None:
            y = y * scale
        return y.astype(BF16)

    q0 = 2 * A_WIDTH
    k0 = q0 + B_WIDTH
    v0 = k0 + B_WIDTH
    for j in range(B_WIDTH // LANES):
        sl = slice(j * LANES, (j + 1) * LANES)
        q_ref[:, sl] = head_pair(p[:, q0 + j * LANES:q0 + (j + 1) * LANES], qg_ref[...],
                                 np.float32(B_HEAD_DIM ** -0.5))
        k_ref[:, sl] = head_pair(p[:, k0 + j * LANES:k0 + (j + 1) * LANES], kg_ref[...], None)
    v_ref[...] = p[:, v0:v0 + B_WIDTH].astype(BF16)


def _odd_in_kernel(h_ref, sh_ref, sc_ref, g_ref, w_ref, qg_ref, kg_ref, c_ref, s_ref,
                   q_ref, k_ref, v_ref):
    n = _norm_mod(h_ref[...], g_ref[...], sh_ref[0], sc_ref[0])
    p = jnp.dot(n.astype(BF16), w_ref[...], preferred_element_type=F32)
    cos, sin = c_ref[...], s_ref[...]

    def head(x, gain):
        y = _rms(x, gain)
        return (y * cos + pltpu.roll(y, C_HEAD_DIM // 2, 1) * sin).astype(BF16)

    nq = C_HEADS * C_HEAD_DIM
    nkv = C_KV_HEADS * C_HEAD_DIM
    for j in range(C_HEADS):
        q_ref[:, j * LANES:(j + 1) * LANES] = head(p[:, j * LANES:(j + 1) * LANES], qg_ref[...])
    for j in range(C_KV_HEADS):
        k_ref[:, j * LANES:(j + 1) * LANES] = head(
            p[:, nq + j * LANES:nq + (j + 1) * LANES], kg_ref[...])
    v_ref[...] = p[:, nq + nkv:nq + 2 * nkv].astype(BF16)


def _softmax_pv(q, k, v, scale):
    s = lax.dot_general(q, k, NT_DIMS, preferred_element_type=F32)
    m = jnp.max(s, axis=-1, keepdims=True)
    p = jnp.exp((s - m) * scale) if scale is not None else jnp.exp(s - m)
    l = jnp.sum(p, axis=-1, keepdims=True)
    return jnp.dot(p.astype(BF16), v, preferred_element_type=F32) / l


def _diff_attn_kernel(lam_ref, q_ref, k_ref, v_ref, o_ref, *, ctx_len, lam_init):
    lv = lam_ref[...]
    lam = (jnp.exp(jnp.sum(lv[0:1] * lv[1:2], axis=-1, keepdims=True))
           - jnp.exp(jnp.sum(lv[2:3] * lv[3:4], axis=-1, keepdims=True)) + np.float32(lam_init))
    q = q_ref[...]
    low = lax.broadcasted_iota(I32, q.shape, 1) < B_HEAD_DIM
    zero = jnp.zeros_like(q)
    q_a = jnp.where(low, q, zero)
    q_b = jnp.where(low, zero, q)

    def attend(n_keys):
        k = k_ref[0:n_keys, :]
        v = v_ref[0:n_keys, :]
        o_ref[...] = _softmax_pv(q_a, k, v, None) - lam * _softmax_pv(q_b, k, v, None)

    is_ctx = pl.program_id(2) == 0

    @pl.when(is_ctx)
    def _():
        attend(ctx_len)

    @pl.when(jnp.logical_not(is_ctx))
    def _():
        attend(k_ref.shape[0])


def _gqa_kernel(q_ref, k_ref, v_ref, o_ref):
    o_ref[...] = _softmax_pv(q_ref[...], k_ref[...], v_ref[...],
                             np.float32(C_HEAD_DIM ** -0.5)).astype(o_ref.dtype)


def _even_out_kernel(o_ref, uv_ref, h_ref, gate_ref, sh_ref, sc_ref, sub_ref, lng_ref, lnb_ref,
                     ws_ref, bs_ref, w_ref, g2_ref, h1_ref, nx_ref, *, lam_init):
    o = o_ref[...]
    uv = uv_ref[...]
    u = uv[:, :A_WIDTH]
    v = uv[:, A_WIDTH:]
    mu = jnp.mean(v, axis=-1, keepdims=True)
    var = jnp.mean(jnp.square(v - mu), axis=-1, keepdims=True)
    vn = ((v - mu) * lax.rsqrt(var + EPS) * lng_ref[...] + lnb_ref[...]).astype(BF16)
    rows = o.shape[0]
    parts = []
    for c in range(rows // GMLP_CHUNK):
        rs = slice(c * GMLP_CHUNK, (c + 1) * GMLP_CHUNK)
        for g in range(A_GROUPS):
            cs = slice(g * A_GROUP_DIM, (g + 1) * A_GROUP_DIM)
            mixed = jnp.dot(ws_ref[g], vn[rs, cs], preferred_element_type=F32) + bs_ref[:, cs]
            parts.append((c, g, u[rs, cs] * mixed))
    a_rows = [jnp.concatenate([p for (c2, _, p) in parts if c2 == c], axis=-1)
              for c in range(rows // GMLP_CHUNK)]
    a = jnp.concatenate(a_rows, axis=0)
    heads = []
    for hh in range(B_HEADS):
        oh = o[:, hh * LANES:(hh + 1) * LANES]
        heads.append(_rms(oh, sub_ref[...]) * np.float32(1.0 - lam_init))
    cat = jnp.concatenate([a] + heads, axis=-1).astype(BF16)
    y = jnp.dot(cat, w_ref[...], preferred_element_type=F32)
    h1 = h_ref[...] + gate_ref[0] * y
    h1_ref[...] = h1
    nx_ref[...] = _norm_mod(h1, g2_ref[...], sh_ref[0], sc_ref[0])


def _odd_out_kernel(o_ref, h_ref, gate_ref, sh_ref, sc_ref, w_ref, g2_ref, h1_ref, nx_ref):
    y = jnp.dot(o_ref[...], w_ref[...], preferred_element_type=F32)
    h1 = h_ref[...] + gate_ref[0] * y
    h1_ref[...] = h1
    nx_ref[...] = _norm_mod(h1, g2_ref[...], sh_ref[0], sc_ref[0])


def _rows_to_block(rows, dtype):
    n = rows[0].shape[1]
    rio = lax.broadcasted_iota(I32, (SUBLANES, n), 0)
    out = jnp.zeros((SUBLANES, n), dtype)
    for r, row in enumerate(rows):
        out = jnp.where(rio == r, jnp.broadcast_to(row.astype(dtype), (SUBLANES, n)), out)
    return out


def _router_kernel(x_ref, wr_ref, b_ref, eidx_ref, rank_ref, wt_ref, cnt_ref, run_ref):
    @pl.when(pl.program_id(0) == 0)
    def _():
        run_ref[...] = jnp.zeros_like(run_ref)

    per = N_EXPERTS // N_GROUPS
    logits = lax.dot_general(wr_ref[...], x_ref[...].astype(BF16), NT_DIMS,
                             preferred_element_type=F32)
    scores = jax.nn.sigmoid(logits)
    sel = scores + b_ref[...]
    tm = sel.shape[1]
    neg = np.float32(-np.inf)
    jio = lax.broadcasted_iota(I32, (per, tm), 0).astype(F32)
    gio = lax.broadcasted_iota(I32, (N_GROUPS, tm), 0).astype(F32)

    def rmax(x):
        return jnp.max(x, axis=0, keepdims=True)

    def rmin(x):
        return jnp.min(x, axis=0, keepdims=True)

    sel_g = [sel[g * per:(g + 1) * per, :] for g in range(N_GROUPS)]
    sc_g = [scores[g * per:(g + 1) * per, :] for g in range(N_GROUPS)]
    gs = jnp.zeros((N_GROUPS, tm), F32)
    for g in range(N_GROUPS):
        m1 = rmax(sel_g[g])
        i1 = rmin(jnp.where(sel_g[g] == m1, jio, np.float32(per)))
        m2 = rmax(jnp.where(jio == i1, neg, sel_g[g]))
        gs = jnp.where(gio == np.float32(g), jnp.broadcast_to(m1 + m2, gs.shape), gs)
    gsel = jnp.zeros((N_GROUPS, tm), I32)
    for _ in range(TOPK_GROUPS):
        m = rmax(gs)
        idx = rmin(jnp.where(gs == m, gio, np.float32(N_GROUPS)))
        hit = gio == idx
        gsel = jnp.where(hit, 1, gsel)
        gs = jnp.where(hit, neg, gs)
    masked = [jnp.where(jnp.broadcast_to(gsel[g:g + 1, :], (per, tm)) == 1, sel_g[g], neg)
              for g in range(N_GROUPS)]
    eio = [jio + np.float32(g * per) for g in range(N_GROUPS)]
    e_rows, w_rows, hits = [], [], []
    for _ in range(TOP_K):
        m = masked[0]
        for g in range(1, N_GROUPS):
            m = jnp.maximum(m, masked[g])
        m = rmax(m)
        cand = jnp.where(masked[0] == m, eio[0], np.float32(N_EXPERTS))
        for g in range(1, N_GROUPS):
            cand = jnp.minimum(cand, jnp.where(masked[g] == m, eio[g], np.float32(N_EXPERTS)))
        idx = rmin(cand)
        hit = [eio[g] == idx for g in range(N_GROUPS)]
        wsel = jnp.where(hit[0], sc_g[0], 0.0)
        for g in range(1, N_GROUPS):
            wsel = wsel + jnp.where(hit[g], sc_g[g], 0.0)
        masked = [jnp.where(hit[g], neg, masked[g]) for g in range(N_GROUPS)]
        e_rows.append(idx)
        w_rows.append(jnp.sum(wsel, axis=0, keepdims=True))
        hits.append(hit)
    wsum = w_rows[0]
    for r in w_rows[1:]:
        wsum = wsum + r
    w_rows = [r / wsum * np.float32(ROUTE_SCALE) for r in w_rows]
    onehot = []
    for g in range(N_GROUPS):
        any_hit = hits[0][g]
        for kk in range(1, TOP_K):
            any_hit = jnp.logical_or(any_hit, hits[kk][g])
        onehot.append(jnp.where(any_hit, 1.0, 0.0))
    mt = jnp.concatenate(onehot, axis=0)
    before = (lax.broadcasted_iota(I32, (tm, tm), 0) < lax.broadcasted_iota(I32, (tm, tm), 1))
    prefix = jnp.dot(mt.astype(BF16), jnp.where(before, 1.0, 0.0).astype(BF16),
                     preferred_element_type=F32)
    pos = prefix + run_ref[...]
    r_rows = []
    for kk in range(TOP_K):
        acc = jnp.where(hits[kk][0], pos[0:per, :], 0.0)
        for g in range(1, N_GROUPS):
            acc = acc + jnp.where(hits[kk][g], pos[g * per:(g + 1) * per, :], 0.0)
        r_rows.append(jnp.sum(acc, axis=0, keepdims=True))
    run = run_ref[...] + jnp.sum(mt, axis=1, keepdims=True)
    run_ref[...] = run
    eidx_ref[...] = _rows_to_block(e_rows, I32)
    rank_ref[...] = _rows_to_block(r_rows, I32)
    wt_ref[...] = _rows_to_block(w_rows, F32)
    cnt_ref[...] = jnp.broadcast_to(run, cnt_ref.shape)


def _dest_kernel(start_ref, eidx_ref, rank_ref, dest_ref):
    per = N_EXPERTS // N_GROUPS
    eidx = eidx_ref[...]
    tm = eidx.shape[1]
    jio = lax.broadcasted_iota(I32, (per, tm), 0)
    rows = []
    for kk in range(TOP_K):
        e = jnp.broadcast_to(eidx[kk:kk + 1, :], (per, tm))
        acc = jnp.zeros((per, tm), F32)
        for g in range(N_GROUPS):
            st = jnp.broadcast_to(start_ref[g * per:(g + 1) * per, :], (per, tm))
            acc = acc + jnp.where(jio + g * per == e, st, 0.0)
        rows.append(jnp.sum(acc, axis=0, keepdims=True))
    dest_ref[...] = _rows_to_block(rows, I32) + rank_ref[...]


def _wcol_kernel(wt_ref, wcol_ref):
    wt = wt_ref[...]
    tm = wt.shape[1]
    eye = jnp.where(lax.broadcasted_iota(I32, (tm, tm), 0) == lax.broadcasted_iota(I32, (tm, tm), 1),
                    1.0, 0.0).astype(BF16)
    acc = jnp.zeros((tm, SUBLANES), F32)
    rem = wt
    for _ in range(3):
        part = rem.astype(BF16)
        acc = acc + lax.dot_general(eye, part, NT_DIMS, preferred_element_type=F32)
        rem = rem - part.astype(F32)
    wcol_ref[...] = acc


def _row_copy(src, s_row, dst, d_row, sem):
    return pltpu.make_async_copy(src.at[pl.ds(s_row, 1)], dst.at[pl.ds(d_row, 1)], sem)


def _dispatch_kernel(dest_ref, x_ref, xs_ref, sem):
    rows = x_ref.shape[0]

    def copies(t):
        return [_row_copy(x_ref, t, xs_ref, dest_ref[kk, t], sem) for kk in range(TOP_K)]

    def issue(t, carry):
        for cp in copies(t):
            cp.start()
        return carry

    def drain(t, carry):
        for cp in copies(t):
            cp.wait()
        return carry

    lax.fori_loop(0, rows, issue, 0)
    lax.fori_loop(0, rows, drain, 0)


def _expert_kernel(tile_s, exp_s, lo_s, hi_s, first_s, new_s, xs_ref, wg_ref, wu_ref, wd_ref,
                   ys_ref, wg_b, wu_b, wd_b):
    v = pl.program_id(0)

    @pl.when(new_s[v] == 1)
    def _():
        wg_b[...] = wg_ref[0].astype(BF16)
        wu_b[...] = wu_ref[0].astype(BF16)
        wd_b[...] = wd_ref[0].astype(BF16)

    @pl.when(first_s[v] == 1)
    def _():
        ys_ref[...] = jnp.zeros_like(ys_ref)

    lo = lo_s[v]
    hi = hi_s[v]

    @pl.when(hi > lo)
    def _():
        x = xs_ref[...].astype(BF16)
        g = jnp.dot(x, wg_b[...], preferred_element_type=F32)
        u = jnp.dot(x, wu_b[...], preferred_element_type=F32)
        y = jnp.dot((_silu(g) * u).astype(BF16), wd_b[...], preferred_element_type=F32)
        row = lax.broadcasted_iota(I32, (y.shape[0], 1), 0)
        mine = jnp.logical_and(row >= lo, row < hi)
        ys_ref[...] = jnp.where(mine, y, ys_ref[...])


def _combine_kernel(dest_ref, wcol_ref, x_ref, h_ref, gate_ref, sg_ref, su_ref, sd_ref, ys_ref,
                    o_ref, buf, sem):
    rows = x_ref.shape[0]

    def copies(t):
        return [_row_copy(ys_ref, dest_ref[kk, t], buf.at[kk], t, sem) for kk in range(TOP_K)]

    def issue(t, carry):
        for cp in copies(t):
            cp.start()
        return carry

    def drain(t, carry):
        for cp in copies(t):
            cp.wait()
        return carry

    lax.fori_loop(0, rows, issue, 0)
    x = x_ref[...].astype(BF16)
    g = jnp.dot(x, sg_ref[...], preferred_element_type=F32)
    u = jnp.dot(x, su_ref[...], preferred_element_type=F32)
    shared = jnp.dot((_silu(g) * u).astype(BF16), sd_ref[...], preferred_element_type=F32)
    lax.fori_loop(0, rows, drain, 0)
    wcol = wcol_ref[...]
    routed = buf[0] * wcol[:, 0:1]
    for kk in range(1, TOP_K):
        routed = routed + buf[kk] * wcol[:, kk:kk + 1]
    o_ref[...] = h_ref[...] + gate_ref[0] * (routed + shared)


def _moe(nx, h, modv, gate_row_of, w_router, router_bias, w_gate, w_up, w_down, sg, su, sd):
    t, d = nx.shape
    n_slots = t * TOP_K
    n_rt = t // ROUTER_TILE
    e = N_EXPERTS
    eidx, rank, wt, cnt = pl.pallas_call(
        _router_kernel,
        grid=(n_rt,),
        in_specs=[
            pl.BlockSpec((ROUTER_TILE, d), lambda i: (i, 0)),
            pl.BlockSpec((e, d), lambda i: (0, 0)),
            pl.BlockSpec((e, 1), lambda i: (0, 0)),
        ],
        out_specs=[
            pl.BlockSpec((SUBLANES, ROUTER_TILE), lambda i: (0, i)),
            pl.BlockSpec((SUBLANES, ROUTER_TILE), lambda i: (0, i)),
            pl.BlockSpec((SUBLANES, ROUTER_TILE), lambda i: (0, i)),
            pl.BlockSpec((e, LANES), lambda i: (0, 0)),
        ],
        out_shape=[
            jax.ShapeDtypeStruct((SUBLANES, t), I32),
            jax.ShapeDtypeStruct((SUBLANES, t), I32),
            jax.ShapeDtypeStruct((SUBLANES, t), F32),
            jax.ShapeDtypeStruct((e, LANES), F32),
        ],
        scratch_shapes=[pltpu.VMEM((e, 1), F32)],
        compiler_params=_cparams("arbitrary"),
        name="moe_router",
    )(nx, w_router.T.astype(BF16), router_bias.reshape(e, 1))

    counts = cnt[:, 0].astype(I32)
    ends = jnp.cumsum(counts)
    starts = ends - counts
    dest = pl.pallas_call(
        _dest_kernel,
        grid=(n_rt,),
        in_specs=[
            pl.BlockSpec((e, 1), lambda i: (0, 0)),
            pl.BlockSpec((SUBLANES, ROUTER_TILE), lambda i: (0, i)),
            pl.BlockSpec((SUBLANES, ROUTER_TILE), lambda i: (0, i)),
        ],
        out_specs=pl.BlockSpec((SUBLANES, ROUTER_TILE), lambda i: (0, i)),
        out_shape=jax.ShapeDtypeStruct((SUBLANES, t), I32),
        compiler_params=_cparams("parallel"),
        name="moe_dest",
    )(starts.astype(F32).reshape(e, 1), eidx, rank)
    wcol = pl.pallas_call(
        _wcol_kernel,
        grid=(n_rt,),
        in_specs=[pl.BlockSpec((SUBLANES, ROUTER_TILE), lambda i: (0, i))],
        out_specs=pl.BlockSpec((ROUTER_TILE, SUBLANES), lambda i: (i, 0)),
        out_shape=jax.ShapeDtypeStruct((t, SUBLANES), F32),
        compiler_params=_cparams("parallel"),
        name="moe_wcol",
    )(wt)

    n_tiles = t // ROW_TILE
    xs = pl.pallas_call(
        _dispatch_kernel,
        grid=(n_tiles,),
        in_specs=[
            pl.BlockSpec((SUBLANES, ROW_TILE), lambda i: (0, i), memory_space=pltpu.SMEM),
            pl.BlockSpec((ROW_TILE, d), lambda i: (i, 0)),
        ],
        out_specs=pl.BlockSpec(memory_space=pl.ANY),
        out_shape=jax.ShapeDtypeStruct((n_slots, d), F32),
        scratch_shapes=[pltpu.SemaphoreType.DMA],
        compiler_params=pltpu.CompilerParams(dimension_semantics=("arbitrary",),
                                             vmem_limit_bytes=VMEM_LIMIT, has_side_effects=True),
        name="moe_dispatch",
    )(dest, nx)

    n_et = n_slots // EXPERT_TILE
    pts = jnp.sort(jnp.concatenate([jnp.arange(n_et, dtype=I32) * EXPERT_TILE, starts]))
    lo = pts
    hi = jnp.concatenate([pts[1:], jnp.full((1,), n_slots, I32)])
    tile = jnp.minimum(lo // EXPERT_TILE, n_et - 1)
    expert = jnp.clip(jnp.searchsorted(ends, lo, side="right"), 0, e - 1).astype(I32)
    one = jnp.ones((1,), I32)
    first = jnp.concatenate([one, (tile[1:] != tile[:-1]).astype(I32)])
    newexp = jnp.concatenate([one, (expert[1:] != expert[:-1]).astype(I32)])
    lo_in = lo - tile * EXPERT_TILE
    hi_in = hi - tile * EXPERT_TILE
    n_visits = n_et + e
    d_exp = w_gate.shape[-1]
    ys = pl.pallas_call(
        _expert_kernel,
        grid_spec=pltpu.PrefetchScalarGridSpec(
            num_scalar_prefetch=6,
            grid=(n_visits,),
            in_specs=[
                pl.BlockSpec((EXPERT_TILE, d), lambda v, ti, ex, *_: (ti[v], 0)),
                pl.BlockSpec((1, d, d_exp), lambda v, ti, ex, *_: (ex[v], 0, 0)),
                pl.BlockSpec((1, d, d_exp), lambda v, ti, ex, *_: (ex[v], 0, 0)),
                pl.BlockSpec((1, d_exp, d), lambda v, ti, ex, *_: (ex[v], 0, 0)),
            ],
            out_specs=pl.BlockSpec((EXPERT_TILE, d), lambda v, ti, ex, *_: (ti[v], 0)),
            scratch_shapes=[pltpu.VMEM((d, d_exp), BF16), pltpu.VMEM((d, d_exp), BF16),
                            pltpu.VMEM((d_exp, d), BF16)],
        ),
        out_shape=jax.ShapeDtypeStruct((n_slots, d), F32),
        compiler_params=_cparams("arbitrary"),
        name="moe_experts",
    )(tile, expert, lo_in, hi_in, first, newexp, xs, w_gate, w_up, w_down)

    d_sh = sg.shape[-1]
    return pl.pallas_call(
        _combine_kernel,
        grid=(n_tiles,),
        in_specs=[
            pl.BlockSpec((SUBLANES, ROW_TILE), lambda i: (0, i), memory_space=pltpu.SMEM),
            pl.BlockSpec((ROW_TILE, SUBLANES), lambda i: (i, 0)),
            pl.BlockSpec((ROW_TILE, d), lambda i: (i, 0)),
            pl.BlockSpec((ROW_TILE, d), lambda i: (i, 0)),
            pl.BlockSpec((1, 1, d), lambda i: (gate_row_of(i) * 6 + 5, 0, 0)),
            pl.BlockSpec((d, d_sh), lambda i: (0, 0)),
            pl.BlockSpec((d, d_sh), lambda i: (0, 0)),
            pl.BlockSpec((d_sh, d), lambda i: (0, 0)),
            pl.BlockSpec(memory_space=pl.ANY),
        ],
        out_specs=pl.BlockSpec((ROW_TILE, d), lambda i: (i, 0)),
        out_shape=jax.ShapeDtypeStruct((t, d), F32),
        scratch_shapes=[pltpu.VMEM((TOP_K, ROW_TILE, d), F32), pltpu.SemaphoreType.DMA],
        compiler_params=_cparams("arbitrary"),
        name="moe_combine",
    )(dest, wcol, nx, h, modv, sg.astype(BF16), su.astype(BF16), sd.astype(BF16), ys)


def _rope_tables(n_lat, n_ctx, head_dim):
    rows = n_lat // GRID_W
    row = jnp.repeat(jnp.arange(rows, dtype=F32), GRID_W)
    col = jnp.tile(jnp.arange(GRID_W, dtype=F32), rows)
    n_freq = head_dim // 4
    inv = ROPE_THETA ** (-jnp.arange(n_freq, dtype=F32) / n_freq)
    ang = jnp.concatenate([row[:, None] * inv, col[:, None] * inv], axis=-1)
    cos = jnp.concatenate([jnp.ones((n_ctx, head_dim // 2), F32), jnp.cos(ang)], axis=0)
    sin = jnp.concatenate([jnp.zeros((n_ctx, head_dim // 2), F32), jnp.sin(ang)], axis=0)
    return cos, sin


def _split_halves_perm(head_dim):
    return np.concatenate([np.arange(0, head_dim, 2), np.arange(1, head_dim, 2)])


def kernel(x, c, ctx, c_ctx, mod_w, mod_b, norm1_g, norm2_g, ev_w_in, ev_w_out, a_ln_g, a_ln_b, a_ws, a_bs, b_q_norm, b_k_norm, b_lam_q1, b_lam_k1, b_lam_q2, b_lam_k2, b_subln, od_w_qkv, od_w_out, c_q_norm, c_k_norm, moe_router, moe_bias, moe_w_gate, moe_w_up, moe_w_down, sh_w_gate, sh_w_up, sh_w_down):
    bsz, n_lat, d = x.shape
    n_ctx = ctx.shape[1]
    depth = mod_w.shape[0]
    assert depth == 2 and n_ctx == ROW_TILE and n_lat % ROW_TILE == 0 and bsz + 1 <= MOD_ROWS
    n_seq = n_ctx + n_lat
    tpb = n_seq // ROW_TILE
    lpb = n_lat // ROW_TILE
    t_all = bsz * n_seq
    n_tiles = t_all // ROW_TILE
    ctx_row = bsz

    cond = jnp.zeros((MOD_ROWS, d), F32).at[:bsz].set(c).at[ctx_row].set(c_ctx)
    mod = _adaln(cond, mod_w, mod_b)
    modv = [mod[l].reshape(MOD_ROWS * 6, 1, d) for l in range(depth)]

    def row_all(i):
        return jnp.where(i % tpb == 0, ctx_row, i // tpb)

    def mspec(j, row_of):
        return pl.BlockSpec((1, 1, d), lambda i: (row_of(i) * 6 + j, 0, 0))

    def full(shape):
        return pl.BlockSpec(shape, lambda *_: (0,) * len(shape))

    h0 = jnp.concatenate([ctx, x], axis=1).reshape(t_all, d)

    lam_init = 0.8 - 0.6 * math.exp(-0.3 * 0)
    p64 = _split_halves_perm(B_HEAD_DIM)
    col_perm = np.concatenate(
        [np.arange(2 * A_WIDTH)]
        + [2 * A_WIDTH + blk * B_HEAD_DIM + p64 for blk in range(2 * B_WIDTH // B_HEAD_DIM)]
        + [np.arange(2 * A_WIDTH + 2 * B_WIDTH, 2 * A_WIDTH + 3 * B_WIDTH)])
    w_in = ev_w_in[0][:, col_perm].astype(BF16)
    even_in = w_in.shape[1]
    cos_b, sin_b = _rope_tables(n_lat, n_ctx, B_HEAD_DIM)
    zeros_b = jnp.zeros_like(sin_b)
    tab_c = jnp.tile(jnp.concatenate([cos_b, cos_b], axis=-1), (1, 2))
    tab_sa = jnp.tile(jnp.concatenate([-sin_b, zeros_b], axis=-1), (1, 2))
    tab_sb = jnp.tile(jnp.concatenate([zeros_b, sin_b], axis=-1), (1, 2))
    qg = jnp.tile(b_q_norm[0][p64], 2).reshape(1, LANES)
    kg = jnp.tile(b_k_norm[0][p64], 2).reshape(1, LANES)
    tab_spec = pl.BlockSpec((ROW_TILE, LANES), lambda i: (i % tpb, 0))
    row_spec = lambda w: pl.BlockSpec((ROW_TILE, w), lambda i: (i, 0))
    uv, q, k, v = pl.pallas_call(
        _even_in_kernel,
        grid=(n_tiles,),
        in_specs=[row_spec(d), mspec(0, row_all), mspec(1, row_all), full((1, d)),
                  full((d, even_in)), full((1, LANES)), full((1, LANES)),
                  tab_spec, tab_spec, tab_spec],
        out_specs=[row_spec(2 * A_WIDTH), row_spec(B_WIDTH), row_spec(B_WIDTH), row_spec(B_WIDTH)],
        out_shape=[jax.ShapeDtypeStruct((t_all, 2 * A_WIDTH), F32),
                   jax.ShapeDtypeStruct((t_all, B_WIDTH), BF16),
                   jax.ShapeDtypeStruct((t_all, B_WIDTH), BF16),
                   jax.ShapeDtypeStruct((t_all, B_WIDTH), BF16)],
        compiler_params=_cparams("parallel"),
        name="even_in",
    )(h0, modv[0], modv[0], norm1_g[0].reshape(1, d), w_in, qg, kg, tab_c, tab_sa, tab_sb)

    lamv = jnp.zeros((SUBLANES, LANES), F32)
    for r, vec in enumerate((b_lam_q1[0], b_lam_k1[0], b_lam_q2[0], b_lam_k2[0])):
        lamv = lamv.at[r, :B_HEAD_DIM].set(vec)
    o = pl.pallas_call(
        functools.partial(_diff_attn_kernel, ctx_len=n_ctx, lam_init=lam_init),
        grid=(bsz, B_HEADS, tpb),
        in_specs=[
            pl.BlockSpec((SUBLANES, LANES), lambda b, hh, qi: (0, 0)),
            pl.BlockSpec((ROW_TILE, LANES), lambda b, hh, qi: (b * tpb + qi, hh)),
            pl.BlockSpec((n_seq, LANES), lambda b, hh, qi: (b, hh)),
            pl.BlockSpec((n_seq, LANES), lambda b, hh, qi: (b, hh)),
        ],
        out_specs=pl.BlockSpec((ROW_TILE, LANES), lambda b, hh, qi: (b * tpb + qi, hh)),
        out_shape=jax.ShapeDtypeStruct((t_all, B_WIDTH), F32),
        compiler_params=_cparams("parallel", "parallel", "arbitrary"),
        name="diff_attn",
    )(lamv, q, k, v)

    bs_col = jnp.repeat(a_bs[0].T, A_GROUP_DIM, axis=1)
    sub_g = b_subln[0].reshape(1, LANES)
    h1, nx = pl.pallas_call(
        functools.partial(_even_out_kernel, lam_init=lam_init),
        grid=(n_tiles,),
        in_specs=[row_spec(B_WIDTH), row_spec(2 * A_WIDTH), row_spec(d),
                  mspec(2, row_all), mspec(3, row_all), mspec(4, row_all),
                  full((1, LANES)), full((1, A_WIDTH)), full((1, A_WIDTH)),
                  full((A_GROUPS, GMLP_CHUNK, GMLP_CHUNK)), full((GMLP_CHUNK, A_WIDTH)),
                  full((A_WIDTH + B_WIDTH, d)), full((1, d))],
        out_specs=[row_spec(d), row_spec(d)],
        out_shape=[jax.ShapeDtypeStruct((t_all, d), F32), jax.ShapeDtypeStruct((t_all, d), F32)],
        compiler_params=_cparams("parallel"),
        name="even_out",
    )(o, uv, h0, modv[0], modv[0], modv[0], sub_g, a_ln_g[0].reshape(1, A_WIDTH),
      a_ln_b[0].reshape(1, A_WIDTH), a_ws[0].astype(BF16), bs_col,
      ev_w_out[0].astype(BF16), norm2_g[0].reshape(1, d))

    h2 = _moe(nx, h1, modv[0], row_all, moe_router[0], moe_bias[0], moe_w_gate[0], moe_w_up[0],
              moe_w_down[0], sh_w_gate[0], sh_w_up[0], sh_w_down[0])

    p128 = _split_halves_perm(C_HEAD_DIM)
    n_qkv_heads = C_HEADS + 2 * C_KV_HEADS
    col_perm = np.concatenate(
        [blk * C_HEAD_DIM + p128 for blk in range(C_HEADS + C_KV_HEADS)]
        + [np.arange((C_HEADS + C_KV_HEADS) * C_HEAD_DIM, n_qkv_heads * C_HEAD_DIM)])
    w_qkv = od_w_qkv[0][:, col_perm].astype(BF16)
    cos_c, sin_c = _rope_tables(n_lat, n_ctx, C_HEAD_DIM)
    tab_c1 = jnp.concatenate([cos_c, cos_c], axis=-1)
    tab_s1 = jnp.concatenate([-sin_c, sin_c], axis=-1)
    qg1 = c_q_norm[0][p128].reshape(1, LANES)
    kg1 = c_k_norm[0][p128].reshape(1, LANES)
    nq = C_HEADS * C_HEAD_DIM
    nkv = C_KV_HEADS * C_HEAD_DIM
    q1, k1, v1 = pl.pallas_call(
        _odd_in_kernel,
        grid=(n_tiles,),
        in_specs=[row_spec(d), mspec(0, row_all), mspec(1, row_all), full((1, d)),
                  full((d, nq + 2 * nkv)), full((1, LANES)), full((1, LANES)), tab_spec, tab_spec],
        out_specs=[row_spec(nq), row_spec(nkv), row_spec(nkv)],
        out_shape=[jax.ShapeDtypeStruct((t_all, nq), BF16),
                   jax.ShapeDtypeStruct((t_all, nkv), BF16),
                   jax.ShapeDtypeStruct((t_all, nkv), BF16)],
        compiler_params=_cparams("parallel"),
        name="odd_in",
    )(h2, modv[1], modv[1], norm1_g[1].reshape(1, d), w_qkv, qg1, kg1, tab_c1, tab_s1)

    t_lat = bsz * n_lat
    grp = C_HEADS // C_KV_HEADS
    o1 = pl.pallas_call(
        _gqa_kernel,
        grid=(bsz, C_HEADS, lpb),
        in_specs=[
            pl.BlockSpec((ROW_TILE, LANES), lambda b, hh, qi: (b * tpb + 1 + qi, hh)),
            pl.BlockSpec((n_seq, LANES), lambda b, hh, qi: (b, hh // grp)),
            pl.BlockSpec((n_seq, LANES), lambda b, hh, qi: (b, hh // grp)),
        ],
        out_specs=pl.BlockSpec((ROW_TILE, LANES), lambda b, hh, qi: (b * lpb + qi, hh)),
        out_shape=jax.ShapeDtypeStruct((t_lat, nq), BF16),
        compiler_params=_cparams("parallel", "parallel", "arbitrary"),
        name="gqa_attn",
    )(q1, k1, v1)

    def row_lat(i):
        return i // lpb

    lat_tiles = t_lat // ROW_TILE
    hx, nx1 = pl.pallas_call(
        _odd_out_kernel,
        grid=(lat_tiles,),
        in_specs=[row_spec(nq),
                  pl.BlockSpec((ROW_TILE, d), lambda i: ((i // lpb) * tpb + 1 + i % lpb, 0)),
                  mspec(2, row_lat), mspec(3, row_lat), mspec(4, row_lat),
                  full((nq, d)), full((1, d))],
        out_specs=[row_spec(d), row_spec(d)],
        out_shape=[jax.ShapeDtypeStruct((t_lat, d), F32), jax.ShapeDtypeStruct((t_lat, d), F32)],
        compiler_params=_cparams("parallel"),
        name="odd_out",
    )(o1, h2, modv[1], modv[1], modv[1], od_w_out[0].astype(BF16), norm2_g[1].reshape(1, d))

    out = _moe(nx1, hx, modv[1], row_lat, moe_router[1], moe_bias[1], moe_w_gate[1], moe_w_up[1],
               moe_w_down[1], sh_w_gate[1], sh_w_up[1], sh_w_down[1])
    return out.reshape(bsz, n_lat, d)
```

```python
import functools
import math

import numpy as np
import jax
import jax.numpy as jnp
from jax import lax
from jax.experimental import pallas as pl
from jax.experimental.pallas import tpu as pltpu

F32 = jnp.float32
BF16 = jnp.bfloat16
I32 = jnp.int32

GRID_W = 64
EPS = 1e-6
ROPE_THETA = 10000.0
A_GROUPS = 4
A_GROUP_DIM = 128
A_WIDTH = A_GROUPS * A_GROUP_DIM
GMLP_CHUNK = 128
B_HEADS = 4
B_HEAD_DIM = 64
B_WIDTH = B_HEADS * 2 * B_HEAD_DIM
C_HEADS = 8
C_KV_HEADS = 2
C_HEAD_DIM = 128
N_EXPERTS = 64
TOP_K = 6
N_GROUPS = 8
TOPK_GROUPS = 4
ROUTE_SCALE = 2.5

LANES = 128
SUBLANES = 8
ROW_TILE = 256
ROUTER_TILE = 512
EXPERT_TILE = 512
MOD_ROWS = 24
VMEM_LIMIT = 56 * 1024 * 1024

NT_DIMS = (((1,), (1,)), ((), ()))


def _cparams(*sem):
    return pltpu.CompilerParams(dimension_semantics=sem, vmem_limit_bytes=VMEM_LIMIT)


def _rms(x, g):
    return x * lax.rsqrt(jnp.mean(x * x, axis=-1, keepdims=True) + EPS) * g


def _norm_mod(h, g, shift, scale):
    return _rms(h, g) * (1.0 + scale) + shift


def _gelu(x):
    return 0.5 * x * (1.0 + lax.erf(x * np.float32(math.sqrt(0.5))))


def _silu(x):
    return x * jax.nn.sigmoid(x)


def _bdot(a, b):
    return jnp.dot(a.astype(BF16), b.astype(BF16), preferred_element_type=F32)


def _from_token_tiles(ref, rows, d):
    ch = d // LANES
    groups = []
    for g in range(rows // SUBLANES):
        groups.append(jnp.concatenate(
            [ref[pl.ds(g * SUBLANES * ch + j, SUBLANES, stride=ch), :] for j in range(ch)], axis=-1))
    return jnp.concatenate(groups, axis=0)


def _to_token_tiles(ref, val):
    rows, d = val.shape
    ch = d // LANES
    for g in range(rows // SUBLANES):
        for j in range(ch):
            ref[pl.ds(g * SUBLANES * ch + j, SUBLANES, stride=ch), :] = (
                val[g * SUBLANES:(g + 1) * SUBLANES, j * LANES:(j + 1) * LANES])


def _adaln_kernel(c_ref, w_ref, b_ref, o_ref):
    o_ref[0] = _bdot(_silu(c_ref[...]), w_ref[0]) + b_ref[0]


def _adaln(cond, mod_w, mod_b):
    depth, d, d6 = mod_w.shape
    tn = d6 // 4
    return pl.pallas_call(
        _adaln_kernel,
        grid=(depth, d6 // tn),
        in_specs=[
            pl.BlockSpec((MOD_ROWS, d), lambda l, j: (0, 0)),
            pl.BlockSpec((1, d, tn), lambda l, j: (l, 0, j)),
            pl.BlockSpec((1, 1, tn), lambda l, j: (l, 0, j)),
        ],
        out_specs=pl.BlockSpec((1, MOD_ROWS, tn), lambda l, j: (l, 0, j)),
        out_shape=jax.ShapeDtypeStruct((depth, MOD_ROWS, d6), F32),
        compiler_params=_cparams("parallel", "parallel"),
        name="adaln",
    )(cond, mod_w, mod_b.reshape(depth, 1, d6))


def _pick_stream(x_ref, ctx_ref, tiles_per_batch):
    is_ctx = pl.program_id(0) % tiles_per_batch == 0
    return jnp.where(is_ctx, ctx_ref[0], x_ref[0])


def _even_in_kernel(x_ref, ctx_ref, sh_ref, sc_ref, g_ref, w_ref, qg_ref, kg_ref, c_ref, sa_ref,
                    sb_ref, uv_ref, q_ref, k_ref, v_ref, *, tiles_per_batch):
    h = _pick_stream(x_ref, ctx_ref, tiles_per_batch)
    n = _norm_mod(h, g_ref[...], sh_ref[0], sc_ref[0])
    p = jnp.dot(n.astype(BF16), w_ref[...], preferred_element_type=F32)
    uv_ref[...] = _gelu(p[:, :2 * A_WIDTH])
    cos, sa, sb = c_ref[...], sa_ref[...], sb_ref[...]
    lane = lax.broadcasted_iota(I32, cos.shape, 1)
    low = lane < B_HEAD_DIM

    def head_pair(x, gain, scale):
        sq = x * x
        s_lo = jnp.sum(jnp.where(low, sq, 0.0), axis=-1, keepdims=True)
        s_hi = jnp.sum(jnp.where(low, 0.0, sq), axis=-1, keepdims=True)
        ms = jnp.where(low, s_lo, s_hi) * np.float32(1.0 / B_HEAD_DIM)
        y = x * lax.rsqrt(ms + EPS) * gain
        y = (y * cos + pltpu.roll(y, LANES - B_HEAD_DIM // 2, 1) * sa
             + pltpu.roll(y, B_HEAD_DIM // 2, 1) * sb)
        if scale is not None:
            y = y * scale
        return y.astype(BF16)

    q0 = 2 * A_WIDTH
    k0 = q0 + B_WIDTH
    v0 = k0 + B_WIDTH
    for j in range(B_WIDTH // LANES):
        sl = slice(j * LANES, (j + 1) * LANES)
        q_ref[:, sl] = head_pair(p[:, q0 + j * LANES:q0 + (j + 1) * LANES], qg_ref[...],
                                 np.float32(B_HEAD_DIM ** -0.5))
        k_ref[:, sl] = head_pair(p[:, k0 + j * LANES:k0 + (j + 1) * LANES], kg_ref[...], None)
    v_ref[...] = p[:, v0:v0 + B_WIDTH].astype(BF16)


def _odd_in_kernel(h_ref, sh_ref, sc_ref, g_ref, w_ref, qg_ref, kg_ref, c_ref, s_ref,
                   q_ref, k_ref, v_ref):
    n = _norm_mod(h_ref[...], g_ref[...], sh_ref[0], sc_ref[0])
    p = jnp.dot(n.astype(BF16), w_ref[...], preferred_element_type=F32)
    cos, sin = c_ref[...], s_ref[...]

    def head(x, gain):
        y = _rms(x, gain)
        return (y * cos + pltpu.roll(y, C_HEAD_DIM // 2, 1) * sin).astype(BF16)

    nq = C_HEADS * C_HEAD_DIM
    nkv = C_KV_HEADS * C_HEAD_DIM
    for j in range(C_HEADS):
        q_ref[:, j * LANES:(j + 1) * LANES] = head(p[:, j * LANES:(j + 1) * LANES], qg_ref[...])
    for j in range(C_KV_HEADS):
        k_ref[:, j * LANES:(j + 1) * LANES] = head(
            p[:, nq + j * LANES:nq + (j + 1) * LANES], kg_ref[...])
    v_ref[...] = p[:, nq + nkv:nq + 2 * nkv].astype(BF16)


def _diff_attn_kernel(lam_ref, q_ref, k_ref, v_ref, o_ref, *, ctx_len, lam_init):
    lv = lam_ref[...]
    lam = (jnp.exp(jnp.sum(lv[0:1] * lv[1:2], axis=-1, keepdims=True))
           - jnp.exp(jnp.sum(lv[2:3] * lv[3:4], axis=-1, keepdims=True)) + np.float32(lam_init))
    q = q_ref[...]
    rows = q.shape[0]
    low = lax.broadcasted_iota(I32, q.shape, 1) < B_HEAD_DIM
    zero = jnp.zeros_like(q)
    q2 = jnp.concatenate([jnp.where(low, q, zero), jnp.where(low, zero, q)], axis=0)

    def attend(n_keys):
        k = k_ref[0:n_keys, :]
        v = v_ref[0:n_keys, :]
        s = lax.dot_general(q2, k, NT_DIMS, preferred_element_type=F32)
        p = jnp.exp(s - jnp.max(s, axis=-1, keepdims=True))
        r = 1.0 / jnp.sum(p, axis=-1, keepdims=True)
        a = p[:rows] * r[:rows] - p[rows:] * (lam * r[rows:])
        o_ref[...] = jnp.dot(a.astype(BF16), v, preferred_element_type=F32)

    is_ctx = pl.program_id(2) == 0

    @pl.when(is_ctx)
    def _():
        attend(ctx_len)

    @pl.when(jnp.logical_not(is_ctx))
    def _():
        attend(k_ref.shape[0])


def _gqa_kernel(q_ref, k_ref, v_ref, o_ref):
    q = q_ref[...]
    rows = q.shape[0]
    grp = q.shape[1] // LANES
    qs = jnp.concatenate([q[:, g * LANES:(g + 1) * LANES] for g in range(grp)], axis=0)
    s = lax.dot_general(qs, k_ref[...], NT_DIMS, preferred_element_type=F32)
    c = np.float32(C_HEAD_DIM ** -0.5 * math.log2(math.e))
    p = jnp.exp2((s - jnp.max(s, axis=-1, keepdims=True)) * c)
    l = jnp.sum(p, axis=-1, keepdims=True)
    o = jnp.dot(p.astype(BF16), v_ref[...], preferred_element_type=F32) / l
    for g in range(grp):
        o_ref[:, g * LANES:(g + 1) * LANES] = o[g * rows:(g + 1) * rows].astype(o_ref.dtype)


def _even_out_kernel(o_ref, uv_ref, x_ref, ctx_ref, gate_ref, sh_ref, sc_ref, sub_ref, lng_ref,
                     lnb_ref, ws_ref, bs_ref, w_ref, g2_ref, h1_ref, nx_ref, *, lam_init,
                     tiles_per_batch):
    o = o_ref[...]
    uv = uv_ref[...]
    u = uv[:, :A_WIDTH]
    v = uv[:, A_WIDTH:]
    mu = jnp.mean(v, axis=-1, keepdims=True)
    var = jnp.mean(jnp.square(v - mu), axis=-1, keepdims=True)
    vn = ((v - mu) * lax.rsqrt(var + EPS) * lng_ref[...] + lnb_ref[...]).astype(BF16)
    rows = o.shape[0]
    parts = []
    for c in range(rows // GMLP_CHUNK):
        rs = slice(c * GMLP_CHUNK, (c + 1) * GMLP_CHUNK)
        for g in range(A_GROUPS):
            cs = slice(g * A_GROUP_DIM, (g + 1) * A_GROUP_DIM)
            mixed = jnp.dot(ws_ref[g], vn[rs, cs], preferred_element_type=F32) + bs_ref[:, cs]
            parts.append((c, g, u[rs, cs] * mixed))
    a_rows = [jnp.concatenate([p for (c2, _, p) in parts if c2 == c], axis=-1)
              for c in range(rows // GMLP_CHUNK)]
    a = jnp.concatenate(a_rows, axis=0)
    heads = []
    for hh in range(B_HEADS):
        oh = o[:, hh * LANES:(hh + 1) * LANES]
        heads.append(_rms(oh, sub_ref[...]) * np.float32(1.0 - lam_init))
    cat = jnp.concatenate([a] + heads, axis=-1).astype(BF16)
    y = jnp.dot(cat, w_ref[...], preferred_element_type=F32)
    h1 = _pick_stream(x_ref, ctx_ref, tiles_per_batch) + gate_ref[0] * y
    h1_ref[...] = h1
    _to_token_tiles(nx_ref, _norm_mod(h1, g2_ref[...], sh_ref[0], sc_ref[0]))


def _odd_out_kernel(o_ref, h_ref, gate_ref, sh_ref, sc_ref, w_ref, g2_ref, h1_ref, nx_ref):
    y = jnp.dot(o_ref[...], w_ref[...], preferred_element_type=F32)
    h1 = h_ref[...] + gate_ref[0] * y
    h1_ref[...] = h1
    _to_token_tiles(nx_ref, _norm_mod(h1, g2_ref[...], sh_ref[0], sc_ref[0]))


def _rows_to_block(rows, dtype):
    n = rows[0].shape[1]
    rio = lax.broadcasted_iota(I32, (SUBLANES, n), 0)
    out = jnp.zeros((SUBLANES, n), dtype)
    for r, row in enumerate(rows):
        out = jnp.where(rio == r, jnp.broadcast_to(row.astype(dtype), (SUBLANES, n)), out)
    return out


def _router_kernel(x_ref, wr_ref, b_ref, eidx_ref, rank_ref, wt_ref, cnt_ref, run_ref):
    @pl.when(pl.program_id(0) == 0)
    def _():
        run_ref[...] = jnp.zeros_like(run_ref)

    per = N_EXPERTS // N_GROUPS
    d = wr_ref.shape[1]
    x = _from_token_tiles(x_ref, x_ref.shape[0] * LANES // d, d)
    logits = lax.dot_general(wr_ref[...], x.astype(BF16), NT_DIMS,
                             preferred_element_type=F32)
    scores = jax.nn.sigmoid(logits)
    sel = scores + b_ref[...]
    tm = sel.shape[1]
    neg = np.float32(-np.inf)
    jio = lax.broadcasted_iota(I32, (per, tm), 0).astype(F32)
    gio = lax.broadcasted_iota(I32, (N_GROUPS, tm), 0).astype(F32)

    def rmax(x):
        return jnp.max(x, axis=0, keepdims=True)

    def rmin(x):
        return jnp.min(x, axis=0, keepdims=True)

    sel_g = [sel[g * per:(g + 1) * per, :] for g in range(N_GROUPS)]
    sc_g = [scores[g * per:(g + 1) * per, :] for g in range(N_GROUPS)]
    gs = jnp.zeros((N_GROUPS, tm), F32)
    for g in range(N_GROUPS):
        m1 = rmax(sel_g[g])
        i1 = rmin(jnp.where(sel_g[g] == m1, jio, np.float32(per)))
        m2 = rmax(jnp.where(jio == i1, neg, sel_g[g]))
        gs = jnp.where(gio == np.float32(g), jnp.broadcast_to(m1 + m2, gs.shape), gs)
    gsel = jnp.zeros((N_GROUPS, tm), I32)
    for _ in range(TOPK_GROUPS):
        m = rmax(gs)
        idx = rmin(jnp.where(gs == m, gio, np.float32(N_GROUPS)))
        hit = gio == idx
        gsel = jnp.where(hit, 1, gsel)
        gs = jnp.where(hit, neg, gs)
    masked = [jnp.where(jnp.broadcast_to(gsel[g:g + 1, :], (per, tm)) == 1, sel_g[g], neg)
              for g in range(N_GROUPS)]
    eio = [jio + np.float32(g * per) for g in range(N_GROUPS)]
    e_rows, w_rows, hits = [], [], []
    for _ in range(TOP_K):
        m = masked[0]
        for g in range(1, N_GROUPS):
            m = jnp.maximum(m, masked[g])
        m = rmax(m)
        cand = jnp.where(masked[0] == m, eio[0], np.float32(N_EXPERTS))
        for g in range(1, N_GROUPS):
            cand = jnp.minimum(cand, jnp.where(masked[g] == m, eio[g], np.float32(N_EXPERTS)))
        idx = rmin(cand)
        hit = [eio[g] == idx for g in range(N_GROUPS)]
        wsel = jnp.where(hit[0], sc_g[0], 0.0)
        for g in range(1, N_GROUPS):
            wsel = wsel + jnp.where(hit[g], sc_g[g], 0.0)
        masked = [jnp.where(hit[g], neg, masked[g]) for g in range(N_GROUPS)]
        e_rows.append(idx)
        w_rows.append(jnp.sum(wsel, axis=0, keepdims=True))
        hits.append(hit)
    wsum = w_rows[0]
    for r in w_rows[1:]:
        wsum = wsum + r
    w_rows = [r / wsum * np.float32(ROUTE_SCALE) for r in w_rows]
    onehot = []
    for g in range(N_GROUPS):
        any_hit = hits[0][g]
        for kk in range(1, TOP_K):
            any_hit = jnp.logical_or(any_hit, hits[kk][g])
        onehot.append(jnp.where(any_hit, 1.0, 0.0))
    mt = jnp.concatenate(onehot, axis=0)
    before = (lax.broadcasted_iota(I32, (tm, tm), 0) < lax.broadcasted_iota(I32, (tm, tm), 1))
    prefix = jnp.dot(mt.astype(BF16), jnp.where(before, 1.0, 0.0).astype(BF16),
                     preferred_element_type=F32)
    pos = prefix + run_ref[...]
    r_rows = []
    for kk in range(TOP_K):
        acc = jnp.where(hits[kk][0], pos[0:per, :], 0.0)
        for g in range(1, N_GROUPS):
            acc = acc + jnp.where(hits[kk][g], pos[g * per:(g + 1) * per, :], 0.0)
        r_rows.append(jnp.sum(acc, axis=0, keepdims=True))
    run = run_ref[...] + jnp.sum(mt, axis=1, keepdims=True)
    run_ref[...] = run
    eidx_ref[...] = _rows_to_block(e_rows, I32)
    rank_ref[...] = _rows_to_block(r_rows, I32)
    wt_ref[...] = _rows_to_block(w_rows, F32)
    cnt_ref[...] = jnp.broadcast_to(run, cnt_ref.shape)


def _dest_kernel(start_ref, eidx_ref, rank_ref, dest_ref):
    per = N_EXPERTS // N_GROUPS
    eidx = eidx_ref[...]
    tm = eidx.shape[1]
    jio = lax.broadcasted_iota(I32, (per, tm), 0)
    rows = []
    for kk in range(TOP_K):
        e = jnp.broadcast_to(eidx[kk:kk + 1, :], (per, tm))
        acc = jnp.zeros((per, tm), F32)
        for g in range(N_GROUPS):
            st = jnp.broadcast_to(start_ref[g * per:(g + 1) * per, :], (per, tm))
            acc = acc + jnp.where(jio + g * per == e, st, 0.0)
        rows.append(jnp.sum(acc, axis=0, keepdims=True))
    dest_ref[...] = _rows_to_block(rows, I32) + rank_ref[...]


def _wcol_kernel(wt_ref, wcol_ref):
    wt = wt_ref[...]
    tm = wt.shape[1]
    eye = jnp.where(lax.broadcasted_iota(I32, (tm, tm), 0) == lax.broadcasted_iota(I32, (tm, tm), 1),
                    1.0, 0.0).astype(BF16)
    acc = jnp.zeros((tm, SUBLANES), F32)
    rem = wt
    for _ in range(3):
        part = rem.astype(BF16)
        acc = acc + lax.dot_general(eye, part, NT_DIMS, preferred_element_type=F32)
        rem = rem - part.astype(F32)
    wcol_ref[...] = acc


def _row_copy(src, s_row, dst, d_row, sem):
    s0 = pl.multiple_of(s_row * SUBLANES, SUBLANES)
    d0 = pl.multiple_of(d_row * SUBLANES, SUBLANES)
    return pltpu.make_async_copy(src.at[pl.ds(s0, SUBLANES)], dst.at[pl.ds(d0, SUBLANES)], sem)


def _dispatch_kernel(dest_ref, x_ref, xs_ref, sem):
    rows = x_ref.shape[0] // SUBLANES

    def copies(t):
        return [_row_copy(x_ref, t, xs_ref, dest_ref[kk, t], sem) for kk in range(TOP_K)]

    def issue(t, carry):
        for cp in copies(t):
            cp.start()
        return carry

    def drain(t, carry):
        for cp in copies(t):
            cp.wait()
        return carry

    lax.fori_loop(0, rows, issue, 0)
    lax.fori_loop(0, rows, drain, 0)


def _expert_kernel(tile_s, exp_s, lo_s, hi_s, first_s, last_s, new_s, xs_ref, wg_ref, wu_ref,
                   wd_ref, ys_ref, wg_b, wu_b, wd_b, acc):
    v = pl.program_id(0)

    @pl.when(new_s[v] == 1)
    def _():
        wg_b[...] = wg_ref[0, 0].astype(BF16)
        wu_b[...] = wu_ref[0, 0].astype(BF16)
        wd_b[...] = wd_ref[0, 0].astype(BF16)

    @pl.when(first_s[v] == 1)
    def _():
        acc[...] = jnp.zeros_like(acc)

    lo = lo_s[v]
    hi = hi_s[v]

    @pl.when(hi > lo)
    def _():
        x = _from_token_tiles(xs_ref, acc.shape[0], acc.shape[1]).astype(BF16)
        g = jnp.dot(x, wg_b[...], preferred_element_type=F32)
        u = jnp.dot(x, wu_b[...], preferred_element_type=F32)
        y = jnp.dot((_silu(g) * u).astype(BF16), wd_b[...], preferred_element_type=F32)
        row = lax.broadcasted_iota(I32, (y.shape[0], 1), 0)
        mine = jnp.logical_and(row >= lo, row < hi)
        acc[...] = jnp.where(mine, y, acc[...])

    @pl.when(last_s[v] == 1)
    def _():
        _to_token_tiles(ys_ref, acc[...])


def _combine_kernel(dest_ref, wcol_ref, x_ref, h_ref, gate_ref, sg_ref, su_ref, sd_ref, ys_ref,
                    o_ref, buf, sem):
    rows, d = h_ref.shape

    def copies(t):
        return [_row_copy(ys_ref, dest_ref[kk, t], buf.at[kk], t, sem) for kk in range(TOP_K)]

    def issue(t, carry):
        for cp in copies(t):
            cp.start()
        return carry

    def drain(t, carry):
        for cp in copies(t):
            cp.wait()
        return carry

    lax.fori_loop(0, rows, issue, 0)
    x = _from_token_tiles(x_ref, rows, d).astype(BF16)
    g = jnp.dot(x, sg_ref[...], preferred_element_type=F32)
    u = jnp.dot(x, su_ref[...], preferred_element_type=F32)
    shared = jnp.dot((_silu(g) * u).astype(BF16), sd_ref[...], preferred_element_type=F32)
    lax.fori_loop(0, rows, drain, 0)
    wcol = wcol_ref[...]
    routed = _from_token_tiles(buf.at[0], rows, d) * wcol[:, 0:1]
    for kk in range(1, TOP_K):
        routed = routed + _from_token_tiles(buf.at[kk], rows, d) * wcol[:, kk:kk + 1]
    o_ref[...] = h_ref[...] + gate_ref[0] * (routed + shared)


def _moe(nx, h, modv, gate_row_of, layer, w_router, router_bias, w_gate, w_up, w_down, sg, su, sd):
    t, d = h.shape
    ch = d // LANES
    n_slots = t * TOP_K
    n_rt = t // ROUTER_TILE
    e = N_EXPERTS
    eidx, rank, wt, cnt = pl.pallas_call(
        _router_kernel,
        grid=(n_rt,),
        in_specs=[
            pl.BlockSpec((ROUTER_TILE * ch, LANES), lambda i: (i, 0)),
            pl.BlockSpec((e, d), lambda i: (0, 0)),
            pl.BlockSpec((e, 1), lambda i: (0, 0)),
        ],
        out_specs=[
            pl.BlockSpec((SUBLANES, ROUTER_TILE), lambda i: (0, i)),
            pl.BlockSpec((SUBLANES, ROUTER_TILE), lambda i: (0, i)),
            pl.BlockSpec((SUBLANES, ROUTER_TILE), lambda i: (0, i)),
            pl.BlockSpec((e, LANES), lambda i: (0, 0)),
        ],
        out_shape=[
            jax.ShapeDtypeStruct((SUBLANES, t), I32),
            jax.ShapeDtypeStruct((SUBLANES, t), I32),
            jax.ShapeDtypeStruct((SUBLANES, t), F32),
            jax.ShapeDtypeStruct((e, LANES), F32),
        ],
        scratch_shapes=[pltpu.VMEM((e, 1), F32)],
        compiler_params=_cparams("arbitrary"),
        name="moe_router",
    )(nx, w_router.T.astype(BF16), router_bias.reshape(e, 1))

    counts = cnt[:, 0].astype(I32)
    ends = jnp.cumsum(counts)
    starts = ends - counts
    dest = pl.pallas_call(
        _dest_kernel,
        grid=(n_rt,),
        in_specs=[
            pl.BlockSpec((e, 1), lambda i: (0, 0)),
            pl.BlockSpec((SUBLANES, ROUTER_TILE), lambda i: (0, i)),
            pl.BlockSpec((SUBLANES, ROUTER_TILE), lambda i: (0, i)),
        ],
        out_specs=pl.BlockSpec((SUBLANES, ROUTER_TILE), lambda i: (0, i)),
        out_shape=jax.ShapeDtypeStruct((SUBLANES, t), I32),
        compiler_params=_cparams("parallel"),
        name="moe_dest",
    )(starts.astype(F32).reshape(e, 1), eidx, rank)
    wcol = pl.pallas_call(
        _wcol_kernel,
        grid=(n_rt,),
        in_specs=[pl.BlockSpec((SUBLANES, ROUTER_TILE), lambda i: (0, i))],
        out_specs=pl.BlockSpec((ROUTER_TILE, SUBLANES), lambda i: (i, 0)),
        out_shape=jax.ShapeDtypeStruct((t, SUBLANES), F32),
        compiler_params=_cparams("parallel"),
        name="moe_wcol",
    )(wt)

    n_tiles = t // ROW_TILE
    xs = pl.pallas_call(
        _dispatch_kernel,
        grid=(n_tiles,),
        in_specs=[
            pl.BlockSpec((SUBLANES, ROW_TILE), lambda i: (0, i), memory_space=pltpu.SMEM),
            pl.BlockSpec((ROW_TILE * ch, LANES), lambda i: (i, 0)),
        ],
        out_specs=pl.BlockSpec(memory_space=pl.ANY),
        out_shape=jax.ShapeDtypeStruct((n_slots * ch, LANES), F32),
        scratch_shapes=[pltpu.SemaphoreType.DMA],
        compiler_params=pltpu.CompilerParams(dimension_semantics=("arbitrary",),
                                             vmem_limit_bytes=VMEM_LIMIT, has_side_effects=True),
        name="moe_dispatch",
    )(dest, nx)

    n_et = n_slots // EXPERT_TILE
    pts = jnp.sort(jnp.concatenate([jnp.arange(n_et, dtype=I32) * EXPERT_TILE, starts]))
    lo = pts
    hi = jnp.concatenate([pts[1:], jnp.full((1,), n_slots, I32)])
    tile = jnp.minimum(lo // EXPERT_TILE, n_et - 1)
    expert = jnp.minimum(jnp.sum((ends[None, :] <= lo[:, None]).astype(I32), axis=1), e - 1)
    one = jnp.ones((1,), I32)
    tile_change = (tile[1:] != tile[:-1]).astype(I32)
    first = jnp.concatenate([one, tile_change])
    last = jnp.concatenate([tile_change, one])
    newexp = jnp.concatenate([one, (expert[1:] != expert[:-1]).astype(I32)])
    lo_in = lo - tile * EXPERT_TILE
    hi_in = hi - tile * EXPERT_TILE
    n_visits = n_et + e
    d_exp = w_gate.shape[-1]
    ys = pl.pallas_call(
        _expert_kernel,
        grid_spec=pltpu.PrefetchScalarGridSpec(
            num_scalar_prefetch=7,
            grid=(n_visits,),
            in_specs=[
                pl.BlockSpec((EXPERT_TILE * ch, LANES), lambda v, ti, ex, *_: (ti[v], 0)),
                pl.BlockSpec((1, 1, d, d_exp), lambda v, ti, ex, *_: (layer, ex[v], 0, 0)),
                pl.BlockSpec((1, 1, d, d_exp), lambda v, ti, ex, *_: (layer, ex[v], 0, 0)),
                pl.BlockSpec((1, 1, d_exp, d), lambda v, ti, ex, *_: (layer, ex[v], 0, 0)),
            ],
            out_specs=pl.BlockSpec((EXPERT_TILE * ch, LANES), lambda v, ti, ex, *_: (ti[v], 0)),
            scratch_shapes=[pltpu.VMEM((d, d_exp), BF16), pltpu.VMEM((d, d_exp), BF16),
                            pltpu.VMEM((d_exp, d), BF16), pltpu.VMEM((EXPERT_TILE, d), F32)],
        ),
        out_shape=jax.ShapeDtypeStruct((n_slots * ch, LANES), F32),
        compiler_params=_cparams("arbitrary"),
        name="moe_experts",
    )(tile, expert, lo_in, hi_in, first, last, newexp, xs, w_gate, w_up, w_down)

    d_sh = sg.shape[-1]
    return pl.pallas_call(
        _combine_kernel,
        grid=(n_tiles,),
        in_specs=[
            pl.BlockSpec((SUBLANES, ROW_TILE), lambda i: (0, i), memory_space=pltpu.SMEM),
            pl.BlockSpec((ROW_TILE, SUBLANES), lambda i: (i, 0)),
            pl.BlockSpec((ROW_TILE * ch, LANES), lambda i: (i, 0)),
            pl.BlockSpec((ROW_TILE, d), lambda i: (i, 0)),
            pl.BlockSpec((1, 1, d), lambda i: (gate_row_of(i) * 6 + 5, 0, 0)),
            pl.BlockSpec((d, d_sh), lambda i: (0, 0)),
            pl.BlockSpec((d, d_sh), lambda i: (0, 0)),
            pl.BlockSpec((d_sh, d), lambda i: (0, 0)),
            pl.BlockSpec(memory_space=pl.ANY),
        ],
        out_specs=pl.BlockSpec((ROW_TILE, d), lambda i: (i, 0)),
        out_shape=jax.ShapeDtypeStruct((t, d), F32),
        scratch_shapes=[pltpu.VMEM((TOP_K, ROW_TILE * ch, LANES), F32), pltpu.SemaphoreType.DMA],
        compiler_params=_cparams("arbitrary"),
        name="moe_combine",
    )(dest, wcol, nx, h, modv, sg.astype(BF16), su.astype(BF16), sd.astype(BF16), ys)


def _rope_tables(n_lat, n_ctx, head_dim):
    rows = n_lat // GRID_W
    row = jnp.repeat(jnp.arange(rows, dtype=F32), GRID_W)
    col = jnp.tile(jnp.arange(GRID_W, dtype=F32), rows)
    n_freq = head_dim // 4
    inv = ROPE_THETA ** (-jnp.arange(n_freq, dtype=F32) / n_freq)
    ang = jnp.concatenate([row[:, None] * inv, col[:, None] * inv], axis=-1)
    cos = jnp.concatenate([jnp.ones((n_ctx, head_dim // 2), F32), jnp.cos(ang)], axis=0)
    sin = jnp.concatenate([jnp.zeros((n_ctx, head_dim // 2), F32), jnp.sin(ang)], axis=0)
    return cos, sin


def _split_halves_perm(head_dim):
    return np.concatenate([np.arange(0, head_dim, 2), np.arange(1, head_dim, 2)])


def kernel(x, c, ctx, c_ctx, mod_w, mod_b, norm1_g, norm2_g, ev_w_in, ev_w_out, a_ln_g, a_ln_b, a_ws, a_bs, b_q_norm, b_k_norm, b_lam_q1, b_lam_k1, b_lam_q2, b_lam_k2, b_subln, od_w_qkv, od_w_out, c_q_norm, c_k_norm, moe_router, moe_bias, moe_w_gate, moe_w_up, moe_w_down, sh_w_gate, sh_w_up, sh_w_down):
    bsz, n_lat, d = x.shape
    n_ctx = ctx.shape[1]
    depth = mod_w.shape[0]
    assert depth == 2 and n_ctx == ROW_TILE and n_lat % ROW_TILE == 0 and bsz + 1 <= MOD_ROWS
    assert d == SUBLANES * LANES
    n_seq = n_ctx + n_lat
    tpb = n_seq // ROW_TILE
    lpb = n_lat // ROW_TILE
    t_all = bsz * n_seq
    n_tiles = t_all // ROW_TILE
    ctx_row = bsz

    cond = jnp.zeros((MOD_ROWS, d), F32).at[:bsz].set(c).at[ctx_row].set(c_ctx)
    mod = _adaln(cond, mod_w, mod_b)
    modv = [mod[l].reshape(MOD_ROWS * 6, 1, d) for l in range(depth)]

    def row_all(i):
        return jnp.where(i % tpb == 0, ctx_row, i // tpb)

    def mspec(j, row_of):
        return pl.BlockSpec((1, 1, d), lambda i: (row_of(i) * 6 + j, 0, 0))

    def full(shape):
        return pl.BlockSpec(shape, lambda *_: (0,) * len(shape))

    x_spec = pl.BlockSpec((1, ROW_TILE, d), lambda i: (i // tpb, jnp.maximum(i % tpb - 1, 0), 0))
    ctx_spec = pl.BlockSpec((1, ROW_TILE, d), lambda i: (i // tpb, 0, 0))
    ch = d // LANES
    tok_spec = pl.BlockSpec((ROW_TILE * ch, LANES), lambda i: (i, 0))

    lam_init = 0.8 - 0.6 * math.exp(-0.3 * 0)
    p64 = _split_halves_perm(B_HEAD_DIM)
    col_perm = np.concatenate(
        [np.arange(2 * A_WIDTH)]
        + [2 * A_WIDTH + blk * B_HEAD_DIM + p64 for blk in range(2 * B_WIDTH // B_HEAD_DIM)]
        + [np.arange(2 * A_WIDTH + 2 * B_WIDTH, 2 * A_WIDTH + 3 * B_WIDTH)])
    w_in = ev_w_in[0][:, col_perm].astype(BF16)
    even_in = w_in.shape[1]
    cos_b, sin_b = _rope_tables(n_lat, n_ctx, B_HEAD_DIM)
    zeros_b = jnp.zeros_like(sin_b)
    tab_c = jnp.tile(jnp.concatenate([cos_b, cos_b], axis=-1), (1, 2))
    tab_sa = jnp.tile(jnp.concatenate([-sin_b, zeros_b], axis=-1), (1, 2))
    tab_sb = jnp.tile(jnp.concatenate([zeros_b, sin_b], axis=-1), (1, 2))
    qg = jnp.tile(b_q_norm[0][p64], 2).reshape(1, LANES)
    kg = jnp.tile(b_k_norm[0][p64], 2).reshape(1, LANES)
    tab_spec = pl.BlockSpec((ROW_TILE, LANES), lambda i: (i % tpb, 0))
    row_spec = lambda w: pl.BlockSpec((ROW_TILE, w), lambda i: (i, 0))
    uv, q, k, v = pl.pallas_call(
        functools.partial(_even_in_kernel, tiles_per_batch=tpb),
        grid=(n_tiles,),
        in_specs=[x_spec, ctx_spec, mspec(0, row_all), mspec(1, row_all), full((1, d)),
                  full((d, even_in)), full((1, LANES)), full((1, LANES)),
                  tab_spec, tab_spec, tab_spec],
        out_specs=[row_spec(2 * A_WIDTH), row_spec(B_WIDTH), row_spec(B_WIDTH), row_spec(B_WIDTH)],
        out_shape=[jax.ShapeDtypeStruct((t_all, 2 * A_WIDTH), F32),
                   jax.ShapeDtypeStruct((t_all, B_WIDTH), BF16),
                   jax.ShapeDtypeStruct((t_all, B_WIDTH), BF16),
                   jax.ShapeDtypeStruct((t_all, B_WIDTH), BF16)],
        compiler_params=_cparams("parallel"),
        name="even_in",
    )(x, ctx, modv[0], modv[0], norm1_g[0].reshape(1, d), w_in, qg, kg, tab_c, tab_sa, tab_sb)

    lamv = jnp.zeros((SUBLANES, LANES), F32)
    for r, vec in enumerate((b_lam_q1[0], b_lam_k1[0], b_lam_q2[0], b_lam_k2[0])):
        lamv = lamv.at[r, :B_HEAD_DIM].set(vec)
    o = pl.pallas_call(
        functools.partial(_diff_attn_kernel, ctx_len=n_ctx, lam_init=lam_init),
        grid=(bsz, B_HEADS, tpb),
        in_specs=[
            pl.BlockSpec((SUBLANES, LANES), lambda b, hh, qi: (0, 0)),
            pl.BlockSpec((ROW_TILE, LANES), lambda b, hh, qi: (b * tpb + qi, hh)),
            pl.BlockSpec((n_seq, LANES), lambda b, hh, qi: (b, hh)),
            pl.BlockSpec((n_seq, LANES), lambda b, hh, qi: (b, hh)),
        ],
        out_specs=pl.BlockSpec((ROW_TILE, LANES), lambda b, hh, qi: (b * tpb + qi, hh)),
        out_shape=jax.ShapeDtypeStruct((t_all, B_WIDTH), F32),
        compiler_params=_cparams("parallel", "parallel", "arbitrary"),
        name="diff_attn",
    )(lamv, q, k, v)

    bs_col = jnp.repeat(a_bs[0].T, A_GROUP_DIM, axis=1)
    sub_g = b_subln[0].reshape(1, LANES)
    h1, nx = pl.pallas_call(
        functools.partial(_even_out_kernel, lam_init=lam_init, tiles_per_batch=tpb),
        grid=(n_tiles,),
        in_specs=[row_spec(B_WIDTH), row_spec(2 * A_WIDTH), x_spec, ctx_spec,
                  mspec(2, row_all), mspec(3, row_all), mspec(4, row_all),
                  full((1, LANES)), full((1, A_WIDTH)), full((1, A_WIDTH)),
                  full((A_GROUPS, GMLP_CHUNK, GMLP_CHUNK)), full((GMLP_CHUNK, A_WIDTH)),
                  full((A_WIDTH + B_WIDTH, d)), full((1, d))],
        out_specs=[row_spec(d), tok_spec],
        out_shape=[jax.ShapeDtypeStruct((t_all, d), F32),
                   jax.ShapeDtypeStruct((t_all * ch, LANES), F32)],
        compiler_params=_cparams("parallel"),
        name="even_out",
    )(o, uv, x, ctx, modv[0], modv[0], modv[0], sub_g, a_ln_g[0].reshape(1, A_WIDTH),
      a_ln_b[0].reshape(1, A_WIDTH), a_ws[0].astype(BF16), bs_col,
      ev_w_out[0].astype(BF16), norm2_g[0].reshape(1, d))

    h2 = _moe(nx, h1, modv[0], row_all, 0, moe_router[0], moe_bias[0], moe_w_gate, moe_w_up,
              moe_w_down, sh_w_gate[0], sh_w_up[0], sh_w_down[0])

    p128 = _split_halves_perm(C_HEAD_DIM)
    n_qkv_heads = C_HEADS + 2 * C_KV_HEADS
    col_perm = np.concatenate(
        [blk * C_HEAD_DIM + p128 for blk in range(C_HEADS + C_KV_HEADS)]
        + [np.arange((C_HEADS + C_KV_HEADS) * C_HEAD_DIM, n_qkv_heads * C_HEAD_DIM)])
    w_qkv = od_w_qkv[0][:, col_perm].astype(BF16)
    cos_c, sin_c = _rope_tables(n_lat, n_ctx, C_HEAD_DIM)
    tab_c1 = jnp.concatenate([cos_c, cos_c], axis=-1)
    tab_s1 = jnp.concatenate([-sin_c, sin_c], axis=-1)
    qg1 = c_q_norm[0][p128].reshape(1, LANES)
    kg1 = c_k_norm[0][p128].reshape(1, LANES)
    nq = C_HEADS * C_HEAD_DIM
    nkv = C_KV_HEADS * C_HEAD_DIM
    q1, k1, v1 = pl.pallas_call(
        _odd_in_kernel,
        grid=(n_tiles,),
        in_specs=[row_spec(d), mspec(0, row_all), mspec(1, row_all), full((1, d)),
                  full((d, nq + 2 * nkv)), full((1, LANES)), full((1, LANES)), tab_spec, tab_spec],
        out_specs=[row_spec(nq), row_spec(nkv), row_spec(nkv)],
        out_shape=[jax.ShapeDtypeStruct((t_all, nq), BF16),
                   jax.ShapeDtypeStruct((t_all, nkv), BF16),
                   jax.ShapeDtypeStruct((t_all, nkv), BF16)],
        compiler_params=_cparams("parallel"),
        name="odd_in",
    )(h2, modv[1], modv[1], norm1_g[1].reshape(1, d), w_qkv, qg1, kg1, tab_c1, tab_s1)

    t_lat = bsz * n_lat
    grp = C_HEADS // C_KV_HEADS
    o1 = pl.pallas_call(
        _gqa_kernel,
        grid=(bsz, C_KV_HEADS, lpb),
        in_specs=[
            pl.BlockSpec((ROW_TILE, grp * LANES), lambda b, n, qi: (b * tpb + 1 + qi, n)),
            pl.BlockSpec((n_seq, LANES), lambda b, n, qi: (b, n)),
            pl.BlockSpec((n_seq, LANES), lambda b, n, qi: (b, n)),
        ],
        out_specs=pl.BlockSpec((ROW_TILE, grp * LANES), lambda b, n, qi: (b * lpb + qi, n)),
        out_shape=jax.ShapeDtypeStruct((t_lat, nq), BF16),
        compiler_params=_cparams("parallel", "parallel", "arbitrary"),
        name="gqa_attn",
    )(q1, k1, v1)

    def row_lat(i):
        return i // lpb

    lat_tiles = t_lat // ROW_TILE
    hx, nx1 = pl.pallas_call(
        _odd_out_kernel,
        grid=(lat_tiles,),
        in_specs=[row_spec(nq),
                  pl.BlockSpec((ROW_TILE, d), lambda i: ((i // lpb) * tpb + 1 + i % lpb, 0)),
                  mspec(2, row_lat), mspec(3, row_lat), mspec(4, row_lat),
                  full((nq, d)), full((1, d))],
        out_specs=[row_spec(d), tok_spec],
        out_shape=[jax.ShapeDtypeStruct((t_lat, d), F32),
                   jax.ShapeDtypeStruct((t_lat * ch, LANES), F32)],
        compiler_params=_cparams("parallel"),
        name="odd_out",
    )(o1, h2, modv[1], modv[1], modv[1], od_w_out[0].astype(BF16), norm2_g[1].reshape(1, d))

    out = _moe(nx1, hx, modv[1], row_lat, 1, moe_router[1], moe_bias[1], moe_w_gate, moe_w_up,
               moe_w_down, sh_w_gate[1], sh_w_up[1], sh_w_down[1])
    return out.reshape(bsz, n_lat, d)
```

```python
import functools
import math

import numpy as np
import jax
import jax.numpy as jnp
from jax import lax
from jax.experimental import pallas as pl
from jax.experimental.pallas import tpu as pltpu

F32 = jnp.float32
BF16 = jnp.bfloat16
I32 = jnp.int32

GRID_W = 64
EPS = 1e-6
ROPE_THETA = 10000.0
A_GROUPS = 4
A_GROUP_DIM = 128
A_WIDTH = A_GROUPS * A_GROUP_DIM
GMLP_CHUNK = 128
B_HEADS = 4
B_HEAD_DIM = 64
B_WIDTH = B_HEADS * 2 * B_HEAD_DIM
C_HEADS = 8
C_KV_HEADS = 2
C_HEAD_DIM = 128
N_EXPERTS = 64
TOP_K = 6
N_GROUPS = 8
TOPK_GROUPS = 4
ROUTE_SCALE = 2.5

LANES = 128
SUBLANES = 8
ROW_TILE = 256
ROUTER_TILE = 512
EXPERT_TILE = 512
ATT_CHUNK_ROWS = 256
MOD_ROWS = 24
VMEM_LIMIT = 56 * 1024 * 1024

NT_DIMS = (((1,), (1,)), ((), ()))


def _cparams(*sem):
    return pltpu.CompilerParams(dimension_semantics=sem, vmem_limit_bytes=VMEM_LIMIT)


def _rms(x, g):
    return x * lax.rsqrt(jnp.mean(x * x, axis=-1, keepdims=True) + EPS) * g


def _norm_mod(h, g, shift, scale):
    return _rms(h, g) * (1.0 + scale) + shift


def _gelu(x):
    return 0.5 * x * (1.0 + lax.erf(x * np.float32(math.sqrt(0.5))))


def _silu(x):
    return x * jax.nn.sigmoid(x)


def _bdot(a, b):
    return jnp.dot(a.astype(BF16), b.astype(BF16), preferred_element_type=F32)


def _from_token_tiles(ref, rows, d):
    ch = d // LANES
    groups = []
    for g in range(rows // SUBLANES):
        groups.append(jnp.concatenate(
            [ref[pl.ds(g * SUBLANES * ch + j, SUBLANES, stride=ch), :] for j in range(ch)], axis=-1))
    return jnp.concatenate(groups, axis=0)


def _to_token_tiles(ref, val):
    rows, d = val.shape
    ch = d // LANES
    for g in range(rows // SUBLANES):
        for j in range(ch):
            ref[pl.ds(g * SUBLANES * ch + j, SUBLANES, stride=ch), :] = (
                val[g * SUBLANES:(g + 1) * SUBLANES, j * LANES:(j + 1) * LANES])


def _adaln_kernel(c_ref, w_ref, b_ref, o_ref):
    o_ref[0] = _bdot(_silu(c_ref[...]), w_ref[0]) + b_ref[0]


def _adaln(cond, mod_w, mod_b):
    depth, d, d6 = mod_w.shape
    tn = d6 // 4
    return pl.pallas_call(
        _adaln_kernel,
        grid=(depth, d6 // tn),
        in_specs=[
            pl.BlockSpec((MOD_ROWS, d), lambda l, j: (0, 0)),
            pl.BlockSpec((1, d, tn), lambda l, j: (l, 0, j)),
            pl.BlockSpec((1, 1, tn), lambda l, j: (l, 0, j)),
        ],
        out_specs=pl.BlockSpec((1, MOD_ROWS, tn), lambda l, j: (l, 0, j)),
        out_shape=jax.ShapeDtypeStruct((depth, MOD_ROWS, d6), F32),
        compiler_params=_cparams("parallel", "parallel"),
        name="adaln",
    )(cond, mod_w, mod_b.reshape(depth, 1, d6))


def _pick_stream(x_ref, ctx_ref, tiles_per_batch):
    is_ctx = pl.program_id(0) % tiles_per_batch == 0
    return jnp.where(is_ctx, ctx_ref[0], x_ref[0])


def _even_in_kernel(x_ref, ctx_ref, sh_ref, sc_ref, g_ref, w_ref, qg_ref, kg_ref, c_ref, sa_ref,
                    sb_ref, uv_ref, q_ref, k_ref, v_ref, *, tiles_per_batch):
    h = _pick_stream(x_ref, ctx_ref, tiles_per_batch)
    n = _norm_mod(h, g_ref[...], sh_ref[0], sc_ref[0])
    p = jnp.dot(n.astype(BF16), w_ref[...], preferred_element_type=F32)
    uv_ref[...] = _gelu(p[:, :2 * A_WIDTH])
    cos, sa, sb = c_ref[...], sa_ref[...], sb_ref[...]
    lane = lax.broadcasted_iota(I32, cos.shape, 1)
    low = lane < B_HEAD_DIM

    def head_pair(x, gain, scale):
        sq = x * x
        s_lo = jnp.sum(jnp.where(low, sq, 0.0), axis=-1, keepdims=True)
        s_hi = jnp.sum(jnp.where(low, 0.0, sq), axis=-1, keepdims=True)
        ms = jnp.where(low, s_lo, s_hi) * np.float32(1.0 / B_HEAD_DIM)
        y = x * lax.rsqrt(ms + EPS) * gain
        y = (y * cos + pltpu.roll(y, LANES - B_HEAD_DIM // 2, 1) * sa
             + pltpu.roll(y, B_HEAD_DIM // 2, 1) * sb)
        if scale is not None:
            y = y * scale
        return y.astype(BF16)

    q0 = 2 * A_WIDTH
    k0 = q0 + B_WIDTH
    v0 = k0 + B_WIDTH
    for j in range(B_WIDTH // LANES):
        sl = slice(j * LANES, (j + 1) * LANES)
        q_ref[:, sl] = head_pair(p[:, q0 + j * LANES:q0 + (j + 1) * LANES], qg_ref[...],
                                 np.float32(B_HEAD_DIM ** -0.5))
        k_ref[:, sl] = head_pair(p[:, k0 + j * LANES:k0 + (j + 1) * LANES], kg_ref[...], None)
    v_ref[...] = p[:, v0:v0 + B_WIDTH].astype(BF16)


def _odd_in_kernel(h_ref, sh_ref, sc_ref, g_ref, w_ref, qg_ref, kg_ref, c_ref, s_ref,
                   q_ref, k_ref, v_ref):
    n = _norm_mod(h_ref[...], g_ref[...], sh_ref[0], sc_ref[0])
    p = jnp.dot(n.astype(BF16), w_ref[...], preferred_element_type=F32)
    cos, sin = c_ref[...], s_ref[...]

    def head(x, gain):
        y = _rms(x, gain)
        return (y * cos + pltpu.roll(y, C_HEAD_DIM // 2, 1) * sin).astype(BF16)

    nq = C_HEADS * C_HEAD_DIM
    nkv = C_KV_HEADS * C_HEAD_DIM
    for j in range(C_HEADS):
        q_ref[:, j * LANES:(j + 1) * LANES] = head(p[:, j * LANES:(j + 1) * LANES], qg_ref[...])
    for j in range(C_KV_HEADS):
        k_ref[:, j * LANES:(j + 1) * LANES] = head(
            p[:, nq + j * LANES:nq + (j + 1) * LANES], kg_ref[...])
    v_ref[...] = p[:, nq + nkv:nq + 2 * nkv].astype(BF16)


def _diff_attn_kernel(lam_ref, q_ref, k_ref, v_ref, o_ref, *, ctx_len, lam_init):
    lv = lam_ref[...]
    lam = (jnp.exp(jnp.sum(lv[0:1] * lv[1:2], axis=-1, keepdims=True))
           - jnp.exp(jnp.sum(lv[2:3] * lv[3:4], axis=-1, keepdims=True)) + np.float32(lam_init))
    low = lax.broadcasted_iota(I32, (q_ref.shape[0], LANES), 1) < B_HEAD_DIM

    def softmax(qm, k, scale):
        s = lax.dot_general(qm, k, NT_DIMS, preferred_element_type=F32)
        p = jnp.exp(s - jnp.max(s, axis=-1, keepdims=True))
        return p * (scale / jnp.sum(p, axis=-1, keepdims=True))

    def attend(n_keys):
        for hh in range(q_ref.shape[1] // LANES):
            cs = slice(hh * LANES, (hh + 1) * LANES)
            q = q_ref[:, cs]
            zero = jnp.zeros_like(q)
            k = k_ref[0:n_keys, cs]
            a = (softmax(jnp.where(low, q, zero), k, 1.0)
                 - softmax(jnp.where(low, zero, q), k, lam))
            o_ref[:, cs] = jnp.dot(a.astype(BF16), v_ref[0:n_keys, cs], preferred_element_type=F32)

    is_ctx = pl.program_id(1) == 0

    @pl.when(is_ctx)
    def _():
        attend(ctx_len)

    @pl.when(jnp.logical_not(is_ctx))
    def _():
        attend(k_ref.shape[0])


def _gqa_kernel(q_ref, k_ref, v_ref, o_ref):
    q = q_ref[...]
    rows = q.shape[0]
    grp = q.shape[1] // LANES
    c = np.float32(C_HEAD_DIM ** -0.5 * math.log2(math.e))
    k = k_ref[...]
    v = v_ref[...]
    for g in range(grp):
        for r0 in range(0, rows, ATT_CHUNK_ROWS):
            qs = q[r0:r0 + ATT_CHUNK_ROWS, g * LANES:(g + 1) * LANES]
            s = lax.dot_general(qs, k, NT_DIMS, preferred_element_type=F32)
            p = jnp.exp2((s - jnp.max(s, axis=-1, keepdims=True)) * c)
            l = jnp.sum(p, axis=-1, keepdims=True)
            o = jnp.dot(p.astype(BF16), v, preferred_element_type=F32) / l
            o_ref[r0:r0 + ATT_CHUNK_ROWS, g * LANES:(g + 1) * LANES] = o.astype(o_ref.dtype)


def _even_out_kernel(o_ref, uv_ref, x_ref, ctx_ref, gate_ref, sh_ref, sc_ref, sub_ref, lng_ref,
                     lnb_ref, ws_ref, bs_ref, w_ref, g2_ref, h1_ref, nx_ref, *, lam_init,
                     tiles_per_batch):
    o = o_ref[...]
    uv = uv_ref[...]
    u = uv[:, :A_WIDTH]
    v = uv[:, A_WIDTH:]
    mu = jnp.mean(v, axis=-1, keepdims=True)
    var = jnp.mean(jnp.square(v - mu), axis=-1, keepdims=True)
    vn = ((v - mu) * lax.rsqrt(var + EPS) * lng_ref[...] + lnb_ref[...]).astype(BF16)
    rows = o.shape[0]
    parts = []
    for c in range(rows // GMLP_CHUNK):
        rs = slice(c * GMLP_CHUNK, (c + 1) * GMLP_CHUNK)
        for g in range(A_GROUPS):
            cs = slice(g * A_GROUP_DIM, (g + 1) * A_GROUP_DIM)
            mixed = jnp.dot(ws_ref[g], vn[rs, cs], preferred_element_type=F32) + bs_ref[:, cs]
            parts.append((c, g, u[rs, cs] * mixed))
    a_rows = [jnp.concatenate([p for (c2, _, p) in parts if c2 == c], axis=-1)
              for c in range(rows // GMLP_CHUNK)]
    a = jnp.concatenate(a_rows, axis=0)
    heads = []
    for hh in range(B_HEADS):
        oh = o[:, hh * LANES:(hh + 1) * LANES]
        heads.append(_rms(oh, sub_ref[...]) * np.float32(1.0 - lam_init))
    cat = jnp.concatenate([a] + heads, axis=-1).astype(BF16)
    y = jnp.dot(cat, w_ref[...], preferred_element_type=F32)
    h1 = _pick_stream(x_ref, ctx_ref, tiles_per_batch) + gate_ref[0] * y
    h1_ref[...] = h1
    _to_token_tiles(nx_ref, _norm_mod(h1, g2_ref[...], sh_ref[0], sc_ref[0]))


def _odd_out_kernel(o_ref, h_ref, gate_ref, sh_ref, sc_ref, w_ref, g2_ref, h1_ref, nx_ref):
    y = jnp.dot(o_ref[...], w_ref[...], preferred_element_type=F32)
    h1 = h_ref[...] + gate_ref[0] * y
    h1_ref[...] = h1
    _to_token_tiles(nx_ref, _norm_mod(h1, g2_ref[...], sh_ref[0], sc_ref[0]))


def _rows_to_block(rows, dtype):
    n = rows[0].shape[1]
    rio = lax.broadcasted_iota(I32, (SUBLANES, n), 0)
    out = jnp.zeros((SUBLANES, n), dtype)
    for r, row in enumerate(rows):
        out = jnp.where(rio == r, jnp.broadcast_to(row.astype(dtype), (SUBLANES, n)), out)
    return out


def _router_kernel(x_ref, wr_ref, b_ref, eidx_ref, rank_ref, wt_ref, cnt_ref, run_ref):
    @pl.when(pl.program_id(0) == 0)
    def _():
        run_ref[...] = jnp.zeros_like(run_ref)

    per = N_EXPERTS // N_GROUPS
    d = wr_ref.shape[1]
    x = _from_token_tiles(x_ref, x_ref.shape[0] * LANES // d, d)
    logits = lax.dot_general(wr_ref[...], x.astype(BF16), NT_DIMS,
                             preferred_element_type=F32)
    scores = jax.nn.sigmoid(logits)
    sel = scores + b_ref[...]
    tm = sel.shape[1]
    neg = np.float32(-np.inf)
    jio = lax.broadcasted_iota(I32, (per, tm), 0).astype(F32)
    gio = lax.broadcasted_iota(I32, (N_GROUPS, tm), 0).astype(F32)

    def rmax(x):
        return jnp.max(x, axis=0, keepdims=True)

    def rmin(x):
        return jnp.min(x, axis=0, keepdims=True)

    sel_g = [sel[g * per:(g + 1) * per, :] for g in range(N_GROUPS)]
    sc_g = [scores[g * per:(g + 1) * per, :] for g in range(N_GROUPS)]
    gs = jnp.zeros((N_GROUPS, tm), F32)
    for g in range(N_GROUPS):
        m1 = rmax(sel_g[g])
        i1 = rmin(jnp.where(sel_g[g] == m1, jio, np.float32(per)))
        m2 = rmax(jnp.where(jio == i1, neg, sel_g[g]))
        gs = jnp.where(gio == np.float32(g), jnp.broadcast_to(m1 + m2, gs.shape), gs)
    gsel = jnp.zeros((N_GROUPS, tm), I32)
    for _ in range(TOPK_GROUPS):
        m = rmax(gs)
        idx = rmin(jnp.where(gs == m, gio, np.float32(N_GROUPS)))
        hit = gio == idx
        gsel = jnp.where(hit, 1, gsel)
        gs = jnp.where(hit, neg, gs)
    masked = [jnp.where(jnp.broadcast_to(gsel[g:g + 1, :], (per, tm)) == 1, sel_g[g], neg)
              for g in range(N_GROUPS)]
    eio = [jio + np.float32(g * per) for g in range(N_GROUPS)]
    e_rows, w_rows, hits = [], [], []
    for _ in range(TOP_K):
        m = masked[0]
        for g in range(1, N_GROUPS):
            m = jnp.maximum(m, masked[g])
        m = rmax(m)
        cand = jnp.where(masked[0] == m, eio[0], np.float32(N_EXPERTS))
        for g in range(1, N_GROUPS):
            cand = jnp.minimum(cand, jnp.where(masked[g] == m, eio[g], np.float32(N_EXPERTS)))
        idx = rmin(cand)
        hit = [eio[g] == idx for g in range(N_GROUPS)]
        wsel = jnp.where(hit[0], sc_g[0], 0.0)
        for g in range(1, N_GROUPS):
            wsel = wsel + jnp.where(hit[g], sc_g[g], 0.0)
        masked = [jnp.where(hit[g], neg, masked[g]) for g in range(N_GROUPS)]
        e_rows.append(idx)
        w_rows.append(jnp.sum(wsel, axis=0, keepdims=True))
        hits.append(hit)
    wsum = w_rows[0]
    for r in w_rows[1:]:
        wsum = wsum + r
    w_rows = [r / wsum * np.float32(ROUTE_SCALE) for r in w_rows]
    onehot = []
    for g in range(N_GROUPS):
        any_hit = hits[0][g]
        for kk in range(1, TOP_K):
            any_hit = jnp.logical_or(any_hit, hits[kk][g])
        onehot.append(jnp.where(any_hit, 1.0, 0.0))
    mt = jnp.concatenate(onehot, axis=0)
    before = (lax.broadcasted_iota(I32, (tm, tm), 0) < lax.broadcasted_iota(I32, (tm, tm), 1))
    prefix = jnp.dot(mt.astype(BF16), jnp.where(before, 1.0, 0.0).astype(BF16),
                     preferred_element_type=F32)
    pos = prefix + run_ref[...]
    r_rows = []
    for kk in range(TOP_K):
        acc = jnp.where(hits[kk][0], pos[0:per, :], 0.0)
        for g in range(1, N_GROUPS):
            acc = acc + jnp.where(hits[kk][g], pos[g * per:(g + 1) * per, :], 0.0)
        r_rows.append(jnp.sum(acc, axis=0, keepdims=True))
    run = run_ref[...] + jnp.sum(mt, axis=1, keepdims=True)
    run_ref[...] = run
    eidx_ref[...] = _rows_to_block(e_rows, I32)
    rank_ref[...] = _rows_to_block(r_rows, I32)
    wt_ref[...] = _rows_to_block(w_rows, F32)
    cnt_ref[...] = jnp.broadcast_to(run, cnt_ref.shape)


def _dest_kernel(start_ref, eidx_ref, rank_ref, dest_ref):
    per = N_EXPERTS // N_GROUPS
    eidx = eidx_ref[...]
    tm = eidx.shape[1]
    jio = lax.broadcasted_iota(I32, (per, tm), 0)
    rows = []
    for kk in range(TOP_K):
        e = jnp.broadcast_to(eidx[kk:kk + 1, :], (per, tm))
        acc = jnp.zeros((per, tm), F32)
        for g in range(N_GROUPS):
            st = jnp.broadcast_to(start_ref[g * per:(g + 1) * per, :], (per, tm))
            acc = acc + jnp.where(jio + g * per == e, st, 0.0)
        rows.append(jnp.sum(acc, axis=0, keepdims=True))
    dest_ref[...] = _rows_to_block(rows, I32) + rank_ref[...]


def _wcol_kernel(wt_ref, wcol_ref):
    wt = wt_ref[...]
    tm = wt.shape[1]
    eye = jnp.where(lax.broadcasted_iota(I32, (tm, tm), 0) == lax.broadcasted_iota(I32, (tm, tm), 1),
                    1.0, 0.0).astype(BF16)
    acc = jnp.zeros((tm, SUBLANES), F32)
    rem = wt
    for _ in range(3):
        part = rem.astype(BF16)
        acc = acc + lax.dot_general(eye, part, NT_DIMS, preferred_element_type=F32)
        rem = rem - part.astype(F32)
    wcol_ref[...] = acc


def _row_copy(src, s_row, dst, d_row, sem):
    s0 = pl.multiple_of(s_row * SUBLANES, SUBLANES)
    d0 = pl.multiple_of(d_row * SUBLANES, SUBLANES)
    return pltpu.make_async_copy(src.at[pl.ds(s0, SUBLANES)], dst.at[pl.ds(d0, SUBLANES)], sem)


def _dispatch_kernel(dest_ref, x_ref, xs_ref, sem):
    rows = x_ref.shape[0] // SUBLANES

    def copies(t):
        return [_row_copy(x_ref, t, xs_ref, dest_ref[kk, t], sem) for kk in range(TOP_K)]

    def issue(t, carry):
        for kk, cp in enumerate(copies(t)):
            cp.start(priority=kk % 2)
        return carry

    def drain(t, carry):
        for cp in copies(t):
            cp.wait()
        return carry

    lax.fori_loop(0, rows, issue, 0)
    lax.fori_loop(0, rows, drain, 0)


def _expert_kernel(tile_s, exp_s, lo_s, hi_s, first_s, last_s, new_s, xs_ref, wg_ref, wu_ref,
                   wd_ref, ys_ref, wg_b, wu_b, wd_b, acc):
    v = pl.program_id(0)

    @pl.when(new_s[v] == 1)
    def _():
        wg_b[...] = wg_ref[0, 0].astype(BF16)
        wu_b[...] = wu_ref[0, 0].astype(BF16)
        wd_b[...] = wd_ref[0, 0].astype(BF16)

    @pl.when(first_s[v] == 1)
    def _():
        acc[...] = jnp.zeros_like(acc)

    lo = lo_s[v]
    hi = hi_s[v]

    @pl.when(hi > lo)
    def _():
        x = _from_token_tiles(xs_ref, acc.shape[0], acc.shape[1]).astype(BF16)
        g = jnp.dot(x, wg_b[...], preferred_element_type=F32)
        u = jnp.dot(x, wu_b[...], preferred_element_type=F32)
        y = jnp.dot((_silu(g) * u).astype(BF16), wd_b[...], preferred_element_type=F32)
        row = lax.broadcasted_iota(I32, (y.shape[0], 1), 0)
        mine = jnp.logical_and(row >= lo, row < hi)
        acc[...] = jnp.where(mine, y, acc[...])

    @pl.when(last_s[v] == 1)
    def _():
        _to_token_tiles(ys_ref, acc[...])


def _combine_kernel(dest_ref, wcol_ref, x_ref, h_ref, gate_ref, sg_ref, su_ref, sd_ref, ys_ref,
                    o_ref, buf, sem):
    rows, d = h_ref.shape

    def copies(t):
        return [_row_copy(ys_ref, dest_ref[kk, t], buf.at[kk], t, sem) for kk in range(TOP_K)]

    def issue(t, carry):
        for kk, cp in enumerate(copies(t)):
            cp.start(priority=kk % 2)
        return carry

    def drain(t, carry):
        for cp in copies(t):
            cp.wait()
        return carry

    lax.fori_loop(0, rows, issue, 0)
    x = _from_token_tiles(x_ref, rows, d).astype(BF16)
    g = jnp.dot(x, sg_ref[...], preferred_element_type=F32)
    u = jnp.dot(x, su_ref[...], preferred_element_type=F32)
    shared = jnp.dot((_silu(g) * u).astype(BF16), sd_ref[...], preferred_element_type=F32)
    lax.fori_loop(0, rows, drain, 0)
    wcol = wcol_ref[...]
    routed = _from_token_tiles(buf.at[0], rows, d) * wcol[:, 0:1]
    for kk in range(1, TOP_K):
        routed = routed + _from_token_tiles(buf.at[kk], rows, d) * wcol[:, kk:kk + 1]
    o_ref[...] = h_ref[...] + gate_ref[0] * (routed + shared)


def _moe(nx, h, modv, gate_row_of, layer, w_router, router_bias, w_gate, w_up, w_down, sg, su, sd):
    t, d = h.shape
    ch = d // LANES
    n_slots = t * TOP_K
    n_rt = t // ROUTER_TILE
    e = N_EXPERTS
    eidx, rank, wt, cnt = pl.pallas_call(
        _router_kernel,
        grid=(n_rt,),
        in_specs=[
            pl.BlockSpec((ROUTER_TILE * ch, LANES), lambda i: (i, 0)),
            pl.BlockSpec((e, d), lambda i: (0, 0)),
            pl.BlockSpec((e, 1), lambda i: (0, 0)),
        ],
        out_specs=[
            pl.BlockSpec((SUBLANES, ROUTER_TILE), lambda i: (0, i)),
            pl.BlockSpec((SUBLANES, ROUTER_TILE), lambda i: (0, i)),
            pl.BlockSpec((SUBLANES, ROUTER_TILE), lambda i: (0, i)),
            pl.BlockSpec((e, LANES), lambda i: (0, 0)),
        ],
        out_shape=[
            jax.ShapeDtypeStruct((SUBLANES, t), I32),
            jax.ShapeDtypeStruct((SUBLANES, t), I32),
            jax.ShapeDtypeStruct((SUBLANES, t), F32),
            jax.ShapeDtypeStruct((e, LANES), F32),
        ],
        scratch_shapes=[pltpu.VMEM((e, 1), F32)],
        compiler_params=_cparams("arbitrary"),
        name="moe_router",
    )(nx, w_router.T.astype(BF16), router_bias.reshape(e, 1))

    counts = cnt[:, 0].astype(I32)
    ends = jnp.cumsum(counts)
    starts = ends - counts
    dest = pl.pallas_call(
        _dest_kernel,
        grid=(n_rt,),
        in_specs=[
            pl.BlockSpec((e, 1), lambda i: (0, 0)),
            pl.BlockSpec((SUBLANES, ROUTER_TILE), lambda i: (0, i)),
            pl.BlockSpec((SUBLANES, ROUTER_TILE), lambda i: (0, i)),
        ],
        out_specs=pl.BlockSpec((SUBLANES, ROUTER_TILE), lambda i: (0, i)),
        out_shape=jax.ShapeDtypeStruct((SUBLANES, t), I32),
        compiler_params=_cparams("parallel"),
        name="moe_dest",
    )(starts.astype(F32).reshape(e, 1), eidx, rank)
    wcol = pl.pallas_call(
        _wcol_kernel,
        grid=(n_rt,),
        in_specs=[pl.BlockSpec((SUBLANES, ROUTER_TILE), lambda i: (0, i))],
        out_specs=pl.BlockSpec((ROUTER_TILE, SUBLANES), lambda i: (i, 0)),
        out_shape=jax.ShapeDtypeStruct((t, SUBLANES), F32),
        compiler_params=_cparams("parallel"),
        name="moe_wcol",
    )(wt)

    n_tiles = t // ROW_TILE
    xs = pl.pallas_call(
        _dispatch_kernel,
        grid=(n_tiles,),
        in_specs=[
            pl.BlockSpec((SUBLANES, ROW_TILE), lambda i: (0, i), memory_space=pltpu.SMEM),
            pl.BlockSpec((ROW_TILE * ch, LANES), lambda i: (i, 0)),
        ],
        out_specs=pl.BlockSpec(memory_space=pl.ANY),
        out_shape=jax.ShapeDtypeStruct((n_slots * ch, LANES), F32),
        scratch_shapes=[pltpu.SemaphoreType.DMA],
        compiler_params=pltpu.CompilerParams(dimension_semantics=("arbitrary",),
                                             vmem_limit_bytes=VMEM_LIMIT, has_side_effects=True),
        name="moe_dispatch",
    )(dest, nx)

    n_et = n_slots // EXPERT_TILE
    pts = jnp.sort(jnp.concatenate([jnp.arange(n_et, dtype=I32) * EXPERT_TILE, starts]))
    lo = pts
    hi = jnp.concatenate([pts[1:], jnp.full((1,), n_slots, I32)])
    tile = jnp.minimum(lo // EXPERT_TILE, n_et - 1)
    expert = jnp.minimum(jnp.sum((ends[None, :] <= lo[:, None]).astype(I32), axis=1), e - 1)
    one = jnp.ones((1,), I32)
    tile_change = (tile[1:] != tile[:-1]).astype(I32)
    first = jnp.concatenate([one, tile_change])
    last = jnp.concatenate([tile_change, one])
    newexp = jnp.concatenate([one, (expert[1:] != expert[:-1]).astype(I32)])
    lo_in = lo - tile * EXPERT_TILE
    hi_in = hi - tile * EXPERT_TILE
    n_visits = n_et + e
    d_exp = w_gate.shape[-1]
    ys = pl.pallas_call(
        _expert_kernel,
        grid_spec=pltpu.PrefetchScalarGridSpec(
            num_scalar_prefetch=7,
            grid=(n_visits,),
            in_specs=[
                pl.BlockSpec((EXPERT_TILE * ch, LANES), lambda v, ti, ex, *_: (ti[v], 0)),
                pl.BlockSpec((1, 1, d, d_exp), lambda v, ti, ex, *_: (layer, ex[v], 0, 0)),
                pl.BlockSpec((1, 1, d, d_exp), lambda v, ti, ex, *_: (layer, ex[v], 0, 0)),
                pl.BlockSpec((1, 1, d_exp, d), lambda v, ti, ex, *_: (layer, ex[v], 0, 0)),
            ],
            out_specs=pl.BlockSpec((EXPERT_TILE * ch, LANES), lambda v, ti, ex, *_: (ti[v], 0)),
            scratch_shapes=[pltpu.VMEM((d, d_exp), BF16), pltpu.VMEM((d, d_exp), BF16),
                            pltpu.VMEM((d_exp, d), BF16), pltpu.VMEM((EXPERT_TILE, d), F32)],
        ),
        out_shape=jax.ShapeDtypeStruct((n_slots * ch, LANES), F32),
        compiler_params=_cparams("arbitrary"),
        name="moe_experts",
    )(tile, expert, lo_in, hi_in, first, last, newexp, xs, w_gate, w_up, w_down)

    d_sh = sg.shape[-1]
    return pl.pallas_call(
        _combine_kernel,
        grid=(n_tiles,),
        in_specs=[
            pl.BlockSpec((SUBLANES, ROW_TILE), lambda i: (0, i), memory_space=pltpu.SMEM),
            pl.BlockSpec((ROW_TILE, SUBLANES), lambda i: (i, 0)),
            pl.BlockSpec((ROW_TILE * ch, LANES), lambda i: (i, 0)),
            pl.BlockSpec((ROW_TILE, d), lambda i: (i, 0)),
            pl.BlockSpec((1, 1, d), lambda i: (gate_row_of(i) * 6 + 5, 0, 0)),
            pl.BlockSpec((d, d_sh), lambda i: (0, 0)),
            pl.BlockSpec((d, d_sh), lambda i: (0, 0)),
            pl.BlockSpec((d_sh, d), lambda i: (0, 0)),
            pl.BlockSpec(memory_space=pl.ANY),
        ],
        out_specs=pl.BlockSpec((ROW_TILE, d), lambda i: (i, 0)),
        out_shape=jax.ShapeDtypeStruct((t, d), F32),
        scratch_shapes=[pltpu.VMEM((TOP_K, ROW_TILE * ch, LANES), F32), pltpu.SemaphoreType.DMA],
        compiler_params=_cparams("arbitrary"),
        name="moe_combine",
    )(dest, wcol, nx, h, modv, sg.astype(BF16), su.astype(BF16), sd.astype(BF16), ys)


def _rope_tables(n_lat, n_ctx, head_dim):
    rows = n_lat // GRID_W
    row = jnp.repeat(jnp.arange(rows, dtype=F32), GRID_W)
    col = jnp.tile(jnp.arange(GRID_W, dtype=F32), rows)
    n_freq = head_dim // 4
    inv = ROPE_THETA ** (-jnp.arange(n_freq, dtype=F32) / n_freq)
    ang = jnp.concatenate([row[:, None] * inv, col[:, None] * inv], axis=-1)
    cos = jnp.concatenate([jnp.ones((n_ctx, head_dim // 2), F32), jnp.cos(ang)], axis=0)
    sin = jnp.concatenate([jnp.zeros((n_ctx, head_dim // 2), F32), jnp.sin(ang)], axis=0)
    return cos, sin


def _split_halves_perm(head_dim):
    return np.concatenate([np.arange(0, head_dim, 2), np.arange(1, head_dim, 2)])


def kernel(x, c, ctx, c_ctx, mod_w, mod_b, norm1_g, norm2_g, ev_w_in, ev_w_out, a_ln_g, a_ln_b, a_ws, a_bs, b_q_norm, b_k_norm, b_lam_q1, b_lam_k1, b_lam_q2, b_lam_k2, b_subln, od_w_qkv, od_w_out, c_q_norm, c_k_norm, moe_router, moe_bias, moe_w_gate, moe_w_up, moe_w_down, sh_w_gate, sh_w_up, sh_w_down):
    bsz, n_lat, d = x.shape
    n_ctx = ctx.shape[1]
    depth = mod_w.shape[0]
    assert depth == 2 and n_ctx == ROW_TILE and n_lat % ROW_TILE == 0 and bsz + 1 <= MOD_ROWS
    assert d == SUBLANES * LANES
    n_seq = n_ctx + n_lat
    tpb = n_seq // ROW_TILE
    lpb = n_lat // ROW_TILE
    t_all = bsz * n_seq
    n_tiles = t_all // ROW_TILE
    ctx_row = bsz

    cond = jnp.zeros((MOD_ROWS, d), F32).at[:bsz].set(c).at[ctx_row].set(c_ctx)
    mod = _adaln(cond, mod_w, mod_b)
    modv = [mod[l].reshape(MOD_ROWS * 6, 1, d) for l in range(depth)]

    def row_all(i):
        return jnp.where(i % tpb == 0, ctx_row, i // tpb)

    def mspec(j, row_of):
        return pl.BlockSpec((1, 1, d), lambda i: (row_of(i) * 6 + j, 0, 0))

    def full(shape):
        return pl.BlockSpec(shape, lambda *_: (0,) * len(shape))

    x_spec = pl.BlockSpec((1, ROW_TILE, d), lambda i: (i // tpb, jnp.maximum(i % tpb - 1, 0), 0))
    ctx_spec = pl.BlockSpec((1, ROW_TILE, d), lambda i: (i // tpb, 0, 0))
    ch = d // LANES
    tok_spec = pl.BlockSpec((ROW_TILE * ch, LANES), lambda i: (i, 0))

    lam_init = 0.8 - 0.6 * math.exp(-0.3 * 0)
    p64 = _split_halves_perm(B_HEAD_DIM)
    col_perm = np.concatenate(
        [np.arange(2 * A_WIDTH)]
        + [2 * A_WIDTH + blk * B_HEAD_DIM + p64 for blk in range(2 * B_WIDTH // B_HEAD_DIM)]
        + [np.arange(2 * A_WIDTH + 2 * B_WIDTH, 2 * A_WIDTH + 3 * B_WIDTH)])
    w_in = ev_w_in[0][:, col_perm].astype(BF16)
    even_in = w_in.shape[1]
    cos_b, sin_b = _rope_tables(n_lat, n_ctx, B_HEAD_DIM)
    zeros_b = jnp.zeros_like(sin_b)
    tab_c = jnp.tile(jnp.concatenate([cos_b, cos_b], axis=-1), (1, 2))
    tab_sa = jnp.tile(jnp.concatenate([-sin_b, zeros_b], axis=-1), (1, 2))
    tab_sb = jnp.tile(jnp.concatenate([zeros_b, sin_b], axis=-1), (1, 2))
    qg = jnp.tile(b_q_norm[0][p64], 2).reshape(1, LANES)
    kg = jnp.tile(b_k_norm[0][p64], 2).reshape(1, LANES)
    tab_spec = pl.BlockSpec((ROW_TILE, LANES), lambda i: (i % tpb, 0))
    row_spec = lambda w: pl.BlockSpec((ROW_TILE, w), lambda i: (i, 0))
    uv, q, k, v = pl.pallas_call(
        functools.partial(_even_in_kernel, tiles_per_batch=tpb),
        grid=(n_tiles,),
        in_specs=[x_spec, ctx_spec, mspec(0, row_all), mspec(1, row_all), full((1, d)),
                  full((d, even_in)), full((1, LANES)), full((1, LANES)),
                  tab_spec, tab_spec, tab_spec],
        out_specs=[row_spec(2 * A_WIDTH), row_spec(B_WIDTH), row_spec(B_WIDTH), row_spec(B_WIDTH)],
        out_shape=[jax.ShapeDtypeStruct((t_all, 2 * A_WIDTH), F32),
                   jax.ShapeDtypeStruct((t_all, B_WIDTH), BF16),
                   jax.ShapeDtypeStruct((t_all, B_WIDTH), BF16),
                   jax.ShapeDtypeStruct((t_all, B_WIDTH), BF16)],
        compiler_params=_cparams("parallel"),
        name="even_in",
    )(x, ctx, modv[0], modv[0], norm1_g[0].reshape(1, d), w_in, qg, kg, tab_c, tab_sa, tab_sb)

    lamv = jnp.zeros((SUBLANES, LANES), F32)
    for r, vec in enumerate((b_lam_q1[0], b_lam_k1[0], b_lam_q2[0], b_lam_k2[0])):
        lamv = lamv.at[r, :B_HEAD_DIM].set(vec)
    o = pl.pallas_call(
        functools.partial(_diff_attn_kernel, ctx_len=n_ctx, lam_init=lam_init),
        grid=(bsz, tpb),
        in_specs=[
            pl.BlockSpec((SUBLANES, LANES), lambda b, qi: (0, 0)),
            pl.BlockSpec((ROW_TILE, B_WIDTH), lambda b, qi: (b * tpb + qi, 0)),
            pl.BlockSpec((n_seq, B_WIDTH), lambda b, qi: (b, 0)),
            pl.BlockSpec((n_seq, B_WIDTH), lambda b, qi: (b, 0)),
        ],
        out_specs=pl.BlockSpec((ROW_TILE, B_WIDTH), lambda b, qi: (b * tpb + qi, 0)),
        out_shape=jax.ShapeDtypeStruct((t_all, B_WIDTH), F32),
        compiler_params=_cparams("parallel", "arbitrary"),
        name="diff_attn",
    )(lamv, q, k, v)

    bs_col = jnp.repeat(a_bs[0].T, A_GROUP_DIM, axis=1)
    sub_g = b_subln[0].reshape(1, LANES)
    h1, nx = pl.pallas_call(
        functools.partial(_even_out_kernel, lam_init=lam_init, tiles_per_batch=tpb),
        grid=(n_tiles,),
        in_specs=[row_spec(B_WIDTH), row_spec(2 * A_WIDTH), x_spec, ctx_spec,
                  mspec(2, row_all), mspec(3, row_all), mspec(4, row_all),
                  full((1, LANES)), full((1, A_WIDTH)), full((1, A_WIDTH)),
                  full((A_GROUPS, GMLP_CHUNK, GMLP_CHUNK)), full((GMLP_CHUNK, A_WIDTH)),
                  full((A_WIDTH + B_WIDTH, d)), full((1, d))],
        out_specs=[row_spec(d), tok_spec],
        out_shape=[jax.ShapeDtypeStruct((t_all, d), F32),
                   jax.ShapeDtypeStruct((t_all * ch, LANES), F32)],
        compiler_params=_cparams("parallel"),
        name="even_out",
    )(o, uv, x, ctx, modv[0], modv[0], modv[0], sub_g, a_ln_g[0].reshape(1, A_WIDTH),
      a_ln_b[0].reshape(1, A_WIDTH), a_ws[0].astype(BF16), bs_col,
      ev_w_out[0].astype(BF16), norm2_g[0].reshape(1, d))

    h2 = _moe(nx, h1, modv[0], row_all, 0, moe_router[0], moe_bias[0], moe_w_gate, moe_w_up,
              moe_w_down, sh_w_gate[0], sh_w_up[0], sh_w_down[0])

    p128 = _split_halves_perm(C_HEAD_DIM)
    n_qkv_heads = C_HEADS + 2 * C_KV_HEADS
    col_perm = np.concatenate(
        [blk * C_HEAD_DIM + p128 for blk in range(C_HEADS + C_KV_HEADS)]
        + [np.arange((C_HEADS + C_KV_HEADS) * C_HEAD_DIM, n_qkv_heads * C_HEAD_DIM)])
    w_qkv = od_w_qkv[0][:, col_perm].astype(BF16)
    cos_c, sin_c = _rope_tables(n_lat, n_ctx, C_HEAD_DIM)
    tab_c1 = jnp.concatenate([cos_c, cos_c], axis=-1)
    tab_s1 = jnp.concatenate([-sin_c, sin_c], axis=-1)
    qg1 = c_q_norm[0][p128].reshape(1, LANES)
    kg1 = c_k_norm[0][p128].reshape(1, LANES)
    nq = C_HEADS * C_HEAD_DIM
    nkv = C_KV_HEADS * C_HEAD_DIM
    q1, k1, v1 = pl.pallas_call(
        _odd_in_kernel,
        grid=(n_tiles,),
        in_specs=[row_spec(d), mspec(0, row_all), mspec(1, row_all), full((1, d)),
                  full((d, nq + 2 * nkv)), full((1, LANES)), full((1, LANES)), tab_spec, tab_spec],
        out_specs=[row_spec(nq), row_spec(nkv), row_spec(nkv)],
        out_shape=[jax.ShapeDtypeStruct((t_all, nq), BF16),
                   jax.ShapeDtypeStruct((t_all, nkv), BF16),
                   jax.ShapeDtypeStruct((t_all, nkv), BF16)],
        compiler_params=_cparams("parallel"),
        name="odd_in",
    )(h2, modv[1], modv[1], norm1_g[1].reshape(1, d), w_qkv, qg1, kg1, tab_c1, tab_s1)

    t_lat = bsz * n_lat
    grp = C_HEADS // C_KV_HEADS
    o1 = pl.pallas_call(
        _gqa_kernel,
        grid=(bsz, C_KV_HEADS, lpb),
        in_specs=[
            pl.BlockSpec((ROW_TILE, grp * LANES), lambda b, n, qi: (b * tpb + 1 + qi, n)),
            pl.BlockSpec((n_seq, LANES), lambda b, n, qi: (b, n)),
            pl.BlockSpec((n_seq, LANES), lambda b, n, qi: (b, n)),
        ],
        out_specs=pl.BlockSpec((ROW_TILE, grp * LANES), lambda b, n, qi: (b * lpb + qi, n)),
        out_shape=jax.ShapeDtypeStruct((t_lat, nq), BF16),
        compiler_params=_cparams("parallel", "parallel", "arbitrary"),
        name="gqa_attn",
    )(q1, k1, v1)

    def row_lat(i):
        return i // lpb

    lat_tiles = t_lat // ROW_TILE
    hx, nx1 = pl.pallas_call(
        _odd_out_kernel,
        grid=(lat_tiles,),
        in_specs=[row_spec(nq),
                  pl.BlockSpec((ROW_TILE, d), lambda i: ((i // lpb) * tpb + 1 + i % lpb, 0)),
                  mspec(2, row_lat), mspec(3, row_lat), mspec(4, row_lat),
                  full((nq, d)), full((1, d))],
        out_specs=[row_spec(d), tok_spec],
        out_shape=[jax.ShapeDtypeStruct((t_lat, d), F32),
                   jax.ShapeDtypeStruct((t_lat * ch, LANES), F32)],
        compiler_params=_cparams("parallel"),
        name="odd_out",
    )(o1, h2, modv[1], modv[1], modv[1], od_w_out[0].astype(BF16), norm2_g[1].reshape(1, d))

    out = _moe(nx1, hx, modv[1], row_lat, 1, moe_router[1], moe_bias[1], moe_w_gate, moe_w_up,
               moe_w_down, sh_w_gate[1], sh_w_up[1], sh_w_down[1])
    return out.reshape(bsz, n_lat, d)
```

```python
import functools
import math

import numpy as np
import jax
import jax.numpy as jnp
from jax import lax
from jax.experimental import pallas as pl
from jax.experimental.pallas import tpu as pltpu

F32 = jnp.float32
BF16 = jnp.bfloat16
I32 = jnp.int32

GRID_W = 64
EPS = 1e-6
ROPE_THETA = 10000.0
A_GROUPS = 4
A_GROUP_DIM = 128
A_WIDTH = A_GROUPS * A_GROUP_DIM
GMLP_CHUNK = 128
B_HEADS = 4
B_HEAD_DIM = 64
B_WIDTH = B_HEADS * 2 * B_HEAD_DIM
C_HEADS = 8
C_KV_HEADS = 2
C_HEAD_DIM = 128
N_EXPERTS = 64
TOP_K = 6
N_GROUPS = 8
TOPK_GROUPS = 4
ROUTE_SCALE = 2.5
LOG2E = math.log2(math.e)

LANES = 128
SUBLANES = 8
ROW_TILE = 256
ROUTER_TILE = 512
EXPERT_TILE = 512
ATT_CHUNK_ROWS = 256
MOD_ROWS = 24
VMEM_LIMIT = 56 * 1024 * 1024

NT_DIMS = (((1,), (1,)), ((), ()))


def _cparams(*sem):
    return pltpu.CompilerParams(dimension_semantics=sem, vmem_limit_bytes=VMEM_LIMIT)


def _rms(x, g):
    return x * lax.rsqrt(jnp.mean(x * x, axis=-1, keepdims=True) + EPS) * g


def _norm_mod(h, g, shift, scale):
    return _rms(h, g) * (1.0 + scale) + shift


def _gelu(x):
    return 0.5 * x * (1.0 + lax.erf(x * np.float32(math.sqrt(0.5))))


def _silu(x):
    return x * jax.nn.sigmoid(x)


def _bdot(a, b):
    return jnp.dot(a.astype(BF16), b.astype(BF16), preferred_element_type=F32)


def _from_token_tiles(ref, rows, d):
    ch = d // LANES
    groups = []
    for g in range(rows // SUBLANES):
        groups.append(jnp.concatenate(
            [ref[pl.ds(g * SUBLANES * ch + j, SUBLANES, stride=ch), :] for j in range(ch)], axis=-1))
    return jnp.concatenate(groups, axis=0)


def _to_token_tiles(ref, val):
    rows, d = val.shape
    ch = d // LANES
    for g in range(rows // SUBLANES):
        for j in range(ch):
            ref[pl.ds(g * SUBLANES * ch + j, SUBLANES, stride=ch), :] = (
                val[g * SUBLANES:(g + 1) * SUBLANES, j * LANES:(j + 1) * LANES])


def _adaln_kernel(c_ref, w_ref, b_ref, o_ref):
    o_ref[0] = _bdot(_silu(c_ref[...]), w_ref[0]) + b_ref[0]


def _adaln(cond, mod_w, mod_b):
    depth, d, d6 = mod_w.shape
    tn = d6 // 4
    return pl.pallas_call(
        _adaln_kernel,
        grid=(depth, d6 // tn),
        in_specs=[
            pl.BlockSpec((MOD_ROWS, d), lambda l, j: (0, 0)),
            pl.BlockSpec((1, d, tn), lambda l, j: (l, 0, j)),
            pl.BlockSpec((1, 1, tn), lambda l, j: (l, 0, j)),
        ],
        out_specs=pl.BlockSpec((1, MOD_ROWS, tn), lambda l, j: (l, 0, j)),
        out_shape=jax.ShapeDtypeStruct((depth, MOD_ROWS, d6), F32),
        compiler_params=_cparams("parallel", "parallel"),
        name="adaln",
    )(cond, mod_w, mod_b.reshape(depth, 1, d6))


def _pick_stream(x_ref, ctx_ref, tiles_per_batch):
    is_ctx = pl.program_id(0) % tiles_per_batch == 0
    return jnp.where(is_ctx, ctx_ref[0], x_ref[0])


def _even_in_kernel(x_ref, ctx_ref, sh_ref, sc_ref, g_ref, w_ref, qg_ref, kg_ref, c_ref, sa_ref,
                    sb_ref, uv_ref, q_ref, k_ref, v_ref, *, tiles_per_batch):
    h = _pick_stream(x_ref, ctx_ref, tiles_per_batch)
    n = _norm_mod(h, g_ref[...], sh_ref[0], sc_ref[0])
    p = jnp.dot(n.astype(BF16), w_ref[...], preferred_element_type=F32)
    uv_ref[...] = _gelu(p[:, :2 * A_WIDTH])
    cos, sa, sb = c_ref[...], sa_ref[...], sb_ref[...]
    lane = lax.broadcasted_iota(I32, cos.shape, 1)
    low = lane < B_HEAD_DIM

    def head_pair(x, gain, scale):
        sq = x * x
        s_lo = jnp.sum(jnp.where(low, sq, 0.0), axis=-1, keepdims=True)
        s_hi = jnp.sum(jnp.where(low, 0.0, sq), axis=-1, keepdims=True)
        ms = jnp.where(low, s_lo, s_hi) * np.float32(1.0 / B_HEAD_DIM)
        y = x * lax.rsqrt(ms + EPS) * gain
        y = (y * cos + pltpu.roll(y, LANES - B_HEAD_DIM // 2, 1) * sa
             + pltpu.roll(y, B_HEAD_DIM // 2, 1) * sb)
        if scale is not None:
            y = y * scale
        return y.astype(BF16)

    q0 = 2 * A_WIDTH
    k0 = q0 + B_WIDTH
    v0 = k0 + B_WIDTH
    for j in range(B_WIDTH // LANES):
        sl = slice(j * LANES, (j + 1) * LANES)
        q_ref[:, sl] = head_pair(p[:, q0 + j * LANES:q0 + (j + 1) * LANES], qg_ref[...],
                                 np.float32(B_HEAD_DIM ** -0.5))
        k_ref[:, sl] = head_pair(p[:, k0 + j * LANES:k0 + (j + 1) * LANES], kg_ref[...], None)
    v_ref[...] = p[:, v0:v0 + B_WIDTH].astype(BF16)


def _odd_in_kernel(h_ref, sh_ref, sc_ref, g_ref, w_ref, qg_ref, kg_ref, c_ref, s_ref,
                   q_ref, k_ref, v_ref):
    n = _norm_mod(h_ref[...], g_ref[...], sh_ref[0], sc_ref[0])
    p = jnp.dot(n.astype(BF16), w_ref[...], preferred_element_type=F32)
    cos, sin = c_ref[...], s_ref[...]

    def head(x, gain, scale):
        y = _rms(x, gain)
        y = y * cos + pltpu.roll(y, C_HEAD_DIM // 2, 1) * sin
        if scale is not None:
            y = y * scale
        return y.astype(BF16)

    nq = C_HEADS * C_HEAD_DIM
    nkv = C_KV_HEADS * C_HEAD_DIM
    for j in range(C_HEADS):
        q_ref[:, j * LANES:(j + 1) * LANES] = head(p[:, j * LANES:(j + 1) * LANES], qg_ref[...],
                                                   None)
    for j in range(C_KV_HEADS):
        k_ref[:, j * LANES:(j + 1) * LANES] = head(
            p[:, nq + j * LANES:nq + (j + 1) * LANES], kg_ref[...], None)
    v_ref[...] = p[:, nq + nkv:nq + 2 * nkv].astype(BF16)


def _diff_attn_kernel(lam_ref, q_ref, k_ref, v_ref, o_ref, *, ctx_len, lam_init):
    lv = lam_ref[...]
    lam = (jnp.exp(jnp.sum(lv[0:1] * lv[1:2], axis=-1, keepdims=True))
           - jnp.exp(jnp.sum(lv[2:3] * lv[3:4], axis=-1, keepdims=True)) + np.float32(lam_init))
    low = lax.broadcasted_iota(I32, (q_ref.shape[0], LANES), 1) < B_HEAD_DIM

    def softmax(qm, k, scale):
        s = lax.dot_general(qm, k, NT_DIMS, preferred_element_type=F32)
        p = jnp.exp(s - jnp.max(s, axis=-1, keepdims=True))
        return p * (scale / jnp.sum(p, axis=-1, keepdims=True))

    def attend(n_keys):
        for hh in range(q_ref.shape[1] // LANES):
            cs = slice(hh * LANES, (hh + 1) * LANES)
            q = q_ref[:, cs]
            zero = jnp.zeros_like(q)
            k = k_ref[0:n_keys, cs]
            a = (softmax(jnp.where(low, q, zero), k, 1.0)
                 - softmax(jnp.where(low, zero, q), k, lam))
            o_ref[:, cs] = jnp.dot(a.astype(BF16), v_ref[0:n_keys, cs], preferred_element_type=F32)

    is_ctx = pl.program_id(1) == 0

    @pl.when(is_ctx)
    def _():
        attend(ctx_len)

    @pl.when(jnp.logical_not(is_ctx))
    def _():
        attend(k_ref.shape[0])


def _gqa_kernel(q_ref, k_ref, v_ref, o_ref):
    q = q_ref[...]
    rows = q.shape[0]
    grp = q.shape[1] // LANES
    c = np.float32(C_HEAD_DIM ** -0.5 * LOG2E)
    k = k_ref[...]
    v = v_ref[...]
    for g in range(grp):
        for r0 in range(0, rows, ATT_CHUNK_ROWS):
            qs = q[r0:r0 + ATT_CHUNK_ROWS, g * LANES:(g + 1) * LANES]
            s = lax.dot_general(qs, k, NT_DIMS, preferred_element_type=F32)
            p = jnp.exp2((s - jnp.max(s, axis=-1, keepdims=True)) * c)
            l = jnp.sum(p, axis=-1, keepdims=True)
            o = jnp.dot(p.astype(BF16), v, preferred_element_type=F32) / l
            o_ref[r0:r0 + ATT_CHUNK_ROWS, g * LANES:(g + 1) * LANES] = o.astype(o_ref.dtype)


def _even_out_kernel(o_ref, uv_ref, x_ref, ctx_ref, gate_ref, sh_ref, sc_ref, sub_ref, lng_ref,
                     lnb_ref, ws_ref, bs_ref, w_ref, g2_ref, h1_ref, nx_ref, *, lam_init,
                     tiles_per_batch):
    o = o_ref[...]
    uv = uv_ref[...]
    u = uv[:, :A_WIDTH]
    v = uv[:, A_WIDTH:]
    mu = jnp.mean(v, axis=-1, keepdims=True)
    var = jnp.mean(jnp.square(v - mu), axis=-1, keepdims=True)
    vn = ((v - mu) * lax.rsqrt(var + EPS) * lng_ref[...] + lnb_ref[...]).astype(BF16)
    rows = o.shape[0]
    parts = []
    for c in range(rows // GMLP_CHUNK):
        rs = slice(c * GMLP_CHUNK, (c + 1) * GMLP_CHUNK)
        for g in range(A_GROUPS):
            cs = slice(g * A_GROUP_DIM, (g + 1) * A_GROUP_DIM)
            mixed = jnp.dot(ws_ref[g], vn[rs, cs], preferred_element_type=F32) + bs_ref[:, cs]
            parts.append((c, g, u[rs, cs] * mixed))
    a_rows = [jnp.concatenate([p for (c2, _, p) in parts if c2 == c], axis=-1)
              for c in range(rows // GMLP_CHUNK)]
    a = jnp.concatenate(a_rows, axis=0)
    heads = []
    for hh in range(B_HEADS):
        oh = o[:, hh * LANES:(hh + 1) * LANES]
        heads.append(_rms(oh, sub_ref[...]) * np.float32(1.0 - lam_init))
    cat = jnp.concatenate([a] + heads, axis=-1).astype(BF16)
    y = jnp.dot(cat, w_ref[...], preferred_element_type=F32)
    h1 = _pick_stream(x_ref, ctx_ref, tiles_per_batch) + gate_ref[0] * y
    h1_ref[...] = h1
    _to_token_tiles(nx_ref, _norm_mod(h1, g2_ref[...], sh_ref[0], sc_ref[0]))


def _odd_out_kernel(o_ref, h_ref, gate_ref, sh_ref, sc_ref, w_ref, g2_ref, h1_ref, nx_ref):
    y = jnp.dot(o_ref[...], w_ref[...], preferred_element_type=F32)
    h1 = h_ref[...] + gate_ref[0] * y
    h1_ref[...] = h1
    _to_token_tiles(nx_ref, _norm_mod(h1, g2_ref[...], sh_ref[0], sc_ref[0]))


def _rows_to_block(rows, dtype):
    n = rows[0].shape[1]
    rio = lax.broadcasted_iota(I32, (SUBLANES, n), 0)
    out = jnp.zeros((SUBLANES, n), dtype)
    for r, row in enumerate(rows):
        out = jnp.where(rio == r, jnp.broadcast_to(row.astype(dtype), (SUBLANES, n)), out)
    return out


def _transpose_block(xt):
    n = xt.shape[1]
    eye = jnp.where(lax.broadcasted_iota(I32, (n, n), 0) == lax.broadcasted_iota(I32, (n, n), 1),
                    1.0, 0.0).astype(BF16)
    acc = jnp.zeros((n, SUBLANES), F32)
    rem = xt
    for _ in range(3):
        part = rem.astype(BF16)
        acc = acc + lax.dot_general(eye, part, NT_DIMS, preferred_element_type=F32)
        rem = rem - part.astype(F32)
    return acc


def _router_kernel(x_ref, wr_ref, b_ref, eidx_ref, rank_ref, wcol_ref, cnt_ref, run_ref):
    @pl.when(pl.program_id(0) == 0)
    def _():
        run_ref[...] = jnp.zeros_like(run_ref)

    per = N_EXPERTS // N_GROUPS
    d = wr_ref.shape[1]
    x = _from_token_tiles(x_ref, x_ref.shape[0] * LANES // d, d)
    logits = lax.dot_general(wr_ref[...], x.astype(BF16), NT_DIMS,
                             preferred_element_type=F32)
    scores = jax.nn.sigmoid(logits)
    sel = scores + b_ref[...]
    tm = sel.shape[1]
    neg = np.float32(-np.inf)
    jio = lax.broadcasted_iota(I32, (per, tm), 0).astype(F32)
    gio = lax.broadcasted_iota(I32, (N_GROUPS, tm), 0).astype(F32)

    def rmax(x):
        return jnp.max(x, axis=0, keepdims=True)

    def rmin(x):
        return jnp.min(x, axis=0, keepdims=True)

    sel_g = [sel[g * per:(g + 1) * per, :] for g in range(N_GROUPS)]
    sc_g = [scores[g * per:(g + 1) * per, :] for g in range(N_GROUPS)]
    gs = jnp.zeros((N_GROUPS, tm), F32)
    for g in range(N_GROUPS):
        m1 = rmax(sel_g[g])
        i1 = rmin(jnp.where(sel_g[g] == m1, jio, np.float32(per)))
        m2 = rmax(jnp.where(jio == i1, neg, sel_g[g]))
        gs = jnp.where(gio == np.float32(g), jnp.broadcast_to(m1 + m2, gs.shape), gs)
    gsel = jnp.zeros((N_GROUPS, tm), I32)
    for _ in range(TOPK_GROUPS):
        m = rmax(gs)
        idx = rmin(jnp.where(gs == m, gio, np.float32(N_GROUPS)))
        hit = gio == idx
        gsel = jnp.where(hit, 1, gsel)
        gs = jnp.where(hit, neg, gs)
    masked = [jnp.where(jnp.broadcast_to(gsel[g:g + 1, :], (per, tm)) == 1, sel_g[g], neg)
              for g in range(N_GROUPS)]
    eio = [jio + np.float32(g * per) for g in range(N_GROUPS)]
    e_rows, w_rows, hits = [], [], []
    for _ in range(TOP_K):
        m = masked[0]
        for g in range(1, N_GROUPS):
            m = jnp.maximum(m, masked[g])
        m = rmax(m)
        cand = jnp.where(masked[0] == m, eio[0], np.float32(N_EXPERTS))
        for g in range(1, N_GROUPS):
            cand = jnp.minimum(cand, jnp.where(masked[g] == m, eio[g], np.float32(N_EXPERTS)))
        idx = rmin(cand)
        hit = [eio[g] == idx for g in range(N_GROUPS)]
        wsel = jnp.where(hit[0], sc_g[0], 0.0)
        for g in range(1, N_GROUPS):
            wsel = wsel + jnp.where(hit[g], sc_g[g], 0.0)
        masked = [jnp.where(hit[g], neg, masked[g]) for g in range(N_GROUPS)]
        e_rows.append(idx)
        w_rows.append(jnp.sum(wsel, axis=0, keepdims=True))
        hits.append(hit)
    wsum = w_rows[0]
    for r in w_rows[1:]:
        wsum = wsum + r
    w_rows = [r / wsum * np.float32(ROUTE_SCALE) for r in w_rows]
    onehot = []
    for g in range(N_GROUPS):
        any_hit = hits[0][g]
        for kk in range(1, TOP_K):
            any_hit = jnp.logical_or(any_hit, hits[kk][g])
        onehot.append(jnp.where(any_hit, 1.0, 0.0))
    mt = jnp.concatenate(onehot, axis=0)
    before = (lax.broadcasted_iota(I32, (tm, tm), 0) < lax.broadcasted_iota(I32, (tm, tm), 1))
    prefix = jnp.dot(mt.astype(BF16), jnp.where(before, 1.0, 0.0).astype(BF16),
                     preferred_element_type=F32)
    pos = prefix + run_ref[...]
    r_rows = []
    for kk in range(TOP_K):
        acc = jnp.where(hits[kk][0], pos[0:per, :], 0.0)
        for g in range(1, N_GROUPS):
            acc = acc + jnp.where(hits[kk][g], pos[g * per:(g + 1) * per, :], 0.0)
        r_rows.append(jnp.sum(acc, axis=0, keepdims=True))
    run = run_ref[...] + jnp.sum(mt, axis=1, keepdims=True)
    run_ref[...] = run
    eidx_ref[...] = _rows_to_block(e_rows, I32)
    rank_ref[...] = _rows_to_block(r_rows, I32)
    wcol_ref[...] = _transpose_block(_rows_to_block(w_rows, F32))
    cnt_ref[...] = jnp.broadcast_to(run, cnt_ref.shape)


def _dest_kernel(start_ref, eidx_ref, rank_ref, dest_ref):
    per = N_EXPERTS // N_GROUPS
    eidx = eidx_ref[...]
    tm = eidx.shape[1]
    jio = lax.broadcasted_iota(I32, (per, tm), 0)
    rows = []
    for kk in range(TOP_K):
        e = jnp.broadcast_to(eidx[kk:kk + 1, :], (per, tm))
        acc = jnp.zeros((per, tm), F32)
        for g in range(N_GROUPS):
            st = jnp.broadcast_to(start_ref[g * per:(g + 1) * per, :], (per, tm))
            acc = acc + jnp.where(jio + g * per == e, st, 0.0)
        rows.append(jnp.sum(acc, axis=0, keepdims=True))
    dest_ref[...] = _rows_to_block(rows, I32) + rank_ref[...]


def _row_copy(src, s_row, dst, d_row, sem):
    s0 = pl.multiple_of(s_row * SUBLANES, SUBLANES)
    d0 = pl.multiple_of(d_row * SUBLANES, SUBLANES)
    return pltpu.make_async_copy(src.at[pl.ds(s0, SUBLANES)], dst.at[pl.ds(d0, SUBLANES)], sem)


def _dispatch_kernel(dest_ref, x_ref, xs_ref, sem):
    rows = x_ref.shape[0] // SUBLANES

    def copies(t):
        return [_row_copy(x_ref, t, xs_ref, dest_ref[kk, t], sem) for kk in range(TOP_K)]

    def issue(t, carry):
        for kk, cp in enumerate(copies(t)):
            cp.start(priority=kk % 2)
        return carry

    def drain(t, carry):
        for cp in copies(t):
            cp.wait()
        return carry

    lax.fori_loop(0, rows, issue, 0)
    lax.fori_loop(0, rows, drain, 0)


def _expert_kernel(tile_s, exp_s, lo_s, hi_s, first_s, last_s, new_s, xs_ref, wg_ref, wu_ref,
                   wd_ref, ys_ref, wg_b, wu_b, wd_b, acc):
    v = pl.program_id(0)

    @pl.when(new_s[v] == 1)
    def _():
        wg_b[...] = wg_ref[0, 0].astype(BF16)
        wu_b[...] = wu_ref[0, 0].astype(BF16)
        wd_b[...] = wd_ref[0, 0].astype(BF16)

    @pl.when(first_s[v] == 1)
    def _():
        acc[...] = jnp.zeros_like(acc)

    lo = lo_s[v]
    hi = hi_s[v]

    @pl.when(hi > lo)
    def _():
        x = _from_token_tiles(xs_ref, acc.shape[0], acc.shape[1]).astype(BF16)
        g = jnp.dot(x, wg_b[...], preferred_element_type=F32)
        u = jnp.dot(x, wu_b[...], preferred_element_type=F32)
        y = jnp.dot((_silu(g) * u).astype(BF16), wd_b[...], preferred_element_type=F32)
        row = lax.broadcasted_iota(I32, (y.shape[0], 1), 0)
        mine = jnp.logical_and(row >= lo, row < hi)
        acc[...] = jnp.where(mine, y, acc[...])

    @pl.when(last_s[v] == 1)
    def _():
        _to_token_tiles(ys_ref, acc[...])


def _combine_kernel(dest_ref, dest_next_ref, wcol_ref, x_ref, h_ref, gate_ref, sg_ref, su_ref,
                    sd_ref, ys_ref, o_ref, buf, sem):
    i = pl.program_id(0)
    n = pl.num_programs(0)
    rows, d = h_ref.shape

    def copies(idx_ref, slot, t):
        return [_row_copy(ys_ref, idx_ref[kk, t], buf.at[slot, kk], t, sem.at[slot])
                for kk in range(TOP_K)]

    def issue(idx_ref, slot):
        def body(t, carry):
            for kk, cp in enumerate(copies(idx_ref, slot, t)):
                cp.start(priority=kk % 2)
            return carry
        lax.fori_loop(0, rows, body, 0)

    def drain(slot):
        def body(t, carry):
            for cp in copies(dest_ref, slot, t):
                cp.wait()
            return carry
        lax.fori_loop(0, rows, body, 0)

    @pl.when(i == 0)
    def _():
        issue(dest_ref, 0)

    for slot in range(2):
        @pl.when(jnp.logical_and(i + 1 < n, (i + 1) % 2 == slot))
        def _(slot=slot):
            issue(dest_next_ref, slot)

    x = _from_token_tiles(x_ref, rows, d).astype(BF16)
    g = jnp.dot(x, sg_ref[...], preferred_element_type=F32)
    u = jnp.dot(x, su_ref[...], preferred_element_type=F32)
    shared = jnp.dot((_silu(g) * u).astype(BF16), sd_ref[...], preferred_element_type=F32)
    wcol = wcol_ref[...]

    for slot in range(2):
        @pl.when(i % 2 == slot)
        def _(slot=slot):
            drain(slot)
            routed = _from_token_tiles(buf.at[slot, 0], rows, d) * wcol[:, 0:1]
            for kk in range(1, TOP_K):
                routed = routed + _from_token_tiles(buf.at[slot, kk], rows, d) * wcol[:, kk:kk + 1]
            o_ref[...] = h_ref[...] + gate_ref[0] * (routed + shared)


def _moe(nx, h, modv, gate_row_of, layer, w_router, router_bias, w_gate, w_up, w_down, sg, su, sd):
    t, d = h.shape
    ch = d // LANES
    n_slots = t * TOP_K
    n_rt = t // ROUTER_TILE
    e = N_EXPERTS
    eidx, rank, wcol, cnt = pl.pallas_call(
        _router_kernel,
        grid=(n_rt,),
        in_specs=[
            pl.BlockSpec((ROUTER_TILE * ch, LANES), lambda i: (i, 0)),
            pl.BlockSpec((e, d), lambda i: (0, 0)),
            pl.BlockSpec((e, 1), lambda i: (0, 0)),
        ],
        out_specs=[
            pl.BlockSpec((SUBLANES, ROUTER_TILE), lambda i: (0, i)),
            pl.BlockSpec((SUBLANES, ROUTER_TILE), lambda i: (0, i)),
            pl.BlockSpec((ROUTER_TILE, SUBLANES), lambda i: (i, 0)),
            pl.BlockSpec((e, LANES), lambda i: (0, 0)),
        ],
        out_shape=[
            jax.ShapeDtypeStruct((SUBLANES, t), I32),
            jax.ShapeDtypeStruct((SUBLANES, t), I32),
            jax.ShapeDtypeStruct((t, SUBLANES), F32),
            jax.ShapeDtypeStruct((e, LANES), F32),
        ],
        scratch_shapes=[pltpu.VMEM((e, 1), F32)],
        compiler_params=_cparams("arbitrary"),
        name="moe_router",
    )(nx, w_router.T.astype(BF16), router_bias.reshape(e, 1))

    counts = cnt[:, 0].astype(I32)
    ends = jnp.cumsum(counts)
    starts = ends - counts
    dest = pl.pallas_call(
        _dest_kernel,
        grid=(n_rt,),
        in_specs=[
            pl.BlockSpec((e, 1), lambda i: (0, 0)),
            pl.BlockSpec((SUBLANES, ROUTER_TILE), lambda i: (0, i)),
            pl.BlockSpec((SUBLANES, ROUTER_TILE), lambda i: (0, i)),
        ],
        out_specs=pl.BlockSpec((SUBLANES, ROUTER_TILE), lambda i: (0, i)),
        out_shape=jax.ShapeDtypeStruct((SUBLANES, t), I32),
        compiler_params=_cparams("parallel"),
        name="moe_dest",
    )(starts.astype(F32).reshape(e, 1), eidx, rank)
    n_tiles = t // ROW_TILE
    xs = pl.pallas_call(
        _dispatch_kernel,
        grid=(n_tiles,),
        in_specs=[
            pl.BlockSpec((SUBLANES, ROW_TILE), lambda i: (0, i), memory_space=pltpu.SMEM),
            pl.BlockSpec((ROW_TILE * ch, LANES), lambda i: (i, 0)),
        ],
        out_specs=pl.BlockSpec(memory_space=pl.ANY),
        out_shape=jax.ShapeDtypeStruct((n_slots * ch, LANES), F32),
        scratch_shapes=[pltpu.SemaphoreType.DMA],
        compiler_params=pltpu.CompilerParams(dimension_semantics=("arbitrary",),
                                             vmem_limit_bytes=VMEM_LIMIT, has_side_effects=True),
        name="moe_dispatch",
    )(dest, nx)

    n_et = n_slots // EXPERT_TILE
    pts = jnp.sort(jnp.concatenate([jnp.arange(n_et, dtype=I32) * EXPERT_TILE, starts]))
    lo = pts
    hi = jnp.concatenate([pts[1:], jnp.full((1,), n_slots, I32)])
    tile = jnp.minimum(lo // EXPERT_TILE, n_et - 1)
    expert = jnp.minimum(jnp.sum((ends[None, :] <= lo[:, None]).astype(I32), axis=1), e - 1)
    one = jnp.ones((1,), I32)
    tile_change = (tile[1:] != tile[:-1]).astype(I32)
    first = jnp.concatenate([one, tile_change])
    last = jnp.concatenate([tile_change, one])
    newexp = jnp.concatenate([one, (expert[1:] != expert[:-1]).astype(I32)])
    lo_in = lo - tile * EXPERT_TILE
    hi_in = hi - tile * EXPERT_TILE
    n_visits = n_et + e
    d_exp = w_gate.shape[-1]
    ys = pl.pallas_call(
        _expert_kernel,
        grid_spec=pltpu.PrefetchScalarGridSpec(
            num_scalar_prefetch=7,
            grid=(n_visits,),
            in_specs=[
                pl.BlockSpec((EXPERT_TILE * ch, LANES), lambda v, ti, ex, *_: (ti[v], 0)),
                pl.BlockSpec((1, 1, d, d_exp), lambda v, ti, ex, *_: (layer, ex[v], 0, 0)),
                pl.BlockSpec((1, 1, d, d_exp), lambda v, ti, ex, *_: (layer, ex[v], 0, 0)),
                pl.BlockSpec((1, 1, d_exp, d), lambda v, ti, ex, *_: (layer, ex[v], 0, 0)),
            ],
            out_specs=pl.BlockSpec((EXPERT_TILE * ch, LANES), lambda v, ti, ex, *_: (ti[v], 0)),
            scratch_shapes=[pltpu.VMEM((d, d_exp), BF16), pltpu.VMEM((d, d_exp), BF16),
                            pltpu.VMEM((d_exp, d), BF16), pltpu.VMEM((EXPERT_TILE, d), F32)],
        ),
        out_shape=jax.ShapeDtypeStruct((n_slots * ch, LANES), F32),
        compiler_params=_cparams("arbitrary"),
        name="moe_experts",
    )(tile, expert, lo_in, hi_in, first, last, newexp, xs, w_gate, w_up, w_down)

    d_sh = sg.shape[-1]
    return pl.pallas_call(
        _combine_kernel,
        grid=(n_tiles,),
        in_specs=[
            pl.BlockSpec((SUBLANES, ROW_TILE), lambda i: (0, i), memory_space=pltpu.SMEM),
            pl.BlockSpec((SUBLANES, ROW_TILE), lambda i: (0, jnp.minimum(i + 1, n_tiles - 1)),
                         memory_space=pltpu.SMEM),
            pl.BlockSpec((ROW_TILE, SUBLANES), lambda i: (i, 0)),
            pl.BlockSpec((ROW_TILE * ch, LANES), lambda i: (i, 0)),
            pl.BlockSpec((ROW_TILE, d), lambda i: (i, 0)),
            pl.BlockSpec((1, 1, d), lambda i: (gate_row_of(i) * 6 + 5, 0, 0)),
            pl.BlockSpec((d, d_sh), lambda i: (0, 0)),
            pl.BlockSpec((d, d_sh), lambda i: (0, 0)),
            pl.BlockSpec((d_sh, d), lambda i: (0, 0)),
            pl.BlockSpec(memory_space=pl.ANY),
        ],
        out_specs=pl.BlockSpec((ROW_TILE, d), lambda i: (i, 0)),
        out_shape=jax.ShapeDtypeStruct((t, d), F32),
        scratch_shapes=[pltpu.VMEM((2, TOP_K, ROW_TILE * ch, LANES), F32),
                        pltpu.SemaphoreType.DMA((2,))],
        compiler_params=_cparams("arbitrary"),
        name="moe_combine",
    )(dest, dest, wcol, nx, h, modv, sg.astype(BF16), su.astype(BF16), sd.astype(BF16), ys)


def _rope_tables(n_lat, n_ctx, head_dim):
    rows = n_lat // GRID_W
    row = jnp.repeat(jnp.arange(rows, dtype=F32), GRID_W)
    col = jnp.tile(jnp.arange(GRID_W, dtype=F32), rows)
    n_freq = head_dim // 4
    inv = ROPE_THETA ** (-jnp.arange(n_freq, dtype=F32) / n_freq)
    ang = jnp.concatenate([row[:, None] * inv, col[:, None] * inv], axis=-1)
    cos = jnp.concatenate([jnp.ones((n_ctx, head_dim // 2), F32), jnp.cos(ang)], axis=0)
    sin = jnp.concatenate([jnp.zeros((n_ctx, head_dim // 2), F32), jnp.sin(ang)], axis=0)
    return cos, sin


def _split_halves_perm(head_dim):
    return np.concatenate([np.arange(0, head_dim, 2), np.arange(1, head_dim, 2)])


def kernel(x, c, ctx, c_ctx, mod_w, mod_b, norm1_g, norm2_g, ev_w_in, ev_w_out, a_ln_g, a_ln_b, a_ws, a_bs, b_q_norm, b_k_norm, b_lam_q1, b_lam_k1, b_lam_q2, b_lam_k2, b_subln, od_w_qkv, od_w_out, c_q_norm, c_k_norm, moe_router, moe_bias, moe_w_gate, moe_w_up, moe_w_down, sh_w_gate, sh_w_up, sh_w_down):
    bsz, n_lat, d = x.shape
    n_ctx = ctx.shape[1]
    depth = mod_w.shape[0]
    assert depth == 2 and n_ctx == ROW_TILE and n_lat % ROW_TILE == 0 and bsz + 1 <= MOD_ROWS
    assert d == SUBLANES * LANES
    n_seq = n_ctx + n_lat
    tpb = n_seq // ROW_TILE
    lpb = n_lat // ROW_TILE
    t_all = bsz * n_seq
    n_tiles = t_all // ROW_TILE
    ctx_row = bsz

    cond = jnp.zeros((MOD_ROWS, d), F32).at[:bsz].set(c).at[ctx_row].set(c_ctx)
    mod = _adaln(cond, mod_w, mod_b)
    modv = [mod[l].reshape(MOD_ROWS * 6, 1, d) for l in range(depth)]

    def row_all(i):
        return jnp.where(i % tpb == 0, ctx_row, i // tpb)

    def mspec(j, row_of):
        return pl.BlockSpec((1, 1, d), lambda i: (row_of(i) * 6 + j, 0, 0))

    def full(shape):
        return pl.BlockSpec(shape, lambda *_: (0,) * len(shape))

    x_spec = pl.BlockSpec((1, ROW_TILE, d), lambda i: (i // tpb, jnp.maximum(i % tpb - 1, 0), 0))
    ctx_spec = pl.BlockSpec((1, ROW_TILE, d), lambda i: (i // tpb, 0, 0))
    ch = d // LANES
    tok_spec = pl.BlockSpec((ROW_TILE * ch, LANES), lambda i: (i, 0))

    lam_init = 0.8 - 0.6 * math.exp(-0.3 * 0)
    p64 = _split_halves_perm(B_HEAD_DIM)
    col_perm = np.concatenate(
        [np.arange(2 * A_WIDTH)]
        + [2 * A_WIDTH + blk * B_HEAD_DIM + p64 for blk in range(2 * B_WIDTH // B_HEAD_DIM)]
        + [np.arange(2 * A_WIDTH + 2 * B_WIDTH, 2 * A_WIDTH + 3 * B_WIDTH)])
    w_in = ev_w_in[0][:, col_perm].astype(BF16)
    even_in = w_in.shape[1]
    cos_b, sin_b = _rope_tables(n_lat, n_ctx, B_HEAD_DIM)
    zeros_b = jnp.zeros_like(sin_b)
    tab_c = jnp.tile(jnp.concatenate([cos_b, cos_b], axis=-1), (1, 2))
    tab_sa = jnp.tile(jnp.concatenate([-sin_b, zeros_b], axis=-1), (1, 2))
    tab_sb = jnp.tile(jnp.concatenate([zeros_b, sin_b], axis=-1), (1, 2))
    qg = jnp.tile(b_q_norm[0][p64], 2).reshape(1, LANES)
    kg = jnp.tile(b_k_norm[0][p64], 2).reshape(1, LANES)
    tab_spec = pl.BlockSpec((ROW_TILE, LANES), lambda i: (i % tpb, 0))
    row_spec = lambda w: pl.BlockSpec((ROW_TILE, w), lambda i: (i, 0))
    uv, q, k, v = pl.pallas_call(
        functools.partial(_even_in_kernel, tiles_per_batch=tpb),
        grid=(n_tiles,),
        in_specs=[x_spec, ctx_spec, mspec(0, row_all), mspec(1, row_all), full((1, d)),
                  full((d, even_in)), full((1, LANES)), full((1, LANES)),
                  tab_spec, tab_spec, tab_spec],
        out_specs=[row_spec(2 * A_WIDTH), row_spec(B_WIDTH), row_spec(B_WIDTH), row_spec(B_WIDTH)],
        out_shape=[jax.ShapeDtypeStruct((t_all, 2 * A_WIDTH), F32),
                   jax.ShapeDtypeStruct((t_all, B_WIDTH), BF16),
                   jax.ShapeDtypeStruct((t_all, B_WIDTH), BF16),
                   jax.ShapeDtypeStruct((t_all, B_WIDTH), BF16)],
        compiler_params=_cparams("parallel"),
        name="even_in",
    )(x, ctx, modv[0], modv[0], norm1_g[0].reshape(1, d), w_in, qg, kg, tab_c, tab_sa, tab_sb)

    lamv = jnp.zeros((SUBLANES, LANES), F32)
    for r, vec in enumerate((b_lam_q1[0], b_lam_k1[0], b_lam_q2[0], b_lam_k2[0])):
        lamv = lamv.at[r, :B_HEAD_DIM].set(vec)
    o = pl.pallas_call(
        functools.partial(_diff_attn_kernel, ctx_len=n_ctx, lam_init=lam_init),
        grid=(bsz, tpb),
        in_specs=[
            pl.BlockSpec((SUBLANES, LANES), lambda b, qi: (0, 0)),
            pl.BlockSpec((ROW_TILE, B_WIDTH), lambda b, qi: (b * tpb + qi, 0)),
            pl.BlockSpec((n_seq, B_WIDTH), lambda b, qi: (b, 0)),
            pl.BlockSpec((n_seq, B_WIDTH), lambda b, qi: (b, 0)),
        ],
        out_specs=pl.BlockSpec((ROW_TILE, B_WIDTH), lambda b, qi: (b * tpb + qi, 0)),
        out_shape=jax.ShapeDtypeStruct((t_all, B_WIDTH), F32),
        compiler_params=_cparams("parallel", "arbitrary"),
        name="diff_attn",
    )(lamv, q, k, v)

    bs_col = jnp.repeat(a_bs[0].T, A_GROUP_DIM, axis=1)
    sub_g = b_subln[0].reshape(1, LANES)
    h1, nx = pl.pallas_call(
        functools.partial(_even_out_kernel, lam_init=lam_init, tiles_per_batch=tpb),
        grid=(n_tiles,),
        in_specs=[row_spec(B_WIDTH), row_spec(2 * A_WIDTH), x_spec, ctx_spec,
                  mspec(2, row_all), mspec(3, row_all), mspec(4, row_all),
                  full((1, LANES)), full((1, A_WIDTH)), full((1, A_WIDTH)),
                  full((A_GROUPS, GMLP_CHUNK, GMLP_CHUNK)), full((GMLP_CHUNK, A_WIDTH)),
                  full((A_WIDTH + B_WIDTH, d)), full((1, d))],
        out_specs=[row_spec(d), tok_spec],
        out_shape=[jax.ShapeDtypeStruct((t_all, d), F32),
                   jax.ShapeDtypeStruct((t_all * ch, LANES), F32)],
        compiler_params=_cparams("parallel"),
        name="even_out",
    )(o, uv, x, ctx, modv[0], modv[0], modv[0], sub_g, a_ln_g[0].reshape(1, A_WIDTH),
      a_ln_b[0].reshape(1, A_WIDTH), a_ws[0].astype(BF16), bs_col,
      ev_w_out[0].astype(BF16), norm2_g[0].reshape(1, d))

    h2 = _moe(nx, h1, modv[0], row_all, 0, moe_router[0], moe_bias[0], moe_w_gate, moe_w_up,
              moe_w_down, sh_w_gate[0], sh_w_up[0], sh_w_down[0])

    p128 = _split_halves_perm(C_HEAD_DIM)
    n_qkv_heads = C_HEADS + 2 * C_KV_HEADS
    col_perm = np.concatenate(
        [blk * C_HEAD_DIM + p128 for blk in range(C_HEADS + C_KV_HEADS)]
        + [np.arange((C_HEADS + C_KV_HEADS) * C_HEAD_DIM, n_qkv_heads * C_HEAD_DIM)])
    w_qkv = od_w_qkv[0][:, col_perm].astype(BF16)
    cos_c, sin_c = _rope_tables(n_lat, n_ctx, C_HEAD_DIM)
    tab_c1 = jnp.concatenate([cos_c, cos_c], axis=-1)
    tab_s1 = jnp.concatenate([-sin_c, sin_c], axis=-1)
    qg1 = c_q_norm[0][p128].reshape(1, LANES)
    kg1 = c_k_norm[0][p128].reshape(1, LANES)
    nq = C_HEADS * C_HEAD_DIM
    nkv = C_KV_HEADS * C_HEAD_DIM
    q1, k1, v1 = pl.pallas_call(
        _odd_in_kernel,
        grid=(n_tiles,),
        in_specs=[row_spec(d), mspec(0, row_all), mspec(1, row_all), full((1, d)),
                  full((d, nq + 2 * nkv)), full((1, LANES)), full((1, LANES)), tab_spec, tab_spec],
        out_specs=[row_spec(nq), row_spec(nkv), row_spec(nkv)],
        out_shape=[jax.ShapeDtypeStruct((t_all, nq), BF16),
                   jax.ShapeDtypeStruct((t_all, nkv), BF16),
                   jax.ShapeDtypeStruct((t_all, nkv), BF16)],
        compiler_params=_cparams("parallel"),
        name="odd_in",
    )(h2, modv[1], modv[1], norm1_g[1].reshape(1, d), w_qkv, qg1, kg1, tab_c1, tab_s1)

    t_lat = bsz * n_lat
    grp = C_HEADS // C_KV_HEADS
    o1 = pl.pallas_call(
        _gqa_kernel,
        grid=(bsz, C_KV_HEADS, lpb),
        in_specs=[
            pl.BlockSpec((ROW_TILE, grp * LANES), lambda b, n, qi: (b * tpb + 1 + qi, n)),
            pl.BlockSpec((n_seq, LANES), lambda b, n, qi: (b, n)),
            pl.BlockSpec((n_seq, LANES), lambda b, n, qi: (b, n)),
        ],
        out_specs=pl.BlockSpec((ROW_TILE, grp * LANES), lambda b, n, qi: (b * lpb + qi, n)),
        out_shape=jax.ShapeDtypeStruct((t_lat, nq), BF16),
        compiler_params=_cparams("parallel", "parallel", "arbitrary"),
        name="gqa_attn",
    )(q1, k1, v1)

    def row_lat(i):
        return i // lpb

    lat_tiles = t_lat // ROW_TILE
    hx, nx1 = pl.pallas_call(
        _odd_out_kernel,
        grid=(lat_tiles,),
        in_specs=[row_spec(nq),
                  pl.BlockSpec((ROW_TILE, d), lambda i: ((i // lpb) * tpb + 1 + i % lpb, 0)),
                  mspec(2, row_lat), mspec(3, row_lat), mspec(4, row_lat),
                  full((nq, d)), full((1, d))],
        out_specs=[row_spec(d), tok_spec],
        out_shape=[jax.ShapeDtypeStruct((t_lat, d), F32),
                   jax.ShapeDtypeStruct((t_lat * ch, LANES), F32)],
        compiler_params=_cparams("parallel"),
        name="odd_out",
    )(o1, h2, modv[1], modv[1], modv[1], od_w_out[0].astype(BF16), norm2_g[1].reshape(1, d))

    out = _moe(nx1, hx, modv[1], row_lat, 1, moe_router[1], moe_bias[1], moe_w_gate, moe_w_up,
               moe_w_down, sh_w_gate[1], sh_w_up[1], sh_w_down[1])
    return out.reshape(bsz, n_lat, d)
```

```python
import functools
import math

import numpy as np
import jax
import jax.numpy as jnp
from jax import lax
from jax.experimental import pallas as pl
from jax.experimental.pallas import tpu as pltpu

F32 = jnp.float32
BF16 = jnp.bfloat16
I32 = jnp.int32

GRID_W = 64
EPS = 1e-6
ROPE_THETA = 10000.0
A_GROUPS = 4
A_GROUP_DIM = 128
A_WIDTH = A_GROUPS * A_GROUP_DIM
GMLP_CHUNK = 128
B_HEADS = 4
B_HEAD_DIM = 64
B_WIDTH = B_HEADS * 2 * B_HEAD_DIM
C_HEADS = 8
C_KV_HEADS = 2
C_HEAD_DIM = 128
N_EXPERTS = 64
TOP_K = 6
N_GROUPS = 8
TOPK_GROUPS = 4
ROUTE_SCALE = 2.5
LOG2E = math.log2(math.e)

LANES = 128
SUBLANES = 8
ROW_TILE = 256
ROUTER_TILE = 512
EXPERT_TILE = 512
EXPERT_PIECE = 512
PROJ_BLOCK = 256
ATT_CHUNK_ROWS = 256
MOD_ROWS = 24
VMEM_LIMIT = 56 * 1024 * 1024

NT_DIMS = (((1,), (1,)), ((), ()))


def _cparams(*sem):
    return pltpu.CompilerParams(dimension_semantics=sem, vmem_limit_bytes=VMEM_LIMIT)


def _rms(x, g):
    return x * lax.rsqrt(jnp.mean(x * x, axis=-1, keepdims=True) + EPS) * g


def _norm_mod(h, g, shift, scale):
    return _rms(h, g) * (1.0 + scale) + shift


def _gelu(x):
    return 0.5 * x * (1.0 + lax.erf(x * np.float32(math.sqrt(0.5))))


def _silu(x):
    return x * jax.nn.sigmoid(x)


def _bdot(a, b):
    return jnp.dot(a.astype(BF16), b.astype(BF16), preferred_element_type=F32)


def _from_token_tiles(ref, rows, d, row0=0):
    ch = d // LANES
    groups = []
    for g in range(row0 // SUBLANES, (row0 + rows) // SUBLANES):
        groups.append(jnp.concatenate(
            [ref[pl.ds(g * SUBLANES * ch + j, SUBLANES, stride=ch), :] for j in range(ch)], axis=-1))
    return jnp.concatenate(groups, axis=0)


def _to_token_tiles(ref, val):
    rows, d = val.shape
    ch = d // LANES
    for g in range(rows // SUBLANES):
        for j in range(ch):
            ref[pl.ds(g * SUBLANES * ch + j, SUBLANES, stride=ch), :] = (
                val[g * SUBLANES:(g + 1) * SUBLANES, j * LANES:(j + 1) * LANES])


def _adaln_kernel(c_ref, w_ref, b_ref, o_ref):
    o_ref[0] = _bdot(_silu(c_ref[...]), w_ref[0]) + b_ref[0]


def _adaln(cond, mod_w, mod_b):
    depth, d, d6 = mod_w.shape
    tn = d6 // 4
    return pl.pallas_call(
        _adaln_kernel,
        grid=(depth, d6 // tn),
        in_specs=[
            pl.BlockSpec((MOD_ROWS, d), lambda l, j: (0, 0)),
            pl.BlockSpec((1, d, tn), lambda l, j: (l, 0, j)),
            pl.BlockSpec((1, 1, tn), lambda l, j: (l, 0, j)),
        ],
        out_specs=pl.BlockSpec((1, MOD_ROWS, tn), lambda l, j: (l, 0, j)),
        out_shape=jax.ShapeDtypeStruct((depth, MOD_ROWS, d6), F32),
        compiler_params=_cparams("parallel", "parallel"),
        name="adaln",
    )(cond, mod_w, mod_b.reshape(depth, 1, d6))


def _pick_stream(x_ref, ctx_ref, tiles_per_batch):
    is_ctx = pl.program_id(0) % tiles_per_batch == 0
    return jnp.where(is_ctx, ctx_ref[0], x_ref[0])


def _even_in_kernel(x_ref, ctx_ref, sh_ref, sc_ref, g_ref, w_ref, qg_ref, kg_ref, c_ref, sa_ref,
                    sb_ref, uv_ref, q_ref, k_ref, v_ref, *, tiles_per_batch):
    h = _pick_stream(x_ref, ctx_ref, tiles_per_batch)
    n = _norm_mod(h, g_ref[...], sh_ref[0], sc_ref[0]).astype(BF16)

    def cols(c0):
        return jnp.dot(n, w_ref[:, c0:c0 + PROJ_BLOCK], preferred_element_type=F32)

    for c0 in range(0, 2 * A_WIDTH, PROJ_BLOCK):
        uv_ref[:, c0:c0 + PROJ_BLOCK] = _gelu(cols(c0))
    cos, sa, sb = c_ref[...], sa_ref[...], sb_ref[...]
    lane = lax.broadcasted_iota(I32, cos.shape, 1)
    low = lane < B_HEAD_DIM

    def head_pair(x, gain, scale):
        sq = x * x
        s_lo = jnp.sum(jnp.where(low, sq, 0.0), axis=-1, keepdims=True)
        s_hi = jnp.sum(jnp.where(low, 0.0, sq), axis=-1, keepdims=True)
        ms = jnp.where(low, s_lo, s_hi) * np.float32(1.0 / B_HEAD_DIM)
        y = x * lax.rsqrt(ms + EPS) * gain
        y = (y * cos + pltpu.roll(y, LANES - B_HEAD_DIM // 2, 1) * sa
             + pltpu.roll(y, B_HEAD_DIM // 2, 1) * sb)
        if scale is not None:
            y = y * scale
        return y.astype(BF16)

    q0 = 2 * A_WIDTH
    k0 = q0 + B_WIDTH
    v0 = k0 + B_WIDTH
    per = PROJ_BLOCK // LANES
    for c0 in range(0, B_WIDTH, PROJ_BLOCK):
        pq = cols(q0 + c0)
        pk = cols(k0 + c0)
        for j in range(per):
            sl = slice(c0 + j * LANES, c0 + (j + 1) * LANES)
            q_ref[:, sl] = head_pair(pq[:, j * LANES:(j + 1) * LANES], qg_ref[...],
                                     np.float32(B_HEAD_DIM ** -0.5))
            k_ref[:, sl] = head_pair(pk[:, j * LANES:(j + 1) * LANES], kg_ref[...], None)
        v_ref[:, c0:c0 + PROJ_BLOCK] = cols(v0 + c0).astype(BF16)


def _odd_in_kernel(h_ref, sh_ref, sc_ref, g_ref, w_ref, qg_ref, kg_ref, c_ref, s_ref,
                   q_ref, k_ref, v_ref):
    n = _norm_mod(h_ref[...], g_ref[...], sh_ref[0], sc_ref[0])
    p = jnp.dot(n.astype(BF16), w_ref[...], preferred_element_type=F32)
    cos, sin = c_ref[...], s_ref[...]

    def head(x, gain, scale):
        y = _rms(x, gain)
        y = y * cos + pltpu.roll(y, C_HEAD_DIM // 2, 1) * sin
        if scale is not None:
            y = y * scale
        return y.astype(BF16)

    nq = C_HEADS * C_HEAD_DIM
    nkv = C_KV_HEADS * C_HEAD_DIM
    for j in range(C_HEADS):
        q_ref[:, j * LANES:(j + 1) * LANES] = head(p[:, j * LANES:(j + 1) * LANES], qg_ref[...],
                                                   None)
    for j in range(C_KV_HEADS):
        k_ref[:, j * LANES:(j + 1) * LANES] = head(
            p[:, nq + j * LANES:nq + (j + 1) * LANES], kg_ref[...], None)
    v_ref[...] = p[:, nq + nkv:nq + 2 * nkv].astype(BF16)


def _diff_attn_kernel(lam_ref, q_ref, k_ref, v_ref, o_ref, *, ctx_len, lam_init):
    lv = lam_ref[...]
    lam = (jnp.exp(jnp.sum(lv[0:1] * lv[1:2], axis=-1, keepdims=True))
           - jnp.exp(jnp.sum(lv[2:3] * lv[3:4], axis=-1, keepdims=True)) + np.float32(lam_init))
    low = lax.broadcasted_iota(I32, (q_ref.shape[0], LANES), 1) < B_HEAD_DIM

    def softmax(qm, k, scale):
        s = lax.dot_general(qm, k, NT_DIMS, preferred_element_type=F32)
        p = jnp.exp(s - jnp.max(s, axis=-1, keepdims=True))
        return p * (scale / jnp.sum(p, axis=-1, keepdims=True))

    def attend(n_keys):
        for hh in range(q_ref.shape[1] // LANES):
            cs = slice(hh * LANES, (hh + 1) * LANES)
            q = q_ref[:, cs]
            zero = jnp.zeros_like(q)
            k = k_ref[0:n_keys, cs]
            a = (softmax(jnp.where(low, q, zero), k, 1.0)
                 - softmax(jnp.where(low, zero, q), k, lam))
            o_ref[:, cs] = jnp.dot(a.astype(BF16), v_ref[0:n_keys, cs], preferred_element_type=F32)

    is_ctx = pl.program_id(1) == 0

    @pl.when(is_ctx)
    def _():
        attend(ctx_len)

    @pl.when(jnp.logical_not(is_ctx))
    def _():
        attend(k_ref.shape[0])


def _gqa_kernel(q_ref, k_ref, v_ref, o_ref):
    q = q_ref[...]
    rows = q.shape[0]
    grp = q.shape[1] // LANES
    c = np.float32(C_HEAD_DIM ** -0.5 * LOG2E)
    k = k_ref[...]
    v = v_ref[...]
    for g in range(grp):
        for r0 in range(0, rows, ATT_CHUNK_ROWS):
            qs = q[r0:r0 + ATT_CHUNK_ROWS, g * LANES:(g + 1) * LANES]
            s = lax.dot_general(qs, k, NT_DIMS, preferred_element_type=F32)
            p = jnp.exp2((s - jnp.max(s, axis=-1, keepdims=True)) * c)
            l = jnp.sum(p, axis=-1, keepdims=True)
            o = jnp.dot(p.astype(BF16), v, preferred_element_type=F32) / l
            o_ref[r0:r0 + ATT_CHUNK_ROWS, g * LANES:(g + 1) * LANES] = o.astype(o_ref.dtype)


def _even_out_kernel(o_ref, uv_ref, x_ref, ctx_ref, gate_ref, sh_ref, sc_ref, sub_ref, lng_ref,
                     lnb_ref, ws_ref, bs_ref, w_ref, g2_ref, h1_ref, nx_ref, *, lam_init,
                     tiles_per_batch):
    o = o_ref[...]
    uv = uv_ref[...]
    u = uv[:, :A_WIDTH]
    v = uv[:, A_WIDTH:]
    mu = jnp.mean(v, axis=-1, keepdims=True)
    var = jnp.mean(jnp.square(v - mu), axis=-1, keepdims=True)
    vn = ((v - mu) * lax.rsqrt(var + EPS) * lng_ref[...] + lnb_ref[...]).astype(BF16)
    rows = o.shape[0]
    parts = []
    for c in range(rows // GMLP_CHUNK):
        rs = slice(c * GMLP_CHUNK, (c + 1) * GMLP_CHUNK)
        for g in range(A_GROUPS):
            cs = slice(g * A_GROUP_DIM, (g + 1) * A_GROUP_DIM)
            mixed = jnp.dot(ws_ref[g], vn[rs, cs], preferred_element_type=F32) + bs_ref[:, cs]
            parts.append((c, g, u[rs, cs] * mixed))
    a_rows = [jnp.concatenate([p for (c2, _, p) in parts if c2 == c], axis=-1)
              for c in range(rows // GMLP_CHUNK)]
    a = jnp.concatenate(a_rows, axis=0)
    heads = []
    for hh in range(B_HEADS):
        oh = o[:, hh * LANES:(hh + 1) * LANES]
        heads.append(_rms(oh, sub_ref[...]) * np.float32(1.0 - lam_init))
    cat = jnp.concatenate([a] + heads, axis=-1).astype(BF16)
    y = jnp.dot(cat, w_ref[...], preferred_element_type=F32)
    h1 = _pick_stream(x_ref, ctx_ref, tiles_per_batch) + gate_ref[0] * y
    h1_ref[...] = h1
    _to_token_tiles(nx_ref, _norm_mod(h1, g2_ref[...], sh_ref[0], sc_ref[0]))


def _odd_out_kernel(o_ref, h_ref, gate_ref, sh_ref, sc_ref, w_ref, g2_ref, h1_ref, nx_ref):
    y = jnp.dot(o_ref[...], w_ref[...], preferred_element_type=F32)
    h1 = h_ref[...] + gate_ref[0] * y
    h1_ref[...] = h1
    _to_token_tiles(nx_ref, _norm_mod(h1, g2_ref[...], sh_ref[0], sc_ref[0]))


def _rows_to_block(rows, dtype):
    n = rows[0].shape[1]
    rio = lax.broadcasted_iota(I32, (SUBLANES, n), 0)
    out = jnp.zeros((SUBLANES, n), dtype)
    for r, row in enumerate(rows):
        out = jnp.where(rio == r, jnp.broadcast_to(row.astype(dtype), (SUBLANES, n)), out)
    return out


def _transpose_block(xt):
    n = xt.shape[1]
    eye = jnp.where(lax.broadcasted_iota(I32, (n, n), 0) == lax.broadcasted_iota(I32, (n, n), 1),
                    1.0, 0.0).astype(BF16)
    acc = jnp.zeros((n, SUBLANES), F32)
    rem = xt
    for _ in range(3):
        part = rem.astype(BF16)
        acc = acc + lax.dot_general(eye, part, NT_DIMS, preferred_element_type=F32)
        rem = rem - part.astype(F32)
    return acc


def _router_kernel(x_ref, wr_ref, b_ref, eidx_ref, rank_ref, wcol_ref, cnt_ref, run_ref):
    @pl.when(pl.program_id(0) == 0)
    def _():
        run_ref[...] = jnp.zeros_like(run_ref)

    per = N_EXPERTS // N_GROUPS
    d = wr_ref.shape[1]
    x = _from_token_tiles(x_ref, x_ref.shape[0] * LANES // d, d)
    logits = lax.dot_general(wr_ref[...], x.astype(BF16), NT_DIMS,
                             preferred_element_type=F32)
    scores = jax.nn.sigmoid(logits)
    sel = scores + b_ref[...]
    tm = sel.shape[1]
    neg = np.float32(-np.inf)
    jio = lax.broadcasted_iota(I32, (per, tm), 0).astype(F32)
    gio = lax.broadcasted_iota(I32, (N_GROUPS, tm), 0).astype(F32)

    def rmax(x):
        return jnp.max(x, axis=0, keepdims=True)

    def rmin(x):
        return jnp.min(x, axis=0, keepdims=True)

    sel_g = [sel[g * per:(g + 1) * per, :] for g in range(N_GROUPS)]
    sc_g = [scores[g * per:(g + 1) * per, :] for g in range(N_GROUPS)]
    gs = jnp.zeros((N_GROUPS, tm), F32)
    for g in range(N_GROUPS):
        m1 = rmax(sel_g[g])
        i1 = rmin(jnp.where(sel_g[g] == m1, jio, np.float32(per)))
        m2 = rmax(jnp.where(jio == i1, neg, sel_g[g]))
        gs = jnp.where(gio == np.float32(g), jnp.broadcast_to(m1 + m2, gs.shape), gs)
    gsel = jnp.zeros((N_GROUPS, tm), I32)
    for _ in range(TOPK_GROUPS):
        m = rmax(gs)
        idx = rmin(jnp.where(gs == m, gio, np.float32(N_GROUPS)))
        hit = gio == idx
        gsel = jnp.where(hit, 1, gsel)
        gs = jnp.where(hit, neg, gs)
    masked = [jnp.where(jnp.broadcast_to(gsel[g:g + 1, :], (per, tm)) == 1, sel_g[g], neg)
              for g in range(N_GROUPS)]
    eio = [jio + np.float32(g * per) for g in range(N_GROUPS)]
    e_rows, w_rows, hits = [], [], []
    for _ in range(TOP_K):
        m = masked[0]
        for g in range(1, N_GROUPS):
            m = jnp.maximum(m, masked[g])
        m = rmax(m)
        cand = jnp.where(masked[0] == m, eio[0], np.float32(N_EXPERTS))
        for g in range(1, N_GROUPS):
            cand = jnp.minimum(cand, jnp.where(masked[g] == m, eio[g], np.float32(N_EXPERTS)))
        idx = rmin(cand)
        hit = [eio[g] == idx for g in range(N_GROUPS)]
        wsel = jnp.where(hit[0], sc_g[0], 0.0)
        for g in range(1, N_GROUPS):
            wsel = wsel + jnp.where(hit[g], sc_g[g], 0.0)
        masked = [jnp.where(hit[g], neg, masked[g]) for g in range(N_GROUPS)]
        e_rows.append(idx)
        w_rows.append(jnp.sum(wsel, axis=0, keepdims=True))
        hits.append(hit)
    wsum = w_rows[0]
    for r in w_rows[1:]:
        wsum = wsum + r
    w_rows = [r / wsum * np.float32(ROUTE_SCALE) for r in w_rows]
    onehot = []
    for g in range(N_GROUPS):
        any_hit = hits[0][g]
        for kk in range(1, TOP_K):
            any_hit = jnp.logical_or(any_hit, hits[kk][g])
        onehot.append(jnp.where(any_hit, 1.0, 0.0))
    mt = jnp.concatenate(onehot, axis=0)
    before = (lax.broadcasted_iota(I32, (tm, tm), 0) < lax.broadcasted_iota(I32, (tm, tm), 1))
    prefix = jnp.dot(mt.astype(BF16), jnp.where(before, 1.0, 0.0).astype(BF16),
                     preferred_element_type=F32)
    pos = prefix + run_ref[...]
    r_rows = []
    for kk in range(TOP_K):
        acc = jnp.where(hits[kk][0], pos[0:per, :], 0.0)
        for g in range(1, N_GROUPS):
            acc = acc + jnp.where(hits[kk][g], pos[g * per:(g + 1) * per, :], 0.0)
        r_rows.append(jnp.sum(acc, axis=0, keepdims=True))
    run = run_ref[...] + jnp.sum(mt, axis=1, keepdims=True)
    run_ref[...] = run
    eidx_ref[...] = _rows_to_block(e_rows, I32)
    rank_ref[...] = _rows_to_block(r_rows, I32)
    wcol_ref[...] = _transpose_block(_rows_to_block(w_rows, F32))
    cnt_ref[...] = jnp.broadcast_to(run, cnt_ref.shape)


def _dest_kernel(start_ref, eidx_ref, rank_ref, dest_ref):
    per = N_EXPERTS // N_GROUPS
    eidx = eidx_ref[...]
    tm = eidx.shape[1]
    jio = lax.broadcasted_iota(I32, (per, tm), 0)
    rows = []
    for kk in range(TOP_K):
        e = jnp.broadcast_to(eidx[kk:kk + 1, :], (per, tm))
        acc = jnp.zeros((per, tm), F32)
        for g in range(N_GROUPS):
            st = jnp.broadcast_to(start_ref[g * per:(g + 1) * per, :], (per, tm))
            acc = acc + jnp.where(jio + g * per == e, st, 0.0)
        rows.append(jnp.sum(acc, axis=0, keepdims=True))
    dest_ref[...] = _rows_to_block(rows, I32) + rank_ref[...]


def _row_copy(src, s_row, dst, d_row, sem):
    def tile_start(row):
        start = row * SUBLANES
        return start if isinstance(row, int) else pl.multiple_of(start, SUBLANES)

    s0 = tile_start(s_row)
    d0 = tile_start(d_row)
    return pltpu.make_async_copy(src.at[pl.ds(s0, SUBLANES)], dst.at[pl.ds(d0, SUBLANES)], sem)


def _dispatch_kernel(dest_ref, x_ref, xs_ref, inv_ref, sem, *, n_tokens):
    rows = x_ref.shape[0] // SUBLANES
    base = pl.program_id(0) * rows

    def copies(t):
        return [_row_copy(x_ref, t, xs_ref, dest_ref[kk, t], sem) for kk in range(TOP_K)]

    def issue(t, carry):
        for kk, cp in enumerate(copies(t)):
            cp.start(priority=kk % 2)
            inv_ref[dest_ref[kk, t]] = kk * n_tokens + base + t
        return carry

    def drain(t, carry):
        for cp in copies(t):
            cp.wait()
        return carry

    lax.fori_loop(0, rows, issue, 0)
    lax.fori_loop(0, rows, drain, 0)


def _expert_kernel(tile_s, exp_s, lo_s, hi_s, first_s, last_s, new_s, inv_prev_ref, inv_ref,
                   xs_ref, wg_ref, wu_ref, wd_ref, ys_ref, wg_b, wu_b, wd_b, acc, stage, sem):
    v = pl.program_id(0)
    tile = tile_s[v]
    rows, d = acc.shape

    @pl.when(new_s[v] == 1)
    def _():
        wg_b[...] = wg_ref[0, 0].astype(BF16)
        wu_b[...] = wu_ref[0, 0].astype(BF16)
        wd_b[...] = wd_ref[0, 0].astype(BF16)

    @pl.when(first_s[v] == 1)
    def _():
        acc[...] = jnp.zeros_like(acc)

    lo = lo_s[v]
    hi = hi_s[v]
    nonempty = hi > lo
    flush_prev = jnp.logical_and(first_s[v] == 1, tile >= 1)

    def compute():
        x = _from_token_tiles(xs_ref, rows, d).astype(BF16)
        g = jnp.dot(x, wg_b[...], preferred_element_type=F32)
        u = jnp.dot(x, wu_b[...], preferred_element_type=F32)
        y = jnp.dot((_silu(g) * u).astype(BF16), wd_b[...], preferred_element_type=F32)
        row = lax.broadcasted_iota(I32, (rows, 1), 0)
        mine = jnp.logical_and(row >= lo, row < hi)
        acc[...] = jnp.where(mine, y, acc[...])

    def start_scatter(idx_ref, slot, unrolled):
        def start(r, par):
            _row_copy(stage.at[slot], r, ys_ref, idx_ref[0, 0, r], sem.at[slot]).start(priority=par)

        if unrolled:
            for r in range(rows):
                start(r, r % 2)
        else:
            def body(r2, carry):
                for par in range(2):
                    start(r2 * 2 + par, par)
                return carry
            lax.fori_loop(0, rows // 2, body, 0)

    def wait_scatter(slot):
        pltpu.make_async_copy(stage.at[slot], stage.at[slot], sem.at[slot]).wait()

    for slot in range(2):
        prev_here = jnp.logical_and(flush_prev, (tile + 1) % 2 == slot)

        @pl.when(jnp.logical_and(prev_here, nonempty))
        def _(slot=slot):
            start_scatter(inv_prev_ref, slot, True)
            compute()

        @pl.when(jnp.logical_and(prev_here, jnp.logical_not(nonempty)))
        def _(slot=slot):
            start_scatter(inv_prev_ref, slot, False)

    @pl.when(jnp.logical_and(nonempty, jnp.logical_not(flush_prev)))
    def _():
        compute()

    is_final = v == pl.num_programs(0) - 1
    for slot in range(2):
        @pl.when(jnp.logical_and(last_s[v] == 1, tile % 2 == slot))
        def _(slot=slot):
            @pl.when(tile >= 2)
            def _():
                wait_scatter(slot)

            _to_token_tiles(stage.at[slot], acc[...])

            @pl.when(is_final)
            def _():
                start_scatter(inv_ref, slot, False)

    @pl.when(is_final)
    def _():
        for slot in range(2):
            @pl.when(jnp.logical_or(tile >= 1, tile % 2 == slot))
            def _(slot=slot):
                wait_scatter(slot)


def _combine_kernel(wcol_ref, x_ref, h_ref, gate_ref, sg_ref, su_ref, sd_ref, *refs):
    y_refs, o_ref = refs[:TOP_K], refs[TOP_K]
    rows, d = h_ref.shape
    x = _from_token_tiles(x_ref, rows, d).astype(BF16)
    g = jnp.dot(x, sg_ref[...], preferred_element_type=F32)
    u = jnp.dot(x, su_ref[...], preferred_element_type=F32)
    acc = jnp.dot((_silu(g) * u).astype(BF16), sd_ref[...], preferred_element_type=F32)
    wcol = wcol_ref[...]
    for kk in range(TOP_K):
        acc = acc + _from_token_tiles(y_refs[kk], rows, d) * wcol[:, kk:kk + 1]
    o_ref[...] = h_ref[...] + gate_ref[0] * acc


def _moe(nx, h, modv, gate_row_of, layer, w_router, router_bias, w_gate, w_up, w_down, sg, su, sd):
    t, d = h.shape
    ch = d // LANES
    n_slots = t * TOP_K
    n_rt = t // ROUTER_TILE
    e = N_EXPERTS
    eidx, rank, wcol, cnt = pl.pallas_call(
        _router_kernel,
        grid=(n_rt,),
        in_specs=[
            pl.BlockSpec((ROUTER_TILE * ch, LANES), lambda i: (i, 0)),
            pl.BlockSpec((e, d), lambda i: (0, 0)),
            pl.BlockSpec((e, 1), lambda i: (0, 0)),
        ],
        out_specs=[
            pl.BlockSpec((SUBLANES, ROUTER_TILE), lambda i: (0, i)),
            pl.BlockSpec((SUBLANES, ROUTER_TILE), lambda i: (0, i)),
            pl.BlockSpec((ROUTER_TILE, SUBLANES), lambda i: (i, 0)),
            pl.BlockSpec((e, LANES), lambda i: (0, 0)),
        ],
        out_shape=[
            jax.ShapeDtypeStruct((SUBLANES, t), I32),
            jax.ShapeDtypeStruct((SUBLANES, t), I32),
            jax.ShapeDtypeStruct((t, SUBLANES), F32),
            jax.ShapeDtypeStruct((e, LANES), F32),
        ],
        scratch_shapes=[pltpu.VMEM((e, 1), F32)],
        compiler_params=_cparams("arbitrary"),
        name="moe_router",
    )(nx, w_router.T.astype(BF16), router_bias.reshape(e, 1))

    counts = cnt[:, 0].astype(I32)
    ends = jnp.cumsum(counts)
    starts = ends - counts
    dest = pl.pallas_call(
        _dest_kernel,
        grid=(n_rt,),
        in_specs=[
            pl.BlockSpec((e, 1), lambda i: (0, 0)),
            pl.BlockSpec((SUBLANES, ROUTER_TILE), lambda i: (0, i)),
            pl.BlockSpec((SUBLANES, ROUTER_TILE), lambda i: (0, i)),
        ],
        out_specs=pl.BlockSpec((SUBLANES, ROUTER_TILE), lambda i: (0, i)),
        out_shape=jax.ShapeDtypeStruct((SUBLANES, t), I32),
        compiler_params=_cparams("parallel"),
        name="moe_dest",
    )(starts.astype(F32).reshape(e, 1), eidx, rank)
    n_tiles = t // ROW_TILE
    xs, inv = pl.pallas_call(
        functools.partial(_dispatch_kernel, n_tokens=t),
        grid=(n_tiles,),
        in_specs=[
            pl.BlockSpec((SUBLANES, ROW_TILE), lambda i: (0, i), memory_space=pltpu.SMEM),
            pl.BlockSpec((ROW_TILE * ch, LANES), lambda i: (i, 0)),
        ],
        out_specs=[pl.BlockSpec(memory_space=pl.ANY), pl.BlockSpec(memory_space=pltpu.SMEM)],
        out_shape=[jax.ShapeDtypeStruct((n_slots * ch, LANES), F32),
                   jax.ShapeDtypeStruct((n_slots,), I32)],
        scratch_shapes=[pltpu.SemaphoreType.DMA],
        compiler_params=pltpu.CompilerParams(dimension_semantics=("arbitrary",),
                                             vmem_limit_bytes=VMEM_LIMIT, has_side_effects=True),
        name="moe_dispatch",
    )(dest, nx)

    n_et = n_slots // EXPERT_TILE
    pts = jnp.sort(jnp.concatenate([jnp.arange(n_et, dtype=I32) * EXPERT_TILE, starts]))
    lo = pts
    hi = jnp.concatenate([pts[1:], jnp.full((1,), n_slots, I32)])
    tile = jnp.minimum(lo // EXPERT_TILE, n_et - 1)
    expert = jnp.minimum(jnp.sum((ends[None, :] <= lo[:, None]).astype(I32), axis=1), e - 1)
    one = jnp.ones((1,), I32)
    tile_change = (tile[1:] != tile[:-1]).astype(I32)
    first = jnp.concatenate([one, tile_change])
    last = jnp.concatenate([tile_change, one])
    newexp = jnp.concatenate([one, (expert[1:] != expert[:-1]).astype(I32)])
    lo_in = lo - tile * EXPERT_TILE
    hi_in = hi - tile * EXPERT_TILE
    n_visits = n_et + e
    d_exp = w_gate.shape[-1]
    ys = pl.pallas_call(
        _expert_kernel,
        grid_spec=pltpu.PrefetchScalarGridSpec(
            num_scalar_prefetch=7,
            grid=(n_visits,),
            in_specs=[
                pl.BlockSpec((1, 1, EXPERT_TILE),
                             lambda v, ti, ex, *_: (jnp.maximum(ti[v] - 1, 0), 0, 0),
                             memory_space=pltpu.SMEM),
                pl.BlockSpec((1, 1, EXPERT_TILE), lambda v, ti, ex, *_: (ti[v], 0, 0),
                             memory_space=pltpu.SMEM),
                pl.BlockSpec((EXPERT_TILE * ch, LANES), lambda v, ti, ex, *_: (ti[v], 0)),
                pl.BlockSpec((1, 1, d, d_exp), lambda v, ti, ex, *_: (layer, ex[v], 0, 0)),
                pl.BlockSpec((1, 1, d, d_exp), lambda v, ti, ex, *_: (layer, ex[v], 0, 0)),
                pl.BlockSpec((1, 1, d_exp, d), lambda v, ti, ex, *_: (layer, ex[v], 0, 0)),
            ],
            out_specs=pl.BlockSpec(memory_space=pl.ANY),
            scratch_shapes=[pltpu.VMEM((d, d_exp), BF16), pltpu.VMEM((d, d_exp), BF16),
                            pltpu.VMEM((d_exp, d), BF16), pltpu.VMEM((EXPERT_TILE, d), F32),
                            pltpu.VMEM((2, EXPERT_TILE * ch, LANES), F32),
                            pltpu.SemaphoreType.DMA((2,))],
        ),
        out_shape=jax.ShapeDtypeStruct((n_slots * ch, LANES), F32),
        compiler_params=_cparams("arbitrary"),
        name="moe_experts",
    )(tile, expert, lo_in, hi_in, first, last, newexp, inv.reshape(n_et, 1, EXPERT_TILE),
      inv.reshape(n_et, 1, EXPERT_TILE), xs, w_gate, w_up, w_down)

    d_sh = sg.shape[-1]
    slot_specs = [pl.BlockSpec((ROW_TILE * ch, LANES), lambda i, kk=kk: (kk * n_tiles + i, 0))
                  for kk in range(TOP_K)]
    return pl.pallas_call(
        _combine_kernel,
        grid=(n_tiles,),
        in_specs=[
            pl.BlockSpec((ROW_TILE, SUBLANES), lambda i: (i, 0)),
            pl.BlockSpec((ROW_TILE * ch, LANES), lambda i: (i, 0)),
            pl.BlockSpec((ROW_TILE, d), lambda i: (i, 0)),
            pl.BlockSpec((1, 1, d), lambda i: (gate_row_of(i) * 6 + 5, 0, 0)),
            pl.BlockSpec((d, d_sh), lambda i: (0, 0)),
            pl.BlockSpec((d, d_sh), lambda i: (0, 0)),
            pl.BlockSpec((d_sh, d), lambda i: (0, 0)),
        ] + slot_specs,
        out_specs=pl.BlockSpec((ROW_TILE, d), lambda i: (i, 0)),
        out_shape=jax.ShapeDtypeStruct((t, d), F32),
        compiler_params=_cparams("parallel"),
        name="moe_combine",
    )(wcol, nx, h, modv, sg.astype(BF16), su.astype(BF16), sd.astype(BF16), *([ys] * TOP_K))


def _rope_tables(n_lat, n_ctx, head_dim):
    rows = n_lat // GRID_W
    row = jnp.repeat(jnp.arange(rows, dtype=F32), GRID_W)
    col = jnp.tile(jnp.arange(GRID_W, dtype=F32), rows)
    n_freq = head_dim // 4
    inv = ROPE_THETA ** (-jnp.arange(n_freq, dtype=F32) / n_freq)
    ang = jnp.concatenate([row[:, None] * inv, col[:, None] * inv], axis=-1)
    cos = jnp.concatenate([jnp.ones((n_ctx, head_dim // 2), F32), jnp.cos(ang)], axis=0)
    sin = jnp.concatenate([jnp.zeros((n_ctx, head_dim // 2), F32), jnp.sin(ang)], axis=0)
    return cos, sin


def _split_halves_perm(head_dim):
    return np.concatenate([np.arange(0, head_dim, 2), np.arange(1, head_dim, 2)])


def kernel(x, c, ctx, c_ctx, mod_w, mod_b, norm1_g, norm2_g, ev_w_in, ev_w_out, a_ln_g, a_ln_b, a_ws, a_bs, b_q_norm, b_k_norm, b_lam_q1, b_lam_k1, b_lam_q2, b_lam_k2, b_subln, od_w_qkv, od_w_out, c_q_norm, c_k_norm, moe_router, moe_bias, moe_w_gate, moe_w_up, moe_w_down, sh_w_gate, sh_w_up, sh_w_down):
    bsz, n_lat, d = x.shape
    n_ctx = ctx.shape[1]
    depth = mod_w.shape[0]
    assert depth == 2 and n_ctx == ROW_TILE and n_lat % ROW_TILE == 0 and bsz + 1 <= MOD_ROWS
    assert d == SUBLANES * LANES
    n_seq = n_ctx + n_lat
    tpb = n_seq // ROW_TILE
    lpb = n_lat // ROW_TILE
    t_all = bsz * n_seq
    n_tiles = t_all // ROW_TILE
    ctx_row = bsz

    cond = jnp.zeros((MOD_ROWS, d), F32).at[:bsz].set(c).at[ctx_row].set(c_ctx)
    mod = _adaln(cond, mod_w, mod_b)
    modv = [mod[l].reshape(MOD_ROWS * 6, 1, d) for l in range(depth)]

    def row_all(i):
        return jnp.where(i % tpb == 0, ctx_row, i // tpb)

    def mspec(j, row_of):
        return pl.BlockSpec((1, 1, d), lambda i: (row_of(i) * 6 + j, 0, 0))

    def full(shape):
        return pl.BlockSpec(shape, lambda *_: (0,) * len(shape))

    x_spec = pl.BlockSpec((1, ROW_TILE, d), lambda i: (i // tpb, jnp.maximum(i % tpb - 1, 0), 0))
    ctx_spec = pl.BlockSpec((1, ROW_TILE, d), lambda i: (i // tpb, 0, 0))
    ch = d // LANES
    tok_spec = pl.BlockSpec((ROW_TILE * ch, LANES), lambda i: (i, 0))

    lam_init = 0.8 - 0.6 * math.exp(-0.3 * 0)
    p64 = _split_halves_perm(B_HEAD_DIM)
    col_perm = np.concatenate(
        [np.arange(2 * A_WIDTH)]
        + [2 * A_WIDTH + blk * B_HEAD_DIM + p64 for blk in range(2 * B_WIDTH // B_HEAD_DIM)]
        + [np.arange(2 * A_WIDTH + 2 * B_WIDTH, 2 * A_WIDTH + 3 * B_WIDTH)])
    w_in = ev_w_in[0][:, col_perm].astype(BF16)
    even_in = w_in.shape[1]
    cos_b, sin_b = _rope_tables(n_lat, n_ctx, B_HEAD_DIM)
    zeros_b = jnp.zeros_like(sin_b)
    tab_c = jnp.tile(jnp.concatenate([cos_b, cos_b], axis=-1), (1, 2))
    tab_sa = jnp.tile(jnp.concatenate([-sin_b, zeros_b], axis=-1), (1, 2))
    tab_sb = jnp.tile(jnp.concatenate([zeros_b, sin_b], axis=-1), (1, 2))
    qg = jnp.tile(b_q_norm[0][p64], 2).reshape(1, LANES)
    kg = jnp.tile(b_k_norm[0][p64], 2).reshape(1, LANES)
    tab_spec = pl.BlockSpec((ROW_TILE, LANES), lambda i: (i % tpb, 0))
    row_spec = lambda w: pl.BlockSpec((ROW_TILE, w), lambda i: (i, 0))
    uv, q, k, v = pl.pallas_call(
        functools.partial(_even_in_kernel, tiles_per_batch=tpb),
        grid=(n_tiles,),
        in_specs=[x_spec, ctx_spec, mspec(0, row_all), mspec(1, row_all), full((1, d)),
                  full((d, even_in)), full((1, LANES)), full((1, LANES)),
                  tab_spec, tab_spec, tab_spec],
        out_specs=[row_spec(2 * A_WIDTH), row_spec(B_WIDTH), row_spec(B_WIDTH), row_spec(B_WIDTH)],
        out_shape=[jax.ShapeDtypeStruct((t_all, 2 * A_WIDTH), F32),
                   jax.ShapeDtypeStruct((t_all, B_WIDTH), BF16),
                   jax.ShapeDtypeStruct((t_all, B_WIDTH), BF16),
                   jax.ShapeDtypeStruct((t_all, B_WIDTH), BF16)],
        compiler_params=_cparams("parallel"),
        name="even_in",
    )(x, ctx, modv[0], modv[0], norm1_g[0].reshape(1, d), w_in, qg, kg, tab_c, tab_sa, tab_sb)

    lamv = jnp.zeros((SUBLANES, LANES), F32)
    for r, vec in enumerate((b_lam_q1[0], b_lam_k1[0], b_lam_q2[0], b_lam_k2[0])):
        lamv = lamv.at[r, :B_HEAD_DIM].set(vec)
    o = pl.pallas_call(
        functools.partial(_diff_attn_kernel, ctx_len=n_ctx, lam_init=lam_init),
        grid=(bsz, tpb),
        in_specs=[
            pl.BlockSpec((SUBLANES, LANES), lambda b, qi: (0, 0)),
            pl.BlockSpec((ROW_TILE, B_WIDTH), lambda b, qi: (b * tpb + qi, 0)),
            pl.BlockSpec((n_seq, B_WIDTH), lambda b, qi: (b, 0)),
            pl.BlockSpec((n_seq, B_WIDTH), lambda b, qi: (b, 0)),
        ],
        out_specs=pl.BlockSpec((ROW_TILE, B_WIDTH), lambda b, qi: (b * tpb + qi, 0)),
        out_shape=jax.ShapeDtypeStruct((t_all, B_WIDTH), F32),
        compiler_params=_cparams("parallel", "arbitrary"),
        name="diff_attn",
    )(lamv, q, k, v)

    bs_col = jnp.repeat(a_bs[0].T, A_GROUP_DIM, axis=1)
    sub_g = b_subln[0].reshape(1, LANES)
    h1, nx = pl.pallas_call(
        functools.partial(_even_out_kernel, lam_init=lam_init, tiles_per_batch=tpb),
        grid=(n_tiles,),
        in_specs=[row_spec(B_WIDTH), row_spec(2 * A_WIDTH), x_spec, ctx_spec,
                  mspec(2, row_all), mspec(3, row_all), mspec(4, row_all),
                  full((1, LANES)), full((1, A_WIDTH)), full((1, A_WIDTH)),
                  full((A_GROUPS, GMLP_CHUNK, GMLP_CHUNK)), full((GMLP_CHUNK, A_WIDTH)),
                  full((A_WIDTH + B_WIDTH, d)), full((1, d))],
        out_specs=[row_spec(d), tok_spec],
        out_shape=[jax.ShapeDtypeStruct((t_all, d), F32),
                   jax.ShapeDtypeStruct((t_all * ch, LANES), F32)],
        compiler_params=_cparams("parallel"),
        name="even_out",
    )(o, uv, x, ctx, modv[0], modv[0], modv[0], sub_g, a_ln_g[0].reshape(1, A_WIDTH),
      a_ln_b[0].reshape(1, A_WIDTH), a_ws[0].astype(BF16), bs_col,
      ev_w_out[0].astype(BF16), norm2_g[0].reshape(1, d))

    h2 = _moe(nx, h1, modv[0], row_all, 0, moe_router[0], moe_bias[0], moe_w_gate, moe_w_up,
              moe_w_down, sh_w_gate[0], sh_w_up[0], sh_w_down[0])

    p128 = _split_halves_perm(C_HEAD_DIM)
    n_qkv_heads = C_HEADS + 2 * C_KV_HEADS
    col_perm = np.concatenate(
        [blk * C_HEAD_DIM + p128 for blk in range(C_HEADS + C_KV_HEADS)]
        + [np.arange((C_HEADS + C_KV_HEADS) * C_HEAD_DIM, n_qkv_heads * C_HEAD_DIM)])
    w_qkv = od_w_qkv[0][:, col_perm].astype(BF16)
    cos_c, sin_c = _rope_tables(n_lat, n_ctx, C_HEAD_DIM)
    tab_c1 = jnp.concatenate([cos_c, cos_c], axis=-1)
    tab_s1 = jnp.concatenate([-sin_c, sin_c], axis=-1)
    qg1 = c_q_norm[0][p128].reshape(1, LANES)
    kg1 = c_k_norm[0][p128].reshape(1, LANES)
    nq = C_HEADS * C_HEAD_DIM
    nkv = C_KV_HEADS * C_HEAD_DIM
    q1, k1, v1 = pl.pallas_call(
        _odd_in_kernel,
        grid=(n_tiles,),
        in_specs=[row_spec(d), mspec(0, row_all), mspec(1, row_all), full((1, d)),
                  full((d, nq + 2 * nkv)), full((1, LANES)), full((1, LANES)), tab_spec, tab_spec],
        out_specs=[row_spec(nq), row_spec(nkv), row_spec(nkv)],
        out_shape=[jax.ShapeDtypeStruct((t_all, nq), BF16),
                   jax.ShapeDtypeStruct((t_all, nkv), BF16),
                   jax.ShapeDtypeStruct((t_all, nkv), BF16)],
        compiler_params=_cparams("parallel"),
        name="odd_in",
    )(h2, modv[1], modv[1], norm1_g[1].reshape(1, d), w_qkv, qg1, kg1, tab_c1, tab_s1)

    t_lat = bsz * n_lat
    grp = C_HEADS // C_KV_HEADS
    o1 = pl.pallas_call(
        _gqa_kernel,
        grid=(bsz, C_KV_HEADS, lpb),
        in_specs=[
            pl.BlockSpec((ROW_TILE, grp * LANES), lambda b, n, qi: (b * tpb + 1 + qi, n)),
            pl.BlockSpec((n_seq, LANES), lambda b, n, qi: (b, n)),
            pl.BlockSpec((n_seq, LANES), lambda b, n, qi: (b, n)),
        ],
        out_specs=pl.BlockSpec((ROW_TILE, grp * LANES), lambda b, n, qi: (b * lpb + qi, n)),
        out_shape=jax.ShapeDtypeStruct((t_lat, nq), BF16),
        compiler_params=_cparams("parallel", "parallel", "arbitrary"),
        name="gqa_attn",
    )(q1, k1, v1)

    def row_lat(i):
        return i // lpb

    lat_tiles = t_lat // ROW_TILE
    hx, nx1 = pl.pallas_call(
        _odd_out_kernel,
        grid=(lat_tiles,),
        in_specs=[row_spec(nq),
                  pl.BlockSpec((ROW_TILE, d), lambda i: ((i // lpb) * tpb + 1 + i % lpb, 0)),
                  mspec(2, row_lat), mspec(3, row_lat), mspec(4, row_lat),
                  full((nq, d)), full((1, d))],
        out_specs=[row_spec(d), tok_spec],
        out_shape=[jax.ShapeDtypeStruct((t_lat, d), F32),
                   jax.ShapeDtypeStruct((t_lat * ch, LANES), F32)],
        compiler_params=_cparams("parallel"),
        name="odd_out",
    )(o1, h2, modv[1], modv[1], modv[1], od_w_out[0].astype(BF16), norm2_g[1].reshape(1, d))

    out = _moe(nx1, hx, modv[1], row_lat, 1, moe_router[1], moe_bias[1], moe_w_gate, moe_w_up,
               moe_w_down, sh_w_gate[1], sh_w_up[1], sh_w_down[1])
    return out.reshape(bsz, n_lat, d)
```

```python
import functools
import math

import numpy as np
import jax
import jax.numpy as jnp
from jax import lax
from jax.experimental import pallas as pl
from jax.experimental.pallas import tpu as pltpu

F32 = jnp.float32
BF16 = jnp.bfloat16
I32 = jnp.int32

GRID_W = 64
EPS = 1e-6
ROPE_THETA = 10000.0
A_GROUPS = 4
A_GROUP_DIM = 128
A_WIDTH = A_GROUPS * A_GROUP_DIM
GMLP_CHUNK = 128
B_HEADS = 4
B_HEAD_DIM = 64
B_WIDTH = B_HEADS * 2 * B_HEAD_DIM
C_HEADS = 8
C_KV_HEADS = 2
C_HEAD_DIM = 128
N_EXPERTS = 64
TOP_K = 6
N_GROUPS = 8
TOPK_GROUPS = 4
ROUTE_SCALE = 2.5
LOG2E = math.log2(math.e)

LANES = 128
SUBLANES = 8
ROW_TILE = 256
ROUTER_TILE = 512
EXPERT_TILE = 512
EXPERT_PIECE = 512
PROJ_BLOCK = 256
ATT_CHUNK_ROWS = 256
MOD_ROWS = 24
VMEM_LIMIT = 56 * 1024 * 1024

NT_DIMS = (((1,), (1,)), ((), ()))


def _cparams(*sem):
    return pltpu.CompilerParams(dimension_semantics=sem, vmem_limit_bytes=VMEM_LIMIT)


def _rms(x, g):
    return x * lax.rsqrt(jnp.mean(x * x, axis=-1, keepdims=True) + EPS) * g


def _norm_mod(h, g, shift, scale):
    return _rms(h, g) * (1.0 + scale) + shift


def _gelu(x):
    return 0.5 * x * (1.0 + lax.erf(x * np.float32(math.sqrt(0.5))))


def _silu(x):
    return x * jax.nn.sigmoid(x)


def _bdot(a, b):
    return jnp.dot(a.astype(BF16), b.astype(BF16), preferred_element_type=F32)


def _from_token_tiles(ref, rows, d, row0=0):
    ch = d // LANES
    groups = []
    for g in range(row0 // SUBLANES, (row0 + rows) // SUBLANES):
        groups.append(jnp.concatenate(
            [ref[pl.ds(g * SUBLANES * ch + j, SUBLANES, stride=ch), :] for j in range(ch)], axis=-1))
    return jnp.concatenate(groups, axis=0)


def _to_token_tiles(ref, val):
    rows, d = val.shape
    ch = d // LANES
    for g in range(rows // SUBLANES):
        for j in range(ch):
            ref[pl.ds(g * SUBLANES * ch + j, SUBLANES, stride=ch), :] = (
                val[g * SUBLANES:(g + 1) * SUBLANES, j * LANES:(j + 1) * LANES])


def _adaln_kernel(c_ref, w_ref, b_ref, o_ref):
    o_ref[0] = _bdot(_silu(c_ref[...]), w_ref[0]) + b_ref[0]


def _adaln(cond, mod_w, mod_b):
    depth, d, d6 = mod_w.shape
    tn = d6 // 4
    return pl.pallas_call(
        _adaln_kernel,
        grid=(depth, d6 // tn),
        in_specs=[
            pl.BlockSpec((MOD_ROWS, d), lambda l, j: (0, 0)),
            pl.BlockSpec((1, d, tn), lambda l, j: (l, 0, j)),
            pl.BlockSpec((1, 1, tn), lambda l, j: (l, 0, j)),
        ],
        out_specs=pl.BlockSpec((1, MOD_ROWS, tn), lambda l, j: (l, 0, j)),
        out_shape=jax.ShapeDtypeStruct((depth, MOD_ROWS, d6), F32),
        compiler_params=_cparams("parallel", "parallel"),
        name="adaln",
    )(cond, mod_w, mod_b.reshape(depth, 1, d6))


def _pick_stream(x_ref, ctx_ref, tiles_per_batch):
    is_ctx = pl.program_id(0) % tiles_per_batch == 0
    return jnp.where(is_ctx, ctx_ref[0], x_ref[0])


def _even_in_kernel(x_ref, ctx_ref, sh_ref, sc_ref, g_ref, w_ref, qg_ref, kg_ref, c_ref, sa_ref,
                    sb_ref, uv_ref, q_ref, k_ref, v_ref, *, tiles_per_batch):
    h = _pick_stream(x_ref, ctx_ref, tiles_per_batch)
    n = _norm_mod(h, g_ref[...], sh_ref[0], sc_ref[0]).astype(BF16)

    def cols(c0):
        return jnp.dot(n, w_ref[:, c0:c0 + PROJ_BLOCK], preferred_element_type=F32)

    for c0 in range(0, 2 * A_WIDTH, PROJ_BLOCK):
        uv_ref[:, c0:c0 + PROJ_BLOCK] = _gelu(cols(c0))
    cos, sa, sb = c_ref[...], sa_ref[...], sb_ref[...]
    lane = lax.broadcasted_iota(I32, cos.shape, 1)
    low = lane < B_HEAD_DIM

    def head_pair(x, gain, scale):
        sq = x * x
        s_lo = jnp.sum(jnp.where(low, sq, 0.0), axis=-1, keepdims=True)
        s_hi = jnp.sum(jnp.where(low, 0.0, sq), axis=-1, keepdims=True)
        ms = jnp.where(low, s_lo, s_hi) * np.float32(1.0 / B_HEAD_DIM)
        y = x * lax.rsqrt(ms + EPS) * gain
        y = (y * cos + pltpu.roll(y, LANES - B_HEAD_DIM // 2, 1) * sa
             + pltpu.roll(y, B_HEAD_DIM // 2, 1) * sb)
        if scale is not None:
            y = y * scale
        return y.astype(BF16)

    q0 = 2 * A_WIDTH
    k0 = q0 + B_WIDTH
    v0 = k0 + B_WIDTH
    per = PROJ_BLOCK // LANES
    for c0 in range(0, B_WIDTH, PROJ_BLOCK):
        pq = cols(q0 + c0)
        pk = cols(k0 + c0)
        for j in range(per):
            sl = slice(c0 + j * LANES, c0 + (j + 1) * LANES)
            q_ref[:, sl] = head_pair(pq[:, j * LANES:(j + 1) * LANES], qg_ref[...],
                                     np.float32(B_HEAD_DIM ** -0.5))
            k_ref[:, sl] = head_pair(pk[:, j * LANES:(j + 1) * LANES], kg_ref[...], None)
        v_ref[:, c0:c0 + PROJ_BLOCK] = cols(v0 + c0).astype(BF16)


def _odd_in_kernel(h_ref, sh_ref, sc_ref, g_ref, w_ref, qg_ref, kg_ref, c_ref, s_ref,
                   q_ref, k_ref, v_ref):
    n = _norm_mod(h_ref[...], g_ref[...], sh_ref[0], sc_ref[0])
    p = jnp.dot(n.astype(BF16), w_ref[...], preferred_element_type=F32)
    cos, sin = c_ref[...], s_ref[...]

    def head(x, gain, scale):
        y = _rms(x, gain)
        y = y * cos + pltpu.roll(y, C_HEAD_DIM // 2, 1) * sin
        if scale is not None:
            y = y * scale
        return y.astype(BF16)

    nq = C_HEADS * C_HEAD_DIM
    nkv = C_KV_HEADS * C_HEAD_DIM
    for j in range(C_HEADS):
        q_ref[:, j * LANES:(j + 1) * LANES] = head(p[:, j * LANES:(j + 1) * LANES], qg_ref[...],
                                                   None)
    for j in range(C_KV_HEADS):
        k_ref[:, j * LANES:(j + 1) * LANES] = head(
            p[:, nq + j * LANES:nq + (j + 1) * LANES], kg_ref[...], None)
    v_ref[...] = p[:, nq + nkv:nq + 2 * nkv].astype(BF16)


def _diff_attn_kernel(lam_ref, q_ref, k_ref, v_ref, o_ref, *, ctx_len, lam_init):
    lv = lam_ref[...]
    lam = (jnp.exp(jnp.sum(lv[0:1] * lv[1:2], axis=-1, keepdims=True))
           - jnp.exp(jnp.sum(lv[2:3] * lv[3:4], axis=-1, keepdims=True)) + np.float32(lam_init))
    low = lax.broadcasted_iota(I32, (q_ref.shape[0], LANES), 1) < B_HEAD_DIM

    def softmax(qm, k, scale):
        s = lax.dot_general(qm, k, NT_DIMS, preferred_element_type=F32)
        p = jnp.exp(s - jnp.max(s, axis=-1, keepdims=True))
        return p * (scale / jnp.sum(p, axis=-1, keepdims=True))

    def attend(n_keys):
        for hh in range(q_ref.shape[1] // LANES):
            cs = slice(hh * LANES, (hh + 1) * LANES)
            q = q_ref[:, cs]
            zero = jnp.zeros_like(q)
            k = k_ref[0:n_keys, cs]
            a = (softmax(jnp.where(low, q, zero), k, 1.0)
                 - softmax(jnp.where(low, zero, q), k, lam))
            o_ref[:, cs] = jnp.dot(a.astype(BF16), v_ref[0:n_keys, cs], preferred_element_type=F32)

    is_ctx = pl.program_id(1) == 0

    @pl.when(is_ctx)
    def _():
        attend(ctx_len)

    @pl.when(jnp.logical_not(is_ctx))
    def _():
        attend(k_ref.shape[0])


def _gqa_kernel(q_ref, k_ref, v_ref, o_ref):
    q = q_ref[...]
    rows = q.shape[0]
    grp = q.shape[1] // LANES
    c = np.float32(C_HEAD_DIM ** -0.5 * LOG2E)
    k = k_ref[...]
    v = v_ref[...]
    for g in range(grp):
        for r0 in range(0, rows, ATT_CHUNK_ROWS):
            qs = q[r0:r0 + ATT_CHUNK_ROWS, g * LANES:(g + 1) * LANES]
            s = lax.dot_general(qs, k, NT_DIMS, preferred_element_type=F32)
            p = jnp.exp2((s - jnp.max(s, axis=-1, keepdims=True)) * c)
            l = jnp.sum(p, axis=-1, keepdims=True)
            o = jnp.dot(p.astype(BF16), v, preferred_element_type=F32) / l
            o_ref[r0:r0 + ATT_CHUNK_ROWS, g * LANES:(g + 1) * LANES] = o.astype(o_ref.dtype)


def _even_out_kernel(o_ref, uv_ref, x_ref, ctx_ref, gate_ref, sh_ref, sc_ref, sub_ref, lng_ref,
                     lnb_ref, ws_ref, bs_ref, w_ref, g2_ref, h1_ref, nx_ref, *, lam_init,
                     tiles_per_batch):
    o = o_ref[...]
    uv = uv_ref[...]
    u = uv[:, :A_WIDTH]
    v = uv[:, A_WIDTH:]
    mu = jnp.mean(v, axis=-1, keepdims=True)
    var = jnp.mean(jnp.square(v - mu), axis=-1, keepdims=True)
    vn = ((v - mu) * lax.rsqrt(var + EPS) * lng_ref[...] + lnb_ref[...]).astype(BF16)
    rows = o.shape[0]
    parts = []
    for c in range(rows // GMLP_CHUNK):
        rs = slice(c * GMLP_CHUNK, (c + 1) * GMLP_CHUNK)
        for g in range(A_GROUPS):
            cs = slice(g * A_GROUP_DIM, (g + 1) * A_GROUP_DIM)
            mixed = jnp.dot(ws_ref[g], vn[rs, cs], preferred_element_type=F32) + bs_ref[:, cs]
            parts.append((c, g, u[rs, cs] * mixed))
    a_rows = [jnp.concatenate([p for (c2, _, p) in parts if c2 == c], axis=-1)
              for c in range(rows // GMLP_CHUNK)]
    a = jnp.concatenate(a_rows, axis=0)
    heads = []
    for hh in range(B_HEADS):
        oh = o[:, hh * LANES:(hh + 1) * LANES]
        heads.append(_rms(oh, sub_ref[...]) * np.float32(1.0 - lam_init))
    cat = jnp.concatenate([a] + heads, axis=-1).astype(BF16)
    y = jnp.dot(cat, w_ref[...], preferred_element_type=F32)
    h1 = _pick_stream(x_ref, ctx_ref, tiles_per_batch) + gate_ref[0] * y
    h1_ref[...] = h1
    _to_token_tiles(nx_ref, _norm_mod(h1, g2_ref[...], sh_ref[0], sc_ref[0]))


def _odd_out_kernel(o_ref, h_ref, gate_ref, sh_ref, sc_ref, w_ref, g2_ref, h1_ref, nx_ref):
    y = jnp.dot(o_ref[...], w_ref[...], preferred_element_type=F32)
    h1 = h_ref[...] + gate_ref[0] * y
    h1_ref[...] = h1
    _to_token_tiles(nx_ref, _norm_mod(h1, g2_ref[...], sh_ref[0], sc_ref[0]))


def _rows_to_block(rows, dtype):
    n = rows[0].shape[1]
    rio = lax.broadcasted_iota(I32, (SUBLANES, n), 0)
    out = jnp.zeros((SUBLANES, n), dtype)
    for r, row in enumerate(rows):
        out = jnp.where(rio == r, jnp.broadcast_to(row.astype(dtype), (SUBLANES, n)), out)
    return out


def _transpose_block(xt):
    n = xt.shape[1]
    eye = jnp.where(lax.broadcasted_iota(I32, (n, n), 0) == lax.broadcasted_iota(I32, (n, n), 1),
                    1.0, 0.0).astype(BF16)
    acc = jnp.zeros((n, SUBLANES), F32)
    rem = xt
    for _ in range(3):
        part = rem.astype(BF16)
        acc = acc + lax.dot_general(eye, part, NT_DIMS, preferred_element_type=F32)
        rem = rem - part.astype(F32)
    return acc


def _router_kernel(x_ref, wr_ref, b_ref, eidx_ref, rank_ref, wcol_ref, cnt_ref, run_ref):
    @pl.when(pl.program_id(0) == 0)
    def _():
        run_ref[...] = jnp.zeros_like(run_ref)

    per = N_EXPERTS // N_GROUPS
    d = wr_ref.shape[1]
    x = _from_token_tiles(x_ref, x_ref.shape[0] * LANES // d, d)
    logits = lax.dot_general(wr_ref[...], x.astype(BF16), NT_DIMS,
                             preferred_element_type=F32)
    scores = jax.nn.sigmoid(logits)
    sel = scores + b_ref[...]
    tm = sel.shape[1]
    neg = np.float32(-np.inf)
    jio = lax.broadcasted_iota(I32, (per, tm), 0).astype(F32)
    gio = lax.broadcasted_iota(I32, (N_GROUPS, tm), 0).astype(F32)

    def rmax(x):
        return jnp.max(x, axis=0, keepdims=True)

    def rmin(x):
        return jnp.min(x, axis=0, keepdims=True)

    sel_g = [sel[g * per:(g + 1) * per, :] for g in range(N_GROUPS)]
    sc_g = [scores[g * per:(g + 1) * per, :] for g in range(N_GROUPS)]
    gs = jnp.zeros((N_GROUPS, tm), F32)
    for g in range(N_GROUPS):
        m1 = rmax(sel_g[g])
        i1 = rmin(jnp.where(sel_g[g] == m1, jio, np.float32(per)))
        m2 = rmax(jnp.where(jio == i1, neg, sel_g[g]))
        gs = jnp.where(gio == np.float32(g), jnp.broadcast_to(m1 + m2, gs.shape), gs)
    gsel = jnp.zeros((N_GROUPS, tm), I32)
    for _ in range(TOPK_GROUPS):
        m = rmax(gs)
        idx = rmin(jnp.where(gs == m, gio, np.float32(N_GROUPS)))
        hit = gio == idx
        gsel = jnp.where(hit, 1, gsel)
        gs = jnp.where(hit, neg, gs)
    masked = [jnp.where(jnp.broadcast_to(gsel[g:g + 1, :], (per, tm)) == 1, sel_g[g], neg)
              for g in range(N_GROUPS)]
    eio = [jio + np.float32(g * per) for g in range(N_GROUPS)]
    e_rows, w_rows, hits = [], [], []
    for _ in range(TOP_K):
        m = masked[0]
        for g in range(1, N_GROUPS):
            m = jnp.maximum(m, masked[g])
        m = rmax(m)
        cand = jnp.where(masked[0] == m, eio[0], np.float32(N_EXPERTS))
        for g in range(1, N_GROUPS):
            cand = jnp.minimum(cand, jnp.where(masked[g] == m, eio[g], np.float32(N_EXPERTS)))
        idx = rmin(cand)
        hit = [eio[g] == idx for g in range(N_GROUPS)]
        wsel = jnp.where(hit[0], sc_g[0], 0.0)
        for g in range(1, N_GROUPS):
            wsel = wsel + jnp.where(hit[g], sc_g[g], 0.0)
        masked = [jnp.where(hit[g], neg, masked[g]) for g in range(N_GROUPS)]
        e_rows.append(idx)
        w_rows.append(jnp.sum(wsel, axis=0, keepdims=True))
        hits.append(hit)
    wsum = w_rows[0]
    for r in w_rows[1:]:
        wsum = wsum + r
    w_rows = [r / wsum * np.float32(ROUTE_SCALE) for r in w_rows]
    onehot = []
    for g in range(N_GROUPS):
        any_hit = hits[0][g]
        for kk in range(1, TOP_K):
            any_hit = jnp.logical_or(any_hit, hits[kk][g])
        onehot.append(jnp.where(any_hit, 1.0, 0.0))
    mt = jnp.concatenate(onehot, axis=0)
    before = (lax.broadcasted_iota(I32, (tm, tm), 0) < lax.broadcasted_iota(I32, (tm, tm), 1))
    prefix = jnp.dot(mt.astype(BF16), jnp.where(before, 1.0, 0.0).astype(BF16),
                     preferred_element_type=F32)
    pos = prefix + run_ref[...]
    r_rows = []
    for kk in range(TOP_K):
        acc = jnp.where(hits[kk][0], pos[0:per, :], 0.0)
        for g in range(1, N_GROUPS):
            acc = acc + jnp.where(hits[kk][g], pos[g * per:(g + 1) * per, :], 0.0)
        r_rows.append(jnp.sum(acc, axis=0, keepdims=True))
    run = run_ref[...] + jnp.sum(mt, axis=1, keepdims=True)
    run_ref[...] = run
    eidx_ref[...] = _rows_to_block(e_rows, I32)
    rank_ref[...] = _rows_to_block(r_rows, I32)
    wcol_ref[...] = _transpose_block(_rows_to_block(w_rows, F32))
    cnt_ref[...] = jnp.broadcast_to(run, cnt_ref.shape)


def _dest_kernel(start_ref, eidx_ref, rank_ref, dest_ref):
    per = N_EXPERTS // N_GROUPS
    eidx = eidx_ref[...]
    tm = eidx.shape[1]
    jio = lax.broadcasted_iota(I32, (per, tm), 0)
    rows = []
    for kk in range(TOP_K):
        e = jnp.broadcast_to(eidx[kk:kk + 1, :], (per, tm))
        acc = jnp.zeros((per, tm), F32)
        for g in range(N_GROUPS):
            st = jnp.broadcast_to(start_ref[g * per:(g + 1) * per, :], (per, tm))
            acc = acc + jnp.where(jio + g * per == e, st, 0.0)
        rows.append(jnp.sum(acc, axis=0, keepdims=True))
    dest_ref[...] = _rows_to_block(rows, I32) + rank_ref[...]


def _row_copy(src, s_row, dst, d_row, sem):
    def tile_start(row):
        start = row * SUBLANES
        return start if isinstance(row, int) else pl.multiple_of(start, SUBLANES)

    s0 = tile_start(s_row)
    d0 = tile_start(d_row)
    return pltpu.make_async_copy(src.at[pl.ds(s0, SUBLANES)], dst.at[pl.ds(d0, SUBLANES)], sem)


def _dispatch_kernel(dest_ref, x_ref, xs_ref, inv_out_ref, inv_ref, sem, inv_sem, *, n_tokens):
    rows = x_ref.shape[0] // SUBLANES
    base = pl.program_id(0) * rows

    def copies(t):
        return [_row_copy(x_ref, t, xs_ref, dest_ref[kk, t], sem) for kk in range(TOP_K)]

    def issue(t, carry):
        for kk, cp in enumerate(copies(t)):
            cp.start(priority=kk % 2)
            inv_ref[dest_ref[kk, t]] = kk * n_tokens + base + t
        return carry

    def drain(t, carry):
        for cp in copies(t):
            cp.wait()
        return carry

    lax.fori_loop(0, rows, issue, 0)
    lax.fori_loop(0, rows, drain, 0)

    @pl.when(pl.program_id(0) == pl.num_programs(0) - 1)
    def _():
        cp = pltpu.make_async_copy(inv_ref, inv_out_ref, inv_sem)
        cp.start()
        cp.wait()


def _expert_kernel(tile_s, exp_s, lo_s, hi_s, first_s, last_s, new_s, inv_prev_ref, inv_ref,
                   xs_ref, wg_ref, wu_ref, wd_ref, ys_ref, wg_b, wu_b, wd_b, acc, stage, sem):
    v = pl.program_id(0)
    tile = tile_s[v]
    rows, d = acc.shape

    @pl.when(new_s[v] == 1)
    def _():
        wg_b[...] = wg_ref[0, 0].astype(BF16)
        wu_b[...] = wu_ref[0, 0].astype(BF16)
        wd_b[...] = wd_ref[0, 0].astype(BF16)

    @pl.when(first_s[v] == 1)
    def _():
        acc[...] = jnp.zeros_like(acc)

    lo = lo_s[v]
    hi = hi_s[v]
    nonempty = hi > lo
    flush_prev = jnp.logical_and(first_s[v] == 1, tile >= 1)

    def compute():
        x = _from_token_tiles(xs_ref, rows, d).astype(BF16)
        g = jnp.dot(x, wg_b[...], preferred_element_type=F32)
        u = jnp.dot(x, wu_b[...], preferred_element_type=F32)
        y = jnp.dot((_silu(g) * u).astype(BF16), wd_b[...], preferred_element_type=F32)
        row = lax.broadcasted_iota(I32, (rows, 1), 0)
        mine = jnp.logical_and(row >= lo, row < hi)
        acc[...] = jnp.where(mine, y, acc[...])

    def start_scatter(idx_ref, slot, unrolled):
        def start(r, par):
            _row_copy(stage.at[slot], r, ys_ref, idx_ref[0, 0, r], sem.at[slot]).start(priority=par)

        if unrolled:
            for r in range(rows):
                start(r, r % 2)
        else:
            def body(r2, carry):
                for par in range(2):
                    start(r2 * 2 + par, par)
                return carry
            lax.fori_loop(0, rows // 2, body, 0)

    def wait_scatter(slot):
        pltpu.make_async_copy(stage.at[slot], stage.at[slot], sem.at[slot]).wait()

    for slot in range(2):
        prev_here = jnp.logical_and(flush_prev, (tile + 1) % 2 == slot)

        @pl.when(jnp.logical_and(prev_here, nonempty))
        def _(slot=slot):
            start_scatter(inv_prev_ref, slot, True)
            compute()

        @pl.when(jnp.logical_and(prev_here, jnp.logical_not(nonempty)))
        def _(slot=slot):
            start_scatter(inv_prev_ref, slot, False)

    @pl.when(jnp.logical_and(nonempty, jnp.logical_not(flush_prev)))
    def _():
        compute()

    is_final = v == pl.num_programs(0) - 1
    for slot in range(2):
        @pl.when(jnp.logical_and(last_s[v] == 1, tile % 2 == slot))
        def _(slot=slot):
            @pl.when(tile >= 2)
            def _():
                wait_scatter(slot)

            _to_token_tiles(stage.at[slot], acc[...])

            @pl.when(is_final)
            def _():
                start_scatter(inv_ref, slot, False)

    @pl.when(is_final)
    def _():
        for slot in range(2):
            @pl.when(jnp.logical_or(tile >= 1, tile % 2 == slot))
            def _(slot=slot):
                wait_scatter(slot)


def _combine_kernel(wcol_ref, x_ref, h_ref, gate_ref, sg_ref, su_ref, sd_ref, *refs):
    y_refs, o_ref = refs[:TOP_K], refs[TOP_K]
    rows, d = h_ref.shape
    x = _from_token_tiles(x_ref, rows, d).astype(BF16)
    g = jnp.dot(x, sg_ref[...], preferred_element_type=F32)
    u = jnp.dot(x, su_ref[...], preferred_element_type=F32)
    acc = jnp.dot((_silu(g) * u).astype(BF16), sd_ref[...], preferred_element_type=F32)
    wcol = wcol_ref[...]
    for kk in range(TOP_K):
        acc = acc + _from_token_tiles(y_refs[kk], rows, d) * wcol[:, kk:kk + 1]
    o_ref[...] = h_ref[...] + gate_ref[0] * acc


def _moe(nx, h, modv, gate_row_of, layer, w_router, router_bias, w_gate, w_up, w_down, sg, su, sd):
    t, d = h.shape
    ch = d // LANES
    n_slots = t * TOP_K
    n_rt = t // ROUTER_TILE
    e = N_EXPERTS
    eidx, rank, wcol, cnt = pl.pallas_call(
        _router_kernel,
        grid=(n_rt,),
        in_specs=[
            pl.BlockSpec((ROUTER_TILE * ch, LANES), lambda i: (i, 0)),
            pl.BlockSpec((e, d), lambda i: (0, 0)),
            pl.BlockSpec((e, 1), lambda i: (0, 0)),
        ],
        out_specs=[
            pl.BlockSpec((SUBLANES, ROUTER_TILE), lambda i: (0, i)),
            pl.BlockSpec((SUBLANES, ROUTER_TILE), lambda i: (0, i)),
            pl.BlockSpec((ROUTER_TILE, SUBLANES), lambda i: (i, 0)),
            pl.BlockSpec((e, LANES), lambda i: (0, 0)),
        ],
        out_shape=[
            jax.ShapeDtypeStruct((SUBLANES, t), I32),
            jax.ShapeDtypeStruct((SUBLANES, t), I32),
            jax.ShapeDtypeStruct((t, SUBLANES), F32),
            jax.ShapeDtypeStruct((e, LANES), F32),
        ],
        scratch_shapes=[pltpu.VMEM((e, 1), F32)],
        compiler_params=_cparams("arbitrary"),
        name="moe_router",
    )(nx, w_router.T.astype(BF16), router_bias.reshape(e, 1))

    counts = cnt[:, 0].astype(I32)
    ends = jnp.cumsum(counts)
    starts = ends - counts
    dest = pl.pallas_call(
        _dest_kernel,
        grid=(n_rt,),
        in_specs=[
            pl.BlockSpec((e, 1), lambda i: (0, 0)),
            pl.BlockSpec((SUBLANES, ROUTER_TILE), lambda i: (0, i)),
            pl.BlockSpec((SUBLANES, ROUTER_TILE), lambda i: (0, i)),
        ],
        out_specs=pl.BlockSpec((SUBLANES, ROUTER_TILE), lambda i: (0, i)),
        out_shape=jax.ShapeDtypeStruct((SUBLANES, t), I32),
        compiler_params=_cparams("parallel"),
        name="moe_dest",
    )(starts.astype(F32).reshape(e, 1), eidx, rank)
    n_tiles = t // ROW_TILE
    xs, inv = pl.pallas_call(
        functools.partial(_dispatch_kernel, n_tokens=t),
        grid=(n_tiles,),
        in_specs=[
            pl.BlockSpec((SUBLANES, ROW_TILE), lambda i: (0, i), memory_space=pltpu.SMEM),
            pl.BlockSpec((ROW_TILE * ch, LANES), lambda i: (i, 0)),
        ],
        out_specs=[pl.BlockSpec(memory_space=pl.ANY), pl.BlockSpec(memory_space=pl.ANY)],
        out_shape=[jax.ShapeDtypeStruct((n_slots * ch, LANES), F32),
                   jax.ShapeDtypeStruct((n_slots,), I32)],
        scratch_shapes=[pltpu.SMEM((n_slots,), I32), pltpu.SemaphoreType.DMA,
                        pltpu.SemaphoreType.DMA],
        compiler_params=pltpu.CompilerParams(dimension_semantics=("arbitrary",),
                                             vmem_limit_bytes=VMEM_LIMIT, has_side_effects=True),
        name="moe_dispatch",
    )(dest, nx)

    n_et = n_slots // EXPERT_TILE
    pts = jnp.sort(jnp.concatenate([jnp.arange(n_et, dtype=I32) * EXPERT_TILE, starts]))
    lo = pts
    hi = jnp.concatenate([pts[1:], jnp.full((1,), n_slots, I32)])
    tile = jnp.minimum(lo // EXPERT_TILE, n_et - 1)
    expert = jnp.minimum(jnp.sum((ends[None, :] <= lo[:, None]).astype(I32), axis=1), e - 1)
    one = jnp.ones((1,), I32)
    tile_change = (tile[1:] != tile[:-1]).astype(I32)
    first = jnp.concatenate([one, tile_change])
    last = jnp.concatenate([tile_change, one])
    newexp = jnp.concatenate([one, (expert[1:] != expert[:-1]).astype(I32)])
    lo_in = lo - tile * EXPERT_TILE
    hi_in = hi - tile * EXPERT_TILE
    n_visits = n_et + e
    d_exp = w_gate.shape[-1]
    ys = pl.pallas_call(
        _expert_kernel,
        grid_spec=pltpu.PrefetchScalarGridSpec(
            num_scalar_prefetch=7,
            grid=(n_visits,),
            in_specs=[
                pl.BlockSpec((1, 1, EXPERT_TILE),
                             lambda v, ti, ex, *_: (jnp.maximum(ti[v] - 1, 0), 0, 0),
                             memory_space=pltpu.SMEM),
                pl.BlockSpec((1, 1, EXPERT_TILE), lambda v, ti, ex, *_: (ti[v], 0, 0),
                             memory_space=pltpu.SMEM),
                pl.BlockSpec((EXPERT_TILE * ch, LANES), lambda v, ti, ex, *_: (ti[v], 0)),
                pl.BlockSpec((1, 1, d, d_exp), lambda v, ti, ex, *_: (layer, ex[v], 0, 0)),
                pl.BlockSpec((1, 1, d, d_exp), lambda v, ti, ex, *_: (layer, ex[v], 0, 0)),
                pl.BlockSpec((1, 1, d_exp, d), lambda v, ti, ex, *_: (layer, ex[v], 0, 0)),
            ],
            out_specs=pl.BlockSpec(memory_space=pl.ANY),
            scratch_shapes=[pltpu.VMEM((d, d_exp), BF16), pltpu.VMEM((d, d_exp), BF16),
                            pltpu.VMEM((d_exp, d), BF16), pltpu.VMEM((EXPERT_TILE, d), F32),
                            pltpu.VMEM((2, EXPERT_TILE * ch, LANES), F32),
                            pltpu.SemaphoreType.DMA((2,))],
        ),
        out_shape=jax.ShapeDtypeStruct((n_slots * ch, LANES), F32),
        compiler_params=_cparams("arbitrary"),
        name="moe_experts",
    )(tile, expert, lo_in, hi_in, first, last, newexp, inv.reshape(n_et, 1, EXPERT_TILE),
      inv.reshape(n_et, 1, EXPERT_TILE), xs, w_gate, w_up, w_down)

    d_sh = sg.shape[-1]
    slot_specs = [pl.BlockSpec((ROW_TILE * ch, LANES), lambda i, kk=kk: (kk * n_tiles + i, 0))
                  for kk in range(TOP_K)]
    return pl.pallas_call(
        _combine_kernel,
        grid=(n_tiles,),
        in_specs=[
            pl.BlockSpec((ROW_TILE, SUBLANES), lambda i: (i, 0)),
            pl.BlockSpec((ROW_TILE * ch, LANES), lambda i: (i, 0)),
            pl.BlockSpec((ROW_TILE, d), lambda i: (i, 0)),
            pl.BlockSpec((1, 1, d), lambda i: (gate_row_of(i) * 6 + 5, 0, 0)),
            pl.BlockSpec((d, d_sh), lambda i: (0, 0)),
            pl.BlockSpec((d, d_sh), lambda i: (0, 0)),
            pl.BlockSpec((d_sh, d), lambda i: (0, 0)),
        ] + slot_specs,
        out_specs=pl.BlockSpec((ROW_TILE, d), lambda i: (i, 0)),
        out_shape=jax.ShapeDtypeStruct((t, d), F32),
        compiler_params=_cparams("parallel"),
        name="moe_combine",
    )(wcol, nx, h, modv, sg.astype(BF16), su.astype(BF16), sd.astype(BF16), *([ys] * TOP_K))


def _rope_tables(n_lat, n_ctx, head_dim):
    rows = n_lat // GRID_W
    row = jnp.repeat(jnp.arange(rows, dtype=F32), GRID_W)
    col = jnp.tile(jnp.arange(GRID_W, dtype=F32), rows)
    n_freq = head_dim // 4
    inv = ROPE_THETA ** (-jnp.arange(n_freq, dtype=F32) / n_freq)
    ang = jnp.concatenate([row[:, None] * inv, col[:, None] * inv], axis=-1)
    cos = jnp.concatenate([jnp.ones((n_ctx, head_dim // 2), F32), jnp.cos(ang)], axis=0)
    sin = jnp.concatenate([jnp.zeros((n_ctx, head_dim // 2), F32), jnp.sin(ang)], axis=0)
    return cos, sin


def _split_halves_perm(head_dim):
    return np.concatenate([np.arange(0, head_dim, 2), np.arange(1, head_dim, 2)])


def kernel(x, c, ctx, c_ctx, mod_w, mod_b, norm1_g, norm2_g, ev_w_in, ev_w_out, a_ln_g, a_ln_b, a_ws, a_bs, b_q_norm, b_k_norm, b_lam_q1, b_lam_k1, b_lam_q2, b_lam_k2, b_subln, od_w_qkv, od_w_out, c_q_norm, c_k_norm, moe_router, moe_bias, moe_w_gate, moe_w_up, moe_w_down, sh_w_gate, sh_w_up, sh_w_down):
    bsz, n_lat, d = x.shape
    n_ctx = ctx.shape[1]
    depth = mod_w.shape[0]
    assert depth == 2 and n_ctx == ROW_TILE and n_lat % ROW_TILE == 0 and bsz + 1 <= MOD_ROWS
    assert d == SUBLANES * LANES
    n_seq = n_ctx + n_lat
    tpb = n_seq // ROW_TILE
    lpb = n_lat // ROW_TILE
    t_all = bsz * n_seq
    n_tiles = t_all // ROW_TILE
    ctx_row = bsz

    cond = jnp.zeros((MOD_ROWS, d), F32).at[:bsz].set(c).at[ctx_row].set(c_ctx)
    mod = _adaln(cond, mod_w, mod_b)
    modv = [mod[l].reshape(MOD_ROWS * 6, 1, d) for l in range(depth)]

    def row_all(i):
        return jnp.where(i % tpb == 0, ctx_row, i // tpb)

    def mspec(j, row_of):
        return pl.BlockSpec((1, 1, d), lambda i: (row_of(i) * 6 + j, 0, 0))

    def full(shape):
        return pl.BlockSpec(shape, lambda *_: (0,) * len(shape))

    x_spec = pl.BlockSpec((1, ROW_TILE, d), lambda i: (i // tpb, jnp.maximum(i % tpb - 1, 0), 0))
    ctx_spec = pl.BlockSpec((1, ROW_TILE, d), lambda i: (i // tpb, 0, 0))
    ch = d // LANES
    tok_spec = pl.BlockSpec((ROW_TILE * ch, LANES), lambda i: (i, 0))

    lam_init = 0.8 - 0.6 * math.exp(-0.3 * 0)
    p64 = _split_halves_perm(B_HEAD_DIM)
    col_perm = np.concatenate(
        [np.arange(2 * A_WIDTH)]
        + [2 * A_WIDTH + blk * B_HEAD_DIM + p64 for blk in range(2 * B_WIDTH // B_HEAD_DIM)]
        + [np.arange(2 * A_WIDTH + 2 * B_WIDTH, 2 * A_WIDTH + 3 * B_WIDTH)])
    w_in = ev_w_in[0][:, col_perm].astype(BF16)
    even_in = w_in.shape[1]
    cos_b, sin_b = _rope_tables(n_lat, n_ctx, B_HEAD_DIM)
    zeros_b = jnp.zeros_like(sin_b)
    tab_c = jnp.tile(jnp.concatenate([cos_b, cos_b], axis=-1), (1, 2))
    tab_sa = jnp.tile(jnp.concatenate([-sin_b, zeros_b], axis=-1), (1, 2))
    tab_sb = jnp.tile(jnp.concatenate([zeros_b, sin_b], axis=-1), (1, 2))
    qg = jnp.tile(b_q_norm[0][p64], 2).reshape(1, LANES)
    kg = jnp.tile(b_k_norm[0][p64], 2).reshape(1, LANES)
    tab_spec = pl.BlockSpec((ROW_TILE, LANES), lambda i: (i % tpb, 0))
    row_spec = lambda w: pl.BlockSpec((ROW_TILE, w), lambda i: (i, 0))
    uv, q, k, v = pl.pallas_call(
        functools.partial(_even_in_kernel, tiles_per_batch=tpb),
        grid=(n_tiles,),
        in_specs=[x_spec, ctx_spec, mspec(0, row_all), mspec(1, row_all), full((1, d)),
                  full((d, even_in)), full((1, LANES)), full((1, LANES)),
                  tab_spec, tab_spec, tab_spec],
        out_specs=[row_spec(2 * A_WIDTH), row_spec(B_WIDTH), row_spec(B_WIDTH), row_spec(B_WIDTH)],
        out_shape=[jax.ShapeDtypeStruct((t_all, 2 * A_WIDTH), F32),
                   jax.ShapeDtypeStruct((t_all, B_WIDTH), BF16),
                   jax.ShapeDtypeStruct((t_all, B_WIDTH), BF16),
                   jax.ShapeDtypeStruct((t_all, B_WIDTH), BF16)],
        compiler_params=_cparams("parallel"),
        name="even_in",
    )(x, ctx, modv[0], modv[0], norm1_g[0].reshape(1, d), w_in, qg, kg, tab_c, tab_sa, tab_sb)

    lamv = jnp.zeros((SUBLANES, LANES), F32)
    for r, vec in enumerate((b_lam_q1[0], b_lam_k1[0], b_lam_q2[0], b_lam_k2[0])):
        lamv = lamv.at[r, :B_HEAD_DIM].set(vec)
    o = pl.pallas_call(
        functools.partial(_diff_attn_kernel, ctx_len=n_ctx, lam_init=lam_init),
        grid=(bsz, tpb),
        in_specs=[
            pl.BlockSpec((SUBLANES, LANES), lambda b, qi: (0, 0)),
            pl.BlockSpec((ROW_TILE, B_WIDTH), lambda b, qi: (b * tpb + qi, 0)),
            pl.BlockSpec((n_seq, B_WIDTH), lambda b, qi: (b, 0)),
            pl.BlockSpec((n_seq, B_WIDTH), lambda b, qi: (b, 0)),
        ],
        out_specs=pl.BlockSpec((ROW_TILE, B_WIDTH), lambda b, qi: (b * tpb + qi, 0)),
        out_shape=jax.ShapeDtypeStruct((t_all, B_WIDTH), F32),
        compiler_params=_cparams("parallel", "arbitrary"),
        name="diff_attn",
    )(lamv, q, k, v)

    bs_col = jnp.repeat(a_bs[0].T, A_GROUP_DIM, axis=1)
    sub_g = b_subln[0].reshape(1, LANES)
    h1, nx = pl.pallas_call(
        functools.partial(_even_out_kernel, lam_init=lam_init, tiles_per_batch=tpb),
        grid=(n_tiles,),
        in_specs=[row_spec(B_WIDTH), row_spec(2 * A_WIDTH), x_spec, ctx_spec,
                  mspec(2, row_all), mspec(3, row_all), mspec(4, row_all),
                  full((1, LANES)), full((1, A_WIDTH)), full((1, A_WIDTH)),
                  full((A_GROUPS, GMLP_CHUNK, GMLP_CHUNK)), full((GMLP_CHUNK, A_WIDTH)),
                  full((A_WIDTH + B_WIDTH, d)), full((1, d))],
        out_specs=[row_spec(d), tok_spec],
        out_shape=[jax.ShapeDtypeStruct((t_all, d), F32),
                   jax.ShapeDtypeStruct((t_all * ch, LANES), F32)],
        compiler_params=_cparams("parallel"),
        name="even_out",
    )(o, uv, x, ctx, modv[0], modv[0], modv[0], sub_g, a_ln_g[0].reshape(1, A_WIDTH),
      a_ln_b[0].reshape(1, A_WIDTH), a_ws[0].astype(BF16), bs_col,
      ev_w_out[0].astype(BF16), norm2_g[0].reshape(1, d))

    h2 = _moe(nx, h1, modv[0], row_all, 0, moe_router[0], moe_bias[0], moe_w_gate, moe_w_up,
              moe_w_down, sh_w_gate[0], sh_w_up[0], sh_w_down[0])

    p128 = _split_halves_perm(C_HEAD_DIM)
    n_qkv_heads = C_HEADS + 2 * C_KV_HEADS
    col_perm = np.concatenate(
        [blk * C_HEAD_DIM + p128 for blk in range(C_HEADS + C_KV_HEADS)]
        + [np.arange((C_HEADS + C_KV_HEADS) * C_HEAD_DIM, n_qkv_heads * C_HEAD_DIM)])
    w_qkv = od_w_qkv[0][:, col_perm].astype(BF16)
    cos_c, sin_c = _rope_tables(n_lat, n_ctx, C_HEAD_DIM)
    tab_c1 = jnp.concatenate([cos_c, cos_c], axis=-1)
    tab_s1 = jnp.concatenate([-sin_c, sin_c], axis=-1)
    qg1 = c_q_norm[0][p128].reshape(1, LANES)
    kg1 = c_k_norm[0][p128].reshape(1, LANES)
    nq = C_HEADS * C_HEAD_DIM
    nkv = C_KV_HEADS * C_HEAD_DIM
    q1, k1, v1 = pl.pallas_call(
        _odd_in_kernel,
        grid=(n_tiles,),
        in_specs=[row_spec(d), mspec(0, row_all), mspec(1, row_all), full((1, d)),
                  full((d, nq + 2 * nkv)), full((1, LANES)), full((1, LANES)), tab_spec, tab_spec],
        out_specs=[row_spec(nq), row_spec(nkv), row_spec(nkv)],
        out_shape=[jax.ShapeDtypeStruct((t_all, nq), BF16),
                   jax.ShapeDtypeStruct((t_all, nkv), BF16),
                   jax.ShapeDtypeStruct((t_all, nkv), BF16)],
        compiler_params=_cparams("parallel"),
        name="odd_in",
    )(h2, modv[1], modv[1], norm1_g[1].reshape(1, d), w_qkv, qg1, kg1, tab_c1, tab_s1)

    t_lat = bsz * n_lat
    grp = C_HEADS // C_KV_HEADS
    o1 = pl.pallas_call(
        _gqa_kernel,
        grid=(bsz, C_KV_HEADS, lpb),
        in_specs=[
            pl.BlockSpec((ROW_TILE, grp * LANES), lambda b, n, qi: (b * tpb + 1 + qi, n)),
            pl.BlockSpec((n_seq, LANES), lambda b, n, qi: (b, n)),
            pl.BlockSpec((n_seq, LANES), lambda b, n, qi: (b, n)),
        ],
        out_specs=pl.BlockSpec((ROW_TILE, grp * LANES), lambda b, n, qi: (b * lpb + qi, n)),
        out_shape=jax.ShapeDtypeStruct((t_lat, nq), BF16),
        compiler_params=_cparams("parallel", "parallel", "arbitrary"),
        name="gqa_attn",
    )(q1, k1, v1)

    def row_lat(i):
        return i // lpb

    lat_tiles = t_lat // ROW_TILE
    hx, nx1 = pl.pallas_call(
        _odd_out_kernel,
        grid=(lat_tiles,),
        in_specs=[row_spec(nq),
                  pl.BlockSpec((ROW_TILE, d), lambda i: ((i // lpb) * tpb + 1 + i % lpb, 0)),
                  mspec(2, row_lat), mspec(3, row_lat), mspec(4, row_lat),
                  full((nq, d)), full((1, d))],
        out_specs=[row_spec(d), tok_spec],
        out_shape=[jax.ShapeDtypeStruct((t_lat, d), F32),
                   jax.ShapeDtypeStruct((t_lat * ch, LANES), F32)],
        compiler_params=_cparams("parallel"),
        name="odd_out",
    )(o1, h2, modv[1], modv[1], modv[1], od_w_out[0].astype(BF16), norm2_g[1].reshape(1, d))

    out = _moe(nx1, hx, modv[1], row_lat, 1, moe_router[1], moe_bias[1], moe_w_gate, moe_w_up,
               moe_w_down, sh_w_gate[1], sh_w_up[1], sh_w_down[1])
    return out.reshape(bsz, n_lat, d)
```

```python
import functools
import math

import numpy as np
import jax
import jax.numpy as jnp
from jax import lax
from jax.experimental import pallas as pl
from jax.experimental.pallas import tpu as pltpu

F32 = jnp.float32
BF16 = jnp.bfloat16
I32 = jnp.int32

GRID_W = 64
EPS = 1e-6
ROPE_THETA = 10000.0
A_GROUPS = 4
A_GROUP_DIM = 128
A_WIDTH = A_GROUPS * A_GROUP_DIM
GMLP_CHUNK = 128
B_HEADS = 4
B_HEAD_DIM = 64
B_WIDTH = B_HEADS * 2 * B_HEAD_DIM
C_HEADS = 8
C_KV_HEADS = 2
C_HEAD_DIM = 128
N_EXPERTS = 64
TOP_K = 6
N_GROUPS = 8
TOPK_GROUPS = 4
ROUTE_SCALE = 2.5
LOG2E = math.log2(math.e)

LANES = 128
SUBLANES = 8
ROW_TILE = 256
ROUTER_TILE = 512
EXPERT_TILE = 512
EXPERT_PIECE = 512
PROJ_BLOCK = 256
ATT_CHUNK_ROWS = 256
MOD_ROWS = 24
VMEM_LIMIT = 56 * 1024 * 1024

NT_DIMS = (((1,), (1,)), ((), ()))


def _cparams(*sem):
    return pltpu.CompilerParams(dimension_semantics=sem, vmem_limit_bytes=VMEM_LIMIT)


def _rms(x, g):
    return x * lax.rsqrt(jnp.mean(x * x, axis=-1, keepdims=True) + EPS) * g


def _norm_mod(h, g, shift, scale):
    return _rms(h, g) * (1.0 + scale) + shift


def _gelu(x):
    return 0.5 * x * (1.0 + lax.erf(x * np.float32(math.sqrt(0.5))))


def _silu(x):
    return x * jax.nn.sigmoid(x)


def _bdot(a, b):
    return jnp.dot(a.astype(BF16), b.astype(BF16), preferred_element_type=F32)


def _from_token_tiles(ref, rows, d, row0=0):
    ch = d // LANES
    groups = []
    for g in range(row0 // SUBLANES, (row0 + rows) // SUBLANES):
        groups.append(jnp.concatenate(
            [ref[pl.ds(g * SUBLANES * ch + j, SUBLANES, stride=ch), :] for j in range(ch)], axis=-1))
    return jnp.concatenate(groups, axis=0)


def _to_token_tiles(ref, val):
    rows, d = val.shape
    ch = d // LANES
    for g in range(rows // SUBLANES):
        for j in range(ch):
            ref[pl.ds(g * SUBLANES * ch + j, SUBLANES, stride=ch), :] = (
                val[g * SUBLANES:(g + 1) * SUBLANES, j * LANES:(j + 1) * LANES])


def _adaln_kernel(c_ref, w_ref, b_ref, o_ref):
    o_ref[0] = _bdot(_silu(c_ref[...]), w_ref[0]) + b_ref[0]


def _adaln(cond, mod_w, mod_b):
    depth, d, d6 = mod_w.shape
    tn = d6 // 4
    return pl.pallas_call(
        _adaln_kernel,
        grid=(depth, d6 // tn),
        in_specs=[
            pl.BlockSpec((MOD_ROWS, d), lambda l, j: (0, 0)),
            pl.BlockSpec((1, d, tn), lambda l, j: (l, 0, j)),
            pl.BlockSpec((1, 1, tn), lambda l, j: (l, 0, j)),
        ],
        out_specs=pl.BlockSpec((1, MOD_ROWS, tn), lambda l, j: (l, 0, j)),
        out_shape=jax.ShapeDtypeStruct((depth, MOD_ROWS, d6), F32),
        compiler_params=_cparams("parallel", "parallel"),
        name="adaln",
    )(cond, mod_w, mod_b.reshape(depth, 1, d6))


def _pick_stream(x_ref, ctx_ref, tiles_per_batch):
    is_ctx = pl.program_id(0) % tiles_per_batch == 0
    return jnp.where(is_ctx, ctx_ref[0], x_ref[0])


def _even_in_kernel(x_ref, ctx_ref, sh_ref, sc_ref, g_ref, w_ref, qg_ref, kg_ref, c_ref, sa_ref,
                    sb_ref, uv_ref, q_ref, k_ref, v_ref, *, tiles_per_batch):
    h = _pick_stream(x_ref, ctx_ref, tiles_per_batch)
    n = _norm_mod(h, g_ref[...], sh_ref[0], sc_ref[0]).astype(BF16)

    def cols(c0):
        return jnp.dot(n, w_ref[:, c0:c0 + PROJ_BLOCK], preferred_element_type=F32)

    for c0 in range(0, 2 * A_WIDTH, PROJ_BLOCK):
        uv_ref[:, c0:c0 + PROJ_BLOCK] = _gelu(cols(c0))
    cos, sa, sb = c_ref[...], sa_ref[...], sb_ref[...]
    lane = lax.broadcasted_iota(I32, cos.shape, 1)
    low = lane < B_HEAD_DIM

    def head_pair(x, gain, scale):
        sq = x * x
        s_lo = jnp.sum(jnp.where(low, sq, 0.0), axis=-1, keepdims=True)
        s_hi = jnp.sum(jnp.where(low, 0.0, sq), axis=-1, keepdims=True)
        ms = jnp.where(low, s_lo, s_hi) * np.float32(1.0 / B_HEAD_DIM)
        y = x * lax.rsqrt(ms + EPS) * gain
        y = (y * cos + pltpu.roll(y, LANES - B_HEAD_DIM // 2, 1) * sa
             + pltpu.roll(y, B_HEAD_DIM // 2, 1) * sb)
        if scale is not None:
            y = y * scale
        return y.astype(BF16)

    q0 = 2 * A_WIDTH
    k0 = q0 + B_WIDTH
    v0 = k0 + B_WIDTH
    per = PROJ_BLOCK // LANES
    for c0 in range(0, B_WIDTH, PROJ_BLOCK):
        pq = cols(q0 + c0)
        pk = cols(k0 + c0)
        for j in range(per):
            sl = slice(c0 + j * LANES, c0 + (j + 1) * LANES)
            q_ref[:, sl] = head_pair(pq[:, j * LANES:(j + 1) * LANES], qg_ref[...],
                                     np.float32(B_HEAD_DIM ** -0.5))
            k_ref[:, sl] = head_pair(pk[:, j * LANES:(j + 1) * LANES], kg_ref[...], None)
        v_ref[:, c0:c0 + PROJ_BLOCK] = cols(v0 + c0).astype(BF16)


def _odd_in_kernel(h_ref, sh_ref, sc_ref, g_ref, w_ref, qg_ref, kg_ref, c_ref, s_ref,
                   q_ref, k_ref, v_ref):
    n = _norm_mod(h_ref[...], g_ref[...], sh_ref[0], sc_ref[0])
    p = jnp.dot(n.astype(BF16), w_ref[...], preferred_element_type=F32)
    cos, sin = c_ref[...], s_ref[...]

    def head(x, gain, scale):
        y = _rms(x, gain)
        y = y * cos + pltpu.roll(y, C_HEAD_DIM // 2, 1) * sin
        if scale is not None:
            y = y * scale
        return y.astype(BF16)

    nq = C_HEADS * C_HEAD_DIM
    nkv = C_KV_HEADS * C_HEAD_DIM
    for j in range(C_HEADS):
        q_ref[:, j * LANES:(j + 1) * LANES] = head(p[:, j * LANES:(j + 1) * LANES], qg_ref[...],
                                                   None)
    for j in range(C_KV_HEADS):
        k_ref[:, j * LANES:(j + 1) * LANES] = head(
            p[:, nq + j * LANES:nq + (j + 1) * LANES], kg_ref[...], None)
    v_ref[...] = p[:, nq + nkv:nq + 2 * nkv].astype(BF16)


def _diff_attn_kernel(lam_ref, q_ref, k_ref, v_ref, o_ref, *, ctx_len, lam_init):
    lv = lam_ref[...]
    lam = (jnp.exp(jnp.sum(lv[0:1] * lv[1:2], axis=-1, keepdims=True))
           - jnp.exp(jnp.sum(lv[2:3] * lv[3:4], axis=-1, keepdims=True)) + np.float32(lam_init))
    low = lax.broadcasted_iota(I32, (q_ref.shape[0], LANES), 1) < B_HEAD_DIM

    def softmax(qm, k, scale):
        s = lax.dot_general(qm, k, NT_DIMS, preferred_element_type=F32)
        p = jnp.exp(s - jnp.max(s, axis=-1, keepdims=True))
        return p * (scale / jnp.sum(p, axis=-1, keepdims=True))

    def attend(n_keys):
        for hh in range(q_ref.shape[1] // LANES):
            cs = slice(hh * LANES, (hh + 1) * LANES)
            q = q_ref[:, cs]
            zero = jnp.zeros_like(q)
            k = k_ref[0:n_keys, cs]
            a = (softmax(jnp.where(low, q, zero), k, 1.0)
                 - softmax(jnp.where(low, zero, q), k, lam))
            o_ref[:, cs] = jnp.dot(a.astype(BF16), v_ref[0:n_keys, cs], preferred_element_type=F32)

    is_ctx = pl.program_id(1) == 0

    @pl.when(is_ctx)
    def _():
        attend(ctx_len)

    @pl.when(jnp.logical_not(is_ctx))
    def _():
        attend(k_ref.shape[0])


def _gqa_kernel(q_ref, k_ref, v_ref, o_ref):
    q = q_ref[...]
    rows = q.shape[0]
    grp = q.shape[1] // LANES
    c = np.float32(C_HEAD_DIM ** -0.5 * LOG2E)
    k = k_ref[...]
    v = v_ref[...]
    for g in range(grp):
        for r0 in range(0, rows, ATT_CHUNK_ROWS):
            qs = q[r0:r0 + ATT_CHUNK_ROWS, g * LANES:(g + 1) * LANES]
            s = lax.dot_general(qs, k, NT_DIMS, preferred_element_type=F32)
            p = jnp.exp2((s - jnp.max(s, axis=-1, keepdims=True)) * c)
            l = jnp.sum(p, axis=-1, keepdims=True)
            o = jnp.dot(p.astype(BF16), v, preferred_element_type=F32) / l
            o_ref[r0:r0 + ATT_CHUNK_ROWS, g * LANES:(g + 1) * LANES] = o.astype(o_ref.dtype)


def _even_out_kernel(o_ref, uv_ref, x_ref, ctx_ref, gate_ref, sh_ref, sc_ref, sub_ref, lng_ref,
                     lnb_ref, ws_ref, bs_ref, w_ref, g2_ref, h1_ref, nx_ref, *, lam_init,
                     tiles_per_batch):
    o = o_ref[...]
    uv = uv_ref[...]
    u = uv[:, :A_WIDTH]
    v = uv[:, A_WIDTH:]
    mu = jnp.mean(v, axis=-1, keepdims=True)
    var = jnp.mean(jnp.square(v - mu), axis=-1, keepdims=True)
    vn = ((v - mu) * lax.rsqrt(var + EPS) * lng_ref[...] + lnb_ref[...]).astype(BF16)
    rows = o.shape[0]
    parts = []
    for c in range(rows // GMLP_CHUNK):
        rs = slice(c * GMLP_CHUNK, (c + 1) * GMLP_CHUNK)
        for g in range(A_GROUPS):
            cs = slice(g * A_GROUP_DIM, (g + 1) * A_GROUP_DIM)
            mixed = jnp.dot(ws_ref[g], vn[rs, cs], preferred_element_type=F32) + bs_ref[:, cs]
            parts.append((c, g, u[rs, cs] * mixed))
    a_rows = [jnp.concatenate([p for (c2, _, p) in parts if c2 == c], axis=-1)
              for c in range(rows // GMLP_CHUNK)]
    a = jnp.concatenate(a_rows, axis=0)
    heads = []
    for hh in range(B_HEADS):
        oh = o[:, hh * LANES:(hh + 1) * LANES]
        heads.append(_rms(oh, sub_ref[...]) * np.float32(1.0 - lam_init))
    cat = jnp.concatenate([a] + heads, axis=-1).astype(BF16)
    y = jnp.dot(cat, w_ref[...], preferred_element_type=F32)
    h1 = _pick_stream(x_ref, ctx_ref, tiles_per_batch) + gate_ref[0] * y
    h1_ref[...] = h1
    _to_token_tiles(nx_ref, _norm_mod(h1, g2_ref[...], sh_ref[0], sc_ref[0]))


def _odd_out_kernel(o_ref, h_ref, gate_ref, sh_ref, sc_ref, w_ref, g2_ref, h1_ref, nx_ref):
    y = jnp.dot(o_ref[...], w_ref[...], preferred_element_type=F32)
    h1 = h_ref[...] + gate_ref[0] * y
    h1_ref[...] = h1
    _to_token_tiles(nx_ref, _norm_mod(h1, g2_ref[...], sh_ref[0], sc_ref[0]))


def _rows_to_block(rows, dtype):
    n = rows[0].shape[1]
    rio = lax.broadcasted_iota(I32, (SUBLANES, n), 0)
    out = jnp.zeros((SUBLANES, n), dtype)
    for r, row in enumerate(rows):
        out = jnp.where(rio == r, jnp.broadcast_to(row.astype(dtype), (SUBLANES, n)), out)
    return out


def _transpose_block(xt):
    n = xt.shape[1]
    eye = jnp.where(lax.broadcasted_iota(I32, (n, n), 0) == lax.broadcasted_iota(I32, (n, n), 1),
                    1.0, 0.0).astype(BF16)
    acc = jnp.zeros((n, SUBLANES), F32)
    rem = xt
    for _ in range(3):
        part = rem.astype(BF16)
        acc = acc + lax.dot_general(eye, part, NT_DIMS, preferred_element_type=F32)
        rem = rem - part.astype(F32)
    return acc


def _router_kernel(x_ref, wr_ref, b_ref, eidx_ref, rank_ref, wcol_ref, cnt_ref, run_ref):
    @pl.when(pl.program_id(0) == 0)
    def _():
        run_ref[...] = jnp.zeros_like(run_ref)

    per = N_EXPERTS // N_GROUPS
    d = wr_ref.shape[1]
    x = _from_token_tiles(x_ref, x_ref.shape[0] * LANES // d, d)
    logits = lax.dot_general(wr_ref[...], x.astype(BF16), NT_DIMS,
                             preferred_element_type=F32)
    scores = jax.nn.sigmoid(logits)
    sel = scores + b_ref[...]
    tm = sel.shape[1]
    neg = np.float32(-np.inf)
    jio = lax.broadcasted_iota(I32, (per, tm), 0).astype(F32)
    gio = lax.broadcasted_iota(I32, (N_GROUPS, tm), 0).astype(F32)

    def rmax(x):
        return jnp.max(x, axis=0, keepdims=True)

    def rmin(x):
        return jnp.min(x, axis=0, keepdims=True)

    sel_g = [sel[g * per:(g + 1) * per, :] for g in range(N_GROUPS)]
    sc_g = [scores[g * per:(g + 1) * per, :] for g in range(N_GROUPS)]
    gs = jnp.zeros((N_GROUPS, tm), F32)
    for g in range(N_GROUPS):
        m1 = rmax(sel_g[g])
        i1 = rmin(jnp.where(sel_g[g] == m1, jio, np.float32(per)))
        m2 = rmax(jnp.where(jio == i1, neg, sel_g[g]))
        gs = jnp.where(gio == np.float32(g), jnp.broadcast_to(m1 + m2, gs.shape), gs)
    gsel = jnp.zeros((N_GROUPS, tm), I32)
    for _ in range(TOPK_GROUPS):
        m = rmax(gs)
        idx = rmin(jnp.where(gs == m, gio, np.float32(N_GROUPS)))
        hit = gio == idx
        gsel = jnp.where(hit, 1, gsel)
        gs = jnp.where(hit, neg, gs)
    masked = [jnp.where(jnp.broadcast_to(gsel[g:g + 1, :], (per, tm)) == 1, sel_g[g], neg)
              for g in range(N_GROUPS)]
    eio = [jio + np.float32(g * per) for g in range(N_GROUPS)]
    e_rows, w_rows, hits = [], [], []
    for _ in range(TOP_K):
        m = masked[0]
        for g in range(1, N_GROUPS):
            m = jnp.maximum(m, masked[g])
        m = rmax(m)
        cand = jnp.where(masked[0] == m, eio[0], np.float32(N_EXPERTS))
        for g in range(1, N_GROUPS):
            cand = jnp.minimum(cand, jnp.where(masked[g] == m, eio[g], np.float32(N_EXPERTS)))
        idx = rmin(cand)
        hit = [eio[g] == idx for g in range(N_GROUPS)]
        wsel = jnp.where(hit[0], sc_g[0], 0.0)
        for g in range(1, N_GROUPS):
            wsel = wsel + jnp.where(hit[g], sc_g[g], 0.0)
        masked = [jnp.where(hit[g], neg, masked[g]) for g in range(N_GROUPS)]
        e_rows.append(idx)
        w_rows.append(jnp.sum(wsel, axis=0, keepdims=True))
        hits.append(hit)
    wsum = w_rows[0]
    for r in w_rows[1:]:
        wsum = wsum + r
    w_rows = [r / wsum * np.float32(ROUTE_SCALE) for r in w_rows]
    onehot = []
    for g in range(N_GROUPS):
        any_hit = hits[0][g]
        for kk in range(1, TOP_K):
            any_hit = jnp.logical_or(any_hit, hits[kk][g])
        onehot.append(jnp.where(any_hit, 1.0, 0.0))
    mt = jnp.concatenate(onehot, axis=0)
    before = (lax.broadcasted_iota(I32, (tm, tm), 0) < lax.broadcasted_iota(I32, (tm, tm), 1))
    prefix = jnp.dot(mt.astype(BF16), jnp.where(before, 1.0, 0.0).astype(BF16),
                     preferred_element_type=F32)
    pos = prefix + run_ref[...]
    r_rows = []
    for kk in range(TOP_K):
        acc = jnp.where(hits[kk][0], pos[0:per, :], 0.0)
        for g in range(1, N_GROUPS):
            acc = acc + jnp.where(hits[kk][g], pos[g * per:(g + 1) * per, :], 0.0)
        r_rows.append(jnp.sum(acc, axis=0, keepdims=True))
    run = run_ref[...] + jnp.sum(mt, axis=1, keepdims=True)
    run_ref[...] = run
    eidx_ref[...] = _rows_to_block(e_rows, I32)
    rank_ref[...] = _rows_to_block(r_rows, I32)
    wcol_ref[...] = _transpose_block(_rows_to_block(w_rows, F32))
    cnt_ref[...] = jnp.broadcast_to(run, cnt_ref.shape)


def _dest_kernel(start_ref, eidx_ref, rank_ref, dest_ref):
    per = N_EXPERTS // N_GROUPS
    eidx = eidx_ref[...]
    tm = eidx.shape[1]
    jio = lax.broadcasted_iota(I32, (per, tm), 0)
    rows = []
    for kk in range(TOP_K):
        e = jnp.broadcast_to(eidx[kk:kk + 1, :], (per, tm))
        acc = jnp.zeros((per, tm), F32)
        for g in range(N_GROUPS):
            st = jnp.broadcast_to(start_ref[g * per:(g + 1) * per, :], (per, tm))
            acc = acc + jnp.where(jio + g * per == e, st, 0.0)
        rows.append(jnp.sum(acc, axis=0, keepdims=True))
    dest_ref[...] = _rows_to_block(rows, I32) + rank_ref[...]


def _row_copy(src, s_row, dst, d_row, sem):
    def tile_start(row):
        start = row * SUBLANES
        return start if isinstance(row, int) else pl.multiple_of(start, SUBLANES)

    s0 = tile_start(s_row)
    d0 = tile_start(d_row)
    return pltpu.make_async_copy(src.at[pl.ds(s0, SUBLANES)], dst.at[pl.ds(d0, SUBLANES)], sem)


def _dispatch_kernel(dest_ref, x_ref, xs_ref, inv_out_ref, inv_ref, sem, inv_sem, *, n_tokens):
    rows = x_ref.shape[0] // SUBLANES
    base = pl.program_id(0) * rows

    def copies(t):
        return [_row_copy(x_ref, t, xs_ref, dest_ref[kk, t], sem) for kk in range(TOP_K)]

    def issue(t, carry):
        for kk, cp in enumerate(copies(t)):
            cp.start(priority=kk % 2)
        return carry

    def record(t, carry):
        for kk in range(TOP_K):
            inv_ref[dest_ref[kk, t]] = kk * n_tokens + base + t
        return carry

    def drain(t, carry):
        for cp in copies(t):
            cp.wait()
        return carry

    lax.fori_loop(0, rows, issue, 0)
    lax.fori_loop(0, rows, record, 0)
    lax.fori_loop(0, rows, drain, 0)

    @pl.when(pl.program_id(0) == pl.num_programs(0) - 1)
    def _():
        cp = pltpu.make_async_copy(inv_ref, inv_out_ref, inv_sem)
        cp.start()
        cp.wait()


def _expert_kernel(tile_s, exp_s, lo_s, hi_s, first_s, last_s, new_s, inv_prev_ref, inv_ref,
                   xs_ref, wg_ref, wu_ref, wd_ref, ys_ref, wg_b, wu_b, wd_b, acc, stage, sem):
    v = pl.program_id(0)
    tile = tile_s[v]
    rows, d = acc.shape

    @pl.when(new_s[v] == 1)
    def _():
        wg_b[...] = wg_ref[0, 0].astype(BF16)
        wu_b[...] = wu_ref[0, 0].astype(BF16)
        wd_b[...] = wd_ref[0, 0].astype(BF16)

    @pl.when(first_s[v] == 1)
    def _():
        acc[...] = jnp.zeros_like(acc)

    lo = lo_s[v]
    hi = hi_s[v]
    nonempty = hi > lo
    flush_prev = jnp.logical_and(first_s[v] == 1, tile >= 1)

    def compute():
        x = _from_token_tiles(xs_ref, rows, d).astype(BF16)
        g = jnp.dot(x, wg_b[...], preferred_element_type=F32)
        u = jnp.dot(x, wu_b[...], preferred_element_type=F32)
        y = jnp.dot((_silu(g) * u).astype(BF16), wd_b[...], preferred_element_type=F32)
        row = lax.broadcasted_iota(I32, (rows, 1), 0)
        mine = jnp.logical_and(row >= lo, row < hi)
        acc[...] = jnp.where(mine, y, acc[...])

    def start_scatter(idx_ref, slot, unrolled):
        def start(r, par):
            _row_copy(stage.at[slot], r, ys_ref, idx_ref[0, 0, r], sem.at[slot]).start(priority=par)

        if unrolled:
            for r in range(rows):
                start(r, r % 2)
        else:
            def body(r2, carry):
                for par in range(2):
                    start(r2 * 2 + par, par)
                return carry
            lax.fori_loop(0, rows // 2, body, 0)

    def wait_scatter(slot):
        pltpu.make_async_copy(stage.at[slot], stage.at[slot], sem.at[slot]).wait()

    for slot in range(2):
        prev_here = jnp.logical_and(flush_prev, (tile + 1) % 2 == slot)

        @pl.when(jnp.logical_and(prev_here, nonempty))
        def _(slot=slot):
            start_scatter(inv_prev_ref, slot, True)
            compute()

        @pl.when(jnp.logical_and(prev_here, jnp.logical_not(nonempty)))
        def _(slot=slot):
            start_scatter(inv_prev_ref, slot, False)

    @pl.when(jnp.logical_and(nonempty, jnp.logical_not(flush_prev)))
    def _():
        compute()

    is_final = v == pl.num_programs(0) - 1
    for slot in range(2):
        @pl.when(jnp.logical_and(last_s[v] == 1, tile % 2 == slot))
        def _(slot=slot):
            @pl.when(tile >= 2)
            def _():
                wait_scatter(slot)

            _to_token_tiles(stage.at[slot], acc[...])

            @pl.when(is_final)
            def _():
                start_scatter(inv_ref, slot, False)

    @pl.when(is_final)
    def _():
        for slot in range(2):
            @pl.when(jnp.logical_or(tile >= 1, tile % 2 == slot))
            def _(slot=slot):
                wait_scatter(slot)


def _combine_kernel(wcol_ref, x_ref, h_ref, gate_ref, sg_ref, su_ref, sd_ref, *refs):
    y_refs, o_ref = refs[:TOP_K], refs[TOP_K]
    rows, d = h_ref.shape
    x = _from_token_tiles(x_ref, rows, d).astype(BF16)
    g = jnp.dot(x, sg_ref[...], preferred_element_type=F32)
    u = jnp.dot(x, su_ref[...], preferred_element_type=F32)
    acc = jnp.dot((_silu(g) * u).astype(BF16), sd_ref[...], preferred_element_type=F32)
    wcol = wcol_ref[...]
    for kk in range(TOP_K):
        acc = acc + _from_token_tiles(y_refs[kk], rows, d) * wcol[:, kk:kk + 1]
    o_ref[...] = h_ref[...] + gate_ref[0] * acc


def _moe(nx, h, modv, gate_row_of, layer, w_router, router_bias, w_gate, w_up, w_down, sg, su, sd):
    t, d = h.shape
    ch = d // LANES
    n_slots = t * TOP_K
    n_rt = t // ROUTER_TILE
    e = N_EXPERTS
    eidx, rank, wcol, cnt = pl.pallas_call(
        _router_kernel,
        grid=(n_rt,),
        in_specs=[
            pl.BlockSpec((ROUTER_TILE * ch, LANES), lambda i: (i, 0)),
            pl.BlockSpec((e, d), lambda i: (0, 0)),
            pl.BlockSpec((e, 1), lambda i: (0, 0)),
        ],
        out_specs=[
            pl.BlockSpec((SUBLANES, ROUTER_TILE), lambda i: (0, i)),
            pl.BlockSpec((SUBLANES, ROUTER_TILE), lambda i: (0, i)),
            pl.BlockSpec((ROUTER_TILE, SUBLANES), lambda i: (i, 0)),
            pl.BlockSpec((e, LANES), lambda i: (0, 0)),
        ],
        out_shape=[
            jax.ShapeDtypeStruct((SUBLANES, t), I32),
            jax.ShapeDtypeStruct((SUBLANES, t), I32),
            jax.ShapeDtypeStruct((t, SUBLANES), F32),
            jax.ShapeDtypeStruct((e, LANES), F32),
        ],
        scratch_shapes=[pltpu.VMEM((e, 1), F32)],
        compiler_params=_cparams("arbitrary"),
        name="moe_router",
    )(nx, w_router.T.astype(BF16), router_bias.reshape(e, 1))

    counts = cnt[:, 0].astype(I32)
    ends = jnp.cumsum(counts)
    starts = ends - counts
    dest = pl.pallas_call(
        _dest_kernel,
        grid=(n_rt,),
        in_specs=[
            pl.BlockSpec((e, 1), lambda i: (0, 0)),
            pl.BlockSpec((SUBLANES, ROUTER_TILE), lambda i: (0, i)),
            pl.BlockSpec((SUBLANES, ROUTER_TILE), lambda i: (0, i)),
        ],
        out_specs=pl.BlockSpec((SUBLANES, ROUTER_TILE), lambda i: (0, i)),
        out_shape=jax.ShapeDtypeStruct((SUBLANES, t), I32),
        compiler_params=_cparams("parallel"),
        name="moe_dest",
    )(starts.astype(F32).reshape(e, 1), eidx, rank)
    n_tiles = t // ROW_TILE
    xs, inv = pl.pallas_call(
        functools.partial(_dispatch_kernel, n_tokens=t),
        grid=(n_tiles,),
        in_specs=[
            pl.BlockSpec((SUBLANES, ROW_TILE), lambda i: (0, i), memory_space=pltpu.SMEM),
            pl.BlockSpec((ROW_TILE * ch, LANES), lambda i: (i, 0)),
        ],
        out_specs=[pl.BlockSpec(memory_space=pl.ANY), pl.BlockSpec(memory_space=pl.ANY)],
        out_shape=[jax.ShapeDtypeStruct((n_slots * ch, LANES), F32),
                   jax.ShapeDtypeStruct((n_slots,), I32)],
        scratch_shapes=[pltpu.SMEM((n_slots,), I32), pltpu.SemaphoreType.DMA,
                        pltpu.SemaphoreType.DMA],
        compiler_params=pltpu.CompilerParams(dimension_semantics=("arbitrary",),
                                             vmem_limit_bytes=VMEM_LIMIT, has_side_effects=True),
        name="moe_dispatch",
    )(dest, nx)

    n_et = n_slots // EXPERT_TILE
    pts = jnp.sort(jnp.concatenate([jnp.arange(n_et, dtype=I32) * EXPERT_TILE, starts]))
    lo = pts
    hi = jnp.concatenate([pts[1:], jnp.full((1,), n_slots, I32)])
    tile = jnp.minimum(lo // EXPERT_TILE, n_et - 1)
    expert = jnp.minimum(jnp.sum((ends[None, :] <= lo[:, None]).astype(I32), axis=1), e - 1)
    one = jnp.ones((1,), I32)
    tile_change = (tile[1:] != tile[:-1]).astype(I32)
    first = jnp.concatenate([one, tile_change])
    last = jnp.concatenate([tile_change, one])
    newexp = jnp.concatenate([one, (expert[1:] != expert[:-1]).astype(I32)])
    lo_in = lo - tile * EXPERT_TILE
    hi_in = hi - tile * EXPERT_TILE
    n_visits = n_et + e
    d_exp = w_gate.shape[-1]
    ys = pl.pallas_call(
        _expert_kernel,
        grid_spec=pltpu.PrefetchScalarGridSpec(
            num_scalar_prefetch=7,
            grid=(n_visits,),
            in_specs=[
                pl.BlockSpec((1, 1, EXPERT_TILE),
                             lambda v, ti, ex, *_: (jnp.maximum(ti[v] - 1, 0), 0, 0),
                             memory_space=pltpu.SMEM),
                pl.BlockSpec((1, 1, EXPERT_TILE), lambda v, ti, ex, *_: (ti[v], 0, 0),
                             memory_space=pltpu.SMEM),
                pl.BlockSpec((EXPERT_TILE * ch, LANES), lambda v, ti, ex, *_: (ti[v], 0)),
                pl.BlockSpec((1, 1, d, d_exp), lambda v, ti, ex, *_: (layer, ex[v], 0, 0)),
                pl.BlockSpec((1, 1, d, d_exp), lambda v, ti, ex, *_: (layer, ex[v], 0, 0)),
                pl.BlockSpec((1, 1, d_exp, d), lambda v, ti, ex, *_: (layer, ex[v], 0, 0)),
            ],
            out_specs=pl.BlockSpec(memory_space=pl.ANY),
            scratch_shapes=[pltpu.VMEM((d, d_exp), BF16), pltpu.VMEM((d, d_exp), BF16),
                            pltpu.VMEM((d_exp, d), BF16), pltpu.VMEM((EXPERT_TILE, d), F32),
                            pltpu.VMEM((2, EXPERT_TILE * ch, LANES), F32),
                            pltpu.SemaphoreType.DMA((2,))],
        ),
        out_shape=jax.ShapeDtypeStruct((n_slots * ch, LANES), F32),
        compiler_params=_cparams("arbitrary"),
        name="moe_experts",
    )(tile, expert, lo_in, hi_in, first, last, newexp, inv.reshape(n_et, 1, EXPERT_TILE),
      inv.reshape(n_et, 1, EXPERT_TILE), xs, w_gate, w_up, w_down)

    d_sh = sg.shape[-1]
    slot_specs = [pl.BlockSpec((ROW_TILE * ch, LANES), lambda i, kk=kk: (kk * n_tiles + i, 0))
                  for kk in range(TOP_K)]
    return pl.pallas_call(
        _combine_kernel,
        grid=(n_tiles,),
        in_specs=[
            pl.BlockSpec((ROW_TILE, SUBLANES), lambda i: (i, 0)),
            pl.BlockSpec((ROW_TILE * ch, LANES), lambda i: (i, 0)),
            pl.BlockSpec((ROW_TILE, d), lambda i: (i, 0)),
            pl.BlockSpec((1, 1, d), lambda i: (gate_row_of(i) * 6 + 5, 0, 0)),
            pl.BlockSpec((d, d_sh), lambda i: (0, 0)),
            pl.BlockSpec((d, d_sh), lambda i: (0, 0)),
            pl.BlockSpec((d_sh, d), lambda i: (0, 0)),
        ] + slot_specs,
        out_specs=pl.BlockSpec((ROW_TILE, d), lambda i: (i, 0)),
        out_shape=jax.ShapeDtypeStruct((t, d), F32),
        compiler_params=_cparams("parallel"),
        name="moe_combine",
    )(wcol, nx, h, modv, sg.astype(BF16), su.astype(BF16), sd.astype(BF16), *([ys] * TOP_K))


def _rope_tables(n_lat, n_ctx, head_dim):
    rows = n_lat // GRID_W
    row = jnp.repeat(jnp.arange(rows, dtype=F32), GRID_W)
    col = jnp.tile(jnp.arange(GRID_W, dtype=F32), rows)
    n_freq = head_dim // 4
    inv = ROPE_THETA ** (-jnp.arange(n_freq, dtype=F32) / n_freq)
    ang = jnp.concatenate([row[:, None] * inv, col[:, None] * inv], axis=-1)
    cos = jnp.concatenate([jnp.ones((n_ctx, head_dim // 2), F32), jnp.cos(ang)], axis=0)
    sin = jnp.concatenate([jnp.zeros((n_ctx, head_dim // 2), F32), jnp.sin(ang)], axis=0)
    return cos, sin


def _split_halves_perm(head_dim):
    return np.concatenate([np.arange(0, head_dim, 2), np.arange(1, head_dim, 2)])


def kernel(x, c, ctx, c_ctx, mod_w, mod_b, norm1_g, norm2_g, ev_w_in, ev_w_out, a_ln_g, a_ln_b, a_ws, a_bs, b_q_norm, b_k_norm, b_lam_q1, b_lam_k1, b_lam_q2, b_lam_k2, b_subln, od_w_qkv, od_w_out, c_q_norm, c_k_norm, moe_router, moe_bias, moe_w_gate, moe_w_up, moe_w_down, sh_w_gate, sh_w_up, sh_w_down):
    bsz, n_lat, d = x.shape
    n_ctx = ctx.shape[1]
    depth = mod_w.shape[0]
    assert depth == 2 and n_ctx == ROW_TILE and n_lat % ROW_TILE == 0 and bsz + 1 <= MOD_ROWS
    assert d == SUBLANES * LANES
    n_seq = n_ctx + n_lat
    tpb = n_seq // ROW_TILE
    lpb = n_lat // ROW_TILE
    t_all = bsz * n_seq
    n_tiles = t_all // ROW_TILE
    ctx_row = bsz

    cond = jnp.zeros((MOD_ROWS, d), F32).at[:bsz].set(c).at[ctx_row].set(c_ctx)
    mod = _adaln(cond, mod_w, mod_b)
    modv = [mod[l].reshape(MOD_ROWS * 6, 1, d) for l in range(depth)]

    def row_all(i):
        return jnp.where(i % tpb == 0, ctx_row, i // tpb)

    def mspec(j, row_of):
        return pl.BlockSpec((1, 1, d), lambda i: (row_of(i) * 6 + j, 0, 0))

    def full(shape):
        return pl.BlockSpec(shape, lambda *_: (0,) * len(shape))

    x_spec = pl.BlockSpec((1, ROW_TILE, d), lambda i: (i // tpb, jnp.maximum(i % tpb - 1, 0), 0))
    ctx_spec = pl.BlockSpec((1, ROW_TILE, d), lambda i: (i // tpb, 0, 0))
    ch = d // LANES
    tok_spec = pl.BlockSpec((ROW_TILE * ch, LANES), lambda i: (i, 0))

    lam_init = 0.8 - 0.6 * math.exp(-0.3 * 0)
    p64 = _split_halves_perm(B_HEAD_DIM)
    col_perm = np.concatenate(
        [np.arange(2 * A_WIDTH)]
        + [2 * A_WIDTH + blk * B_HEAD_DIM + p64 for blk in range(2 * B_WIDTH // B_HEAD_DIM)]
        + [np.arange(2 * A_WIDTH + 2 * B_WIDTH, 2 * A_WIDTH + 3 * B_WIDTH)])
    w_in = ev_w_in[0][:, col_perm].astype(BF16)
    even_in = w_in.shape[1]
    cos_b, sin_b = _rope_tables(n_lat, n_ctx, B_HEAD_DIM)
    zeros_b = jnp.zeros_like(sin_b)
    tab_c = jnp.tile(jnp.concatenate([cos_b, cos_b], axis=-1), (1, 2))
    tab_sa = jnp.tile(jnp.concatenate([-sin_b, zeros_b], axis=-1), (1, 2))
    tab_sb = jnp.tile(jnp.concatenate([zeros_b, sin_b], axis=-1), (1, 2))
    qg = jnp.tile(b_q_norm[0][p64], 2).reshape(1, LANES)
    kg = jnp.tile(b_k_norm[0][p64], 2).reshape(1, LANES)
    tab_spec = pl.BlockSpec((ROW_TILE, LANES), lambda i: (i % tpb, 0))
    row_spec = lambda w: pl.BlockSpec((ROW_TILE, w), lambda i: (i, 0))
    uv, q, k, v = pl.pallas_call(
        functools.partial(_even_in_kernel, tiles_per_batch=tpb),
        grid=(n_tiles,),
        in_specs=[x_spec, ctx_spec, mspec(0, row_all), mspec(1, row_all), full((1, d)),
                  full((d, even_in)), full((1, LANES)), full((1, LANES)),
                  tab_spec, tab_spec, tab_spec],
        out_specs=[row_spec(2 * A_WIDTH), row_spec(B_WIDTH), row_spec(B_WIDTH), row_spec(B_WIDTH)],
        out_shape=[jax.ShapeDtypeStruct((t_all, 2 * A_WIDTH), F32),
                   jax.ShapeDtypeStruct((t_all, B_WIDTH), BF16),
                   jax.ShapeDtypeStruct((t_all, B_WIDTH), BF16),
                   jax.ShapeDtypeStruct((t_all, B_WIDTH), BF16)],
        compiler_params=_cparams("parallel"),
        name="even_in",
    )(x, ctx, modv[0], modv[0], norm1_g[0].reshape(1, d), w_in, qg, kg, tab_c, tab_sa, tab_sb)

    lamv = jnp.zeros((SUBLANES, LANES), F32)
    for r, vec in enumerate((b_lam_q1[0], b_lam_k1[0], b_lam_q2[0], b_lam_k2[0])):
        lamv = lamv.at[r, :B_HEAD_DIM].set(vec)
    o = pl.pallas_call(
        functools.partial(_diff_attn_kernel, ctx_len=n_ctx, lam_init=lam_init),
        grid=(bsz, tpb),
        in_specs=[
            pl.BlockSpec((SUBLANES, LANES), lambda b, qi: (0, 0)),
            pl.BlockSpec((ROW_TILE, B_WIDTH), lambda b, qi: (b * tpb + qi, 0)),
            pl.BlockSpec((n_seq, B_WIDTH), lambda b, qi: (b, 0)),
            pl.BlockSpec((n_seq, B_WIDTH), lambda b, qi: (b, 0)),
        ],
        out_specs=pl.BlockSpec((ROW_TILE, B_WIDTH), lambda b, qi: (b * tpb + qi, 0)),
        out_shape=jax.ShapeDtypeStruct((t_all, B_WIDTH), F32),
        compiler_params=_cparams("parallel", "arbitrary"),
        name="diff_attn",
    )(lamv, q, k, v)

    bs_col = jnp.repeat(a_bs[0].T, A_GROUP_DIM, axis=1)
    sub_g = b_subln[0].reshape(1, LANES)
    h1, nx = pl.pallas_call(
        functools.partial(_even_out_kernel, lam_init=lam_init, tiles_per_batch=tpb),
        grid=(n_tiles,),
        in_specs=[row_spec(B_WIDTH), row_spec(2 * A_WIDTH), x_spec, ctx_spec,
                  mspec(2, row_all), mspec(3, row_all), mspec(4, row_all),
                  full((1, LANES)), full((1, A_WIDTH)), full((1, A_WIDTH)),
                  full((A_GROUPS, GMLP_CHUNK, GMLP_CHUNK)), full((GMLP_CHUNK, A_WIDTH)),
                  full((A_WIDTH + B_WIDTH, d)), full((1, d))],
        out_specs=[row_spec(d), tok_spec],
        out_shape=[jax.ShapeDtypeStruct((t_all, d), F32),
                   jax.ShapeDtypeStruct((t_all * ch, LANES), F32)],
        compiler_params=_cparams("parallel"),
        name="even_out",
    )(o, uv, x, ctx, modv[0], modv[0], modv[0], sub_g, a_ln_g[0].reshape(1, A_WIDTH),
      a_ln_b[0].reshape(1, A_WIDTH), a_ws[0].astype(BF16), bs_col,
      ev_w_out[0].astype(BF16), norm2_g[0].reshape(1, d))

    h2 = _moe(nx, h1, modv[0], row_all, 0, moe_router[0], moe_bias[0], moe_w_gate, moe_w_up,
              moe_w_down, sh_w_gate[0], sh_w_up[0], sh_w_down[0])

    p128 = _split_halves_perm(C_HEAD_DIM)
    n_qkv_heads = C_HEADS + 2 * C_KV_HEADS
    col_perm = np.concatenate(
        [blk * C_HEAD_DIM + p128 for blk in range(C_HEADS + C_KV_HEADS)]
        + [np.arange((C_HEADS + C_KV_HEADS) * C_HEAD_DIM, n_qkv_heads * C_HEAD_DIM)])
    w_qkv = od_w_qkv[0][:, col_perm].astype(BF16)
    cos_c, sin_c = _rope_tables(n_lat, n_ctx, C_HEAD_DIM)
    tab_c1 = jnp.concatenate([cos_c, cos_c], axis=-1)
    tab_s1 = jnp.concatenate([-sin_c, sin_c], axis=-1)
    qg1 = c_q_norm[0][p128].reshape(1, LANES)
    kg1 = c_k_norm[0][p128].reshape(1, LANES)
    nq = C_HEADS * C_HEAD_DIM
    nkv = C_KV_HEADS * C_HEAD_DIM
    q1, k1, v1 = pl.pallas_call(
        _odd_in_kernel,
        grid=(n_tiles,),
        in_specs=[row_spec(d), mspec(0, row_all), mspec(1, row_all), full((1, d)),
                  full((d, nq + 2 * nkv)), full((1, LANES)), full((1, LANES)), tab_spec, tab_spec],
        out_specs=[row_spec(nq), row_spec(nkv), row_spec(nkv)],
        out_shape=[jax.ShapeDtypeStruct((t_all, nq), BF16),
                   jax.ShapeDtypeStruct((t_all, nkv), BF16),
                   jax.ShapeDtypeStruct((t_all, nkv), BF16)],
        compiler_params=_cparams("parallel"),
        name="odd_in",
    )(h2, modv[1], modv[1], norm1_g[1].reshape(1, d), w_qkv, qg1, kg1, tab_c1, tab_s1)

    t_lat = bsz * n_lat
    grp = C_HEADS // C_KV_HEADS
    o1 = pl.pallas_call(
        _gqa_kernel,
        grid=(bsz, C_KV_HEADS, lpb),
        in_specs=[
            pl.BlockSpec((ROW_TILE, grp * LANES), lambda b, n, qi: (b * tpb + 1 + qi, n)),
            pl.BlockSpec((n_seq, LANES), lambda b, n, qi: (b, n)),
            pl.BlockSpec((n_seq, LANES), lambda b, n, qi: (b, n)),
        ],
        out_specs=pl.BlockSpec((ROW_TILE, grp * LANES), lambda b, n, qi: (b * lpb + qi, n)),
        out_shape=jax.ShapeDtypeStruct((t_lat, nq), BF16),
        compiler_params=_cparams("parallel", "parallel", "arbitrary"),
        name="gqa_attn",
    )(q1, k1, v1)

    def row_lat(i):
        return i // lpb

    lat_tiles = t_lat // ROW_TILE
    hx, nx1 = pl.pallas_call(
        _odd_out_kernel,
        grid=(lat_tiles,),
        in_specs=[row_spec(nq),
                  pl.BlockSpec((ROW_TILE, d), lambda i: ((i // lpb) * tpb + 1 + i % lpb, 0)),
                  mspec(2, row_lat), mspec(3, row_lat), mspec(4, row_lat),
                  full((nq, d)), full((1, d))],
        out_specs=[row_spec(d), tok_spec],
        out_shape=[jax.ShapeDtypeStruct((t_lat, d), F32),
                   jax.ShapeDtypeStruct((t_lat * ch, LANES), F32)],
        compiler_params=_cparams("parallel"),
        name="odd_out",
    )(o1, h2, modv[1], modv[1], modv[1], od_w_out[0].astype(BF16), norm2_g[1].reshape(1, d))

    out = _moe(nx1, hx, modv[1], row_lat, 1, moe_router[1], moe_bias[1], moe_w_gate, moe_w_up,
               moe_w_down, sh_w_gate[1], sh_w_up[1], sh_w_down[1])
    return out.reshape(bsz, n_lat, d)
```

```python
import functools
import math

import numpy as np
import jax
import jax.numpy as jnp
from jax import lax
from jax.experimental import pallas as pl
from jax.experimental.pallas import tpu as pltpu

F32 = jnp.float32
BF16 = jnp.bfloat16
I32 = jnp.int32

GRID_W = 64
EPS = 1e-6
ROPE_THETA = 10000.0
A_GROUPS = 4
A_GROUP_DIM = 128
A_WIDTH = A_GROUPS * A_GROUP_DIM
GMLP_CHUNK = 128
B_HEADS = 4
B_HEAD_DIM = 64
B_WIDTH = B_HEADS * 2 * B_HEAD_DIM
C_HEADS = 8
C_KV_HEADS = 2
C_HEAD_DIM = 128
N_EXPERTS = 64
TOP_K = 6
N_GROUPS = 8
TOPK_GROUPS = 4
ROUTE_SCALE = 2.5
LOG2E = math.log2(math.e)

LANES = 128
SUBLANES = 8
ROW_TILE = 256
ROUTER_TILE = 512
EXPERT_TILE = 512
EXPERT_PIECE = 512
PROJ_BLOCK = 256
ATT_CHUNK_ROWS = 256
MOD_ROWS = 24
VMEM_LIMIT = 56 * 1024 * 1024

NT_DIMS = (((1,), (1,)), ((), ()))


def _cparams(*sem):
    return pltpu.CompilerParams(dimension_semantics=sem, vmem_limit_bytes=VMEM_LIMIT)


def _rms(x, g):
    return x * lax.rsqrt(jnp.mean(x * x, axis=-1, keepdims=True) + EPS) * g


def _norm_mod(h, g, shift, scale):
    return _rms(h, g) * (1.0 + scale) + shift


def _gelu(x):
    return 0.5 * x * (1.0 + lax.erf(x * np.float32(math.sqrt(0.5))))


def _silu(x):
    return x * jax.nn.sigmoid(x)


def _bdot(a, b):
    return jnp.dot(a.astype(BF16), b.astype(BF16), preferred_element_type=F32)


def _from_token_tiles(ref, rows, d, row0=0):
    ch = d // LANES
    groups = []
    for g in range(row0 // SUBLANES, (row0 + rows) // SUBLANES):
        groups.append(jnp.concatenate(
            [ref[pl.ds(g * SUBLANES * ch + j, SUBLANES, stride=ch), :] for j in range(ch)], axis=-1))
    return jnp.concatenate(groups, axis=0)


def _to_token_tiles(ref, val):
    rows, d = val.shape
    ch = d // LANES
    for g in range(rows // SUBLANES):
        for j in range(ch):
            ref[pl.ds(g * SUBLANES * ch + j, SUBLANES, stride=ch), :] = (
                val[g * SUBLANES:(g + 1) * SUBLANES, j * LANES:(j + 1) * LANES])


def _adaln_kernel(c_ref, w_ref, b_ref, o_ref):
    o_ref[0] = _bdot(_silu(c_ref[...]), w_ref[0]) + b_ref[0]


def _adaln(cond, mod_w, mod_b):
    depth, d, d6 = mod_w.shape
    tn = d6 // 4
    return pl.pallas_call(
        _adaln_kernel,
        grid=(depth, d6 // tn),
        in_specs=[
            pl.BlockSpec((MOD_ROWS, d), lambda l, j: (0, 0)),
            pl.BlockSpec((1, d, tn), lambda l, j: (l, 0, j)),
            pl.BlockSpec((1, 1, tn), lambda l, j: (l, 0, j)),
        ],
        out_specs=pl.BlockSpec((1, MOD_ROWS, tn), lambda l, j: (l, 0, j)),
        out_shape=jax.ShapeDtypeStruct((depth, MOD_ROWS, d6), F32),
        compiler_params=_cparams("parallel", "parallel"),
        name="adaln",
    )(cond, mod_w, mod_b.reshape(depth, 1, d6))


def _pick_stream(x_ref, ctx_ref, tiles_per_batch):
    is_ctx = pl.program_id(0) % tiles_per_batch == 0
    return jnp.where(is_ctx, ctx_ref[0], x_ref[0])


def _even_in_kernel(x_ref, ctx_ref, sh_ref, sc_ref, g_ref, w_ref, qg_ref, kg_ref, c_ref, sa_ref,
                    sb_ref, uv_ref, q_ref, k_ref, v_ref, *, tiles_per_batch):
    h = _pick_stream(x_ref, ctx_ref, tiles_per_batch)
    n = _norm_mod(h, g_ref[...], sh_ref[0], sc_ref[0]).astype(BF16)

    def cols(c0):
        return jnp.dot(n, w_ref[:, c0:c0 + PROJ_BLOCK], preferred_element_type=F32)

    for c0 in range(0, 2 * A_WIDTH, PROJ_BLOCK):
        uv_ref[:, c0:c0 + PROJ_BLOCK] = _gelu(cols(c0))
    cos, sa, sb = c_ref[...], sa_ref[...], sb_ref[...]
    lane = lax.broadcasted_iota(I32, cos.shape, 1)
    low = lane < B_HEAD_DIM

    def head_pair(x, gain, scale):
        sq = x * x
        s_lo = jnp.sum(jnp.where(low, sq, 0.0), axis=-1, keepdims=True)
        s_hi = jnp.sum(jnp.where(low, 0.0, sq), axis=-1, keepdims=True)
        ms = jnp.where(low, s_lo, s_hi) * np.float32(1.0 / B_HEAD_DIM)
        y = x * lax.rsqrt(ms + EPS) * gain
        y = (y * cos + pltpu.roll(y, LANES - B_HEAD_DIM // 2, 1) * sa
             + pltpu.roll(y, B_HEAD_DIM // 2, 1) * sb)
        if scale is not None:
            y = y * scale
        return y.astype(BF16)

    q0 = 2 * A_WIDTH
    k0 = q0 + B_WIDTH
    v0 = k0 + B_WIDTH
    per = PROJ_BLOCK // LANES
    for c0 in range(0, B_WIDTH, PROJ_BLOCK):
        pq = cols(q0 + c0)
        pk = cols(k0 + c0)
        for j in range(per):
            sl = slice(c0 + j * LANES, c0 + (j + 1) * LANES)
            q_ref[:, sl] = head_pair(pq[:, j * LANES:(j + 1) * LANES], qg_ref[...],
                                     np.float32(B_HEAD_DIM ** -0.5))
            k_ref[:, sl] = head_pair(pk[:, j * LANES:(j + 1) * LANES], kg_ref[...], None)
        v_ref[:, c0:c0 + PROJ_BLOCK] = cols(v0 + c0).astype(BF16)


def _odd_in_kernel(h_ref, sh_ref, sc_ref, g_ref, w_ref, qg_ref, kg_ref, c_ref, s_ref,
                   q_ref, k_ref, v_ref):
    n = _norm_mod(h_ref[...], g_ref[...], sh_ref[0], sc_ref[0])
    p = jnp.dot(n.astype(BF16), w_ref[...], preferred_element_type=F32)
    cos, sin = c_ref[...], s_ref[...]

    def head(x, gain, scale):
        y = _rms(x, gain)
        y = y * cos + pltpu.roll(y, C_HEAD_DIM // 2, 1) * sin
        if scale is not None:
            y = y * scale
        return y.astype(BF16)

    nq = C_HEADS * C_HEAD_DIM
    nkv = C_KV_HEADS * C_HEAD_DIM
    for j in range(C_HEADS):
        q_ref[:, j * LANES:(j + 1) * LANES] = head(p[:, j * LANES:(j + 1) * LANES], qg_ref[...],
                                                   None)
    for j in range(C_KV_HEADS):
        k_ref[:, j * LANES:(j + 1) * LANES] = head(
            p[:, nq + j * LANES:nq + (j + 1) * LANES], kg_ref[...], None)
    v_ref[...] = p[:, nq + nkv:nq + 2 * nkv].astype(BF16)


def _diff_attn_kernel(lam_ref, q_ref, k_ref, v_ref, o_ref, *, ctx_len, lam_init):
    lv = lam_ref[...]
    lam = (jnp.exp(jnp.sum(lv[0:1] * lv[1:2], axis=-1, keepdims=True))
           - jnp.exp(jnp.sum(lv[2:3] * lv[3:4], axis=-1, keepdims=True)) + np.float32(lam_init))
    low = lax.broadcasted_iota(I32, (q_ref.shape[0], LANES), 1) < B_HEAD_DIM

    def softmax(qm, k, scale):
        s = lax.dot_general(qm, k, NT_DIMS, preferred_element_type=F32)
        p = jnp.exp(s - jnp.max(s, axis=-1, keepdims=True))
        return p * (scale / jnp.sum(p, axis=-1, keepdims=True))

    def attend(n_keys):
        for hh in range(q_ref.shape[1] // LANES):
            cs = slice(hh * LANES, (hh + 1) * LANES)
            q = q_ref[:, cs]
            zero = jnp.zeros_like(q)
            k = k_ref[0:n_keys, cs]
            a = (softmax(jnp.where(low, q, zero), k, 1.0)
                 - softmax(jnp.where(low, zero, q), k, lam))
            o_ref[:, cs] = jnp.dot(a.astype(BF16), v_ref[0:n_keys, cs], preferred_element_type=F32)

    is_ctx = pl.program_id(1) == 0

    @pl.when(is_ctx)
    def _():
        attend(ctx_len)

    @pl.when(jnp.logical_not(is_ctx))
    def _():
        attend(k_ref.shape[0])


def _gqa_kernel(q_ref, k_ref, v_ref, o_ref):
    q = q_ref[...]
    rows = q.shape[0]
    grp = q.shape[1] // LANES
    c = np.float32(C_HEAD_DIM ** -0.5 * LOG2E)
    k = k_ref[...]
    v = v_ref[...]
    for g in range(grp):
        for r0 in range(0, rows, ATT_CHUNK_ROWS):
            qs = q[r0:r0 + ATT_CHUNK_ROWS, g * LANES:(g + 1) * LANES]
            s = lax.dot_general(qs, k, NT_DIMS, preferred_element_type=F32)
            p = jnp.exp2((s - jnp.max(s, axis=-1, keepdims=True)) * c)
            l = jnp.sum(p, axis=-1, keepdims=True)
            o = jnp.dot(p.astype(BF16), v, preferred_element_type=F32) / l
            o_ref[r0:r0 + ATT_CHUNK_ROWS, g * LANES:(g + 1) * LANES] = o.astype(o_ref.dtype)


def _even_out_kernel(o_ref, uv_ref, x_ref, ctx_ref, gate_ref, sh_ref, sc_ref, sub_ref, lng_ref,
                     lnb_ref, ws_ref, bs_ref, w_ref, g2_ref, h1_ref, nx_ref, *, lam_init,
                     tiles_per_batch):
    o = o_ref[...]
    uv = uv_ref[...]
    u = uv[:, :A_WIDTH]
    v = uv[:, A_WIDTH:]
    mu = jnp.mean(v, axis=-1, keepdims=True)
    var = jnp.mean(jnp.square(v - mu), axis=-1, keepdims=True)
    vn = ((v - mu) * lax.rsqrt(var + EPS) * lng_ref[...] + lnb_ref[...]).astype(BF16)
    rows = o.shape[0]
    parts = []
    for c in range(rows // GMLP_CHUNK):
        rs = slice(c * GMLP_CHUNK, (c + 1) * GMLP_CHUNK)
        for g in range(A_GROUPS):
            cs = slice(g * A_GROUP_DIM, (g + 1) * A_GROUP_DIM)
            mixed = jnp.dot(ws_ref[g], vn[rs, cs], preferred_element_type=F32) + bs_ref[:, cs]
            parts.append((c, g, u[rs, cs] * mixed))
    a_rows = [jnp.concatenate([p for (c2, _, p) in parts if c2 == c], axis=-1)
              for c in range(rows // GMLP_CHUNK)]
    a = jnp.concatenate(a_rows, axis=0)
    heads = []
    for hh in range(B_HEADS):
        oh = o[:, hh * LANES:(hh + 1) * LANES]
        heads.append(_rms(oh, sub_ref[...]) * np.float32(1.0 - lam_init))
    cat = jnp.concatenate([a] + heads, axis=-1).astype(BF16)
    y = jnp.dot(cat, w_ref[...], preferred_element_type=F32)
    h1 = _pick_stream(x_ref, ctx_ref, tiles_per_batch) + gate_ref[0] * y
    h1_ref[...] = h1
    _to_token_tiles(nx_ref, _norm_mod(h1, g2_ref[...], sh_ref[0], sc_ref[0]))


def _odd_out_kernel(o_ref, h_ref, gate_ref, sh_ref, sc_ref, w_ref, g2_ref, h1_ref, nx_ref):
    y = jnp.dot(o_ref[...], w_ref[...], preferred_element_type=F32)
    h1 = h_ref[...] + gate_ref[0] * y
    h1_ref[...] = h1
    _to_token_tiles(nx_ref, _norm_mod(h1, g2_ref[...], sh_ref[0], sc_ref[0]))


def _rows_to_block(rows, dtype):
    n = rows[0].shape[1]
    rio = lax.broadcasted_iota(I32, (SUBLANES, n), 0)
    out = jnp.zeros((SUBLANES, n), dtype)
    for r, row in enumerate(rows):
        out = jnp.where(rio == r, jnp.broadcast_to(row.astype(dtype), (SUBLANES, n)), out)
    return out


def _transpose_block(xt):
    n = xt.shape[1]
    eye = jnp.where(lax.broadcasted_iota(I32, (n, n), 0) == lax.broadcasted_iota(I32, (n, n), 1),
                    1.0, 0.0).astype(BF16)
    acc = jnp.zeros((n, SUBLANES), F32)
    rem = xt
    for _ in range(3):
        part = rem.astype(BF16)
        acc = acc + lax.dot_general(eye, part, NT_DIMS, preferred_element_type=F32)
        rem = rem - part.astype(F32)
    return acc


def _router_kernel(x_ref, wr_ref, b_ref, eidx_ref, rank_ref, wcol_ref, cnt_ref, run_ref):
    @pl.when(pl.program_id(0) == 0)
    def _():
        run_ref[...] = jnp.zeros_like(run_ref)

    per = N_EXPERTS // N_GROUPS
    d = wr_ref.shape[1]
    x = _from_token_tiles(x_ref, x_ref.shape[0] * LANES // d, d)
    logits = lax.dot_general(wr_ref[...], x.astype(BF16), NT_DIMS,
                             preferred_element_type=F32)
    scores = jax.nn.sigmoid(logits)
    sel = scores + b_ref[...]
    tm = sel.shape[1]
    neg = np.float32(-np.inf)
    jio = lax.broadcasted_iota(I32, (per, tm), 0).astype(F32)
    gio = lax.broadcasted_iota(I32, (N_GROUPS, tm), 0).astype(F32)

    def rmax(x):
        return jnp.max(x, axis=0, keepdims=True)

    def rmin(x):
        return jnp.min(x, axis=0, keepdims=True)

    sel_g = [sel[g * per:(g + 1) * per, :] for g in range(N_GROUPS)]
    sc_g = [scores[g * per:(g + 1) * per, :] for g in range(N_GROUPS)]
    gs = jnp.zeros((N_GROUPS, tm), F32)
    for g in range(N_GROUPS):
        m1 = rmax(sel_g[g])
        i1 = rmin(jnp.where(sel_g[g] == m1, jio, np.float32(per)))
        m2 = rmax(jnp.where(jio == i1, neg, sel_g[g]))
        gs = jnp.where(gio == np.float32(g), jnp.broadcast_to(m1 + m2, gs.shape), gs)
    gsel = jnp.zeros((N_GROUPS, tm), I32)
    for _ in range(TOPK_GROUPS):
        m = rmax(gs)
        idx = rmin(jnp.where(gs == m, gio, np.float32(N_GROUPS)))
        hit = gio == idx
        gsel = jnp.where(hit, 1, gsel)
        gs = jnp.where(hit, neg, gs)
    masked = [jnp.where(jnp.broadcast_to(gsel[g:g + 1, :], (per, tm)) == 1, sel_g[g], neg)
              for g in range(N_GROUPS)]
    eio = [jio + np.float32(g * per) for g in range(N_GROUPS)]
    e_rows, w_rows, hits = [], [], []
    for _ in range(TOP_K):
        m = masked[0]
        for g in range(1, N_GROUPS):
            m = jnp.maximum(m, masked[g])
        m = rmax(m)
        cand = jnp.where(masked[0] == m, eio[0], np.float32(N_EXPERTS))
        for g in range(1, N_GROUPS):
            cand = jnp.minimum(cand, jnp.where(masked[g] == m, eio[g], np.float32(N_EXPERTS)))
        idx = rmin(cand)
        hit = [eio[g] == idx for g in range(N_GROUPS)]
        wsel = jnp.where(hit[0], sc_g[0], 0.0)
        for g in range(1, N_GROUPS):
            wsel = wsel + jnp.where(hit[g], sc_g[g], 0.0)
        masked = [jnp.where(hit[g], neg, masked[g]) for g in range(N_GROUPS)]
        e_rows.append(idx)
        w_rows.append(jnp.sum(wsel, axis=0, keepdims=True))
        hits.append(hit)
    wsum = w_rows[0]
    for r in w_rows[1:]:
        wsum = wsum + r
    w_rows = [r / wsum * np.float32(ROUTE_SCALE) for r in w_rows]
    onehot = []
    for g in range(N_GROUPS):
        any_hit = hits[0][g]
        for kk in range(1, TOP_K):
            any_hit = jnp.logical_or(any_hit, hits[kk][g])
        onehot.append(jnp.where(any_hit, 1.0, 0.0))
    mt = jnp.concatenate(onehot, axis=0)
    before = (lax.broadcasted_iota(I32, (tm, tm), 0) < lax.broadcasted_iota(I32, (tm, tm), 1))
    prefix = jnp.dot(mt.astype(BF16), jnp.where(before, 1.0, 0.0).astype(BF16),
                     preferred_element_type=F32)
    pos = prefix + run_ref[...]
    r_rows = []
    for kk in range(TOP_K):
        acc = jnp.where(hits[kk][0], pos[0:per, :], 0.0)
        for g in range(1, N_GROUPS):
            acc = acc + jnp.where(hits[kk][g], pos[g * per:(g + 1) * per, :], 0.0)
        r_rows.append(jnp.sum(acc, axis=0, keepdims=True))
    run = run_ref[...] + jnp.sum(mt, axis=1, keepdims=True)
    run_ref[...] = run
    eidx_ref[...] = _rows_to_block(e_rows, I32)
    rank_ref[...] = _rows_to_block(r_rows, I32)
    wcol_ref[...] = _transpose_block(_rows_to_block(w_rows, F32))
    cnt_ref[...] = jnp.broadcast_to(run, cnt_ref.shape)


def _dest_kernel(start_ref, eidx_ref, rank_ref, dest_ref):
    per = N_EXPERTS // N_GROUPS
    eidx = eidx_ref[...]
    tm = eidx.shape[1]
    jio = lax.broadcasted_iota(I32, (per, tm), 0)
    rows = []
    for kk in range(TOP_K):
        e = jnp.broadcast_to(eidx[kk:kk + 1, :], (per, tm))
        acc = jnp.zeros((per, tm), F32)
        for g in range(N_GROUPS):
            st = jnp.broadcast_to(start_ref[g * per:(g + 1) * per, :], (per, tm))
            acc = acc + jnp.where(jio + g * per == e, st, 0.0)
        rows.append(jnp.sum(acc, axis=0, keepdims=True))
    dest_ref[...] = _rows_to_block(rows, I32) + rank_ref[...]


def _row_copy(src, s_row, dst, d_row, sem):
    def tile_start(row):
        start = row * SUBLANES
        return start if isinstance(row, int) else pl.multiple_of(start, SUBLANES)

    s0 = tile_start(s_row)
    d0 = tile_start(d_row)
    return pltpu.make_async_copy(src.at[pl.ds(s0, SUBLANES)], dst.at[pl.ds(d0, SUBLANES)], sem)


def _dispatch_kernel(dest_ref, x_ref, xs_ref, sem):
    rows = x_ref.shape[0] // SUBLANES

    def copies(t):
        return [_row_copy(x_ref, t, xs_ref, dest_ref[kk, t], sem) for kk in range(TOP_K)]

    def issue(t, carry):
        for kk, cp in enumerate(copies(t)):
            cp.start(priority=kk % 2)
        return carry

    def drain(t, carry):
        for cp in copies(t):
            cp.wait()
        return carry

    lax.fori_loop(0, rows, issue, 0)
    lax.fori_loop(0, rows, drain, 0)


def _expert_kernel(tile_s, exp_s, lo_s, hi_s, first_s, last_s, new_s, inv_prev_ref, inv_ref,
                   xs_ref, wg_ref, wu_ref, wd_ref, ys_ref, wg_b, wu_b, wd_b, acc, stage, sem):
    v = pl.program_id(0)
    tile = tile_s[v]
    rows, d = acc.shape

    @pl.when(new_s[v] == 1)
    def _():
        wg_b[...] = wg_ref[0, 0].astype(BF16)
        wu_b[...] = wu_ref[0, 0].astype(BF16)
        wd_b[...] = wd_ref[0, 0].astype(BF16)

    @pl.when(first_s[v] == 1)
    def _():
        acc[...] = jnp.zeros_like(acc)

    lo = lo_s[v]
    hi = hi_s[v]
    nonempty = hi > lo
    flush_prev = jnp.logical_and(first_s[v] == 1, tile >= 1)

    def compute():
        x = _from_token_tiles(xs_ref, rows, d).astype(BF16)
        g = jnp.dot(x, wg_b[...], preferred_element_type=F32)
        u = jnp.dot(x, wu_b[...], preferred_element_type=F32)
        y = jnp.dot((_silu(g) * u).astype(BF16), wd_b[...], preferred_element_type=F32)
        row = lax.broadcasted_iota(I32, (rows, 1), 0)
        mine = jnp.logical_and(row >= lo, row < hi)
        acc[...] = jnp.where(mine, y, acc[...])

    def start_scatter(idx_ref, slot, unrolled):
        def start(r, par):
            _row_copy(stage.at[slot], r, ys_ref, idx_ref[0, 0, r], sem.at[slot]).start(priority=par)

        if unrolled:
            for r in range(rows):
                start(r, r % 2)
        else:
            def body(r2, carry):
                for par in range(2):
                    start(r2 * 2 + par, par)
                return carry
            lax.fori_loop(0, rows // 2, body, 0)

    def wait_scatter(slot):
        pltpu.make_async_copy(stage.at[slot], stage.at[slot], sem.at[slot]).wait()

    for slot in range(2):
        prev_here = jnp.logical_and(flush_prev, (tile + 1) % 2 == slot)

        @pl.when(jnp.logical_and(prev_here, nonempty))
        def _(slot=slot):
            start_scatter(inv_prev_ref, slot, True)
            compute()

        @pl.when(jnp.logical_and(prev_here, jnp.logical_not(nonempty)))
        def _(slot=slot):
            start_scatter(inv_prev_ref, slot, False)

    @pl.when(jnp.logical_and(nonempty, jnp.logical_not(flush_prev)))
    def _():
        compute()

    is_final = v == pl.num_programs(0) - 1
    for slot in range(2):
        @pl.when(jnp.logical_and(last_s[v] == 1, tile % 2 == slot))
        def _(slot=slot):
            @pl.when(tile >= 2)
            def _():
                wait_scatter(slot)

            _to_token_tiles(stage.at[slot], acc[...])

            @pl.when(is_final)
            def _():
                start_scatter(inv_ref, slot, False)

    @pl.when(is_final)
    def _():
        for slot in range(2):
            @pl.when(jnp.logical_or(tile >= 1, tile % 2 == slot))
            def _(slot=slot):
                wait_scatter(slot)


def _combine_kernel(wcol_ref, x_ref, h_ref, gate_ref, sg_ref, su_ref, sd_ref, *refs):
    y_refs, o_ref = refs[:TOP_K], refs[TOP_K]
    rows, d = h_ref.shape
    x = _from_token_tiles(x_ref, rows, d).astype(BF16)
    g = jnp.dot(x, sg_ref[...], preferred_element_type=F32)
    u = jnp.dot(x, su_ref[...], preferred_element_type=F32)
    acc = jnp.dot((_silu(g) * u).astype(BF16), sd_ref[...], preferred_element_type=F32)
    wcol = wcol_ref[...]
    for kk in range(TOP_K):
        acc = acc + _from_token_tiles(y_refs[kk], rows, d) * wcol[:, kk:kk + 1]
    o_ref[...] = h_ref[...] + gate_ref[0] * acc


def _moe(nx, h, modv, gate_row_of, layer, w_router, router_bias, w_gate, w_up, w_down, sg, su, sd):
    t, d = h.shape
    ch = d // LANES
    n_slots = t * TOP_K
    n_rt = t // ROUTER_TILE
    e = N_EXPERTS
    eidx, rank, wcol, cnt = pl.pallas_call(
        _router_kernel,
        grid=(n_rt,),
        in_specs=[
            pl.BlockSpec((ROUTER_TILE * ch, LANES), lambda i: (i, 0)),
            pl.BlockSpec((e, d), lambda i: (0, 0)),
            pl.BlockSpec((e, 1), lambda i: (0, 0)),
        ],
        out_specs=[
            pl.BlockSpec((SUBLANES, ROUTER_TILE), lambda i: (0, i)),
            pl.BlockSpec((SUBLANES, ROUTER_TILE), lambda i: (0, i)),
            pl.BlockSpec((ROUTER_TILE, SUBLANES), lambda i: (i, 0)),
            pl.BlockSpec((e, LANES), lambda i: (0, 0)),
        ],
        out_shape=[
            jax.ShapeDtypeStruct((SUBLANES, t), I32),
            jax.ShapeDtypeStruct((SUBLANES, t), I32),
            jax.ShapeDtypeStruct((t, SUBLANES), F32),
            jax.ShapeDtypeStruct((e, LANES), F32),
        ],
        scratch_shapes=[pltpu.VMEM((e, 1), F32)],
        compiler_params=_cparams("arbitrary"),
        name="moe_router",
    )(nx, w_router.T.astype(BF16), router_bias.reshape(e, 1))

    counts = cnt[:, 0].astype(I32)
    ends = jnp.cumsum(counts)
    starts = ends - counts
    dest = pl.pallas_call(
        _dest_kernel,
        grid=(n_rt,),
        in_specs=[
            pl.BlockSpec((e, 1), lambda i: (0, 0)),
            pl.BlockSpec((SUBLANES, ROUTER_TILE), lambda i: (0, i)),
            pl.BlockSpec((SUBLANES, ROUTER_TILE), lambda i: (0, i)),
        ],
        out_specs=pl.BlockSpec((SUBLANES, ROUTER_TILE), lambda i: (0, i)),
        out_shape=jax.ShapeDtypeStruct((SUBLANES, t), I32),
        compiler_params=_cparams("parallel"),
        name="moe_dest",
    )(starts.astype(F32).reshape(e, 1), eidx, rank)
    n_tiles = t // ROW_TILE
    inv = jnp.argsort(dest[:TOP_K].reshape(-1)).astype(I32)
    xs = pl.pallas_call(
        _dispatch_kernel,
        grid=(n_tiles,),
        in_specs=[
            pl.BlockSpec((SUBLANES, ROW_TILE), lambda i: (0, i), memory_space=pltpu.SMEM),
            pl.BlockSpec((ROW_TILE * ch, LANES), lambda i: (i, 0)),
        ],
        out_specs=pl.BlockSpec(memory_space=pl.ANY),
        out_shape=jax.ShapeDtypeStruct((n_slots * ch, LANES), F32),
        scratch_shapes=[pltpu.SemaphoreType.DMA],
        compiler_params=pltpu.CompilerParams(dimension_semantics=("arbitrary",),
                                             vmem_limit_bytes=VMEM_LIMIT, has_side_effects=True),
        name="moe_dispatch",
    )(dest, nx)

    n_et = n_slots // EXPERT_TILE
    pts = jnp.sort(jnp.concatenate([jnp.arange(n_et, dtype=I32) * EXPERT_TILE, starts]))
    lo = pts
    hi = jnp.concatenate([pts[1:], jnp.full((1,), n_slots, I32)])
    tile = jnp.minimum(lo // EXPERT_TILE, n_et - 1)
    expert = jnp.minimum(jnp.sum((ends[None, :] <= lo[:, None]).astype(I32), axis=1), e - 1)
    one = jnp.ones((1,), I32)
    tile_change = (tile[1:] != tile[:-1]).astype(I32)
    first = jnp.concatenate([one, tile_change])
    last = jnp.concatenate([tile_change, one])
    newexp = jnp.concatenate([one, (expert[1:] != expert[:-1]).astype(I32)])
    lo_in = lo - tile * EXPERT_TILE
    hi_in = hi - tile * EXPERT_TILE
    n_visits = n_et + e
    d_exp = w_gate.shape[-1]
    ys = pl.pallas_call(
        _expert_kernel,
        grid_spec=pltpu.PrefetchScalarGridSpec(
            num_scalar_prefetch=7,
            grid=(n_visits,),
            in_specs=[
                pl.BlockSpec((1, 1, EXPERT_TILE),
                             lambda v, ti, ex, *_: (jnp.maximum(ti[v] - 1, 0), 0, 0),
                             memory_space=pltpu.SMEM),
                pl.BlockSpec((1, 1, EXPERT_TILE), lambda v, ti, ex, *_: (ti[v], 0, 0),
                             memory_space=pltpu.SMEM),
                pl.BlockSpec((EXPERT_TILE * ch, LANES), lambda v, ti, ex, *_: (ti[v], 0)),
                pl.BlockSpec((1, 1, d, d_exp), lambda v, ti, ex, *_: (layer, ex[v], 0, 0)),
                pl.BlockSpec((1, 1, d, d_exp), lambda v, ti, ex, *_: (layer, ex[v], 0, 0)),
                pl.BlockSpec((1, 1, d_exp, d), lambda v, ti, ex, *_: (layer, ex[v], 0, 0)),
            ],
            out_specs=pl.BlockSpec(memory_space=pl.ANY),
            scratch_shapes=[pltpu.VMEM((d, d_exp), BF16), pltpu.VMEM((d, d_exp), BF16),
                            pltpu.VMEM((d_exp, d), BF16), pltpu.VMEM((EXPERT_TILE, d), F32),
                            pltpu.VMEM((2, EXPERT_TILE * ch, LANES), F32),
                            pltpu.SemaphoreType.DMA((2,))],
        ),
        out_shape=jax.ShapeDtypeStruct((n_slots * ch, LANES), F32),
        compiler_params=_cparams("arbitrary"),
        name="moe_experts",
    )(tile, expert, lo_in, hi_in, first, last, newexp, inv.reshape(n_et, 1, EXPERT_TILE),
      inv.reshape(n_et, 1, EXPERT_TILE), xs, w_gate, w_up, w_down)

    d_sh = sg.shape[-1]
    slot_specs = [pl.BlockSpec((ROW_TILE * ch, LANES), lambda i, kk=kk: (kk * n_tiles + i, 0))
                  for kk in range(TOP_K)]
    return pl.pallas_call(
        _combine_kernel,
        grid=(n_tiles,),
        in_specs=[
            pl.BlockSpec((ROW_TILE, SUBLANES), lambda i: (i, 0)),
            pl.BlockSpec((ROW_TILE * ch, LANES), lambda i: (i, 0)),
            pl.BlockSpec((ROW_TILE, d), lambda i: (i, 0)),
            pl.BlockSpec((1, 1, d), lambda i: (gate_row_of(i) * 6 + 5, 0, 0)),
            pl.BlockSpec((d, d_sh), lambda i: (0, 0)),
            pl.BlockSpec((d, d_sh), lambda i: (0, 0)),
            pl.BlockSpec((d_sh, d), lambda i: (0, 0)),
        ] + slot_specs,
        out_specs=pl.BlockSpec((ROW_TILE, d), lambda i: (i, 0)),
        out_shape=jax.ShapeDtypeStruct((t, d), F32),
        compiler_params=_cparams("parallel"),
        name="moe_combine",
    )(wcol, nx, h, modv, sg.astype(BF16), su.astype(BF16), sd.astype(BF16), *([ys] * TOP_K))


def _rope_tables(n_lat, n_ctx, head_dim):
    rows = n_lat // GRID_W
    row = jnp.repeat(jnp.arange(rows, dtype=F32), GRID_W)
    col = jnp.tile(jnp.arange(GRID_W, dtype=F32), rows)
    n_freq = head_dim // 4
    inv = ROPE_THETA ** (-jnp.arange(n_freq, dtype=F32) / n_freq)
    ang = jnp.concatenate([row[:, None] * inv, col[:, None] * inv], axis=-1)
    cos = jnp.concatenate([jnp.ones((n_ctx, head_dim // 2), F32), jnp.cos(ang)], axis=0)
    sin = jnp.concatenate([jnp.zeros((n_ctx, head_dim // 2), F32), jnp.sin(ang)], axis=0)
    return cos, sin


def _split_halves_perm(head_dim):
    return np.concatenate([np.arange(0, head_dim, 2), np.arange(1, head_dim, 2)])


def kernel(x, c, ctx, c_ctx, mod_w, mod_b, norm1_g, norm2_g, ev_w_in, ev_w_out, a_ln_g, a_ln_b, a_ws, a_bs, b_q_norm, b_k_norm, b_lam_q1, b_lam_k1, b_lam_q2, b_lam_k2, b_subln, od_w_qkv, od_w_out, c_q_norm, c_k_norm, moe_router, moe_bias, moe_w_gate, moe_w_up, moe_w_down, sh_w_gate, sh_w_up, sh_w_down):
    bsz, n_lat, d = x.shape
    n_ctx = ctx.shape[1]
    depth = mod_w.shape[0]
    assert depth == 2 and n_ctx == ROW_TILE and n_lat % ROW_TILE == 0 and bsz + 1 <= MOD_ROWS
    assert d == SUBLANES * LANES
    n_seq = n_ctx + n_lat
    tpb = n_seq // ROW_TILE
    lpb = n_lat // ROW_TILE
    t_all = bsz * n_seq
    n_tiles = t_all // ROW_TILE
    ctx_row = bsz

    cond = jnp.zeros((MOD_ROWS, d), F32).at[:bsz].set(c).at[ctx_row].set(c_ctx)
    mod = _adaln(cond, mod_w, mod_b)
    modv = [mod[l].reshape(MOD_ROWS * 6, 1, d) for l in range(depth)]

    def row_all(i):
        return jnp.where(i % tpb == 0, ctx_row, i // tpb)

    def mspec(j, row_of):
        return pl.BlockSpec((1, 1, d), lambda i: (row_of(i) * 6 + j, 0, 0))

    def full(shape):
        return pl.BlockSpec(shape, lambda *_: (0,) * len(shape))

    x_spec = pl.BlockSpec((1, ROW_TILE, d), lambda i: (i // tpb, jnp.maximum(i % tpb - 1, 0), 0))
    ctx_spec = pl.BlockSpec((1, ROW_TILE, d), lambda i: (i // tpb, 0, 0))
    ch = d // LANES
    tok_spec = pl.BlockSpec((ROW_TILE * ch, LANES), lambda i: (i, 0))

    lam_init = 0.8 - 0.6 * math.exp(-0.3 * 0)
    p64 = _split_halves_perm(B_HEAD_DIM)
    col_perm = np.concatenate(
        [np.arange(2 * A_WIDTH)]
        + [2 * A_WIDTH + blk * B_HEAD_DIM + p64 for blk in range(2 * B_WIDTH // B_HEAD_DIM)]
        + [np.arange(2 * A_WIDTH + 2 * B_WIDTH, 2 * A_WIDTH + 3 * B_WIDTH)])
    w_in = ev_w_in[0][:, col_perm].astype(BF16)
    even_in = w_in.shape[1]
    cos_b, sin_b = _rope_tables(n_lat, n_ctx, B_HEAD_DIM)
    zeros_b = jnp.zeros_like(sin_b)
    tab_c = jnp.tile(jnp.concatenate([cos_b, cos_b], axis=-1), (1, 2))
    tab_sa = jnp.tile(jnp.concatenate([-sin_b, zeros_b], axis=-1), (1, 2))
    tab_sb = jnp.tile(jnp.concatenate([zeros_b, sin_b], axis=-1), (1, 2))
    qg = jnp.tile(b_q_norm[0][p64], 2).reshape(1, LANES)
    kg = jnp.tile(b_k_norm[0][p64], 2).reshape(1, LANES)
    tab_spec = pl.BlockSpec((ROW_TILE, LANES), lambda i: (i % tpb, 0))
    row_spec = lambda w: pl.BlockSpec((ROW_TILE, w), lambda i: (i, 0))
    uv, q, k, v = pl.pallas_call(
        functools.partial(_even_in_kernel, tiles_per_batch=tpb),
        grid=(n_tiles,),
        in_specs=[x_spec, ctx_spec, mspec(0, row_all), mspec(1, row_all), full((1, d)),
                  full((d, even_in)), full((1, LANES)), full((1, LANES)),
                  tab_spec, tab_spec, tab_spec],
        out_specs=[row_spec(2 * A_WIDTH), row_spec(B_WIDTH), row_spec(B_WIDTH), row_spec(B_WIDTH)],
        out_shape=[jax.ShapeDtypeStruct((t_all, 2 * A_WIDTH), F32),
                   jax.ShapeDtypeStruct((t_all, B_WIDTH), BF16),
                   jax.ShapeDtypeStruct((t_all, B_WIDTH), BF16),
                   jax.ShapeDtypeStruct((t_all, B_WIDTH), BF16)],
        compiler_params=_cparams("parallel"),
        name="even_in",
    )(x, ctx, modv[0], modv[0], norm1_g[0].reshape(1, d), w_in, qg, kg, tab_c, tab_sa, tab_sb)

    lamv = jnp.zeros((SUBLANES, LANES), F32)
    for r, vec in enumerate((b_lam_q1[0], b_lam_k1[0], b_lam_q2[0], b_lam_k2[0])):
        lamv = lamv.at[r, :B_HEAD_DIM].set(vec)
    o = pl.pallas_call(
        functools.partial(_diff_attn_kernel, ctx_len=n_ctx, lam_init=lam_init),
        grid=(bsz, tpb),
        in_specs=[
            pl.BlockSpec((SUBLANES, LANES), lambda b, qi: (0, 0)),
            pl.BlockSpec((ROW_TILE, B_WIDTH), lambda b, qi: (b * tpb + qi, 0)),
            pl.BlockSpec((n_seq, B_WIDTH), lambda b, qi: (b, 0)),
            pl.BlockSpec((n_seq, B_WIDTH), lambda b, qi: (b, 0)),
        ],
        out_specs=pl.BlockSpec((ROW_TILE, B_WIDTH), lambda b, qi: (b * tpb + qi, 0)),
        out_shape=jax.ShapeDtypeStruct((t_all, B_WIDTH), F32),
        compiler_params=_cparams("parallel", "arbitrary"),
        name="diff_attn",
    )(lamv, q, k, v)

    bs_col = jnp.repeat(a_bs[0].T, A_GROUP_DIM, axis=1)
    sub_g = b_subln[0].reshape(1, LANES)
    h1, nx = pl.pallas_call(
        functools.partial(_even_out_kernel, lam_init=lam_init, tiles_per_batch=tpb),
        grid=(n_tiles,),
        in_specs=[row_spec(B_WIDTH), row_spec(2 * A_WIDTH), x_spec, ctx_spec,
                  mspec(2, row_all), mspec(3, row_all), mspec(4, row_all),
                  full((1, LANES)), full((1, A_WIDTH)), full((1, A_WIDTH)),
                  full((A_GROUPS, GMLP_CHUNK, GMLP_CHUNK)), full((GMLP_CHUNK, A_WIDTH)),
                  full((A_WIDTH + B_WIDTH, d)), full((1, d))],
        out_specs=[row_spec(d), tok_spec],
        out_shape=[jax.ShapeDtypeStruct((t_all, d), F32),
                   jax.ShapeDtypeStruct((t_all * ch, LANES), F32)],
        compiler_params=_cparams("parallel"),
        name="even_out",
    )(o, uv, x, ctx, modv[0], modv[0], modv[0], sub_g, a_ln_g[0].reshape(1, A_WIDTH),
      a_ln_b[0].reshape(1, A_WIDTH), a_ws[0].astype(BF16), bs_col,
      ev_w_out[0].astype(BF16), norm2_g[0].reshape(1, d))

    h2 = _moe(nx, h1, modv[0], row_all, 0, moe_router[0], moe_bias[0], moe_w_gate, moe_w_up,
              moe_w_down, sh_w_gate[0], sh_w_up[0], sh_w_down[0])

    p128 = _split_halves_perm(C_HEAD_DIM)
    n_qkv_heads = C_HEADS + 2 * C_KV_HEADS
    col_perm = np.concatenate(
        [blk * C_HEAD_DIM + p128 for blk in range(C_HEADS + C_KV_HEADS)]
        + [np.arange((C_HEADS + C_KV_HEADS) * C_HEAD_DIM, n_qkv_heads * C_HEAD_DIM)])
    w_qkv = od_w_qkv[0][:, col_perm].astype(BF16)
    cos_c, sin_c = _rope_tables(n_lat, n_ctx, C_HEAD_DIM)
    tab_c1 = jnp.concatenate([cos_c, cos_c], axis=-1)
    tab_s1 = jnp.concatenate([-sin_c, sin_c], axis=-1)
    qg1 = c_q_norm[0][p128].reshape(1, LANES)
    kg1 = c_k_norm[0][p128].reshape(1, LANES)
    nq = C_HEADS * C_HEAD_DIM
    nkv = C_KV_HEADS * C_HEAD_DIM
    q1, k1, v1 = pl.pallas_call(
        _odd_in_kernel,
        grid=(n_tiles,),
        in_specs=[row_spec(d), mspec(0, row_all), mspec(1, row_all), full((1, d)),
                  full((d, nq + 2 * nkv)), full((1, LANES)), full((1, LANES)), tab_spec, tab_spec],
        out_specs=[row_spec(nq), row_spec(nkv), row_spec(nkv)],
        out_shape=[jax.ShapeDtypeStruct((t_all, nq), BF16),
                   jax.ShapeDtypeStruct((t_all, nkv), BF16),
                   jax.ShapeDtypeStruct((t_all, nkv), BF16)],
        compiler_params=_cparams("parallel"),
        name="odd_in",
    )(h2, modv[1], modv[1], norm1_g[1].reshape(1, d), w_qkv, qg1, kg1, tab_c1, tab_s1)

    t_lat = bsz * n_lat
    grp = C_HEADS // C_KV_HEADS
    o1 = pl.pallas_call(
        _gqa_kernel,
        grid=(bsz, C_KV_HEADS, lpb),
        in_specs=[
            pl.BlockSpec((ROW_TILE, grp * LANES), lambda b, n, qi: (b * tpb + 1 + qi, n)),
            pl.BlockSpec((n_seq, LANES), lambda b, n, qi: (b, n)),
            pl.BlockSpec((n_seq, LANES), lambda b, n, qi: (b, n)),
        ],
        out_specs=pl.BlockSpec((ROW_TILE, grp * LANES), lambda b, n, qi: (b * lpb + qi, n)),
        out_shape=jax.ShapeDtypeStruct((t_lat, nq), BF16),
        compiler_params=_cparams("parallel", "parallel", "arbitrary"),
        name="gqa_attn",
    )(q1, k1, v1)

    def row_lat(i):
        return i // lpb

    lat_tiles = t_lat // ROW_TILE
    hx, nx1 = pl.pallas_call(
        _odd_out_kernel,
        grid=(lat_tiles,),
        in_specs=[row_spec(nq),
                  pl.BlockSpec((ROW_TILE, d), lambda i: ((i // lpb) * tpb + 1 + i % lpb, 0)),
                  mspec(2, row_lat), mspec(3, row_lat), mspec(4, row_lat),
                  full((nq, d)), full((1, d))],
        out_specs=[row_spec(d), tok_spec],
        out_shape=[jax.ShapeDtypeStruct((t_lat, d), F32),
                   jax.ShapeDtypeStruct((t_lat * ch, LANES), F32)],
        compiler_params=_cparams("parallel"),
        name="odd_out",
    )(o1, h2, modv[1], modv[1], modv[1], od_w_out[0].astype(BF16), norm2_g[1].reshape(1, d))

    out = _moe(nx1, hx, modv[1], row_lat, 1, moe_router[1], moe_bias[1], moe_w_gate, moe_w_up,
               moe_w_down, sh_w_gate[1], sh_w_up[1], sh_w_down[1])
    return out.reshape(bsz, n_lat, d)
```

```python
import functools
import math

import numpy as np
import jax
import jax.numpy as jnp
from jax import lax
from jax.experimental import pallas as pl
from jax.experimental.pallas import tpu as pltpu

F32 = jnp.float32
BF16 = jnp.bfloat16
I32 = jnp.int32

GRID_W = 64
EPS = 1e-6
ROPE_THETA = 10000.0
A_GROUPS = 4
A_GROUP_DIM = 128
A_WIDTH = A_GROUPS * A_GROUP_DIM
GMLP_CHUNK = 128
B_HEADS = 4
B_HEAD_DIM = 64
B_WIDTH = B_HEADS * 2 * B_HEAD_DIM
C_HEADS = 8
C_KV_HEADS = 2
C_HEAD_DIM = 128
N_EXPERTS = 64
TOP_K = 6
N_GROUPS = 8
TOPK_GROUPS = 4
ROUTE_SCALE = 2.5
LOG2E = math.log2(math.e)

LANES = 128
SUBLANES = 8
ROW_TILE = 256
ROUTER_TILE = 512
EXPERT_TILE = 512
PROJ_BLOCK = 256
MOD_ROWS = 24
VMEM_LIMIT = 56 * 1024 * 1024

NT_DIMS = (((1,), (1,)), ((), ()))


def _cparams(*sem):
    return pltpu.CompilerParams(dimension_semantics=sem, vmem_limit_bytes=VMEM_LIMIT)


def _rms(x, g):
    return x * lax.rsqrt(jnp.mean(x * x, axis=-1, keepdims=True) + EPS) * g


def _norm_mod(h, g, shift, scale):
    return _rms(h, g) * (1.0 + scale) + shift


def _gelu(x):
    return 0.5 * x * (1.0 + lax.erf(x * np.float32(math.sqrt(0.5))))


def _silu(x):
    return x * jax.nn.sigmoid(x)


def _bdot(a, b):
    return jnp.dot(a.astype(BF16), b.astype(BF16), preferred_element_type=F32)


def _from_token_tiles(ref, rows, d):
    ch = d // LANES
    groups = []
    for g in range(rows // SUBLANES):
        groups.append(jnp.concatenate(
            [ref[pl.ds(g * SUBLANES * ch + j, SUBLANES, stride=ch), :] for j in range(ch)], axis=-1))
    return jnp.concatenate(groups, axis=0)


def _to_token_tiles(ref, val):
    rows, d = val.shape
    ch = d // LANES
    for g in range(rows // SUBLANES):
        for j in range(ch):
            ref[pl.ds(g * SUBLANES * ch + j, SUBLANES, stride=ch), :] = (
                val[g * SUBLANES:(g + 1) * SUBLANES, j * LANES:(j + 1) * LANES])


def _adaln_kernel(c_ref, w_ref, b_ref, o_ref):
    o_ref[0] = _bdot(_silu(c_ref[...]), w_ref[0]) + b_ref[0]


def _adaln(cond, mod_w, mod_b):
    depth, d, d6 = mod_w.shape
    tn = d6 // 4
    return pl.pallas_call(
        _adaln_kernel,
        grid=(depth, d6 // tn),
        in_specs=[
            pl.BlockSpec((MOD_ROWS, d), lambda l, j: (0, 0)),
            pl.BlockSpec((1, d, tn), lambda l, j: (l, 0, j)),
            pl.BlockSpec((1, 1, tn), lambda l, j: (l, 0, j)),
        ],
        out_specs=pl.BlockSpec((1, MOD_ROWS, tn), lambda l, j: (l, 0, j)),
        out_shape=jax.ShapeDtypeStruct((depth, MOD_ROWS, d6), F32),
        compiler_params=_cparams("parallel", "parallel"),
        name="adaln",
    )(cond, mod_w, mod_b.reshape(depth, 1, d6))


def _pick_stream(x_ref, ctx_ref, tiles_per_batch):
    is_ctx = pl.program_id(0) % tiles_per_batch == 0
    return jnp.where(is_ctx, ctx_ref[0], x_ref[0])


def _even_in_kernel(x_ref, ctx_ref, sh_ref, sc_ref, g_ref, w_ref, qg_ref, kg_ref, c_ref, sa_ref,
                    sb_ref, uv_ref, q_ref, k_ref, v_ref, *, tiles_per_batch):
    h = _pick_stream(x_ref, ctx_ref, tiles_per_batch)
    n = _norm_mod(h, g_ref[...], sh_ref[0], sc_ref[0]).astype(BF16)

    def cols(c0):
        return jnp.dot(n, w_ref[:, c0:c0 + PROJ_BLOCK], preferred_element_type=F32)

    for c0 in range(0, 2 * A_WIDTH, PROJ_BLOCK):
        uv_ref[:, c0:c0 + PROJ_BLOCK] = _gelu(cols(c0))
    cos, sa, sb = c_ref[...], sa_ref[...], sb_ref[...]
    lane = lax.broadcasted_iota(I32, cos.shape, 1)
    low = lane < B_HEAD_DIM

    def head_pair(x, gain, scale):
        sq = x * x
        s_lo = jnp.sum(jnp.where(low, sq, 0.0), axis=-1, keepdims=True)
        s_hi = jnp.sum(jnp.where(low, 0.0, sq), axis=-1, keepdims=True)
        ms = jnp.where(low, s_lo, s_hi) * np.float32(1.0 / B_HEAD_DIM)
        y = x * lax.rsqrt(ms + EPS) * gain
        y = (y * cos + pltpu.roll(y, LANES - B_HEAD_DIM // 2, 1) * sa
             + pltpu.roll(y, B_HEAD_DIM // 2, 1) * sb)
        if scale is not None:
            y = y * scale
        return y.astype(BF16)

    q0 = 2 * A_WIDTH
    k0 = q0 + B_WIDTH
    v0 = k0 + B_WIDTH
    per = PROJ_BLOCK // LANES
    for c0 in range(0, B_WIDTH, PROJ_BLOCK):
        pq = cols(q0 + c0)
        pk = cols(k0 + c0)
        for j in range(per):
            sl = slice(c0 + j * LANES, c0 + (j + 1) * LANES)
            q_ref[:, sl] = head_pair(pq[:, j * LANES:(j + 1) * LANES], qg_ref[...],
                                     np.float32(B_HEAD_DIM ** -0.5))
            k_ref[:, sl] = head_pair(pk[:, j * LANES:(j + 1) * LANES], kg_ref[...], None)
        v_ref[:, c0:c0 + PROJ_BLOCK] = cols(v0 + c0).astype(BF16)


def _odd_in_kernel(h_ref, sh_ref, sc_ref, g_ref, w_ref, qg_ref, kg_ref, c_ref, s_ref,
                   q_ref, k_ref, v_ref):
    n = _norm_mod(h_ref[...], g_ref[...], sh_ref[0], sc_ref[0])
    p = jnp.dot(n.astype(BF16), w_ref[...], preferred_element_type=F32)
    cos, sin = c_ref[...], s_ref[...]

    def head(x, gain, scale):
        y = _rms(x, gain)
        y = y * cos + pltpu.roll(y, C_HEAD_DIM // 2, 1) * sin
        if scale is not None:
            y = y * scale
        return y.astype(BF16)

    nq = C_HEADS * C_HEAD_DIM
    nkv = C_KV_HEADS * C_HEAD_DIM
    for j in range(C_HEADS):
        q_ref[:, j * LANES:(j + 1) * LANES] = head(p[:, j * LANES:(j + 1) * LANES], qg_ref[...],
                                                   None)
    for j in range(C_KV_HEADS):
        k_ref[:, j * LANES:(j + 1) * LANES] = head(
            p[:, nq + j * LANES:nq + (j + 1) * LANES], kg_ref[...], None)
    v_ref[...] = p[:, nq + nkv:nq + 2 * nkv].astype(BF16)


def _diff_attn_kernel(lam_ref, q_ref, k_ref, v_ref, o_ref, *, ctx_len, lam_init):
    lv = lam_ref[...]
    lam = (jnp.exp(jnp.sum(lv[0:1] * lv[1:2], axis=-1, keepdims=True))
           - jnp.exp(jnp.sum(lv[2:3] * lv[3:4], axis=-1, keepdims=True)) + np.float32(lam_init))
    low = lax.broadcasted_iota(I32, (q_ref.shape[0], LANES), 1) < B_HEAD_DIM

    def softmax(qm, k, scale):
        s = lax.dot_general(qm, k, NT_DIMS, preferred_element_type=F32)
        p = jnp.exp(s - jnp.max(s, axis=-1, keepdims=True))
        return p * (scale / jnp.sum(p, axis=-1, keepdims=True))

    def attend(n_keys):
        for hh in range(q_ref.shape[1] // LANES):
            cs = slice(hh * LANES, (hh + 1) * LANES)
            q = q_ref[:, cs]
            zero = jnp.zeros_like(q)
            k = k_ref[0:n_keys, cs]
            a = (softmax(jnp.where(low, q, zero), k, 1.0)
                 - softmax(jnp.where(low, zero, q), k, lam))
            o_ref[:, cs] = jnp.dot(a.astype(BF16), v_ref[0:n_keys, cs], preferred_element_type=F32)

    is_ctx = pl.program_id(1) == 0

    @pl.when(is_ctx)
    def _():
        attend(ctx_len)

    @pl.when(jnp.logical_not(is_ctx))
    def _():
        attend(k_ref.shape[0])


def _gqa_kernel(q_ref, k_ref, v_ref, o_ref):
    grp = q_ref.shape[1] // LANES
    c = np.float32(C_HEAD_DIM ** -0.5 * LOG2E)
    k = k_ref[...]
    v = v_ref[...]
    for g in range(grp):
        cs = slice(g * LANES, (g + 1) * LANES)
        s = lax.dot_general(q_ref[:, cs], k, NT_DIMS, preferred_element_type=F32)
        p = jnp.exp2((s - jnp.max(s, axis=-1, keepdims=True)) * c)
        l = jnp.sum(p, axis=-1, keepdims=True)
        o = jnp.dot(p.astype(BF16), v, preferred_element_type=F32) / l
        o_ref[:, cs] = o.astype(o_ref.dtype)


def _even_out_kernel(o_ref, uv_ref, x_ref, ctx_ref, gate_ref, sh_ref, sc_ref, sub_ref, lng_ref,
                     lnb_ref, ws_ref, bs_ref, w_ref, g2_ref, h1_ref, nx_ref, *, lam_init,
                     tiles_per_batch):
    o = o_ref[...]
    uv = uv_ref[...]
    u = uv[:, :A_WIDTH]
    v = uv[:, A_WIDTH:]
    mu = jnp.mean(v, axis=-1, keepdims=True)
    var = jnp.mean(jnp.square(v - mu), axis=-1, keepdims=True)
    vn = ((v - mu) * lax.rsqrt(var + EPS) * lng_ref[...] + lnb_ref[...]).astype(BF16)
    rows = o.shape[0]
    parts = []
    for c in range(rows // GMLP_CHUNK):
        rs = slice(c * GMLP_CHUNK, (c + 1) * GMLP_CHUNK)
        for g in range(A_GROUPS):
            cs = slice(g * A_GROUP_DIM, (g + 1) * A_GROUP_DIM)
            mixed = jnp.dot(ws_ref[g], vn[rs, cs], preferred_element_type=F32) + bs_ref[:, cs]
            parts.append((c, g, u[rs, cs] * mixed))
    a_rows = [jnp.concatenate([p for (c2, _, p) in parts if c2 == c], axis=-1)
              for c in range(rows // GMLP_CHUNK)]
    a = jnp.concatenate(a_rows, axis=0)
    heads = []
    for hh in range(B_HEADS):
        oh = o[:, hh * LANES:(hh + 1) * LANES]
        heads.append(_rms(oh, sub_ref[...]) * np.float32(1.0 - lam_init))
    cat = jnp.concatenate([a] + heads, axis=-1).astype(BF16)
    y = jnp.dot(cat, w_ref[...], preferred_element_type=F32)
    h1 = _pick_stream(x_ref, ctx_ref, tiles_per_batch) + gate_ref[0] * y
    h1_ref[...] = h1
    _to_token_tiles(nx_ref, _norm_mod(h1, g2_ref[...], sh_ref[0], sc_ref[0]))


def _odd_out_kernel(o_ref, h_ref, gate_ref, sh_ref, sc_ref, w_ref, g2_ref, h1_ref, nx_ref):
    y = jnp.dot(o_ref[...], w_ref[...], preferred_element_type=F32)
    h1 = h_ref[...] + gate_ref[0] * y
    h1_ref[...] = h1
    _to_token_tiles(nx_ref, _norm_mod(h1, g2_ref[...], sh_ref[0], sc_ref[0]))


def _rows_to_block(rows, dtype):
    n = rows[0].shape[1]
    rio = lax.broadcasted_iota(I32, (SUBLANES, n), 0)
    out = jnp.zeros((SUBLANES, n), dtype)
    for r, row in enumerate(rows):
        out = jnp.where(rio == r, jnp.broadcast_to(row.astype(dtype), (SUBLANES, n)), out)
    return out


def _transpose_block(xt):
    n = xt.shape[1]
    eye = jnp.where(lax.broadcasted_iota(I32, (n, n), 0) == lax.broadcasted_iota(I32, (n, n), 1),
                    1.0, 0.0).astype(BF16)
    acc = jnp.zeros((n, SUBLANES), F32)
    rem = xt
    for _ in range(3):
        part = rem.astype(BF16)
        acc = acc + lax.dot_general(eye, part, NT_DIMS, preferred_element_type=F32)
        rem = rem - part.astype(F32)
    return acc


def _router_kernel(x_ref, wr_ref, b_ref, eidx_ref, rank_ref, wcol_ref, cnt_ref, run_ref):
    @pl.when(pl.program_id(0) == 0)
    def _():
        run_ref[...] = jnp.zeros_like(run_ref)

    per = N_EXPERTS // N_GROUPS
    d = wr_ref.shape[1]
    x = _from_token_tiles(x_ref, x_ref.shape[0] * LANES // d, d)
    logits = lax.dot_general(wr_ref[...], x.astype(BF16), NT_DIMS,
                             preferred_element_type=F32)
    scores = jax.nn.sigmoid(logits)
    sel = scores + b_ref[...]
    tm = sel.shape[1]
    neg = np.float32(-np.inf)
    jio = lax.broadcasted_iota(I32, (per, tm), 0).astype(F32)
    gio = lax.broadcasted_iota(I32, (N_GROUPS, tm), 0).astype(F32)

    def rmax(x):
        return jnp.max(x, axis=0, keepdims=True)

    def rmin(x):
        return jnp.min(x, axis=0, keepdims=True)

    sel_g = [sel[g * per:(g + 1) * per, :] for g in range(N_GROUPS)]
    sc_g = [scores[g * per:(g + 1) * per, :] for g in range(N_GROUPS)]
    gs = jnp.zeros((N_GROUPS, tm), F32)
    for g in range(N_GROUPS):
        m1 = rmax(sel_g[g])
        i1 = rmin(jnp.where(sel_g[g] == m1, jio, np.float32(per)))
        m2 = rmax(jnp.where(jio == i1, neg, sel_g[g]))
        gs = jnp.where(gio == np.float32(g), jnp.broadcast_to(m1 + m2, gs.shape), gs)
    gsel = jnp.zeros((N_GROUPS, tm), I32)
    for _ in range(TOPK_GROUPS):
        m = rmax(gs)
        idx = rmin(jnp.where(gs == m, gio, np.float32(N_GROUPS)))
        hit = gio == idx
        gsel = jnp.where(hit, 1, gsel)
        gs = jnp.where(hit, neg, gs)
    masked = [jnp.where(jnp.broadcast_to(gsel[g:g + 1, :], (per, tm)) == 1, sel_g[g], neg)
              for g in range(N_GROUPS)]
    eio = [jio + np.float32(g * per) for g in range(N_GROUPS)]
    e_rows, w_rows, hits = [], [], []
    for _ in range(TOP_K):
        m = masked[0]
        for g in range(1, N_GROUPS):
            m = jnp.maximum(m, masked[g])
        m = rmax(m)
        cand = jnp.where(masked[0] == m, eio[0], np.float32(N_EXPERTS))
        for g in range(1, N_GROUPS):
            cand = jnp.minimum(cand, jnp.where(masked[g] == m, eio[g], np.float32(N_EXPERTS)))
        idx = rmin(cand)
        hit = [eio[g] == idx for g in range(N_GROUPS)]
        wsel = jnp.where(hit[0], sc_g[0], 0.0)
        for g in range(1, N_GROUPS):
            wsel = wsel + jnp.where(hit[g], sc_g[g], 0.0)
        masked = [jnp.where(hit[g], neg, masked[g]) for g in range(N_GROUPS)]
        e_rows.append(idx)
        w_rows.append(jnp.sum(wsel, axis=0, keepdims=True))
        hits.append(hit)
    wsum = w_rows[0]
    for r in w_rows[1:]:
        wsum = wsum + r
    w_rows = [r / wsum * np.float32(ROUTE_SCALE) for r in w_rows]
    onehot = []
    for g in range(N_GROUPS):
        any_hit = hits[0][g]
        for kk in range(1, TOP_K):
            any_hit = jnp.logical_or(any_hit, hits[kk][g])
        onehot.append(jnp.where(any_hit, 1.0, 0.0))
    mt = jnp.concatenate(onehot, axis=0)
    before = (lax.broadcasted_iota(I32, (tm, tm), 0) < lax.broadcasted_iota(I32, (tm, tm), 1))
    prefix = jnp.dot(mt.astype(BF16), jnp.where(before, 1.0, 0.0).astype(BF16),
                     preferred_element_type=F32)
    pos = prefix + run_ref[...]
    r_rows = []
    for kk in range(TOP_K):
        acc = jnp.where(hits[kk][0], pos[0:per, :], 0.0)
        for g in range(1, N_GROUPS):
            acc = acc + jnp.where(hits[kk][g], pos[g * per:(g + 1) * per, :], 0.0)
        r_rows.append(jnp.sum(acc, axis=0, keepdims=True))
    run = run_ref[...] + jnp.sum(mt, axis=1, keepdims=True)
    run_ref[...] = run
    eidx_ref[...] = _rows_to_block(e_rows, I32)
    rank_ref[...] = _rows_to_block(r_rows, I32)
    wcol_ref[...] = _transpose_block(_rows_to_block(w_rows, F32))
    cnt_ref[...] = jnp.broadcast_to(run, cnt_ref.shape)


def _dest_kernel(start_ref, eidx_ref, rank_ref, dest_ref):
    per = N_EXPERTS // N_GROUPS
    eidx = eidx_ref[...]
    tm = eidx.shape[1]
    jio = lax.broadcasted_iota(I32, (per, tm), 0)
    rows = []
    for kk in range(TOP_K):
        e = jnp.broadcast_to(eidx[kk:kk + 1, :], (per, tm))
        acc = jnp.zeros((per, tm), F32)
        for g in range(N_GROUPS):
            st = jnp.broadcast_to(start_ref[g * per:(g + 1) * per, :], (per, tm))
            acc = acc + jnp.where(jio + g * per == e, st, 0.0)
        rows.append(jnp.sum(acc, axis=0, keepdims=True))
    dest_ref[...] = _rows_to_block(rows, I32) + rank_ref[...]


def _row_copy(src, s_row, dst, d_row, sem):
    def tile_start(row):
        start = row * SUBLANES
        return start if isinstance(row, int) else pl.multiple_of(start, SUBLANES)

    s0 = tile_start(s_row)
    d0 = tile_start(d_row)
    return pltpu.make_async_copy(src.at[pl.ds(s0, SUBLANES)], dst.at[pl.ds(d0, SUBLANES)], sem)


def _dispatch_kernel(dest_ref, x_ref, xs_ref, sem):
    rows = x_ref.shape[0] // SUBLANES

    def copies(t):
        return [_row_copy(x_ref, t, xs_ref, dest_ref[kk, t], sem) for kk in range(TOP_K)]

    def issue(t, carry):
        for kk, cp in enumerate(copies(t)):
            cp.start(priority=kk % 2)
        return carry

    def drain(t, carry):
        for cp in copies(t):
            cp.wait()
        return carry

    lax.fori_loop(0, rows, issue, 0)
    lax.fori_loop(0, rows, drain, 0)


def _expert_kernel(tile_s, exp_s, lo_s, hi_s, first_s, last_s, new_s, inv_prev_ref, inv_ref,
                   xs_ref, wg_ref, wu_ref, wd_ref, ys_ref, wg_b, wu_b, wd_b, acc, stage, sem):
    v = pl.program_id(0)
    tile = tile_s[v]
    rows, d = acc.shape

    @pl.when(new_s[v] == 1)
    def _():
        wg_b[...] = wg_ref[0, 0].astype(BF16)
        wu_b[...] = wu_ref[0, 0].astype(BF16)
        wd_b[...] = wd_ref[0, 0].astype(BF16)

    @pl.when(first_s[v] == 1)
    def _():
        acc[...] = jnp.zeros_like(acc)

    lo = lo_s[v]
    hi = hi_s[v]
    nonempty = hi > lo
    flush_prev = jnp.logical_and(first_s[v] == 1, tile >= 1)

    def compute():
        x = _from_token_tiles(xs_ref, rows, d).astype(BF16)
        g = jnp.dot(x, wg_b[...], preferred_element_type=F32)
        u = jnp.dot(x, wu_b[...], preferred_element_type=F32)
        y = jnp.dot((_silu(g) * u).astype(BF16), wd_b[...], preferred_element_type=F32)
        row = lax.broadcasted_iota(I32, (rows, 1), 0)
        mine = jnp.logical_and(row >= lo, row < hi)
        acc[...] = jnp.where(mine, y, acc[...])

    def start_scatter(idx_ref, slot, unrolled):
        def start(r, par):
            _row_copy(stage.at[slot], r, ys_ref, idx_ref[0, 0, r], sem.at[slot]).start(priority=par)

        if unrolled:
            for r in range(rows):
                start(r, r % 2)
        else:
            def body(r2, carry):
                for par in range(2):
                    start(r2 * 2 + par, par)
                return carry
            lax.fori_loop(0, rows // 2, body, 0)

    def wait_scatter(slot):
        pltpu.make_async_copy(stage.at[slot], stage.at[slot], sem.at[slot]).wait()

    for slot in range(2):
        prev_here = jnp.logical_and(flush_prev, (tile + 1) % 2 == slot)

        @pl.when(jnp.logical_and(prev_here, nonempty))
        def _(slot=slot):
            start_scatter(inv_prev_ref, slot, True)
            compute()

        @pl.when(jnp.logical_and(prev_here, jnp.logical_not(nonempty)))
        def _(slot=slot):
            start_scatter(inv_prev_ref, slot, False)

    @pl.when(jnp.logical_and(nonempty, jnp.logical_not(flush_prev)))
    def _():
        compute()

    is_final = v == pl.num_programs(0) - 1
    for slot in range(2):
        @pl.when(jnp.logical_and(last_s[v] == 1, tile % 2 == slot))
        def _(slot=slot):
            @pl.when(tile >= 2)
            def _():
                wait_scatter(slot)

            _to_token_tiles(stage.at[slot], acc[...])

            @pl.when(is_final)
            def _():
                start_scatter(inv_ref, slot, False)

    @pl.when(is_final)
    def _():
        for slot in range(2):
            @pl.when(jnp.logical_or(tile >= 1, tile % 2 == slot))
            def _(slot=slot):
                wait_scatter(slot)


def _combine_kernel(wcol_ref, x_ref, h_ref, gate_ref, sg_ref, su_ref, sd_ref, *refs):
    y_refs, o_ref = refs[:TOP_K], refs[TOP_K]
    rows, d = h_ref.shape
    x = _from_token_tiles(x_ref, rows, d).astype(BF16)
    g = jnp.dot(x, sg_ref[...], preferred_element_type=F32)
    u = jnp.dot(x, su_ref[...], preferred_element_type=F32)
    acc = jnp.dot((_silu(g) * u).astype(BF16), sd_ref[...], preferred_element_type=F32)
    wcol = wcol_ref[...]
    for kk in range(TOP_K):
        acc = acc + _from_token_tiles(y_refs[kk], rows, d) * wcol[:, kk:kk + 1]
    o_ref[...] = h_ref[...] + gate_ref[0] * acc


def _moe(nx, h, modv, gate_row_of, layer, w_router, router_bias, w_gate, w_up, w_down, sg, su, sd):
    t, d = h.shape
    ch = d // LANES
    n_slots = t * TOP_K
    n_rt = t // ROUTER_TILE
    e = N_EXPERTS
    eidx, rank, wcol, cnt = pl.pallas_call(
        _router_kernel,
        grid=(n_rt,),
        in_specs=[
            pl.BlockSpec((ROUTER_TILE * ch, LANES), lambda i: (i, 0)),
            pl.BlockSpec((e, d), lambda i: (0, 0)),
            pl.BlockSpec((e, 1), lambda i: (0, 0)),
        ],
        out_specs=[
            pl.BlockSpec((SUBLANES, ROUTER_TILE), lambda i: (0, i)),
            pl.BlockSpec((SUBLANES, ROUTER_TILE), lambda i: (0, i)),
            pl.BlockSpec((ROUTER_TILE, SUBLANES), lambda i: (i, 0)),
            pl.BlockSpec((e, LANES), lambda i: (0, 0)),
        ],
        out_shape=[
            jax.ShapeDtypeStruct((SUBLANES, t), I32),
            jax.ShapeDtypeStruct((SUBLANES, t), I32),
            jax.ShapeDtypeStruct((t, SUBLANES), F32),
            jax.ShapeDtypeStruct((e, LANES), F32),
        ],
        scratch_shapes=[pltpu.VMEM((e, 1), F32)],
        compiler_params=_cparams("arbitrary"),
        name="moe_router",
    )(nx, w_router.T.astype(BF16), router_bias.reshape(e, 1))

    counts = cnt[:, 0].astype(I32)
    ends = jnp.cumsum(counts)
    starts = ends - counts
    dest = pl.pallas_call(
        _dest_kernel,
        grid=(n_rt,),
        in_specs=[
            pl.BlockSpec((e, 1), lambda i: (0, 0)),
            pl.BlockSpec((SUBLANES, ROUTER_TILE), lambda i: (0, i)),
            pl.BlockSpec((SUBLANES, ROUTER_TILE), lambda i: (0, i)),
        ],
        out_specs=pl.BlockSpec((SUBLANES, ROUTER_TILE), lambda i: (0, i)),
        out_shape=jax.ShapeDtypeStruct((SUBLANES, t), I32),
        compiler_params=_cparams("parallel"),
        name="moe_dest",
    )(starts.astype(F32).reshape(e, 1), eidx, rank)
    n_tiles = t // ROW_TILE
    inv = jnp.argsort(dest[:TOP_K].reshape(-1)).astype(I32)
    xs = pl.pallas_call(
        _dispatch_kernel,
        grid=(n_tiles,),
        in_specs=[
            pl.BlockSpec((SUBLANES, ROW_TILE), lambda i: (0, i), memory_space=pltpu.SMEM),
            pl.BlockSpec((ROW_TILE * ch, LANES), lambda i: (i, 0)),
        ],
        out_specs=pl.BlockSpec(memory_space=pl.ANY),
        out_shape=jax.ShapeDtypeStruct((n_slots * ch, LANES), F32),
        scratch_shapes=[pltpu.SemaphoreType.DMA],
        compiler_params=pltpu.CompilerParams(dimension_semantics=("arbitrary",),
                                             vmem_limit_bytes=VMEM_LIMIT, has_side_effects=True),
        name="moe_dispatch",
    )(dest, nx)

    n_et = n_slots // EXPERT_TILE
    pts = jnp.sort(jnp.concatenate([jnp.arange(n_et, dtype=I32) * EXPERT_TILE, starts]))
    lo = pts
    hi = jnp.concatenate([pts[1:], jnp.full((1,), n_slots, I32)])
    tile = jnp.minimum(lo // EXPERT_TILE, n_et - 1)
    expert = jnp.minimum(jnp.sum((ends[None, :] <= lo[:, None]).astype(I32), axis=1), e - 1)
    one = jnp.ones((1,), I32)
    tile_change = (tile[1:] != tile[:-1]).astype(I32)
    first = jnp.concatenate([one, tile_change])
    last = jnp.concatenate([tile_change, one])
    newexp = jnp.concatenate([one, (expert[1:] != expert[:-1]).astype(I32)])
    lo_in = lo - tile * EXPERT_TILE
    hi_in = hi - tile * EXPERT_TILE
    n_visits = n_et + e
    d_exp = w_gate.shape[-1]
    ys = pl.pallas_call(
        _expert_kernel,
        grid_spec=pltpu.PrefetchScalarGridSpec(
            num_scalar_prefetch=7,
            grid=(n_visits,),
            in_specs=[
                pl.BlockSpec((1, 1, EXPERT_TILE),
                             lambda v, ti, ex, *_: (jnp.maximum(ti[v] - 1, 0), 0, 0),
                             memory_space=pltpu.SMEM),
                pl.BlockSpec((1, 1, EXPERT_TILE), lambda v, ti, ex, *_: (ti[v], 0, 0),
                             memory_space=pltpu.SMEM),
                pl.BlockSpec((EXPERT_TILE * ch, LANES), lambda v, ti, ex, *_: (ti[v], 0)),
                pl.BlockSpec((1, 1, d, d_exp), lambda v, ti, ex, *_: (layer, ex[v], 0, 0)),
                pl.BlockSpec((1, 1, d, d_exp), lambda v, ti, ex, *_: (layer, ex[v], 0, 0)),
                pl.BlockSpec((1, 1, d_exp, d), lambda v, ti, ex, *_: (layer, ex[v], 0, 0)),
            ],
            out_specs=pl.BlockSpec(memory_space=pl.ANY),
            scratch_shapes=[pltpu.VMEM((d, d_exp), BF16), pltpu.VMEM((d, d_exp), BF16),
                            pltpu.VMEM((d_exp, d), BF16), pltpu.VMEM((EXPERT_TILE, d), F32),
                            pltpu.VMEM((2, EXPERT_TILE * ch, LANES), F32),
                            pltpu.SemaphoreType.DMA((2,))],
        ),
        out_shape=jax.ShapeDtypeStruct((n_slots * ch, LANES), F32),
        compiler_params=_cparams("arbitrary"),
        name="moe_experts",
    )(tile, expert, lo_in, hi_in, first, last, newexp, inv.reshape(n_et, 1, EXPERT_TILE),
      inv.reshape(n_et, 1, EXPERT_TILE), xs, w_gate, w_up, w_down)

    d_sh = sg.shape[-1]
    slot_specs = [pl.BlockSpec((ROW_TILE * ch, LANES), lambda i, kk=kk: (kk * n_tiles + i, 0))
                  for kk in range(TOP_K)]
    return pl.pallas_call(
        _combine_kernel,
        grid=(n_tiles,),
        in_specs=[
            pl.BlockSpec((ROW_TILE, SUBLANES), lambda i: (i, 0)),
            pl.BlockSpec((ROW_TILE * ch, LANES), lambda i: (i, 0)),
            pl.BlockSpec((ROW_TILE, d), lambda i: (i, 0)),
            pl.BlockSpec((1, 1, d), lambda i: (gate_row_of(i) * 6 + 5, 0, 0)),
            pl.BlockSpec((d, d_sh), lambda i: (0, 0)),
            pl.BlockSpec((d, d_sh), lambda i: (0, 0)),
            pl.BlockSpec((d_sh, d), lambda i: (0, 0)),
        ] + slot_specs,
        out_specs=pl.BlockSpec((ROW_TILE, d), lambda i: (i, 0)),
        out_shape=jax.ShapeDtypeStruct((t, d), F32),
        compiler_params=_cparams("parallel"),
        name="moe_combine",
    )(wcol, nx, h, modv, sg.astype(BF16), su.astype(BF16), sd.astype(BF16), *([ys] * TOP_K))


def _rope_tables(n_lat, n_ctx, head_dim):
    rows = n_lat // GRID_W
    row = jnp.repeat(jnp.arange(rows, dtype=F32), GRID_W)
    col = jnp.tile(jnp.arange(GRID_W, dtype=F32), rows)
    n_freq = head_dim // 4
    inv = ROPE_THETA ** (-jnp.arange(n_freq, dtype=F32) / n_freq)
    ang = jnp.concatenate([row[:, None] * inv, col[:, None] * inv], axis=-1)
    cos = jnp.concatenate([jnp.ones((n_ctx, head_dim // 2), F32), jnp.cos(ang)], axis=0)
    sin = jnp.concatenate([jnp.zeros((n_ctx, head_dim // 2), F32), jnp.sin(ang)], axis=0)
    return cos, sin


def _split_halves_perm(head_dim):
    return np.concatenate([np.arange(0, head_dim, 2), np.arange(1, head_dim, 2)])


def kernel(x, c, ctx, c_ctx, mod_w, mod_b, norm1_g, norm2_g, ev_w_in, ev_w_out, a_ln_g, a_ln_b, a_ws, a_bs, b_q_norm, b_k_norm, b_lam_q1, b_lam_k1, b_lam_q2, b_lam_k2, b_subln, od_w_qkv, od_w_out, c_q_norm, c_k_norm, moe_router, moe_bias, moe_w_gate, moe_w_up, moe_w_down, sh_w_gate, sh_w_up, sh_w_down):
    bsz, n_lat, d = x.shape
    n_ctx = ctx.shape[1]
    depth = mod_w.shape[0]
    assert depth == 2 and n_ctx == ROW_TILE and n_lat % ROW_TILE == 0 and bsz + 1 <= MOD_ROWS
    assert d == SUBLANES * LANES
    n_seq = n_ctx + n_lat
    tpb = n_seq // ROW_TILE
    lpb = n_lat // ROW_TILE
    t_all = bsz * n_seq
    n_tiles = t_all // ROW_TILE
    ctx_row = bsz

    cond = jnp.zeros((MOD_ROWS, d), F32).at[:bsz].set(c).at[ctx_row].set(c_ctx)
    mod = _adaln(cond, mod_w, mod_b)
    modv = [mod[l].reshape(MOD_ROWS * 6, 1, d) for l in range(depth)]

    def row_all(i):
        return jnp.where(i % tpb == 0, ctx_row, i // tpb)

    def mspec(j, row_of):
        return pl.BlockSpec((1, 1, d), lambda i: (row_of(i) * 6 + j, 0, 0))

    def full(shape):
        return pl.BlockSpec(shape, lambda *_: (0,) * len(shape))

    x_spec = pl.BlockSpec((1, ROW_TILE, d), lambda i: (i // tpb, jnp.maximum(i % tpb - 1, 0), 0))
    ctx_spec = pl.BlockSpec((1, ROW_TILE, d), lambda i: (i // tpb, 0, 0))
    ch = d // LANES
    tok_spec = pl.BlockSpec((ROW_TILE * ch, LANES), lambda i: (i, 0))

    lam_init = 0.8 - 0.6 * math.exp(-0.3 * 0)
    p64 = _split_halves_perm(B_HEAD_DIM)
    col_perm = np.concatenate(
        [np.arange(2 * A_WIDTH)]
        + [2 * A_WIDTH + blk * B_HEAD_DIM + p64 for blk in range(2 * B_WIDTH // B_HEAD_DIM)]
        + [np.arange(2 * A_WIDTH + 2 * B_WIDTH, 2 * A_WIDTH + 3 * B_WIDTH)])
    w_in = ev_w_in[0][:, col_perm].astype(BF16)
    even_in = w_in.shape[1]
    cos_b, sin_b = _rope_tables(n_lat, n_ctx, B_HEAD_DIM)
    zeros_b = jnp.zeros_like(sin_b)
    tab_c = jnp.tile(jnp.concatenate([cos_b, cos_b], axis=-1), (1, 2))
    tab_sa = jnp.tile(jnp.concatenate([-sin_b, zeros_b], axis=-1), (1, 2))
    tab_sb = jnp.tile(jnp.concatenate([zeros_b, sin_b], axis=-1), (1, 2))
    qg = jnp.tile(b_q_norm[0][p64], 2).reshape(1, LANES)
    kg = jnp.tile(b_k_norm[0][p64], 2).reshape(1, LANES)
    tab_spec = pl.BlockSpec((ROW_TILE, LANES), lambda i: (i % tpb, 0))
    row_spec = lambda w: pl.BlockSpec((ROW_TILE, w), lambda i: (i, 0))
    uv, q, k, v = pl.pallas_call(
        functools.partial(_even_in_kernel, tiles_per_batch=tpb),
        grid=(n_tiles,),
        in_specs=[x_spec, ctx_spec, mspec(0, row_all), mspec(1, row_all), full((1, d)),
                  full((d, even_in)), full((1, LANES)), full((1, LANES)),
                  tab_spec, tab_spec, tab_spec],
        out_specs=[row_spec(2 * A_WIDTH), row_spec(B_WIDTH), row_spec(B_WIDTH), row_spec(B_WIDTH)],
        out_shape=[jax.ShapeDtypeStruct((t_all, 2 * A_WIDTH), F32),
                   jax.ShapeDtypeStruct((t_all, B_WIDTH), BF16),
                   jax.ShapeDtypeStruct((t_all, B_WIDTH), BF16),
                   jax.ShapeDtypeStruct((t_all, B_WIDTH), BF16)],
        compiler_params=_cparams("parallel"),
        name="even_in",
    )(x, ctx, modv[0], modv[0], norm1_g[0].reshape(1, d), w_in, qg, kg, tab_c, tab_sa, tab_sb)

    lamv = jnp.zeros((SUBLANES, LANES), F32)
    for r, vec in enumerate((b_lam_q1[0], b_lam_k1[0], b_lam_q2[0], b_lam_k2[0])):
        lamv = lamv.at[r, :B_HEAD_DIM].set(vec)
    o = pl.pallas_call(
        functools.partial(_diff_attn_kernel, ctx_len=n_ctx, lam_init=lam_init),
        grid=(bsz, tpb),
        in_specs=[
            pl.BlockSpec((SUBLANES, LANES), lambda b, qi: (0, 0)),
            pl.BlockSpec((ROW_TILE, B_WIDTH), lambda b, qi: (b * tpb + qi, 0)),
            pl.BlockSpec((n_seq, B_WIDTH), lambda b, qi: (b, 0)),
            pl.BlockSpec((n_seq, B_WIDTH), lambda b, qi: (b, 0)),
        ],
        out_specs=pl.BlockSpec((ROW_TILE, B_WIDTH), lambda b, qi: (b * tpb + qi, 0)),
        out_shape=jax.ShapeDtypeStruct((t_all, B_WIDTH), F32),
        compiler_params=_cparams("parallel", "arbitrary"),
        name="diff_attn",
    )(lamv, q, k, v)

    bs_col = jnp.repeat(a_bs[0].T, A_GROUP_DIM, axis=1)
    sub_g = b_subln[0].reshape(1, LANES)
    h1, nx = pl.pallas_call(
        functools.partial(_even_out_kernel, lam_init=lam_init, tiles_per_batch=tpb),
        grid=(n_tiles,),
        in_specs=[row_spec(B_WIDTH), row_spec(2 * A_WIDTH), x_spec, ctx_spec,
                  mspec(2, row_all), mspec(3, row_all), mspec(4, row_all),
                  full((1, LANES)), full((1, A_WIDTH)), full((1, A_WIDTH)),
                  full((A_GROUPS, GMLP_CHUNK, GMLP_CHUNK)), full((GMLP_CHUNK, A_WIDTH)),
                  full((A_WIDTH + B_WIDTH, d)), full((1, d))],
        out_specs=[row_spec(d), tok_spec],
        out_shape=[jax.ShapeDtypeStruct((t_all, d), F32),
                   jax.ShapeDtypeStruct((t_all * ch, LANES), F32)],
        compiler_params=_cparams("parallel"),
        name="even_out",
    )(o, uv, x, ctx, modv[0], modv[0], modv[0], sub_g, a_ln_g[0].reshape(1, A_WIDTH),
      a_ln_b[0].reshape(1, A_WIDTH), a_ws[0].astype(BF16), bs_col,
      ev_w_out[0].astype(BF16), norm2_g[0].reshape(1, d))

    h2 = _moe(nx, h1, modv[0], row_all, 0, moe_router[0], moe_bias[0], moe_w_gate, moe_w_up,
              moe_w_down, sh_w_gate[0], sh_w_up[0], sh_w_down[0])

    p128 = _split_halves_perm(C_HEAD_DIM)
    n_qkv_heads = C_HEADS + 2 * C_KV_HEADS
    col_perm = np.concatenate(
        [blk * C_HEAD_DIM + p128 for blk in range(C_HEADS + C_KV_HEADS)]
        + [np.arange((C_HEADS + C_KV_HEADS) * C_HEAD_DIM, n_qkv_heads * C_HEAD_DIM)])
    w_qkv = od_w_qkv[0][:, col_perm].astype(BF16)
    cos_c, sin_c = _rope_tables(n_lat, n_ctx, C_HEAD_DIM)
    tab_c1 = jnp.concatenate([cos_c, cos_c], axis=-1)
    tab_s1 = jnp.concatenate([-sin_c, sin_c], axis=-1)
    qg1 = c_q_norm[0][p128].reshape(1, LANES)
    kg1 = c_k_norm[0][p128].reshape(1, LANES)
    nq = C_HEADS * C_HEAD_DIM
    nkv = C_KV_HEADS * C_HEAD_DIM
    q1, k1, v1 = pl.pallas_call(
        _odd_in_kernel,
        grid=(n_tiles,),
        in_specs=[row_spec(d), mspec(0, row_all), mspec(1, row_all), full((1, d)),
                  full((d, nq + 2 * nkv)), full((1, LANES)), full((1, LANES)), tab_spec, tab_spec],
        out_specs=[row_spec(nq), row_spec(nkv), row_spec(nkv)],
        out_shape=[jax.ShapeDtypeStruct((t_all, nq), BF16),
                   jax.ShapeDtypeStruct((t_all, nkv), BF16),
                   jax.ShapeDtypeStruct((t_all, nkv), BF16)],
        compiler_params=_cparams("parallel"),
        name="odd_in",
    )(h2, modv[1], modv[1], norm1_g[1].reshape(1, d), w_qkv, qg1, kg1, tab_c1, tab_s1)

    t_lat = bsz * n_lat
    grp = C_HEADS // C_KV_HEADS
    o1 = pl.pallas_call(
        _gqa_kernel,
        grid=(bsz, C_KV_HEADS, lpb),
        in_specs=[
            pl.BlockSpec((ROW_TILE, grp * LANES), lambda b, n, qi: (b * tpb + 1 + qi, n)),
            pl.BlockSpec((n_seq, LANES), lambda b, n, qi: (b, n)),
            pl.BlockSpec((n_seq, LANES), lambda b, n, qi: (b, n)),
        ],
        out_specs=pl.BlockSpec((ROW_TILE, grp * LANES), lambda b, n, qi: (b * lpb + qi, n)),
        out_shape=jax.ShapeDtypeStruct((t_lat, nq), BF16),
        compiler_params=_cparams("parallel", "parallel", "arbitrary"),
        name="gqa_attn",
    )(q1, k1, v1)

    def row_lat(i):
        return i // lpb

    lat_tiles = t_lat // ROW_TILE
    hx, nx1 = pl.pallas_call(
        _odd_out_kernel,
        grid=(lat_tiles,),
        in_specs=[row_spec(nq),
                  pl.BlockSpec((ROW_TILE, d), lambda i: ((i // lpb) * tpb + 1 + i % lpb, 0)),
                  mspec(2, row_lat), mspec(3, row_lat), mspec(4, row_lat),
                  full((nq, d)), full((1, d))],
        out_specs=[row_spec(d), tok_spec],
        out_shape=[jax.ShapeDtypeStruct((t_lat, d), F32),
                   jax.ShapeDtypeStruct((t_lat * ch, LANES), F32)],
        compiler_params=_cparams("parallel"),
        name="odd_out",
    )(o1, h2, modv[1], modv[1], modv[1], od_w_out[0].astype(BF16), norm2_g[1].reshape(1, d))

    out = _moe(nx1, hx, modv[1], row_lat, 1, moe_router[1], moe_bias[1], moe_w_gate, moe_w_up,
               moe_w_down, sh_w_gate[1], sh_w_up[1], sh_w_down[1])
    return out.reshape(bsz, n_lat, d)
```

```python
import functools
import math

import numpy as np
import jax
import jax.numpy as jnp
from jax import lax
from jax.experimental import pallas as pl
from jax.experimental.pallas import tpu as pltpu

F32 = jnp.float32
BF16 = jnp.bfloat16
I32 = jnp.int32

GRID_W = 64
EPS = 1e-6
ROPE_THETA = 10000.0
A_GROUPS = 4
A_GROUP_DIM = 128
A_WIDTH = A_GROUPS * A_GROUP_DIM
GMLP_CHUNK = 128
B_HEADS = 4
B_HEAD_DIM = 64
B_WIDTH = B_HEADS * 2 * B_HEAD_DIM
C_HEADS = 8
C_KV_HEADS = 2
C_HEAD_DIM = 128
N_EXPERTS = 64
TOP_K = 6
N_GROUPS = 8
TOPK_GROUPS = 4
ROUTE_SCALE = 2.5
LOG2E = math.log2(math.e)

LANES = 128
SUBLANES = 8
ROW_TILE = 256
ROUTER_TILE = ROW_TILE
EXPERT_TILE = 512
PROJ_BLOCK = 256
MOD_ROWS = 24
VMEM_LIMIT = 56 * 1024 * 1024

NT_DIMS = (((1,), (1,)), ((), ()))


def _cparams(*sem):
    return pltpu.CompilerParams(dimension_semantics=sem, vmem_limit_bytes=VMEM_LIMIT)


def _rms(x, g):
    return x * lax.rsqrt(jnp.mean(x * x, axis=-1, keepdims=True) + EPS) * g


def _norm_mod(h, g, shift, scale):
    return _rms(h, g) * (1.0 + scale) + shift


def _gelu(x):
    return 0.5 * x * (1.0 + lax.erf(x * np.float32(math.sqrt(0.5))))


def _silu(x):
    return x * jax.nn.sigmoid(x)


def _bdot(a, b):
    return jnp.dot(a.astype(BF16), b.astype(BF16), preferred_element_type=F32)


def _from_token_tiles(ref, rows, d):
    ch = d // LANES
    groups = []
    for g in range(rows // SUBLANES):
        groups.append(jnp.concatenate(
            [ref[pl.ds(g * SUBLANES * ch + j, SUBLANES, stride=ch), :] for j in range(ch)], axis=-1))
    return jnp.concatenate(groups, axis=0)


def _to_token_tiles(ref, val):
    rows, d = val.shape
    ch = d // LANES
    for g in range(rows // SUBLANES):
        for j in range(ch):
            ref[pl.ds(g * SUBLANES * ch + j, SUBLANES, stride=ch), :] = (
                val[g * SUBLANES:(g + 1) * SUBLANES, j * LANES:(j + 1) * LANES])


def _adaln_kernel(c_ref, w_ref, b_ref, o_ref):
    o_ref[0] = _bdot(_silu(c_ref[...]), w_ref[0]) + b_ref[0]


def _adaln(cond, mod_w, mod_b):
    depth, d, d6 = mod_w.shape
    tn = d6 // 4
    return pl.pallas_call(
        _adaln_kernel,
        grid=(depth, d6 // tn),
        in_specs=[
            pl.BlockSpec((MOD_ROWS, d), lambda l, j: (0, 0)),
            pl.BlockSpec((1, d, tn), lambda l, j: (l, 0, j)),
            pl.BlockSpec((1, 1, tn), lambda l, j: (l, 0, j)),
        ],
        out_specs=pl.BlockSpec((1, MOD_ROWS, tn), lambda l, j: (l, 0, j)),
        out_shape=jax.ShapeDtypeStruct((depth, MOD_ROWS, d6), F32),
        compiler_params=_cparams("parallel", "parallel"),
        name="adaln",
    )(cond, mod_w, mod_b.reshape(depth, 1, d6))


def _pick_stream(x_ref, ctx_ref, tiles_per_batch):
    is_ctx = pl.program_id(0) % tiles_per_batch == 0
    return jnp.where(is_ctx, ctx_ref[0], x_ref[0])


def _even_in_kernel(x_ref, ctx_ref, sh_ref, sc_ref, g_ref, w_ref, qg_ref, kg_ref, c_ref, sa_ref,
                    sb_ref, uv_ref, q_ref, k_ref, v_ref, *, tiles_per_batch):
    h = _pick_stream(x_ref, ctx_ref, tiles_per_batch)
    n = _norm_mod(h, g_ref[...], sh_ref[0], sc_ref[0]).astype(BF16)

    def cols(c0):
        return jnp.dot(n, w_ref[:, c0:c0 + PROJ_BLOCK], preferred_element_type=F32)

    for c0 in range(0, 2 * A_WIDTH, PROJ_BLOCK):
        uv_ref[:, c0:c0 + PROJ_BLOCK] = _gelu(cols(c0))
    cos, sa, sb = c_ref[...], sa_ref[...], sb_ref[...]
    lane = lax.broadcasted_iota(I32, cos.shape, 1)
    low = lane < B_HEAD_DIM

    def head_pair(x, gain, scale):
        sq = x * x
        s_lo = jnp.sum(jnp.where(low, sq, 0.0), axis=-1, keepdims=True)
        s_hi = jnp.sum(jnp.where(low, 0.0, sq), axis=-1, keepdims=True)
        ms = jnp.where(low, s_lo, s_hi) * np.float32(1.0 / B_HEAD_DIM)
        y = x * lax.rsqrt(ms + EPS) * gain
        y = (y * cos + pltpu.roll(y, LANES - B_HEAD_DIM // 2, 1) * sa
             + pltpu.roll(y, B_HEAD_DIM // 2, 1) * sb)
        if scale is not None:
            y = y * scale
        return y.astype(BF16)

    q0 = 2 * A_WIDTH
    k0 = q0 + B_WIDTH
    v0 = k0 + B_WIDTH
    per = PROJ_BLOCK // LANES
    for c0 in range(0, B_WIDTH, PROJ_BLOCK):
        pq = cols(q0 + c0)
        pk = cols(k0 + c0)
        for j in range(per):
            sl = slice(c0 + j * LANES, c0 + (j + 1) * LANES)
            q_ref[:, sl] = head_pair(pq[:, j * LANES:(j + 1) * LANES], qg_ref[...],
                                     np.float32(B_HEAD_DIM ** -0.5))
            k_ref[:, sl] = head_pair(pk[:, j * LANES:(j + 1) * LANES], kg_ref[...], None)
        v_ref[:, c0:c0 + PROJ_BLOCK] = cols(v0 + c0).astype(BF16)


def _odd_in_kernel(h_ref, sh_ref, sc_ref, g_ref, w_ref, qg_ref, kg_ref, c_ref, s_ref,
                   q_ref, k_ref, v_ref):
    n = _norm_mod(h_ref[...], g_ref[...], sh_ref[0], sc_ref[0])
    p = jnp.dot(n.astype(BF16), w_ref[...], preferred_element_type=F32)
    cos, sin = c_ref[...], s_ref[...]

    def head(x, gain, scale):
        y = _rms(x, gain)
        y = y * cos + pltpu.roll(y, C_HEAD_DIM // 2, 1) * sin
        if scale is not None:
            y = y * scale
        return y.astype(BF16)

    nq = C_HEADS * C_HEAD_DIM
    nkv = C_KV_HEADS * C_HEAD_DIM
    for j in range(C_HEADS):
        q_ref[:, j * LANES:(j + 1) * LANES] = head(p[:, j * LANES:(j + 1) * LANES], qg_ref[...],
                                                   None)
    for j in range(C_KV_HEADS):
        k_ref[:, j * LANES:(j + 1) * LANES] = head(
            p[:, nq + j * LANES:nq + (j + 1) * LANES], kg_ref[...], None)
    v_ref[...] = p[:, nq + nkv:nq + 2 * nkv].astype(BF16)


def _diff_attn_kernel(lam_ref, q_ref, k_ref, v_ref, o_ref, *, ctx_len, lam_init):
    lv = lam_ref[...]
    lam = (jnp.exp(jnp.sum(lv[0:1] * lv[1:2], axis=-1, keepdims=True))
           - jnp.exp(jnp.sum(lv[2:3] * lv[3:4], axis=-1, keepdims=True)) + np.float32(lam_init))
    low = lax.broadcasted_iota(I32, (q_ref.shape[0], LANES), 1) < B_HEAD_DIM

    def softmax(qm, k, scale):
        s = lax.dot_general(qm, k, NT_DIMS, preferred_element_type=F32)
        p = jnp.exp(s - jnp.max(s, axis=-1, keepdims=True))
        return p * (scale / jnp.sum(p, axis=-1, keepdims=True))

    def attend(n_keys):
        for hh in range(q_ref.shape[1] // LANES):
            cs = slice(hh * LANES, (hh + 1) * LANES)
            q = q_ref[:, cs]
            zero = jnp.zeros_like(q)
            k = k_ref[0:n_keys, cs]
            a = (softmax(jnp.where(low, q, zero), k, 1.0)
                 - softmax(jnp.where(low, zero, q), k, lam))
            o_ref[:, cs] = jnp.dot(a.astype(BF16), v_ref[0:n_keys, cs], preferred_element_type=F32)

    is_ctx = pl.program_id(1) == 0

    @pl.when(is_ctx)
    def _():
        attend(ctx_len)

    @pl.when(jnp.logical_not(is_ctx))
    def _():
        attend(k_ref.shape[0])


def _gqa_kernel(q_ref, k_ref, v_ref, o_ref):
    grp = q_ref.shape[1] // LANES
    c = np.float32(C_HEAD_DIM ** -0.5 * LOG2E)
    k = k_ref[...]
    v = v_ref[...]
    for g in range(grp):
        cs = slice(g * LANES, (g + 1) * LANES)
        s = lax.dot_general(q_ref[:, cs], k, NT_DIMS, preferred_element_type=F32)
        p = jnp.exp2((s - jnp.max(s, axis=-1, keepdims=True)) * c)
        l = jnp.sum(p, axis=-1, keepdims=True)
        o = jnp.dot(p.astype(BF16), v, preferred_element_type=F32) / l
        o_ref[:, cs] = o.astype(o_ref.dtype)


def _even_out_kernel(o_ref, uv_ref, x_ref, ctx_ref, gate_ref, sh_ref, sc_ref, sub_ref, lng_ref,
                     lnb_ref, ws_ref, bs_ref, w_ref, g2_ref, h1_ref, nx_ref, *, lam_init,
                     tiles_per_batch):
    o = o_ref[...]
    uv = uv_ref[...]
    u = uv[:, :A_WIDTH]
    v = uv[:, A_WIDTH:]
    mu = jnp.mean(v, axis=-1, keepdims=True)
    var = jnp.mean(jnp.square(v - mu), axis=-1, keepdims=True)
    vn = ((v - mu) * lax.rsqrt(var + EPS) * lng_ref[...] + lnb_ref[...]).astype(BF16)
    rows = o.shape[0]
    parts = []
    for c in range(rows // GMLP_CHUNK):
        rs = slice(c * GMLP_CHUNK, (c + 1) * GMLP_CHUNK)
        for g in range(A_GROUPS):
            cs = slice(g * A_GROUP_DIM, (g + 1) * A_GROUP_DIM)
            mixed = jnp.dot(ws_ref[g], vn[rs, cs], preferred_element_type=F32) + bs_ref[:, cs]
            parts.append((c, g, u[rs, cs] * mixed))
    a_rows = [jnp.concatenate([p for (c2, _, p) in parts if c2 == c], axis=-1)
              for c in range(rows // GMLP_CHUNK)]
    a = jnp.concatenate(a_rows, axis=0)
    heads = []
    for hh in range(B_HEADS):
        oh = o[:, hh * LANES:(hh + 1) * LANES]
        heads.append(_rms(oh, sub_ref[...]) * np.float32(1.0 - lam_init))
    cat = jnp.concatenate([a] + heads, axis=-1).astype(BF16)
    y = jnp.dot(cat, w_ref[...], preferred_element_type=F32)
    h1 = _pick_stream(x_ref, ctx_ref, tiles_per_batch) + gate_ref[0] * y
    h1_ref[...] = h1
    _to_token_tiles(nx_ref, _norm_mod(h1, g2_ref[...], sh_ref[0], sc_ref[0]))


def _odd_out_kernel(o_ref, h_ref, gate_ref, sh_ref, sc_ref, w_ref, g2_ref, h1_ref, nx_ref):
    y = jnp.dot(o_ref[...], w_ref[...], preferred_element_type=F32)
    h1 = h_ref[...] + gate_ref[0] * y
    h1_ref[...] = h1
    _to_token_tiles(nx_ref, _norm_mod(h1, g2_ref[...], sh_ref[0], sc_ref[0]))


def _rows_to_block(rows, dtype):
    n = rows[0].shape[1]
    rio = lax.broadcasted_iota(I32, (SUBLANES, n), 0)
    out = jnp.zeros((SUBLANES, n), dtype)
    for r, row in enumerate(rows):
        out = jnp.where(rio == r, jnp.broadcast_to(row.astype(dtype), (SUBLANES, n)), out)
    return out


def _transpose_block(xt):
    n = xt.shape[1]
    eye = jnp.where(lax.broadcasted_iota(I32, (n, n), 0) == lax.broadcasted_iota(I32, (n, n), 1),
                    1.0, 0.0).astype(BF16)
    acc = jnp.zeros((n, SUBLANES), F32)
    rem = xt
    for _ in range(3):
        part = rem.astype(BF16)
        acc = acc + lax.dot_general(eye, part, NT_DIMS, preferred_element_type=F32)
        rem = rem - part.astype(F32)
    return acc


def _router_kernel(x_ref, wr_ref, b_ref, eidx_ref, rank_ref, lrank_ref, wcol_ref, cnt_ref, tcnt_ref,
                   run_ref):
    @pl.when(pl.program_id(0) == 0)
    def _():
        run_ref[...] = jnp.zeros_like(run_ref)

    per = N_EXPERTS // N_GROUPS
    d = wr_ref.shape[1]
    x = _from_token_tiles(x_ref, x_ref.shape[0] * LANES // d, d)
    logits = lax.dot_general(wr_ref[...], x.astype(BF16), NT_DIMS,
                             preferred_element_type=F32)
    scores = jax.nn.sigmoid(logits)
    sel = scores + b_ref[...]
    tm = sel.shape[1]
    neg = np.float32(-np.inf)
    jio = lax.broadcasted_iota(I32, (per, tm), 0).astype(F32)
    gio = lax.broadcasted_iota(I32, (N_GROUPS, tm), 0).astype(F32)

    def rmax(x):
        return jnp.max(x, axis=0, keepdims=True)

    def rmin(x):
        return jnp.min(x, axis=0, keepdims=True)

    sel_g = [sel[g * per:(g + 1) * per, :] for g in range(N_GROUPS)]
    sc_g = [scores[g * per:(g + 1) * per, :] for g in range(N_GROUPS)]
    gs = jnp.zeros((N_GROUPS, tm), F32)
    for g in range(N_GROUPS):
        m1 = rmax(sel_g[g])
        i1 = rmin(jnp.where(sel_g[g] == m1, jio, np.float32(per)))
        m2 = rmax(jnp.where(jio == i1, neg, sel_g[g]))
        gs = jnp.where(gio == np.float32(g), jnp.broadcast_to(m1 + m2, gs.shape), gs)
    gsel = jnp.zeros((N_GROUPS, tm), I32)
    for _ in range(TOPK_GROUPS):
        m = rmax(gs)
        idx = rmin(jnp.where(gs == m, gio, np.float32(N_GROUPS)))
        hit = gio == idx
        gsel = jnp.where(hit, 1, gsel)
        gs = jnp.where(hit, neg, gs)
    masked = [jnp.where(jnp.broadcast_to(gsel[g:g + 1, :], (per, tm)) == 1, sel_g[g], neg)
              for g in range(N_GROUPS)]
    eio = [jio + np.float32(g * per) for g in range(N_GROUPS)]
    e_rows, w_rows, hits = [], [], []
    for _ in range(TOP_K):
        m = masked[0]
        for g in range(1, N_GROUPS):
            m = jnp.maximum(m, masked[g])
        m = rmax(m)
        cand = jnp.where(masked[0] == m, eio[0], np.float32(N_EXPERTS))
        for g in range(1, N_GROUPS):
            cand = jnp.minimum(cand, jnp.where(masked[g] == m, eio[g], np.float32(N_EXPERTS)))
        idx = rmin(cand)
        hit = [eio[g] == idx for g in range(N_GROUPS)]
        wsel = jnp.where(hit[0], sc_g[0], 0.0)
        for g in range(1, N_GROUPS):
            wsel = wsel + jnp.where(hit[g], sc_g[g], 0.0)
        masked = [jnp.where(hit[g], neg, masked[g]) for g in range(N_GROUPS)]
        e_rows.append(idx)
        w_rows.append(jnp.sum(wsel, axis=0, keepdims=True))
        hits.append(hit)
    wsum = w_rows[0]
    for r in w_rows[1:]:
        wsum = wsum + r
    w_rows = [r / wsum * np.float32(ROUTE_SCALE) for r in w_rows]
    onehot = []
    for g in range(N_GROUPS):
        any_hit = hits[0][g]
        for kk in range(1, TOP_K):
            any_hit = jnp.logical_or(any_hit, hits[kk][g])
        onehot.append(jnp.where(any_hit, 1.0, 0.0))
    mt = jnp.concatenate(onehot, axis=0)
    before = (lax.broadcasted_iota(I32, (tm, tm), 0) < lax.broadcasted_iota(I32, (tm, tm), 1))
    prefix = jnp.dot(mt.astype(BF16), jnp.where(before, 1.0, 0.0).astype(BF16),
                     preferred_element_type=F32)
    def pick(table):
        rows = []
        for kk in range(TOP_K):
            acc = jnp.where(hits[kk][0], table[0:per, :], 0.0)
            for g in range(1, N_GROUPS):
                acc = acc + jnp.where(hits[kk][g], table[g * per:(g + 1) * per, :], 0.0)
            rows.append(jnp.sum(acc, axis=0, keepdims=True))
        return rows

    here = jnp.sum(mt, axis=1, keepdims=True)
    run = run_ref[...] + here
    eidx_ref[...] = _rows_to_block(e_rows, I32)
    rank_ref[...] = _rows_to_block(pick(prefix + run_ref[...]), I32)
    lrank_ref[...] = _rows_to_block(pick(prefix), I32)
    wcol_ref[...] = _transpose_block(_rows_to_block(w_rows, F32))
    cnt_ref[...] = jnp.broadcast_to(run, cnt_ref.shape)
    tcnt_ref[...] = jnp.broadcast_to(here, tcnt_ref.shape)
    run_ref[...] = run


def _dest_kernel(start_ref, eidx_ref, rank_ref, dest_ref):
    per = N_EXPERTS // N_GROUPS
    eidx = eidx_ref[...]
    tm = eidx.shape[1]
    jio = lax.broadcasted_iota(I32, (per, tm), 0)
    rows = []
    for kk in range(TOP_K):
        e = jnp.broadcast_to(eidx[kk:kk + 1, :], (per, tm))
        acc = jnp.zeros((per, tm), F32)
        for g in range(N_GROUPS):
            st = jnp.broadcast_to(start_ref[g * per:(g + 1) * per, :], (per, tm))
            acc = acc + jnp.where(jio + g * per == e, st, 0.0)
        rows.append(jnp.sum(acc, axis=0, keepdims=True))
    dest_ref[...] = _rows_to_block(rows, I32) + rank_ref[...]


def _row_copy(src, s_row, dst, d_row, sem):
    def tile_start(row):
        start = row * SUBLANES
        return start if isinstance(row, int) else pl.multiple_of(start, SUBLANES)

    s0 = tile_start(s_row)
    d0 = tile_start(d_row)
    return pltpu.make_async_copy(src.at[pl.ds(s0, SUBLANES)], dst.at[pl.ds(d0, SUBLANES)], sem)


def _dispatch_kernel(lpos_ref, cnt_ref, loff_ref, xpos_ref, x_ref, xs_ref, stage, sem):
    rows = x_ref.shape[0] // SUBLANES

    def tile_rows(row, n):
        return pl.ds(pl.multiple_of(row * SUBLANES, SUBLANES), n * SUBLANES)

    def place(t2, carry):
        for u in range(2):
            t = t2 * 2 + u
            tok = x_ref[tile_rows(t, 1), :]
            for kk in range(TOP_K):
                stage[tile_rows(lpos_ref[kk, t], 1), :] = tok
        return carry

    lax.fori_loop(0, rows // 2, place, 0)

    sizes = [1 << b for b in range(rows.bit_length() - 1, -1, -1)]

    def send(e, carry):
        n = cnt_ref[0, 0, e]
        src = loff_ref[0, 0, e]
        dst = xpos_ref[0, 0, e]
        for j, size in enumerate(sizes):
            take = (n & size) != 0

            @pl.when(take)
            def _(src=src, dst=dst, size=size, j=j):
                pltpu.make_async_copy(stage.at[tile_rows(src, size)], xs_ref.at[tile_rows(dst, size)],
                                      sem).start(priority=j % 2)

            step = jnp.where(take, size, 0)
            src = src + step
            dst = dst + step
        return carry

    lax.fori_loop(0, cnt_ref.shape[2], send, 0)
    pltpu.make_async_copy(stage, stage, sem).wait()


def _expert_kernel(tile_s, exp_s, lo_s, hi_s, first_s, last_s, new_s, inv_prev_ref, inv_ref,
                   xs_ref, wg_ref, wu_ref, wd_ref, ys_ref, wg_b, wu_b, wd_b, acc, stage, sem):
    v = pl.program_id(0)
    tile = tile_s[v]
    rows, d = acc.shape

    @pl.when(new_s[v] == 1)
    def _():
        wg_b[...] = wg_ref[0, 0].astype(BF16)
        wu_b[...] = wu_ref[0, 0].astype(BF16)
        wd_b[...] = wd_ref[0, 0].astype(BF16)

    @pl.when(first_s[v] == 1)
    def _():
        acc[...] = jnp.zeros_like(acc)

    lo = lo_s[v]
    hi = hi_s[v]
    nonempty = hi > lo
    flush_prev = jnp.logical_and(first_s[v] == 1, tile >= 1)

    def compute():
        x = _from_token_tiles(xs_ref, rows, d).astype(BF16)
        g = jnp.dot(x, wg_b[...], preferred_element_type=F32)
        u = jnp.dot(x, wu_b[...], preferred_element_type=F32)
        y = jnp.dot((_silu(g) * u).astype(BF16), wd_b[...], preferred_element_type=F32)
        row = lax.broadcasted_iota(I32, (rows, 1), 0)
        mine = jnp.logical_and(row >= lo, row < hi)
        acc[...] = jnp.where(mine, y, acc[...])

    def start_scatter(idx_ref, slot, unrolled):
        def start(r, par):
            _row_copy(stage.at[slot], r, ys_ref, idx_ref[0, 0, r], sem.at[slot]).start(priority=par)

        if unrolled:
            for r in range(rows):
                start(r, r % 2)
        else:
            def body(r2, carry):
                for par in range(2):
                    start(r2 * 2 + par, par)
                return carry
            lax.fori_loop(0, rows // 2, body, 0)

    def wait_scatter(slot):
        pltpu.make_async_copy(stage.at[slot], stage.at[slot], sem.at[slot]).wait()

    for slot in range(2):
        prev_here = jnp.logical_and(flush_prev, (tile + 1) % 2 == slot)

        @pl.when(jnp.logical_and(prev_here, nonempty))
        def _(slot=slot):
            start_scatter(inv_prev_ref, slot, True)
            compute()

        @pl.when(jnp.logical_and(prev_here, jnp.logical_not(nonempty)))
        def _(slot=slot):
            start_scatter(inv_prev_ref, slot, False)

    @pl.when(jnp.logical_and(nonempty, jnp.logical_not(flush_prev)))
    def _():
        compute()

    is_final = v == pl.num_programs(0) - 1
    for slot in range(2):
        @pl.when(jnp.logical_and(last_s[v] == 1, tile % 2 == slot))
        def _(slot=slot):
            @pl.when(tile >= 2)
            def _():
                wait_scatter(slot)

            _to_token_tiles(stage.at[slot], acc[...])

            @pl.when(is_final)
            def _():
                start_scatter(inv_ref, slot, False)

    @pl.when(is_final)
    def _():
        for slot in range(2):
            @pl.when(jnp.logical_or(tile >= 1, tile % 2 == slot))
            def _(slot=slot):
                wait_scatter(slot)


def _combine_kernel(wcol_ref, x_ref, h_ref, gate_ref, sg_ref, su_ref, sd_ref, *refs):
    y_refs, o_ref = refs[:TOP_K], refs[TOP_K]
    rows, d = h_ref.shape
    x = _from_token_tiles(x_ref, rows, d).astype(BF16)
    g = jnp.dot(x, sg_ref[...], preferred_element_type=F32)
    u = jnp.dot(x, su_ref[...], preferred_element_type=F32)
    acc = jnp.dot((_silu(g) * u).astype(BF16), sd_ref[...], preferred_element_type=F32)
    wcol = wcol_ref[...]
    for kk in range(TOP_K):
        acc = acc + _from_token_tiles(y_refs[kk], rows, d) * wcol[:, kk:kk + 1]
    o_ref[...] = h_ref[...] + gate_ref[0] * acc


def _moe(nx, h, modv, gate_row_of, layer, w_router, router_bias, w_gate, w_up, w_down, sg, su, sd):
    t, d = h.shape
    ch = d // LANES
    n_slots = t * TOP_K
    n_rt = t // ROUTER_TILE
    e = N_EXPERTS
    slot_spec = pl.BlockSpec((SUBLANES, ROUTER_TILE), lambda i: (0, i))
    slot_shape = jax.ShapeDtypeStruct((SUBLANES, t), I32)
    eidx, rank, lrank, wcol, cnt, tcnt = pl.pallas_call(
        _router_kernel,
        grid=(n_rt,),
        in_specs=[
            pl.BlockSpec((ROUTER_TILE * ch, LANES), lambda i: (i, 0)),
            pl.BlockSpec((e, d), lambda i: (0, 0)),
            pl.BlockSpec((e, 1), lambda i: (0, 0)),
        ],
        out_specs=[
            slot_spec, slot_spec, slot_spec,
            pl.BlockSpec((ROUTER_TILE, SUBLANES), lambda i: (i, 0)),
            pl.BlockSpec((e, LANES), lambda i: (0, 0)),
            pl.BlockSpec((e, LANES), lambda i: (i, 0)),
        ],
        out_shape=[
            slot_shape, slot_shape, slot_shape,
            jax.ShapeDtypeStruct((t, SUBLANES), F32),
            jax.ShapeDtypeStruct((e, LANES), F32),
            jax.ShapeDtypeStruct((n_rt * e, LANES), F32),
        ],
        scratch_shapes=[pltpu.VMEM((e, 1), F32)],
        compiler_params=_cparams("arbitrary"),
        name="moe_router",
    )(nx, w_router.T.astype(BF16), router_bias.reshape(e, 1))

    counts = cnt[:, 0].astype(I32)
    ends = jnp.cumsum(counts)
    starts = ends - counts
    tile_cnt = tcnt[:, 0].astype(I32).reshape(n_rt, e)
    tile_off = jnp.cumsum(tile_cnt, axis=1) - tile_cnt
    run_pos = starts[None, :] + jnp.cumsum(tile_cnt, axis=0) - tile_cnt

    def slot_table(table, rows, name):
        return pl.pallas_call(
            _dest_kernel,
            grid=(n_rt,),
            in_specs=[table, slot_spec, slot_spec],
            out_specs=slot_spec,
            out_shape=slot_shape,
            compiler_params=_cparams("parallel"),
            name=name,
        )

    dest = slot_table(pl.BlockSpec((e, 1), lambda i: (0, 0)), rank, "moe_dest")(
        starts.astype(F32).reshape(e, 1), eidx, rank)
    lpos = slot_table(pl.BlockSpec((e, 1), lambda i: (i, 0)), lrank, "moe_lpos")(
        tile_off.astype(F32).reshape(n_rt * e, 1), eidx, lrank)
    n_tiles = t // ROW_TILE
    inv = jnp.argsort(dest[:TOP_K].reshape(-1)).astype(I32)
    run_spec = pl.BlockSpec((1, 1, e), lambda i: (i, 0, 0), memory_space=pltpu.SMEM)
    xs = pl.pallas_call(
        _dispatch_kernel,
        grid=(n_tiles,),
        in_specs=[
            pl.BlockSpec((SUBLANES, ROW_TILE), lambda i: (0, i), memory_space=pltpu.SMEM),
            run_spec, run_spec, run_spec,
            pl.BlockSpec((ROW_TILE * ch, LANES), lambda i: (i, 0)),
        ],
        out_specs=pl.BlockSpec(memory_space=pl.ANY),
        out_shape=jax.ShapeDtypeStruct((n_slots * ch, LANES), F32),
        scratch_shapes=[pltpu.VMEM((TOP_K * ROW_TILE * ch, LANES), F32), pltpu.SemaphoreType.DMA],
        compiler_params=pltpu.CompilerParams(dimension_semantics=("arbitrary",),
                                             vmem_limit_bytes=VMEM_LIMIT, has_side_effects=True),
        name="moe_dispatch",
    )(lpos, tile_cnt.reshape(n_rt, 1, e), tile_off.reshape(n_rt, 1, e),
      run_pos.reshape(n_rt, 1, e), nx)

    n_et = n_slots // EXPERT_TILE
    pts = jnp.sort(jnp.concatenate([jnp.arange(n_et, dtype=I32) * EXPERT_TILE, starts]))
    lo = pts
    hi = jnp.concatenate([pts[1:], jnp.full((1,), n_slots, I32)])
    tile = jnp.minimum(lo // EXPERT_TILE, n_et - 1)
    expert = jnp.minimum(jnp.sum((ends[None, :] <= lo[:, None]).astype(I32), axis=1), e - 1)
    one = jnp.ones((1,), I32)
    tile_change = (tile[1:] != tile[:-1]).astype(I32)
    first = jnp.concatenate([one, tile_change])
    last = jnp.concatenate([tile_change, one])
    newexp = jnp.concatenate([one, (expert[1:] != expert[:-1]).astype(I32)])
    lo_in = lo - tile * EXPERT_TILE
    hi_in = hi - tile * EXPERT_TILE
    n_visits = n_et + e
    d_exp = w_gate.shape[-1]
    ys = pl.pallas_call(
        _expert_kernel,
        grid_spec=pltpu.PrefetchScalarGridSpec(
            num_scalar_prefetch=7,
            grid=(n_visits,),
            in_specs=[
                pl.BlockSpec((1, 1, EXPERT_TILE),
                             lambda v, ti, ex, *_: (jnp.maximum(ti[v] - 1, 0), 0, 0),
                             memory_space=pltpu.SMEM),
                pl.BlockSpec((1, 1, EXPERT_TILE), lambda v, ti, ex, *_: (ti[v], 0, 0),
                             memory_space=pltpu.SMEM),
                pl.BlockSpec((EXPERT_TILE * ch, LANES), lambda v, ti, ex, *_: (ti[v], 0)),
                pl.BlockSpec((1, 1, d, d_exp), lambda v, ti, ex, *_: (layer, ex[v], 0, 0)),
                pl.BlockSpec((1, 1, d, d_exp), lambda v, ti, ex, *_: (layer, ex[v], 0, 0)),
                pl.BlockSpec((1, 1, d_exp, d), lambda v, ti, ex, *_: (layer, ex[v], 0, 0)),
            ],
            out_specs=pl.BlockSpec(memory_space=pl.ANY),
            scratch_shapes=[pltpu.VMEM((d, d_exp), BF16), pltpu.VMEM((d, d_exp), BF16),
                            pltpu.VMEM((d_exp, d), BF16), pltpu.VMEM((EXPERT_TILE, d), F32),
                            pltpu.VMEM((2, EXPERT_TILE * ch, LANES), F32),
                            pltpu.SemaphoreType.DMA((2,))],
        ),
        out_shape=jax.ShapeDtypeStruct((n_slots * ch, LANES), F32),
        compiler_params=_cparams("arbitrary"),
        name="moe_experts",
    )(tile, expert, lo_in, hi_in, first, last, newexp, inv.reshape(n_et, 1, EXPERT_TILE),
      inv.reshape(n_et, 1, EXPERT_TILE), xs, w_gate, w_up, w_down)

    d_sh = sg.shape[-1]
    slot_specs = [pl.BlockSpec((ROW_TILE * ch, LANES), lambda i, kk=kk: (kk * n_tiles + i, 0))
                  for kk in range(TOP_K)]
    return pl.pallas_call(
        _combine_kernel,
        grid=(n_tiles,),
        in_specs=[
            pl.BlockSpec((ROW_TILE, SUBLANES), lambda i: (i, 0)),
            pl.BlockSpec((ROW_TILE * ch, LANES), lambda i: (i, 0)),
            pl.BlockSpec((ROW_TILE, d), lambda i: (i, 0)),
            pl.BlockSpec((1, 1, d), lambda i: (gate_row_of(i) * 6 + 5, 0, 0)),
            pl.BlockSpec((d, d_sh), lambda i: (0, 0)),
            pl.BlockSpec((d, d_sh), lambda i: (0, 0)),
            pl.BlockSpec((d_sh, d), lambda i: (0, 0)),
        ] + slot_specs,
        out_specs=pl.BlockSpec((ROW_TILE, d), lambda i: (i, 0)),
        out_shape=jax.ShapeDtypeStruct((t, d), F32),
        compiler_params=_cparams("parallel"),
        name="moe_combine",
    )(wcol, nx, h, modv, sg.astype(BF16), su.astype(BF16), sd.astype(BF16), *([ys] * TOP_K))


def _rope_tables(n_lat, n_ctx, head_dim):
    rows = n_lat // GRID_W
    row = jnp.repeat(jnp.arange(rows, dtype=F32), GRID_W)
    col = jnp.tile(jnp.arange(GRID_W, dtype=F32), rows)
    n_freq = head_dim // 4
    inv = ROPE_THETA ** (-jnp.arange(n_freq, dtype=F32) / n_freq)
    ang = jnp.concatenate([row[:, None] * inv, col[:, None] * inv], axis=-1)
    cos = jnp.concatenate([jnp.ones((n_ctx, head_dim // 2), F32), jnp.cos(ang)], axis=0)
    sin = jnp.concatenate([jnp.zeros((n_ctx, head_dim // 2), F32), jnp.sin(ang)], axis=0)
    return cos, sin


def _split_halves_perm(head_dim):
    return np.concatenate([np.arange(0, head_dim, 2), np.arange(1, head_dim, 2)])


def kernel(x, c, ctx, c_ctx, mod_w, mod_b, norm1_g, norm2_g, ev_w_in, ev_w_out, a_ln_g, a_ln_b, a_ws, a_bs, b_q_norm, b_k_norm, b_lam_q1, b_lam_k1, b_lam_q2, b_lam_k2, b_subln, od_w_qkv, od_w_out, c_q_norm, c_k_norm, moe_router, moe_bias, moe_w_gate, moe_w_up, moe_w_down, sh_w_gate, sh_w_up, sh_w_down):
    bsz, n_lat, d = x.shape
    n_ctx = ctx.shape[1]
    depth = mod_w.shape[0]
    assert depth == 2 and n_ctx == ROW_TILE and n_lat % ROW_TILE == 0 and bsz + 1 <= MOD_ROWS
    assert d == SUBLANES * LANES
    n_seq = n_ctx + n_lat
    tpb = n_seq // ROW_TILE
    lpb = n_lat // ROW_TILE
    t_all = bsz * n_seq
    n_tiles = t_all // ROW_TILE
    ctx_row = bsz

    cond = jnp.zeros((MOD_ROWS, d), F32).at[:bsz].set(c).at[ctx_row].set(c_ctx)
    mod = _adaln(cond, mod_w, mod_b)
    modv = [mod[l].reshape(MOD_ROWS * 6, 1, d) for l in range(depth)]

    def row_all(i):
        return jnp.where(i % tpb == 0, ctx_row, i // tpb)

    def mspec(j, row_of):
        return pl.BlockSpec((1, 1, d), lambda i: (row_of(i) * 6 + j, 0, 0))

    def full(shape):
        return pl.BlockSpec(shape, lambda *_: (0,) * len(shape))

    x_spec = pl.BlockSpec((1, ROW_TILE, d), lambda i: (i // tpb, jnp.maximum(i % tpb - 1, 0), 0))
    ctx_spec = pl.BlockSpec((1, ROW_TILE, d), lambda i: (i // tpb, 0, 0))
    ch = d // LANES
    tok_spec = pl.BlockSpec((ROW_TILE * ch, LANES), lambda i: (i, 0))

    lam_init = 0.8 - 0.6 * math.exp(-0.3 * 0)
    p64 = _split_halves_perm(B_HEAD_DIM)
    col_perm = np.concatenate(
        [np.arange(2 * A_WIDTH)]
        + [2 * A_WIDTH + blk * B_HEAD_DIM + p64 for blk in range(2 * B_WIDTH // B_HEAD_DIM)]
        + [np.arange(2 * A_WIDTH + 2 * B_WIDTH, 2 * A_WIDTH + 3 * B_WIDTH)])
    w_in = ev_w_in[0][:, col_perm].astype(BF16)
    even_in = w_in.shape[1]
    cos_b, sin_b = _rope_tables(n_lat, n_ctx, B_HEAD_DIM)
    zeros_b = jnp.zeros_like(sin_b)
    tab_c = jnp.tile(jnp.concatenate([cos_b, cos_b], axis=-1), (1, 2))
    tab_sa = jnp.tile(jnp.concatenate([-sin_b, zeros_b], axis=-1), (1, 2))
    tab_sb = jnp.tile(jnp.concatenate([zeros_b, sin_b], axis=-1), (1, 2))
    qg = jnp.tile(b_q_norm[0][p64], 2).reshape(1, LANES)
    kg = jnp.tile(b_k_norm[0][p64], 2).reshape(1, LANES)
    tab_spec = pl.BlockSpec((ROW_TILE, LANES), lambda i: (i % tpb, 0))
    row_spec = lambda w: pl.BlockSpec((ROW_TILE, w), lambda i: (i, 0))
    uv, q, k, v = pl.pallas_call(
        functools.partial(_even_in_kernel, tiles_per_batch=tpb),
        grid=(n_tiles,),
        in_specs=[x_spec, ctx_spec, mspec(0, row_all), mspec(1, row_all), full((1, d)),
                  full((d, even_in)), full((1, LANES)), full((1, LANES)),
                  tab_spec, tab_spec, tab_spec],
        out_specs=[row_spec(2 * A_WIDTH), row_spec(B_WIDTH), row_spec(B_WIDTH), row_spec(B_WIDTH)],
        out_shape=[jax.ShapeDtypeStruct((t_all, 2 * A_WIDTH), F32),
                   jax.ShapeDtypeStruct((t_all, B_WIDTH), BF16),
                   jax.ShapeDtypeStruct((t_all, B_WIDTH), BF16),
                   jax.ShapeDtypeStruct((t_all, B_WIDTH), BF16)],
        compiler_params=_cparams("parallel"),
        name="even_in",
    )(x, ctx, modv[0], modv[0], norm1_g[0].reshape(1, d), w_in, qg, kg, tab_c, tab_sa, tab_sb)

    lamv = jnp.zeros((SUBLANES, LANES), F32)
    for r, vec in enumerate((b_lam_q1[0], b_lam_k1[0], b_lam_q2[0], b_lam_k2[0])):
        lamv = lamv.at[r, :B_HEAD_DIM].set(vec)
    o = pl.pallas_call(
        functools.partial(_diff_attn_kernel, ctx_len=n_ctx, lam_init=lam_init),
        grid=(bsz, tpb),
        in_specs=[
            pl.BlockSpec((SUBLANES, LANES), lambda b, qi: (0, 0)),
            pl.BlockSpec((ROW_TILE, B_WIDTH), lambda b, qi: (b * tpb + qi, 0)),
            pl.BlockSpec((n_seq, B_WIDTH), lambda b, qi: (b, 0)),
            pl.BlockSpec((n_seq, B_WIDTH), lambda b, qi: (b, 0)),
        ],
        out_specs=pl.BlockSpec((ROW_TILE, B_WIDTH), lambda b, qi: (b * tpb + qi, 0)),
        out_shape=jax.ShapeDtypeStruct((t_all, B_WIDTH), F32),
        compiler_params=_cparams("parallel", "arbitrary"),
        name="diff_attn",
    )(lamv, q, k, v)

    bs_col = jnp.repeat(a_bs[0].T, A_GROUP_DIM, axis=1)
    sub_g = b_subln[0].reshape(1, LANES)
    h1, nx = pl.pallas_call(
        functools.partial(_even_out_kernel, lam_init=lam_init, tiles_per_batch=tpb),
        grid=(n_tiles,),
        in_specs=[row_spec(B_WIDTH), row_spec(2 * A_WIDTH), x_spec, ctx_spec,
                  mspec(2, row_all), mspec(3, row_all), mspec(4, row_all),
                  full((1, LANES)), full((1, A_WIDTH)), full((1, A_WIDTH)),
                  full((A_GROUPS, GMLP_CHUNK, GMLP_CHUNK)), full((GMLP_CHUNK, A_WIDTH)),
                  full((A_WIDTH + B_WIDTH, d)), full((1, d))],
        out_specs=[row_spec(d), tok_spec],
        out_shape=[jax.ShapeDtypeStruct((t_all, d), F32),
                   jax.ShapeDtypeStruct((t_all * ch, LANES), F32)],
        compiler_params=_cparams("parallel"),
        name="even_out",
    )(o, uv, x, ctx, modv[0], modv[0], modv[0], sub_g, a_ln_g[0].reshape(1, A_WIDTH),
      a_ln_b[0].reshape(1, A_WIDTH), a_ws[0].astype(BF16), bs_col,
      ev_w_out[0].astype(BF16), norm2_g[0].reshape(1, d))

    h2 = _moe(nx, h1, modv[0], row_all, 0, moe_router[0], moe_bias[0], moe_w_gate, moe_w_up,
              moe_w_down, sh_w_gate[0], sh_w_up[0], sh_w_down[0])

    p128 = _split_halves_perm(C_HEAD_DIM)
    n_qkv_heads = C_HEADS + 2 * C_KV_HEADS
    col_perm = np.concatenate(
        [blk * C_HEAD_DIM + p128 for blk in range(C_HEADS + C_KV_HEADS)]
        + [np.arange((C_HEADS + C_KV_HEADS) * C_HEAD_DIM, n_qkv_heads * C_HEAD_DIM)])
    w_qkv = od_w_qkv[0][:, col_perm].astype(BF16)
    cos_c, sin_c = _rope_tables(n_lat, n_ctx, C_HEAD_DIM)
    tab_c1 = jnp.concatenate([cos_c, cos_c], axis=-1)
    tab_s1 = jnp.concatenate([-sin_c, sin_c], axis=-1)
    qg1 = c_q_norm[0][p128].reshape(1, LANES)
    kg1 = c_k_norm[0][p128].reshape(1, LANES)
    nq = C_HEADS * C_HEAD_DIM
    nkv = C_KV_HEADS * C_HEAD_DIM
    q1, k1, v1 = pl.pallas_call(
        _odd_in_kernel,
        grid=(n_tiles,),
        in_specs=[row_spec(d), mspec(0, row_all), mspec(1, row_all), full((1, d)),
                  full((d, nq + 2 * nkv)), full((1, LANES)), full((1, LANES)), tab_spec, tab_spec],
        out_specs=[row_spec(nq), row_spec(nkv), row_spec(nkv)],
        out_shape=[jax.ShapeDtypeStruct((t_all, nq), BF16),
                   jax.ShapeDtypeStruct((t_all, nkv), BF16),
                   jax.ShapeDtypeStruct((t_all, nkv), BF16)],
        compiler_params=_cparams("parallel"),
        name="odd_in",
    )(h2, modv[1], modv[1], norm1_g[1].reshape(1, d), w_qkv, qg1, kg1, tab_c1, tab_s1)

    t_lat = bsz * n_lat
    grp = C_HEADS // C_KV_HEADS
    o1 = pl.pallas_call(
        _gqa_kernel,
        grid=(bsz, C_KV_HEADS, lpb),
        in_specs=[
            pl.BlockSpec((ROW_TILE, grp * LANES), lambda b, n, qi: (b * tpb + 1 + qi, n)),
            pl.BlockSpec((n_seq, LANES), lambda b, n, qi: (b, n)),
            pl.BlockSpec((n_seq, LANES), lambda b, n, qi: (b, n)),
        ],
        out_specs=pl.BlockSpec((ROW_TILE, grp * LANES), lambda b, n, qi: (b * lpb + qi, n)),
        out_shape=jax.ShapeDtypeStruct((t_lat, nq), BF16),
        compiler_params=_cparams("parallel", "parallel", "arbitrary"),
        name="gqa_attn",
    )(q1, k1, v1)

    def row_lat(i):
        return i // lpb

    lat_tiles = t_lat // ROW_TILE
    hx, nx1 = pl.pallas_call(
        _odd_out_kernel,
        grid=(lat_tiles,),
        in_specs=[row_spec(nq),
                  pl.BlockSpec((ROW_TILE, d), lambda i: ((i // lpb) * tpb + 1 + i % lpb, 0)),
                  mspec(2, row_lat), mspec(3, row_lat), mspec(4, row_lat),
                  full((nq, d)), full((1, d))],
        out_specs=[row_spec(d), tok_spec],
        out_shape=[jax.ShapeDtypeStruct((t_lat, d), F32),
                   jax.ShapeDtypeStruct((t_lat * ch, LANES), F32)],
        compiler_params=_cparams("parallel"),
        name="odd_out",
    )(o1, h2, modv[1], modv[1], modv[1], od_w_out[0].astype(BF16), norm2_g[1].reshape(1, d))

    out = _moe(nx1, hx, modv[1], row_lat, 1, moe_router[1], moe_bias[1], moe_w_gate, moe_w_up,
               moe_w_down, sh_w_gate[1], sh_w_up[1], sh_w_down[1])
    return out.reshape(bsz, n_lat, d)
```

```python
import functools
import math

import numpy as np
import jax
import jax.numpy as jnp
from jax import lax
from jax.experimental import pallas as pl
from jax.experimental.pallas import tpu as pltpu

F32 = jnp.float32
BF16 = jnp.bfloat16
I32 = jnp.int32

GRID_W = 64
EPS = 1e-6
ROPE_THETA = 10000.0
A_GROUPS = 4
A_GROUP_DIM = 128
A_WIDTH = A_GROUPS * A_GROUP_DIM
GMLP_CHUNK = 128
B_HEADS = 4
B_HEAD_DIM = 64
B_WIDTH = B_HEADS * 2 * B_HEAD_DIM
C_HEADS = 8
C_KV_HEADS = 2
C_HEAD_DIM = 128
N_EXPERTS = 64
TOP_K = 6
N_GROUPS = 8
TOPK_GROUPS = 4
ROUTE_SCALE = 2.5
LOG2E = math.log2(math.e)

LANES = 128
SUBLANES = 8
ROW_TILE = 256
ROUTER_TILE = 512
EXPERT_TILE = 512
PROJ_BLOCK = 256
MOD_ROWS = 24
VMEM_LIMIT = 56 * 1024 * 1024

NT_DIMS = (((1,), (1,)), ((), ()))


def _cparams(*sem):
    return pltpu.CompilerParams(dimension_semantics=sem, vmem_limit_bytes=VMEM_LIMIT)


def _rms(x, g):
    return x * lax.rsqrt(jnp.mean(x * x, axis=-1, keepdims=True) + EPS) * g


def _norm_mod(h, g, shift, scale):
    return _rms(h, g) * (1.0 + scale) + shift


def _gelu(x):
    return 0.5 * x * (1.0 + lax.erf(x * np.float32(math.sqrt(0.5))))


def _silu(x):
    return x * jax.nn.sigmoid(x)


def _bdot(a, b):
    return jnp.dot(a.astype(BF16), b.astype(BF16), preferred_element_type=F32)


def _from_token_tiles(ref, rows, d):
    ch = d // LANES
    groups = []
    for g in range(rows // SUBLANES):
        groups.append(jnp.concatenate(
            [ref[pl.ds(g * SUBLANES * ch + j, SUBLANES, stride=ch), :] for j in range(ch)], axis=-1))
    return jnp.concatenate(groups, axis=0)


def _to_token_tiles(ref, val):
    rows, d = val.shape
    ch = d // LANES
    for g in range(rows // SUBLANES):
        for j in range(ch):
            ref[pl.ds(g * SUBLANES * ch + j, SUBLANES, stride=ch), :] = (
                val[g * SUBLANES:(g + 1) * SUBLANES, j * LANES:(j + 1) * LANES])


def _adaln_kernel(c_ref, w_ref, b_ref, o_ref):
    o_ref[0] = _bdot(_silu(c_ref[...]), w_ref[0]) + b_ref[0]


def _adaln(cond, mod_w, mod_b):
    depth, d, d6 = mod_w.shape
    tn = d6 // 4
    return pl.pallas_call(
        _adaln_kernel,
        grid=(depth, d6 // tn),
        in_specs=[
            pl.BlockSpec((MOD_ROWS, d), lambda l, j: (0, 0)),
            pl.BlockSpec((1, d, tn), lambda l, j: (l, 0, j)),
            pl.BlockSpec((1, 1, tn), lambda l, j: (l, 0, j)),
        ],
        out_specs=pl.BlockSpec((1, MOD_ROWS, tn), lambda l, j: (l, 0, j)),
        out_shape=jax.ShapeDtypeStruct((depth, MOD_ROWS, d6), F32),
        compiler_params=_cparams("parallel", "parallel"),
        name="adaln",
    )(cond, mod_w, mod_b.reshape(depth, 1, d6))


def _pick_stream(x_ref, ctx_ref, tiles_per_batch):
    is_ctx = pl.program_id(0) % tiles_per_batch == 0
    return jnp.where(is_ctx, ctx_ref[0], x_ref[0])


def _even_in_kernel(x_ref, ctx_ref, sh_ref, sc_ref, g_ref, w_ref, qg_ref, kg_ref, c_ref, sa_ref,
                    sb_ref, uv_ref, q_ref, k_ref, v_ref, *, tiles_per_batch):
    h = _pick_stream(x_ref, ctx_ref, tiles_per_batch)
    n = _norm_mod(h, g_ref[...], sh_ref[0], sc_ref[0]).astype(BF16)

    def cols(c0):
        return jnp.dot(n, w_ref[:, c0:c0 + PROJ_BLOCK], preferred_element_type=F32)

    for c0 in range(0, 2 * A_WIDTH, PROJ_BLOCK):
        uv_ref[:, c0:c0 + PROJ_BLOCK] = _gelu(cols(c0))
    cos, sa, sb = c_ref[...], sa_ref[...], sb_ref[...]
    lane = lax.broadcasted_iota(I32, cos.shape, 1)
    low = lane < B_HEAD_DIM

    def head_pair(x, gain, scale):
        sq = x * x
        s_lo = jnp.sum(jnp.where(low, sq, 0.0), axis=-1, keepdims=True)
        s_hi = jnp.sum(jnp.where(low, 0.0, sq), axis=-1, keepdims=True)
        ms = jnp.where(low, s_lo, s_hi) * np.float32(1.0 / B_HEAD_DIM)
        y = x * lax.rsqrt(ms + EPS) * gain
        y = (y * cos + pltpu.roll(y, LANES - B_HEAD_DIM // 2, 1) * sa
             + pltpu.roll(y, B_HEAD_DIM // 2, 1) * sb)
        if scale is not None:
            y = y * scale
        return y.astype(BF16)

    q0 = 2 * A_WIDTH
    k0 = q0 + B_WIDTH
    v0 = k0 + B_WIDTH
    per = PROJ_BLOCK // LANES
    for c0 in range(0, B_WIDTH, PROJ_BLOCK):
        pq = cols(q0 + c0)
        pk = cols(k0 + c0)
        for j in range(per):
            sl = slice(c0 + j * LANES, c0 + (j + 1) * LANES)
            q_ref[:, sl] = head_pair(pq[:, j * LANES:(j + 1) * LANES], qg_ref[...],
                                     np.float32(B_HEAD_DIM ** -0.5))
            k_ref[:, sl] = head_pair(pk[:, j * LANES:(j + 1) * LANES], kg_ref[...], None)
        v_ref[:, c0:c0 + PROJ_BLOCK] = cols(v0 + c0).astype(BF16)


def _odd_in_kernel(h_ref, sh_ref, sc_ref, g_ref, w_ref, qg_ref, kg_ref, c_ref, s_ref,
                   q_ref, k_ref, v_ref):
    n = _norm_mod(h_ref[...], g_ref[...], sh_ref[0], sc_ref[0])
    p = jnp.dot(n.astype(BF16), w_ref[...], preferred_element_type=F32)
    cos, sin = c_ref[...], s_ref[...]

    def head(x, gain, scale):
        y = _rms(x, gain)
        y = y * cos + pltpu.roll(y, C_HEAD_DIM // 2, 1) * sin
        if scale is not None:
            y = y * scale
        return y.astype(BF16)

    nq = C_HEADS * C_HEAD_DIM
    nkv = C_KV_HEADS * C_HEAD_DIM
    for j in range(C_HEADS):
        q_ref[:, j * LANES:(j + 1) * LANES] = head(p[:, j * LANES:(j + 1) * LANES], qg_ref[...],
                                                   None)
    for j in range(C_KV_HEADS):
        k_ref[:, j * LANES:(j + 1) * LANES] = head(
            p[:, nq + j * LANES:nq + (j + 1) * LANES], kg_ref[...], None)
    v_ref[...] = p[:, nq + nkv:nq + 2 * nkv].astype(BF16)


def _diff_attn_kernel(lam_ref, q_ref, k_ref, v_ref, o_ref, *, ctx_len, lam_init):
    lv = lam_ref[...]
    lam = (jnp.exp(jnp.sum(lv[0:1] * lv[1:2], axis=-1, keepdims=True))
           - jnp.exp(jnp.sum(lv[2:3] * lv[3:4], axis=-1, keepdims=True)) + np.float32(lam_init))
    low = lax.broadcasted_iota(I32, (q_ref.shape[0], LANES), 1) < B_HEAD_DIM

    def softmax(qm, k, scale):
        s = lax.dot_general(qm, k, NT_DIMS, preferred_element_type=F32)
        p = jnp.exp(s - jnp.max(s, axis=-1, keepdims=True))
        return p * (scale / jnp.sum(p, axis=-1, keepdims=True))

    def attend(n_keys):
        for hh in range(q_ref.shape[1] // LANES):
            cs = slice(hh * LANES, (hh + 1) * LANES)
            q = q_ref[:, cs]
            zero = jnp.zeros_like(q)
            k = k_ref[0:n_keys, cs]
            a = (softmax(jnp.where(low, q, zero), k, 1.0)
                 - softmax(jnp.where(low, zero, q), k, lam))
            o_ref[:, cs] = jnp.dot(a.astype(BF16), v_ref[0:n_keys, cs], preferred_element_type=F32)

    is_ctx = pl.program_id(1) == 0

    @pl.when(is_ctx)
    def _():
        attend(ctx_len)

    @pl.when(jnp.logical_not(is_ctx))
    def _():
        attend(k_ref.shape[0])


def _gqa_kernel(q_ref, k_ref, v_ref, o_ref):
    grp = q_ref.shape[1] // LANES
    c = np.float32(C_HEAD_DIM ** -0.5 * LOG2E)
    k = k_ref[...]
    v = v_ref[...]
    for g in range(grp):
        cs = slice(g * LANES, (g + 1) * LANES)
        s = lax.dot_general(q_ref[:, cs], k, NT_DIMS, preferred_element_type=F32)
        p = jnp.exp2((s - jnp.max(s, axis=-1, keepdims=True)) * c)
        l = jnp.sum(p, axis=-1, keepdims=True)
        o = jnp.dot(p.astype(BF16), v, preferred_element_type=F32) / l
        o_ref[:, cs] = o.astype(o_ref.dtype)


def _even_out_kernel(o_ref, uv_ref, x_ref, ctx_ref, gate_ref, sh_ref, sc_ref, sub_ref, lng_ref,
                     lnb_ref, ws_ref, bs_ref, w_ref, g2_ref, h1_ref, nx_ref, *, lam_init,
                     tiles_per_batch):
    o = o_ref[...]
    uv = uv_ref[...]
    u = uv[:, :A_WIDTH]
    v = uv[:, A_WIDTH:]
    mu = jnp.mean(v, axis=-1, keepdims=True)
    var = jnp.mean(jnp.square(v - mu), axis=-1, keepdims=True)
    vn = ((v - mu) * lax.rsqrt(var + EPS) * lng_ref[...] + lnb_ref[...]).astype(BF16)
    rows = o.shape[0]
    parts = []
    for c in range(rows // GMLP_CHUNK):
        rs = slice(c * GMLP_CHUNK, (c + 1) * GMLP_CHUNK)
        for g in range(A_GROUPS):
            cs = slice(g * A_GROUP_DIM, (g + 1) * A_GROUP_DIM)
            mixed = jnp.dot(ws_ref[g], vn[rs, cs], preferred_element_type=F32) + bs_ref[:, cs]
            parts.append((c, g, u[rs, cs] * mixed))
    a_rows = [jnp.concatenate([p for (c2, _, p) in parts if c2 == c], axis=-1)
              for c in range(rows // GMLP_CHUNK)]
    a = jnp.concatenate(a_rows, axis=0)
    heads = []
    for hh in range(B_HEADS):
        oh = o[:, hh * LANES:(hh + 1) * LANES]
        heads.append(_rms(oh, sub_ref[...]) * np.float32(1.0 - lam_init))
    cat = jnp.concatenate([a] + heads, axis=-1).astype(BF16)
    y = jnp.dot(cat, w_ref[...], preferred_element_type=F32)
    h1 = _pick_stream(x_ref, ctx_ref, tiles_per_batch) + gate_ref[0] * y
    h1_ref[...] = h1
    _to_token_tiles(nx_ref, _norm_mod(h1, g2_ref[...], sh_ref[0], sc_ref[0]))


def _odd_out_kernel(o_ref, h_ref, gate_ref, sh_ref, sc_ref, w_ref, g2_ref, h1_ref, nx_ref):
    y = jnp.dot(o_ref[...], w_ref[...], preferred_element_type=F32)
    h1 = h_ref[...] + gate_ref[0] * y
    h1_ref[...] = h1
    _to_token_tiles(nx_ref, _norm_mod(h1, g2_ref[...], sh_ref[0], sc_ref[0]))


def _rows_to_block(rows, dtype):
    n = rows[0].shape[1]
    rio = lax.broadcasted_iota(I32, (SUBLANES, n), 0)
    out = jnp.zeros((SUBLANES, n), dtype)
    for r, row in enumerate(rows):
        out = jnp.where(rio == r, jnp.broadcast_to(row.astype(dtype), (SUBLANES, n)), out)
    return out


def _transpose_block(xt):
    n = xt.shape[1]
    eye = jnp.where(lax.broadcasted_iota(I32, (n, n), 0) == lax.broadcasted_iota(I32, (n, n), 1),
                    1.0, 0.0).astype(BF16)
    acc = jnp.zeros((n, SUBLANES), F32)
    rem = xt
    for _ in range(3):
        part = rem.astype(BF16)
        acc = acc + lax.dot_general(eye, part, NT_DIMS, preferred_element_type=F32)
        rem = rem - part.astype(F32)
    return acc


def _router_kernel(x_ref, wr_ref, b_ref, eidx_ref, rank_ref, lpos_ref, wcol_ref, cnt_ref, tcnt_ref,
                   run_ref):
    @pl.when(pl.program_id(0) == 0)
    def _():
        run_ref[...] = jnp.zeros_like(run_ref)

    per = N_EXPERTS // N_GROUPS
    d = wr_ref.shape[1]
    x = _from_token_tiles(x_ref, x_ref.shape[0] * LANES // d, d)
    logits = lax.dot_general(wr_ref[...], x.astype(BF16), NT_DIMS,
                             preferred_element_type=F32)
    scores = jax.nn.sigmoid(logits)
    sel = scores + b_ref[...]
    tm = sel.shape[1]
    neg = np.float32(-np.inf)
    jio = lax.broadcasted_iota(I32, (per, tm), 0).astype(F32)
    gio = lax.broadcasted_iota(I32, (N_GROUPS, tm), 0).astype(F32)

    def rmax(x):
        return jnp.max(x, axis=0, keepdims=True)

    def rmin(x):
        return jnp.min(x, axis=0, keepdims=True)

    sel_g = [sel[g * per:(g + 1) * per, :] for g in range(N_GROUPS)]
    sc_g = [scores[g * per:(g + 1) * per, :] for g in range(N_GROUPS)]
    gs = jnp.zeros((N_GROUPS, tm), F32)
    for g in range(N_GROUPS):
        m1 = rmax(sel_g[g])
        i1 = rmin(jnp.where(sel_g[g] == m1, jio, np.float32(per)))
        m2 = rmax(jnp.where(jio == i1, neg, sel_g[g]))
        gs = jnp.where(gio == np.float32(g), jnp.broadcast_to(m1 + m2, gs.shape), gs)
    gsel = jnp.zeros((N_GROUPS, tm), I32)
    for _ in range(TOPK_GROUPS):
        m = rmax(gs)
        idx = rmin(jnp.where(gs == m, gio, np.float32(N_GROUPS)))
        hit = gio == idx
        gsel = jnp.where(hit, 1, gsel)
        gs = jnp.where(hit, neg, gs)
    masked = [jnp.where(jnp.broadcast_to(gsel[g:g + 1, :], (per, tm)) == 1, sel_g[g], neg)
              for g in range(N_GROUPS)]
    eio = [jio + np.float32(g * per) for g in range(N_GROUPS)]
    e_rows, w_rows, hits = [], [], []
    for _ in range(TOP_K):
        m = masked[0]
        for g in range(1, N_GROUPS):
            m = jnp.maximum(m, masked[g])
        m = rmax(m)
        cand = jnp.where(masked[0] == m, eio[0], np.float32(N_EXPERTS))
        for g in range(1, N_GROUPS):
            cand = jnp.minimum(cand, jnp.where(masked[g] == m, eio[g], np.float32(N_EXPERTS)))
        idx = rmin(cand)
        hit = [eio[g] == idx for g in range(N_GROUPS)]
        wsel = jnp.where(hit[0], sc_g[0], 0.0)
        for g in range(1, N_GROUPS):
            wsel = wsel + jnp.where(hit[g], sc_g[g], 0.0)
        masked = [jnp.where(hit[g], neg, masked[g]) for g in range(N_GROUPS)]
        e_rows.append(idx)
        w_rows.append(jnp.sum(wsel, axis=0, keepdims=True))
        hits.append(hit)
    wsum = w_rows[0]
    for r in w_rows[1:]:
        wsum = wsum + r
    w_rows = [r / wsum * np.float32(ROUTE_SCALE) for r in w_rows]
    onehot = []
    for g in range(N_GROUPS):
        any_hit = hits[0][g]
        for kk in range(1, TOP_K):
            any_hit = jnp.logical_or(any_hit, hits[kk][g])
        onehot.append(jnp.where(any_hit, 1.0, 0.0))
    mt = jnp.concatenate(onehot, axis=0)
    before = (lax.broadcasted_iota(I32, (tm, tm), 0) < lax.broadcasted_iota(I32, (tm, tm), 1))
    prefix = jnp.dot(mt.astype(BF16), jnp.where(before, 1.0, 0.0).astype(BF16),
                     preferred_element_type=F32)
    def pick(table):
        rows = []
        for kk in range(TOP_K):
            acc = jnp.where(hits[kk][0], table[0:per, :], 0.0)
            for g in range(1, N_GROUPS):
                acc = acc + jnp.where(hits[kk][g], table[g * per:(g + 1) * per, :], 0.0)
            rows.append(jnp.sum(acc, axis=0, keepdims=True))
        return rows

    n_exp = mt.shape[0]
    lower = (lax.broadcasted_iota(I32, (n_exp, n_exp), 0) > lax.broadcasted_iota(I32, (n_exp, n_exp), 1))
    lower = jnp.where(lower, 1.0, 0.0).astype(BF16)
    tok = lax.broadcasted_iota(I32, (1, tm), 1)
    local = jnp.zeros_like(prefix)
    seen = jnp.zeros((n_exp, 1), F32)
    for hh in range(tm // ROW_TILE):
        in_tile = jnp.logical_and(tok >= hh * ROW_TILE, tok < (hh + 1) * ROW_TILE)
        c_tile = jnp.sum(jnp.where(in_tile, mt, 0.0), axis=1, keepdims=True)
        offs = jnp.dot(lower, jnp.broadcast_to(c_tile, (n_exp, LANES)).astype(BF16),
                       preferred_element_type=F32)[:, 0:1]
        local = jnp.where(in_tile, prefix - seen + offs, local)
        tcnt_ref[hh * n_exp:(hh + 1) * n_exp, :] = jnp.broadcast_to(c_tile, (n_exp, LANES))
        seen = seen + c_tile

    run = run_ref[...] + jnp.sum(mt, axis=1, keepdims=True)
    eidx_ref[...] = _rows_to_block(e_rows, I32)
    rank_ref[...] = _rows_to_block(pick(prefix + run_ref[...]), I32)
    lpos_ref[...] = _rows_to_block(pick(local), I32) * SUBLANES
    wcol_ref[...] = _transpose_block(_rows_to_block(w_rows, F32))
    cnt_ref[...] = jnp.broadcast_to(run, cnt_ref.shape)
    run_ref[...] = run


def _dest_kernel(start_ref, eidx_ref, rank_ref, dest_ref):
    per = N_EXPERTS // N_GROUPS
    eidx = eidx_ref[...]
    tm = eidx.shape[1]
    jio = lax.broadcasted_iota(I32, (per, tm), 0)
    rows = []
    for kk in range(TOP_K):
        e = jnp.broadcast_to(eidx[kk:kk + 1, :], (per, tm))
        acc = jnp.zeros((per, tm), F32)
        for g in range(N_GROUPS):
            st = jnp.broadcast_to(start_ref[g * per:(g + 1) * per, :], (per, tm))
            acc = acc + jnp.where(jio + g * per == e, st, 0.0)
        rows.append(jnp.sum(acc, axis=0, keepdims=True))
    dest_ref[...] = _rows_to_block(rows, I32) + rank_ref[...]


def _row_copy(src, s_row, dst, d_row, sem):
    def tile_start(row):
        start = row * SUBLANES
        return start if isinstance(row, int) else pl.multiple_of(start, SUBLANES)

    s0 = tile_start(s_row)
    d0 = tile_start(d_row)
    return pltpu.make_async_copy(src.at[pl.ds(s0, SUBLANES)], dst.at[pl.ds(d0, SUBLANES)], sem)


def _dispatch_kernel(lpos_ref, cnt_ref, xpos_ref, x_ref, xs_ref, stage, sem):
    i = pl.program_id(0)
    rows = x_ref.shape[0] // SUBLANES
    n_exp = cnt_ref.shape[2]
    unroll = 4
    small = [1 << b for b in range(4, -1, -1)]
    big = [1 << b for b in range(rows.bit_length() - 1, 4, -1)]

    def span(start, n_rows):
        return pl.ds(pl.multiple_of(start, SUBLANES), n_rows * SUBLANES)

    def fill_and_send(slot):
        def place(t4, carry):
            for u in range(unroll):
                t = t4 * unroll + u
                tok = x_ref[span(t * SUBLANES, 1), :]
                for kk in range(TOP_K):
                    stage[slot, span(lpos_ref[kk, t], 1), :] = tok
            return carry

        lax.fori_loop(0, rows // unroll, place, 0)

        def chunks(sizes, n, src, dst):
            for j, size in enumerate(sizes):
                take = (n & size) != 0

                @pl.when(take)
                def _(src=src, dst=dst, size=size, j=j):
                    pltpu.make_async_copy(stage.at[slot, span(src, size)], xs_ref.at[span(dst, size)],
                                          sem.at[slot]).start(priority=j % 2)

                step = jnp.where(take, size * SUBLANES, 0)
                src = src + step
                dst = dst + step

        def send(e, src):
            n = cnt_ref[0, 0, e]
            dst = xpos_ref[0, 0, e] * SUBLANES
            n_big = n & ~(2 * small[0] - 1)

            @pl.when(n_big != 0)
            def _():
                chunks(big, n, src, dst)

            chunks(small, n, src + n_big * SUBLANES, dst + n_big * SUBLANES)
            return src + n * SUBLANES

        lax.fori_loop(0, n_exp, send, 0)

    def wait_stage(slot):
        pltpu.make_async_copy(stage.at[slot], stage.at[slot], sem.at[slot]).wait()

    for slot in range(2):
        @pl.when(i % 2 == slot)
        def _(slot=slot):
            @pl.when(i >= 2)
            def _():
                wait_stage(slot)

            fill_and_send(slot)

    @pl.when(i == pl.num_programs(0) - 1)
    def _():
        for slot in range(2):
            @pl.when(jnp.logical_or(i >= 1, i % 2 == slot))
            def _(slot=slot):
                wait_stage(slot)


def _expert_kernel(tile_s, exp_s, lo_s, hi_s, first_s, last_s, new_s, inv_prev_ref, inv_ref,
                   xs_ref, wg_ref, wu_ref, wd_ref, ys_ref, wg_b, wu_b, wd_b, acc, stage, sem):
    v = pl.program_id(0)
    tile = tile_s[v]
    rows, d = acc.shape

    @pl.when(new_s[v] == 1)
    def _():
        wg_b[...] = wg_ref[0, 0].astype(BF16)
        wu_b[...] = wu_ref[0, 0].astype(BF16)
        wd_b[...] = wd_ref[0, 0].astype(BF16)

    @pl.when(first_s[v] == 1)
    def _():
        acc[...] = jnp.zeros_like(acc)

    lo = lo_s[v]
    hi = hi_s[v]
    nonempty = hi > lo
    flush_prev = jnp.logical_and(first_s[v] == 1, tile >= 1)

    def compute():
        x = _from_token_tiles(xs_ref, rows, d).astype(BF16)
        g = jnp.dot(x, wg_b[...], preferred_element_type=F32)
        u = jnp.dot(x, wu_b[...], preferred_element_type=F32)
        y = jnp.dot((_silu(g) * u).astype(BF16), wd_b[...], preferred_element_type=F32)
        row = lax.broadcasted_iota(I32, (rows, 1), 0)
        mine = jnp.logical_and(row >= lo, row < hi)
        acc[...] = jnp.where(mine, y, acc[...])

    def start_scatter(idx_ref, slot, unrolled):
        def start(r, par):
            _row_copy(stage.at[slot], r, ys_ref, idx_ref[0, 0, r], sem.at[slot]).start(priority=par)

        if unrolled:
            for r in range(rows):
                start(r, r % 2)
        else:
            def body(r2, carry):
                for par in range(2):
                    start(r2 * 2 + par, par)
                return carry
            lax.fori_loop(0, rows // 2, body, 0)

    def wait_scatter(slot):
        pltpu.make_async_copy(stage.at[slot], stage.at[slot], sem.at[slot]).wait()

    for slot in range(2):
        prev_here = jnp.logical_and(flush_prev, (tile + 1) % 2 == slot)

        @pl.when(jnp.logical_and(prev_here, nonempty))
        def _(slot=slot):
            start_scatter(inv_prev_ref, slot, True)
            compute()

        @pl.when(jnp.logical_and(prev_here, jnp.logical_not(nonempty)))
        def _(slot=slot):
            start_scatter(inv_prev_ref, slot, False)

    @pl.when(jnp.logical_and(nonempty, jnp.logical_not(flush_prev)))
    def _():
        compute()

    is_final = v == pl.num_programs(0) - 1
    for slot in range(2):
        @pl.when(jnp.logical_and(last_s[v] == 1, tile % 2 == slot))
        def _(slot=slot):
            @pl.when(tile >= 2)
            def _():
                wait_scatter(slot)

            _to_token_tiles(stage.at[slot], acc[...])

            @pl.when(is_final)
            def _():
                start_scatter(inv_ref, slot, False)

    @pl.when(is_final)
    def _():
        for slot in range(2):
            @pl.when(jnp.logical_or(tile >= 1, tile % 2 == slot))
            def _(slot=slot):
                wait_scatter(slot)


def _combine_kernel(wcol_ref, x_ref, h_ref, gate_ref, sg_ref, su_ref, sd_ref, *refs):
    y_refs, o_ref = refs[:TOP_K], refs[TOP_K]
    rows, d = h_ref.shape
    x = _from_token_tiles(x_ref, rows, d).astype(BF16)
    g = jnp.dot(x, sg_ref[...], preferred_element_type=F32)
    u = jnp.dot(x, su_ref[...], preferred_element_type=F32)
    acc = jnp.dot((_silu(g) * u).astype(BF16), sd_ref[...], preferred_element_type=F32)
    wcol = wcol_ref[...]
    for kk in range(TOP_K):
        acc = acc + _from_token_tiles(y_refs[kk], rows, d) * wcol[:, kk:kk + 1]
    o_ref[...] = h_ref[...] + gate_ref[0] * acc


def _moe(nx, h, modv, gate_row_of, layer, w_router, router_bias, w_gate, w_up, w_down, sg, su, sd):
    t, d = h.shape
    ch = d // LANES
    n_slots = t * TOP_K
    n_rt = t // ROUTER_TILE
    e = N_EXPERTS
    tiles_per_rt = ROUTER_TILE // ROW_TILE
    eidx, rank, lpos, wcol, cnt, tcnt = pl.pallas_call(
        _router_kernel,
        grid=(n_rt,),
        in_specs=[
            pl.BlockSpec((ROUTER_TILE * ch, LANES), lambda i: (i, 0)),
            pl.BlockSpec((e, d), lambda i: (0, 0)),
            pl.BlockSpec((e, 1), lambda i: (0, 0)),
        ],
        out_specs=[
            pl.BlockSpec((SUBLANES, ROUTER_TILE), lambda i: (0, i)),
            pl.BlockSpec((SUBLANES, ROUTER_TILE), lambda i: (0, i)),
            pl.BlockSpec((SUBLANES, ROUTER_TILE), lambda i: (0, i)),
            pl.BlockSpec((ROUTER_TILE, SUBLANES), lambda i: (i, 0)),
            pl.BlockSpec((e, LANES), lambda i: (0, 0)),
            pl.BlockSpec((tiles_per_rt * e, LANES), lambda i: (i, 0)),
        ],
        out_shape=[
            jax.ShapeDtypeStruct((SUBLANES, t), I32),
            jax.ShapeDtypeStruct((SUBLANES, t), I32),
            jax.ShapeDtypeStruct((SUBLANES, t), I32),
            jax.ShapeDtypeStruct((t, SUBLANES), F32),
            jax.ShapeDtypeStruct((e, LANES), F32),
            jax.ShapeDtypeStruct((n_rt * tiles_per_rt * e, LANES), F32),
        ],
        scratch_shapes=[pltpu.VMEM((e, 1), F32)],
        compiler_params=_cparams("arbitrary"),
        name="moe_router",
    )(nx, w_router.T.astype(BF16), router_bias.reshape(e, 1))

    counts = cnt[:, 0].astype(I32)
    ends = jnp.cumsum(counts)
    starts = ends - counts
    dest = pl.pallas_call(
        _dest_kernel,
        grid=(n_rt,),
        in_specs=[
            pl.BlockSpec((e, 1), lambda i: (0, 0)),
            pl.BlockSpec((SUBLANES, ROUTER_TILE), lambda i: (0, i)),
            pl.BlockSpec((SUBLANES, ROUTER_TILE), lambda i: (0, i)),
        ],
        out_specs=pl.BlockSpec((SUBLANES, ROUTER_TILE), lambda i: (0, i)),
        out_shape=jax.ShapeDtypeStruct((SUBLANES, t), I32),
        compiler_params=_cparams("parallel"),
        name="moe_dest",
    )(starts.astype(F32).reshape(e, 1), eidx, rank)
    n_tiles = t // ROW_TILE
    inv = jnp.argsort(dest[:TOP_K].reshape(-1)).astype(I32)
    tile_cnt = tcnt[:, 0].astype(I32).reshape(n_tiles, e)
    run_pos = starts[None, :] + jnp.cumsum(tile_cnt, axis=0) - tile_cnt
    run_spec = pl.BlockSpec((1, 1, e), lambda i: (i, 0, 0), memory_space=pltpu.SMEM)
    xs = pl.pallas_call(
        _dispatch_kernel,
        grid=(n_tiles,),
        in_specs=[
            pl.BlockSpec((SUBLANES, ROW_TILE), lambda i: (0, i), memory_space=pltpu.SMEM),
            run_spec, run_spec,
            pl.BlockSpec((ROW_TILE * ch, LANES), lambda i: (i, 0)),
        ],
        out_specs=pl.BlockSpec(memory_space=pl.ANY),
        out_shape=jax.ShapeDtypeStruct((n_slots * ch, LANES), F32),
        scratch_shapes=[pltpu.VMEM((2, TOP_K * ROW_TILE * ch, LANES), F32),
                        pltpu.SemaphoreType.DMA((2,))],
        compiler_params=pltpu.CompilerParams(dimension_semantics=("arbitrary",),
                                             vmem_limit_bytes=VMEM_LIMIT, has_side_effects=True),
        name="moe_dispatch",
    )(lpos, tile_cnt.reshape(n_tiles, 1, e), run_pos.reshape(n_tiles, 1, e), nx)

    n_et = n_slots // EXPERT_TILE
    pts = jnp.sort(jnp.concatenate([jnp.arange(n_et, dtype=I32) * EXPERT_TILE, starts]))
    lo = pts
    hi = jnp.concatenate([pts[1:], jnp.full((1,), n_slots, I32)])
    tile = jnp.minimum(lo // EXPERT_TILE, n_et - 1)
    expert = jnp.minimum(jnp.sum((ends[None, :] <= lo[:, None]).astype(I32), axis=1), e - 1)
    one = jnp.ones((1,), I32)
    tile_change = (tile[1:] != tile[:-1]).astype(I32)
    first = jnp.concatenate([one, tile_change])
    last = jnp.concatenate([tile_change, one])
    newexp = jnp.concatenate([one, (expert[1:] != expert[:-1]).astype(I32)])
    lo_in = lo - tile * EXPERT_TILE
    hi_in = hi - tile * EXPERT_TILE
    n_visits = n_et + e
    d_exp = w_gate.shape[-1]
    ys = pl.pallas_call(
        _expert_kernel,
        grid_spec=pltpu.PrefetchScalarGridSpec(
            num_scalar_prefetch=7,
            grid=(n_visits,),
            in_specs=[
                pl.BlockSpec((1, 1, EXPERT_TILE),
                             lambda v, ti, ex, *_: (jnp.maximum(ti[v] - 1, 0), 0, 0),
                             memory_space=pltpu.SMEM),
                pl.BlockSpec((1, 1, EXPERT_TILE), lambda v, ti, ex, *_: (ti[v], 0, 0),
                             memory_space=pltpu.SMEM),
                pl.BlockSpec((EXPERT_TILE * ch, LANES), lambda v, ti, ex, *_: (ti[v], 0)),
                pl.BlockSpec((1, 1, d, d_exp), lambda v, ti, ex, *_: (layer, ex[v], 0, 0)),
                pl.BlockSpec((1, 1, d, d_exp), lambda v, ti, ex, *_: (layer, ex[v], 0, 0)),
                pl.BlockSpec((1, 1, d_exp, d), lambda v, ti, ex, *_: (layer, ex[v], 0, 0)),
            ],
            out_specs=pl.BlockSpec(memory_space=pl.ANY),
            scratch_shapes=[pltpu.VMEM((d, d_exp), BF16), pltpu.VMEM((d, d_exp), BF16),
                            pltpu.VMEM((d_exp, d), BF16), pltpu.VMEM((EXPERT_TILE, d), F32),
                            pltpu.VMEM((2, EXPERT_TILE * ch, LANES), F32),
                            pltpu.SemaphoreType.DMA((2,))],
        ),
        out_shape=jax.ShapeDtypeStruct((n_slots * ch, LANES), F32),
        compiler_params=_cparams("arbitrary"),
        name="moe_experts",
    )(tile, expert, lo_in, hi_in, first, last, newexp, inv.reshape(n_et, 1, EXPERT_TILE),
      inv.reshape(n_et, 1, EXPERT_TILE), xs, w_gate, w_up, w_down)

    d_sh = sg.shape[-1]
    slot_specs = [pl.BlockSpec((ROW_TILE * ch, LANES), lambda i, kk=kk: (kk * n_tiles + i, 0))
                  for kk in range(TOP_K)]
    return pl.pallas_call(
        _combine_kernel,
        grid=(n_tiles,),
        in_specs=[
            pl.BlockSpec((ROW_TILE, SUBLANES), lambda i: (i, 0)),
            pl.BlockSpec((ROW_TILE * ch, LANES), lambda i: (i, 0)),
            pl.BlockSpec((ROW_TILE, d), lambda i: (i, 0)),
            pl.BlockSpec((1, 1, d), lambda i: (gate_row_of(i) * 6 + 5, 0, 0)),
            pl.BlockSpec((d, d_sh), lambda i: (0, 0)),
            pl.BlockSpec((d, d_sh), lambda i: (0, 0)),
            pl.BlockSpec((d_sh, d), lambda i: (0, 0)),
        ] + slot_specs,
        out_specs=pl.BlockSpec((ROW_TILE, d), lambda i: (i, 0)),
        out_shape=jax.ShapeDtypeStruct((t, d), F32),
        compiler_params=_cparams("parallel"),
        name="moe_combine",
    )(wcol, nx, h, modv, sg.astype(BF16), su.astype(BF16), sd.astype(BF16), *([ys] * TOP_K))


def _rope_tables(n_lat, n_ctx, head_dim):
    rows = n_lat // GRID_W
    row = jnp.repeat(jnp.arange(rows, dtype=F32), GRID_W)
    col = jnp.tile(jnp.arange(GRID_W, dtype=F32), rows)
    n_freq = head_dim // 4
    inv = ROPE_THETA ** (-jnp.arange(n_freq, dtype=F32) / n_freq)
    ang = jnp.concatenate([row[:, None] * inv, col[:, None] * inv], axis=-1)
    cos = jnp.concatenate([jnp.ones((n_ctx, head_dim // 2), F32), jnp.cos(ang)], axis=0)
    sin = jnp.concatenate([jnp.zeros((n_ctx, head_dim // 2), F32), jnp.sin(ang)], axis=0)
    return cos, sin


def _split_halves_perm(head_dim):
    return np.concatenate([np.arange(0, head_dim, 2), np.arange(1, head_dim, 2)])


def kernel(x, c, ctx, c_ctx, mod_w, mod_b, norm1_g, norm2_g, ev_w_in, ev_w_out, a_ln_g, a_ln_b, a_ws, a_bs, b_q_norm, b_k_norm, b_lam_q1, b_lam_k1, b_lam_q2, b_lam_k2, b_subln, od_w_qkv, od_w_out, c_q_norm, c_k_norm, moe_router, moe_bias, moe_w_gate, moe_w_up, moe_w_down, sh_w_gate, sh_w_up, sh_w_down):
    bsz, n_lat, d = x.shape
    n_ctx = ctx.shape[1]
    depth = mod_w.shape[0]
    assert depth == 2 and n_ctx == ROW_TILE and n_lat % ROW_TILE == 0 and bsz + 1 <= MOD_ROWS
    assert d == SUBLANES * LANES
    n_seq = n_ctx + n_lat
    tpb = n_seq // ROW_TILE
    lpb = n_lat // ROW_TILE
    t_all = bsz * n_seq
    n_tiles = t_all // ROW_TILE
    ctx_row = bsz

    cond = jnp.zeros((MOD_ROWS, d), F32).at[:bsz].set(c).at[ctx_row].set(c_ctx)
    mod = _adaln(cond, mod_w, mod_b)
    modv = [mod[l].reshape(MOD_ROWS * 6, 1, d) for l in range(depth)]

    def row_all(i):
        return jnp.where(i % tpb == 0, ctx_row, i // tpb)

    def mspec(j, row_of):
        return pl.BlockSpec((1, 1, d), lambda i: (row_of(i) * 6 + j, 0, 0))

    def full(shape):
        return pl.BlockSpec(shape, lambda *_: (0,) * len(shape))

    x_spec = pl.BlockSpec((1, ROW_TILE, d), lambda i: (i // tpb, jnp.maximum(i % tpb - 1, 0), 0))
    ctx_spec = pl.BlockSpec((1, ROW_TILE, d), lambda i: (i // tpb, 0, 0))
    ch = d // LANES
    tok_spec = pl.BlockSpec((ROW_TILE * ch, LANES), lambda i: (i, 0))

    lam_init = 0.8 - 0.6 * math.exp(-0.3 * 0)
    p64 = _split_halves_perm(B_HEAD_DIM)
    col_perm = np.concatenate(
        [np.arange(2 * A_WIDTH)]
        + [2 * A_WIDTH + blk * B_HEAD_DIM + p64 for blk in range(2 * B_WIDTH // B_HEAD_DIM)]
        + [np.arange(2 * A_WIDTH + 2 * B_WIDTH, 2 * A_WIDTH + 3 * B_WIDTH)])
    w_in = ev_w_in[0][:, col_perm].astype(BF16)
    even_in = w_in.shape[1]
    cos_b, sin_b = _rope_tables(n_lat, n_ctx, B_HEAD_DIM)
    zeros_b = jnp.zeros_like(sin_b)
    tab_c = jnp.tile(jnp.concatenate([cos_b, cos_b], axis=-1), (1, 2))
    tab_sa = jnp.tile(jnp.concatenate([-sin_b, zeros_b], axis=-1), (1, 2))
    tab_sb = jnp.tile(jnp.concatenate([zeros_b, sin_b], axis=-1), (1, 2))
    qg = jnp.tile(b_q_norm[0][p64], 2).reshape(1, LANES)
    kg = jnp.tile(b_k_norm[0][p64], 2).reshape(1, LANES)
    tab_spec = pl.BlockSpec((ROW_TILE, LANES), lambda i: (i % tpb, 0))
    row_spec = lambda w: pl.BlockSpec((ROW_TILE, w), lambda i: (i, 0))
    uv, q, k, v = pl.pallas_call(
        functools.partial(_even_in_kernel, tiles_per_batch=tpb),
        grid=(n_tiles,),
        in_specs=[x_spec, ctx_spec, mspec(0, row_all), mspec(1, row_all), full((1, d)),
                  full((d, even_in)), full((1, LANES)), full((1, LANES)),
                  tab_spec, tab_spec, tab_spec],
        out_specs=[row_spec(2 * A_WIDTH), row_spec(B_WIDTH), row_spec(B_WIDTH), row_spec(B_WIDTH)],
        out_shape=[jax.ShapeDtypeStruct((t_all, 2 * A_WIDTH), F32),
                   jax.ShapeDtypeStruct((t_all, B_WIDTH), BF16),
                   jax.ShapeDtypeStruct((t_all, B_WIDTH), BF16),
                   jax.ShapeDtypeStruct((t_all, B_WIDTH), BF16)],
        compiler_params=_cparams("parallel"),
        name="even_in",
    )(x, ctx, modv[0], modv[0], norm1_g[0].reshape(1, d), w_in, qg, kg, tab_c, tab_sa, tab_sb)

    lamv = jnp.zeros((SUBLANES, LANES), F32)
    for r, vec in enumerate((b_lam_q1[0], b_lam_k1[0], b_lam_q2[0], b_lam_k2[0])):
        lamv = lamv.at[r, :B_HEAD_DIM].set(vec)
    o = pl.pallas_call(
        functools.partial(_diff_attn_kernel, ctx_len=n_ctx, lam_init=lam_init),
        grid=(bsz, tpb),
        in_specs=[
            pl.BlockSpec((SUBLANES, LANES), lambda b, qi: (0, 0)),
            pl.BlockSpec((ROW_TILE, B_WIDTH), lambda b, qi: (b * tpb + qi, 0)),
            pl.BlockSpec((n_seq, B_WIDTH), lambda b, qi: (b, 0)),
            pl.BlockSpec((n_seq, B_WIDTH), lambda b, qi: (b, 0)),
        ],
        out_specs=pl.BlockSpec((ROW_TILE, B_WIDTH), lambda b, qi: (b * tpb + qi, 0)),
        out_shape=jax.ShapeDtypeStruct((t_all, B_WIDTH), F32),
        compiler_params=_cparams("parallel", "arbitrary"),
        name="diff_attn",
    )(lamv, q, k, v)

    bs_col = jnp.repeat(a_bs[0].T, A_GROUP_DIM, axis=1)
    sub_g = b_subln[0].reshape(1, LANES)
    h1, nx = pl.pallas_call(
        functools.partial(_even_out_kernel, lam_init=lam_init, tiles_per_batch=tpb),
        grid=(n_tiles,),
        in_specs=[row_spec(B_WIDTH), row_spec(2 * A_WIDTH), x_spec, ctx_spec,
                  mspec(2, row_all), mspec(3, row_all), mspec(4, row_all),
                  full((1, LANES)), full((1, A_WIDTH)), full((1, A_WIDTH)),
                  full((A_GROUPS, GMLP_CHUNK, GMLP_CHUNK)), full((GMLP_CHUNK, A_WIDTH)),
                  full((A_WIDTH + B_WIDTH, d)), full((1, d))],
        out_specs=[row_spec(d), tok_spec],
        out_shape=[jax.ShapeDtypeStruct((t_all, d), F32),
                   jax.ShapeDtypeStruct((t_all * ch, LANES), F32)],
        compiler_params=_cparams("parallel"),
        name="even_out",
    )(o, uv, x, ctx, modv[0], modv[0], modv[0], sub_g, a_ln_g[0].reshape(1, A_WIDTH),
      a_ln_b[0].reshape(1, A_WIDTH), a_ws[0].astype(BF16), bs_col,
      ev_w_out[0].astype(BF16), norm2_g[0].reshape(1, d))

    h2 = _moe(nx, h1, modv[0], row_all, 0, moe_router[0], moe_bias[0], moe_w_gate, moe_w_up,
              moe_w_down, sh_w_gate[0], sh_w_up[0], sh_w_down[0])

    p128 = _split_halves_perm(C_HEAD_DIM)
    n_qkv_heads = C_HEADS + 2 * C_KV_HEADS
    col_perm = np.concatenate(
        [blk * C_HEAD_DIM + p128 for blk in range(C_HEADS + C_KV_HEADS)]
        + [np.arange((C_HEADS + C_KV_HEADS) * C_HEAD_DIM, n_qkv_heads * C_HEAD_DIM)])
    w_qkv = od_w_qkv[0][:, col_perm].astype(BF16)
    cos_c, sin_c = _rope_tables(n_lat, n_ctx, C_HEAD_DIM)
    tab_c1 = jnp.concatenate([cos_c, cos_c], axis=-1)
    tab_s1 = jnp.concatenate([-sin_c, sin_c], axis=-1)
    qg1 = c_q_norm[0][p128].reshape(1, LANES)
    kg1 = c_k_norm[0][p128].reshape(1, LANES)
    nq = C_HEADS * C_HEAD_DIM
    nkv = C_KV_HEADS * C_HEAD_DIM
    q1, k1, v1 = pl.pallas_call(
        _odd_in_kernel,
        grid=(n_tiles,),
        in_specs=[row_spec(d), mspec(0, row_all), mspec(1, row_all), full((1, d)),
                  full((d, nq + 2 * nkv)), full((1, LANES)), full((1, LANES)), tab_spec, tab_spec],
        out_specs=[row_spec(nq), row_spec(nkv), row_spec(nkv)],
        out_shape=[jax.ShapeDtypeStruct((t_all, nq), BF16),
                   jax.ShapeDtypeStruct((t_all, nkv), BF16),
                   jax.ShapeDtypeStruct((t_all, nkv), BF16)],
        compiler_params=_cparams("parallel"),
        name="odd_in",
    )(h2, modv[1], modv[1], norm1_g[1].reshape(1, d), w_qkv, qg1, kg1, tab_c1, tab_s1)

    t_lat = bsz * n_lat
    grp = C_HEADS // C_KV_HEADS
    o1 = pl.pallas_call(
        _gqa_kernel,
        grid=(bsz, C_KV_HEADS, lpb),
        in_specs=[
            pl.BlockSpec((ROW_TILE, grp * LANES), lambda b, n, qi: (b * tpb + 1 + qi, n)),
            pl.BlockSpec((n_seq, LANES), lambda b, n, qi: (b, n)),
            pl.BlockSpec((n_seq, LANES), lambda b, n, qi: (b, n)),
        ],
        out_specs=pl.BlockSpec((ROW_TILE, grp * LANES), lambda b, n, qi: (b * lpb + qi, n)),
        out_shape=jax.ShapeDtypeStruct((t_lat, nq), BF16),
        compiler_params=_cparams("parallel", "parallel", "arbitrary"),
        name="gqa_attn",
    )(q1, k1, v1)

    def row_lat(i):
        return i // lpb

    lat_tiles = t_lat // ROW_TILE
    hx, nx1 = pl.pallas_call(
        _odd_out_kernel,
        grid=(lat_tiles,),
        in_specs=[row_spec(nq),
                  pl.BlockSpec((ROW_TILE, d), lambda i: ((i // lpb) * tpb + 1 + i % lpb, 0)),
                  mspec(2, row_lat), mspec(3, row_lat), mspec(4, row_lat),
                  full((nq, d)), full((1, d))],
        out_specs=[row_spec(d), tok_spec],
        out_shape=[jax.ShapeDtypeStruct((t_lat, d), F32),
                   jax.ShapeDtypeStruct((t_lat * ch, LANES), F32)],
        compiler_params=_cparams("parallel"),
        name="odd_out",
    )(o1, h2, modv[1], modv[1], modv[1], od_w_out[0].astype(BF16), norm2_g[1].reshape(1, d))

    out = _moe(nx1, hx, modv[1], row_lat, 1, moe_router[1], moe_bias[1], moe_w_gate, moe_w_up,
               moe_w_down, sh_w_gate[1], sh_w_up[1], sh_w_down[1])
    return out.reshape(bsz, n_lat, d)
```

```python
import functools
import math

import numpy as np
import jax
import jax.numpy as jnp
from jax import lax
from jax.experimental import pallas as pl
from jax.experimental.pallas import tpu as pltpu

F32 = jnp.float32
BF16 = jnp.bfloat16
I32 = jnp.int32

GRID_W = 64
EPS = 1e-6
ROPE_THETA = 10000.0
A_GROUPS = 4
A_GROUP_DIM = 128
A_WIDTH = A_GROUPS * A_GROUP_DIM
GMLP_CHUNK = 128
B_HEADS = 4
B_HEAD_DIM = 64
B_WIDTH = B_HEADS * 2 * B_HEAD_DIM
C_HEADS = 8
C_KV_HEADS = 2
C_HEAD_DIM = 128
N_EXPERTS = 64
TOP_K = 6
N_GROUPS = 8
TOPK_GROUPS = 4
ROUTE_SCALE = 2.5
LOG2E = math.log2(math.e)

LANES = 128
SUBLANES = 8
ROW_TILE = 256
ROUTER_TILE = 512
EXPERT_TILE = 512
GQA_Q_TILES = 2
PROJ_BLOCK = 256
MOD_ROWS = 24
VMEM_LIMIT = 56 * 1024 * 1024

NT_DIMS = (((1,), (1,)), ((), ()))


def _cparams(*sem):
    return pltpu.CompilerParams(dimension_semantics=sem, vmem_limit_bytes=VMEM_LIMIT)


def _rms(x, g):
    return x * lax.rsqrt(jnp.mean(x * x, axis=-1, keepdims=True) + EPS) * g


def _norm_mod(h, g, shift, scale):
    return _rms(h, g) * (1.0 + scale) + shift


def _gelu(x):
    return 0.5 * x * (1.0 + lax.erf(x * np.float32(math.sqrt(0.5))))


def _silu(x):
    return x * jax.nn.sigmoid(x)


def _bdot(a, b):
    return jnp.dot(a.astype(BF16), b.astype(BF16), preferred_element_type=F32)


def _from_token_tiles(ref, rows, d):
    ch = d // LANES
    groups = []
    for g in range(rows // SUBLANES):
        groups.append(jnp.concatenate(
            [ref[pl.ds(g * SUBLANES * ch + j, SUBLANES, stride=ch), :] for j in range(ch)], axis=-1))
    return jnp.concatenate(groups, axis=0)


def _to_token_tiles(ref, val):
    rows, d = val.shape
    ch = d // LANES
    for g in range(rows // SUBLANES):
        for j in range(ch):
            ref[pl.ds(g * SUBLANES * ch + j, SUBLANES, stride=ch), :] = (
                val[g * SUBLANES:(g + 1) * SUBLANES, j * LANES:(j + 1) * LANES])


def _adaln_kernel(c_ref, w_ref, b_ref, o_ref):
    o_ref[0] = _bdot(_silu(c_ref[...]), w_ref[0]) + b_ref[0]


def _adaln(cond, mod_w, mod_b):
    depth, d, d6 = mod_w.shape
    tn = d6 // 4
    return pl.pallas_call(
        _adaln_kernel,
        grid=(depth, d6 // tn),
        in_specs=[
            pl.BlockSpec((MOD_ROWS, d), lambda l, j: (0, 0)),
            pl.BlockSpec((1, d, tn), lambda l, j: (l, 0, j)),
            pl.BlockSpec((1, 1, tn), lambda l, j: (l, 0, j)),
        ],
        out_specs=pl.BlockSpec((1, MOD_ROWS, tn), lambda l, j: (l, 0, j)),
        out_shape=jax.ShapeDtypeStruct((depth, MOD_ROWS, d6), F32),
        compiler_params=_cparams("parallel", "parallel"),
        name="adaln",
    )(cond, mod_w, mod_b.reshape(depth, 1, d6))


def _pick_stream(x_ref, ctx_ref, tiles_per_batch):
    is_ctx = pl.program_id(0) % tiles_per_batch == 0
    return jnp.where(is_ctx, ctx_ref[0], x_ref[0])


def _even_in_kernel(x_ref, ctx_ref, sh_ref, sc_ref, g_ref, w_ref, qg_ref, kg_ref, c_ref, sa_ref,
                    sb_ref, uv_ref, q_ref, k_ref, v_ref, *, tiles_per_batch):
    h = _pick_stream(x_ref, ctx_ref, tiles_per_batch)
    n = _norm_mod(h, g_ref[...], sh_ref[0], sc_ref[0]).astype(BF16)

    def cols(c0):
        return jnp.dot(n, w_ref[:, c0:c0 + PROJ_BLOCK], preferred_element_type=F32)

    for c0 in range(0, 2 * A_WIDTH, PROJ_BLOCK):
        uv_ref[:, c0:c0 + PROJ_BLOCK] = _gelu(cols(c0))
    cos, sa, sb = c_ref[...], sa_ref[...], sb_ref[...]
    lane = lax.broadcasted_iota(I32, cos.shape, 1)
    low = lane < B_HEAD_DIM

    def head_pair(x, gain, scale):
        sq = x * x
        s_lo = jnp.sum(jnp.where(low, sq, 0.0), axis=-1, keepdims=True)
        s_hi = jnp.sum(jnp.where(low, 0.0, sq), axis=-1, keepdims=True)
        ms = jnp.where(low, s_lo, s_hi) * np.float32(1.0 / B_HEAD_DIM)
        y = x * lax.rsqrt(ms + EPS) * gain
        y = (y * cos + pltpu.roll(y, LANES - B_HEAD_DIM // 2, 1) * sa
             + pltpu.roll(y, B_HEAD_DIM // 2, 1) * sb)
        if scale is not None:
            y = y * scale
        return y.astype(BF16)

    q0 = 2 * A_WIDTH
    k0 = q0 + B_WIDTH
    v0 = k0 + B_WIDTH
    per = PROJ_BLOCK // LANES
    for c0 in range(0, B_WIDTH, PROJ_BLOCK):
        pq = cols(q0 + c0)
        pk = cols(k0 + c0)
        for j in range(per):
            sl = slice(c0 + j * LANES, c0 + (j + 1) * LANES)
            q_ref[:, sl] = head_pair(pq[:, j * LANES:(j + 1) * LANES], qg_ref[...],
                                     np.float32(B_HEAD_DIM ** -0.5))
            k_ref[:, sl] = head_pair(pk[:, j * LANES:(j + 1) * LANES], kg_ref[...], None)
        v_ref[:, c0:c0 + PROJ_BLOCK] = cols(v0 + c0).astype(BF16)


def _odd_in_kernel(h_ref, sh_ref, sc_ref, g_ref, w_ref, qg_ref, kg_ref, c_ref, s_ref,
                   q_ref, k_ref, v_ref):
    n = _norm_mod(h_ref[...], g_ref[...], sh_ref[0], sc_ref[0])
    p = jnp.dot(n.astype(BF16), w_ref[...], preferred_element_type=F32)
    cos, sin = c_ref[...], s_ref[...]

    def head(x, gain, scale):
        y = _rms(x, gain)
        y = y * cos + pltpu.roll(y, C_HEAD_DIM // 2, 1) * sin
        if scale is not None:
            y = y * scale
        return y.astype(BF16)

    nq = C_HEADS * C_HEAD_DIM
    nkv = C_KV_HEADS * C_HEAD_DIM
    for j in range(C_HEADS):
        q_ref[:, j * LANES:(j + 1) * LANES] = head(p[:, j * LANES:(j + 1) * LANES], qg_ref[...],
                                                   None)
    for j in range(C_KV_HEADS):
        k_ref[:, j * LANES:(j + 1) * LANES] = head(
            p[:, nq + j * LANES:nq + (j + 1) * LANES], kg_ref[...], None)
    v_ref[...] = p[:, nq + nkv:nq + 2 * nkv].astype(BF16)


def _diff_attn_kernel(lam_ref, q_ref, k_ref, v_ref, o_ref, *, ctx_len, lam_init):
    lv = lam_ref[...]
    lam = (jnp.exp(jnp.sum(lv[0:1] * lv[1:2], axis=-1, keepdims=True))
           - jnp.exp(jnp.sum(lv[2:3] * lv[3:4], axis=-1, keepdims=True)) + np.float32(lam_init))
    low = lax.broadcasted_iota(I32, (q_ref.shape[0], LANES), 1) < B_HEAD_DIM

    def softmax(qm, k, scale):
        s = lax.dot_general(qm, k, NT_DIMS, preferred_element_type=F32)
        p = jnp.exp(s - jnp.max(s, axis=-1, keepdims=True))
        return p * (scale / jnp.sum(p, axis=-1, keepdims=True))

    def attend(n_keys):
        for hh in range(q_ref.shape[1] // LANES):
            cs = slice(hh * LANES, (hh + 1) * LANES)
            q = q_ref[:, cs]
            zero = jnp.zeros_like(q)
            k = k_ref[0:n_keys, cs]
            a = (softmax(jnp.where(low, q, zero), k, 1.0)
                 - softmax(jnp.where(low, zero, q), k, lam))
            o_ref[:, cs] = jnp.dot(a.astype(BF16), v_ref[0:n_keys, cs], preferred_element_type=F32)

    is_ctx = pl.program_id(1) == 0

    @pl.when(is_ctx)
    def _():
        attend(ctx_len)

    @pl.when(jnp.logical_not(is_ctx))
    def _():
        attend(k_ref.shape[0])


def _gqa_kernel(*refs):
    q_refs = refs[:GQA_Q_TILES]
    k_ref, v_ref, o_ref = refs[GQA_Q_TILES:]
    rows = q_refs[0].shape[0]
    grp = q_refs[0].shape[1] // LANES
    c = np.float32(C_HEAD_DIM ** -0.5 * LOG2E)
    k = k_ref[...]
    v = v_ref[...]
    for j, q_ref in enumerate(q_refs):
        for g in range(grp):
            cs = slice(g * LANES, (g + 1) * LANES)
            s = lax.dot_general(q_ref[:, cs], k, NT_DIMS, preferred_element_type=F32)
            p = jnp.exp2((s - jnp.max(s, axis=-1, keepdims=True)) * c)
            l = jnp.sum(p, axis=-1, keepdims=True)
            o = jnp.dot(p.astype(BF16), v, preferred_element_type=F32) / l
            o_ref[j * rows:(j + 1) * rows, cs] = o.astype(o_ref.dtype)


def _even_out_kernel(o_ref, uv_ref, x_ref, ctx_ref, gate_ref, sh_ref, sc_ref, sub_ref, lng_ref,
                     lnb_ref, ws_ref, bs_ref, w_ref, g2_ref, h1_ref, nx_ref, *, lam_init,
                     tiles_per_batch):
    o = o_ref[...]
    uv = uv_ref[...]
    u = uv[:, :A_WIDTH]
    v = uv[:, A_WIDTH:]
    mu = jnp.mean(v, axis=-1, keepdims=True)
    var = jnp.mean(jnp.square(v - mu), axis=-1, keepdims=True)
    vn = ((v - mu) * lax.rsqrt(var + EPS) * lng_ref[...] + lnb_ref[...]).astype(BF16)
    rows = o.shape[0]
    parts = []
    for c in range(rows // GMLP_CHUNK):
        rs = slice(c * GMLP_CHUNK, (c + 1) * GMLP_CHUNK)
        for g in range(A_GROUPS):
            cs = slice(g * A_GROUP_DIM, (g + 1) * A_GROUP_DIM)
            mixed = jnp.dot(ws_ref[g], vn[rs, cs], preferred_element_type=F32) + bs_ref[:, cs]
            parts.append((c, g, u[rs, cs] * mixed))
    a_rows = [jnp.concatenate([p for (c2, _, p) in parts if c2 == c], axis=-1)
              for c in range(rows // GMLP_CHUNK)]
    a = jnp.concatenate(a_rows, axis=0)
    heads = []
    for hh in range(B_HEADS):
        oh = o[:, hh * LANES:(hh + 1) * LANES]
        heads.append(_rms(oh, sub_ref[...]) * np.float32(1.0 - lam_init))
    cat = jnp.concatenate([a] + heads, axis=-1).astype(BF16)
    y = jnp.dot(cat, w_ref[...], preferred_element_type=F32)
    h1 = _pick_stream(x_ref, ctx_ref, tiles_per_batch) + gate_ref[0] * y
    h1_ref[...] = h1
    _to_token_tiles(nx_ref, _norm_mod(h1, g2_ref[...], sh_ref[0], sc_ref[0]))


def _odd_out_kernel(o_ref, h_ref, gate_ref, sh_ref, sc_ref, w_ref, g2_ref, h1_ref, nx_ref):
    y = jnp.dot(o_ref[...], w_ref[...], preferred_element_type=F32)
    h1 = h_ref[...] + gate_ref[0] * y
    h1_ref[...] = h1
    _to_token_tiles(nx_ref, _norm_mod(h1, g2_ref[...], sh_ref[0], sc_ref[0]))


def _rows_to_block(rows, dtype):
    n = rows[0].shape[1]
    rio = lax.broadcasted_iota(I32, (SUBLANES, n), 0)
    out = jnp.zeros((SUBLANES, n), dtype)
    for r, row in enumerate(rows):
        out = jnp.where(rio == r, jnp.broadcast_to(row.astype(dtype), (SUBLANES, n)), out)
    return out


def _transpose_block(xt):
    n = xt.shape[1]
    eye = jnp.where(lax.broadcasted_iota(I32, (n, n), 0) == lax.broadcasted_iota(I32, (n, n), 1),
                    1.0, 0.0).astype(BF16)
    acc = jnp.zeros((n, SUBLANES), F32)
    rem = xt
    for _ in range(3):
        part = rem.astype(BF16)
        acc = acc + lax.dot_general(eye, part, NT_DIMS, preferred_element_type=F32)
        rem = rem - part.astype(F32)
    return acc


def _router_kernel(x_ref, wr_ref, b_ref, eidx_ref, rank_ref, lpos_ref, wcol_ref, cnt_ref, tcnt_ref,
                   run_ref):
    @pl.when(pl.program_id(0) == 0)
    def _():
        run_ref[...] = jnp.zeros_like(run_ref)

    per = N_EXPERTS // N_GROUPS
    d = wr_ref.shape[1]
    x = _from_token_tiles(x_ref, x_ref.shape[0] * LANES // d, d)
    logits = lax.dot_general(wr_ref[...], x.astype(BF16), NT_DIMS,
                             preferred_element_type=F32)
    scores = jax.nn.sigmoid(logits)
    sel = scores + b_ref[...]
    tm = sel.shape[1]
    neg = np.float32(-np.inf)
    jio = lax.broadcasted_iota(I32, (per, tm), 0).astype(F32)
    gio = lax.broadcasted_iota(I32, (N_GROUPS, tm), 0).astype(F32)

    def rmax(x):
        return jnp.max(x, axis=0, keepdims=True)

    def rmin(x):
        return jnp.min(x, axis=0, keepdims=True)

    sel_g = [sel[g * per:(g + 1) * per, :] for g in range(N_GROUPS)]
    sc_g = [scores[g * per:(g + 1) * per, :] for g in range(N_GROUPS)]
    gs = jnp.zeros((N_GROUPS, tm), F32)
    for g in range(N_GROUPS):
        m1 = rmax(sel_g[g])
        i1 = rmin(jnp.where(sel_g[g] == m1, jio, np.float32(per)))
        m2 = rmax(jnp.where(jio == i1, neg, sel_g[g]))
        gs = jnp.where(gio == np.float32(g), jnp.broadcast_to(m1 + m2, gs.shape), gs)
    gsel = jnp.zeros((N_GROUPS, tm), I32)
    for _ in range(TOPK_GROUPS):
        m = rmax(gs)
        idx = rmin(jnp.where(gs == m, gio, np.float32(N_GROUPS)))
        hit = gio == idx
        gsel = jnp.where(hit, 1, gsel)
        gs = jnp.where(hit, neg, gs)
    masked = [jnp.where(jnp.broadcast_to(gsel[g:g + 1, :], (per, tm)) == 1, sel_g[g], neg)
              for g in range(N_GROUPS)]
    eio = [jio + np.float32(g * per) for g in range(N_GROUPS)]
    e_rows, w_rows, hits = [], [], []
    for _ in range(TOP_K):
        m = masked[0]
        for g in range(1, N_GROUPS):
            m = jnp.maximum(m, masked[g])
        m = rmax(m)
        cand = jnp.where(masked[0] == m, eio[0], np.float32(N_EXPERTS))
        for g in range(1, N_GROUPS):
            cand = jnp.minimum(cand, jnp.where(masked[g] == m, eio[g], np.float32(N_EXPERTS)))
        idx = rmin(cand)
        hit = [eio[g] == idx for g in range(N_GROUPS)]
        wsel = jnp.where(hit[0], sc_g[0], 0.0)
        for g in range(1, N_GROUPS):
            wsel = wsel + jnp.where(hit[g], sc_g[g], 0.0)
        masked = [jnp.where(hit[g], neg, masked[g]) for g in range(N_GROUPS)]
        e_rows.append(idx)
        w_rows.append(jnp.sum(wsel, axis=0, keepdims=True))
        hits.append(hit)
    wsum = w_rows[0]
    for r in w_rows[1:]:
        wsum = wsum + r
    w_rows = [r / wsum * np.float32(ROUTE_SCALE) for r in w_rows]
    onehot = []
    for g in range(N_GROUPS):
        any_hit = hits[0][g]
        for kk in range(1, TOP_K):
            any_hit = jnp.logical_or(any_hit, hits[kk][g])
        onehot.append(jnp.where(any_hit, 1.0, 0.0))
    mt = jnp.concatenate(onehot, axis=0)
    before = (lax.broadcasted_iota(I32, (tm, tm), 0) < lax.broadcasted_iota(I32, (tm, tm), 1))
    prefix = jnp.dot(mt.astype(BF16), jnp.where(before, 1.0, 0.0).astype(BF16),
                     preferred_element_type=F32)
    def pick(table):
        rows = []
        for kk in range(TOP_K):
            acc = jnp.where(hits[kk][0], table[0:per, :], 0.0)
            for g in range(1, N_GROUPS):
                acc = acc + jnp.where(hits[kk][g], table[g * per:(g + 1) * per, :], 0.0)
            rows.append(jnp.sum(acc, axis=0, keepdims=True))
        return rows

    n_exp = mt.shape[0]
    lower = (lax.broadcasted_iota(I32, (n_exp, n_exp), 0) > lax.broadcasted_iota(I32, (n_exp, n_exp), 1))
    lower = jnp.where(lower, 1.0, 0.0).astype(BF16)
    tok = lax.broadcasted_iota(I32, (1, tm), 1)
    local = jnp.zeros_like(prefix)
    seen = jnp.zeros((n_exp, 1), F32)
    for hh in range(tm // ROW_TILE):
        in_tile = jnp.logical_and(tok >= hh * ROW_TILE, tok < (hh + 1) * ROW_TILE)
        c_tile = jnp.sum(jnp.where(in_tile, mt, 0.0), axis=1, keepdims=True)
        offs = jnp.dot(lower, jnp.broadcast_to(c_tile, (n_exp, LANES)).astype(BF16),
                       preferred_element_type=F32)[:, 0:1]
        local = jnp.where(in_tile, prefix - seen + offs, local)
        tcnt_ref[hh * n_exp:(hh + 1) * n_exp, :] = jnp.broadcast_to(c_tile, (n_exp, LANES))
        seen = seen + c_tile

    run = run_ref[...] + jnp.sum(mt, axis=1, keepdims=True)
    eidx_ref[...] = _rows_to_block(e_rows, I32)
    rank_ref[...] = _rows_to_block(pick(prefix + run_ref[...]), I32)
    lpos_ref[...] = _rows_to_block(pick(local), I32) * SUBLANES
    wcol_ref[...] = _transpose_block(_rows_to_block(w_rows, F32))
    cnt_ref[...] = jnp.broadcast_to(run, cnt_ref.shape)
    run_ref[...] = run


def _dest_kernel(start_ref, eidx_ref, rank_ref, dest_ref):
    per = N_EXPERTS // N_GROUPS
    eidx = eidx_ref[...]
    tm = eidx.shape[1]
    jio = lax.broadcasted_iota(I32, (per, tm), 0)
    rows = []
    for kk in range(TOP_K):
        e = jnp.broadcast_to(eidx[kk:kk + 1, :], (per, tm))
        acc = jnp.zeros((per, tm), F32)
        for g in range(N_GROUPS):
            st = jnp.broadcast_to(start_ref[g * per:(g + 1) * per, :], (per, tm))
            acc = acc + jnp.where(jio + g * per == e, st, 0.0)
        rows.append(jnp.sum(acc, axis=0, keepdims=True))
    dest_ref[...] = _rows_to_block(rows, I32) + rank_ref[...]


def _row_copy(src, s_row, dst, d_row, sem):
    def tile_start(row):
        start = row * SUBLANES
        return start if isinstance(row, int) else pl.multiple_of(start, SUBLANES)

    s0 = tile_start(s_row)
    d0 = tile_start(d_row)
    return pltpu.make_async_copy(src.at[pl.ds(s0, SUBLANES)], dst.at[pl.ds(d0, SUBLANES)], sem)


def _dispatch_kernel(lpos_ref, cnt_ref, xpos_ref, x_ref, xs_ref, stage, sem):
    i = pl.program_id(0)
    rows = x_ref.shape[0] // SUBLANES
    n_exp = cnt_ref.shape[2]
    unroll = 4
    small = [1 << b for b in range(4, -1, -1)]
    big = [1 << b for b in range(rows.bit_length() - 1, 4, -1)]

    def span(start, n_rows):
        return pl.ds(pl.multiple_of(start, SUBLANES), n_rows * SUBLANES)

    def fill_and_send(slot):
        def place(t4, carry):
            for u in range(unroll):
                t = t4 * unroll + u
                tok = x_ref[span(t * SUBLANES, 1), :]
                for kk in range(TOP_K):
                    stage[slot, span(lpos_ref[kk, t], 1), :] = tok
            return carry

        lax.fori_loop(0, rows // unroll, place, 0)

        def chunks(sizes, n, src, dst):
            for j, size in enumerate(sizes):
                take = (n & size) != 0

                @pl.when(take)
                def _(src=src, dst=dst, size=size, j=j):
                    pltpu.make_async_copy(stage.at[slot, span(src, size)], xs_ref.at[span(dst, size)],
                                          sem.at[slot]).start(priority=j % 2)

                step = jnp.where(take, size * SUBLANES, 0)
                src = src + step
                dst = dst + step

        def send(e, src):
            n = cnt_ref[0, 0, e]
            dst = xpos_ref[0, 0, e] * SUBLANES
            n_big = n & ~(2 * small[0] - 1)

            @pl.when(n_big != 0)
            def _():
                chunks(big, n, src, dst)

            chunks(small, n, src + n_big * SUBLANES, dst + n_big * SUBLANES)
            return src + n * SUBLANES

        lax.fori_loop(0, n_exp, send, 0)

    def wait_stage(slot):
        pltpu.make_async_copy(stage.at[slot], stage.at[slot], sem.at[slot]).wait()

    for slot in range(2):
        @pl.when(i % 2 == slot)
        def _(slot=slot):
            @pl.when(i >= 2)
            def _():
                wait_stage(slot)

            fill_and_send(slot)

    @pl.when(i == pl.num_programs(0) - 1)
    def _():
        for slot in range(2):
            @pl.when(jnp.logical_or(i >= 1, i % 2 == slot))
            def _(slot=slot):
                wait_stage(slot)


def _expert_kernel(tile_s, exp_s, lo_s, hi_s, first_s, last_s, new_s, inv_prev_ref, inv_ref,
                   xs_ref, wg_ref, wu_ref, wd_ref, ys_ref, wg_b, wu_b, wd_b, acc, stage, sem):
    v = pl.program_id(0)
    tile = tile_s[v]
    rows, d = acc.shape

    @pl.when(new_s[v] == 1)
    def _():
        wg_b[...] = wg_ref[0, 0].astype(BF16)
        wu_b[...] = wu_ref[0, 0].astype(BF16)
        wd_b[...] = wd_ref[0, 0].astype(BF16)

    @pl.when(first_s[v] == 1)
    def _():
        acc[...] = jnp.zeros_like(acc)

    lo = lo_s[v]
    hi = hi_s[v]
    nonempty = hi > lo
    flush_prev = jnp.logical_and(first_s[v] == 1, tile >= 1)

    def compute():
        x = _from_token_tiles(xs_ref, rows, d).astype(BF16)
        g = jnp.dot(x, wg_b[...], preferred_element_type=F32)
        u = jnp.dot(x, wu_b[...], preferred_element_type=F32)
        y = jnp.dot((_silu(g) * u).astype(BF16), wd_b[...], preferred_element_type=F32)
        row = lax.broadcasted_iota(I32, (rows, 1), 0)
        mine = jnp.logical_and(row >= lo, row < hi)
        acc[...] = jnp.where(mine, y, acc[...])

    def start_scatter(idx_ref, slot, unrolled):
        def start(r, par):
            _row_copy(stage.at[slot], r, ys_ref, idx_ref[0, 0, r], sem.at[slot]).start(priority=par)

        if unrolled:
            for r in range(rows):
                start(r, r % 2)
        else:
            def body(r2, carry):
                for par in range(2):
                    start(r2 * 2 + par, par)
                return carry
            lax.fori_loop(0, rows // 2, body, 0)

    def wait_scatter(slot):
        pltpu.make_async_copy(stage.at[slot], stage.at[slot], sem.at[slot]).wait()

    for slot in range(2):
        prev_here = jnp.logical_and(flush_prev, (tile + 1) % 2 == slot)

        @pl.when(jnp.logical_and(prev_here, nonempty))
        def _(slot=slot):
            start_scatter(inv_prev_ref, slot, True)
            compute()

        @pl.when(jnp.logical_and(prev_here, jnp.logical_not(nonempty)))
        def _(slot=slot):
            start_scatter(inv_prev_ref, slot, False)

    @pl.when(jnp.logical_and(nonempty, jnp.logical_not(flush_prev)))
    def _():
        compute()

    is_final = v == pl.num_programs(0) - 1
    for slot in range(2):
        @pl.when(jnp.logical_and(last_s[v] == 1, tile % 2 == slot))
        def _(slot=slot):
            @pl.when(tile >= 2)
            def _():
                wait_scatter(slot)

            _to_token_tiles(stage.at[slot], acc[...])

            @pl.when(is_final)
            def _():
                start_scatter(inv_ref, slot, False)

    @pl.when(is_final)
    def _():
        for slot in range(2):
            @pl.when(jnp.logical_or(tile >= 1, tile % 2 == slot))
            def _(slot=slot):
                wait_scatter(slot)


def _combine_kernel(wcol_ref, x_ref, h_ref, gate_ref, sg_ref, su_ref, sd_ref, *refs):
    y_refs, o_ref = refs[:TOP_K], refs[TOP_K]
    rows, d = h_ref.shape
    x = _from_token_tiles(x_ref, rows, d).astype(BF16)
    g = jnp.dot(x, sg_ref[...], preferred_element_type=F32)
    u = jnp.dot(x, su_ref[...], preferred_element_type=F32)
    acc = jnp.dot((_silu(g) * u).astype(BF16), sd_ref[...], preferred_element_type=F32)
    wcol = wcol_ref[...]
    for kk in range(TOP_K):
        acc = acc + _from_token_tiles(y_refs[kk], rows, d) * wcol[:, kk:kk + 1]
    o_ref[...] = h_ref[...] + gate_ref[0] * acc


def _moe(nx, h, modv, gate_row_of, layer, w_router, router_bias, w_gate, w_up, w_down, sg, su, sd):
    t, d = h.shape
    ch = d // LANES
    n_slots = t * TOP_K
    n_rt = t // ROUTER_TILE
    e = N_EXPERTS
    tiles_per_rt = ROUTER_TILE // ROW_TILE
    eidx, rank, lpos, wcol, cnt, tcnt = pl.pallas_call(
        _router_kernel,
        grid=(n_rt,),
        in_specs=[
            pl.BlockSpec((ROUTER_TILE * ch, LANES), lambda i: (i, 0)),
            pl.BlockSpec((e, d), lambda i: (0, 0)),
            pl.BlockSpec((e, 1), lambda i: (0, 0)),
        ],
        out_specs=[
            pl.BlockSpec((SUBLANES, ROUTER_TILE), lambda i: (0, i)),
            pl.BlockSpec((SUBLANES, ROUTER_TILE), lambda i: (0, i)),
            pl.BlockSpec((SUBLANES, ROUTER_TILE), lambda i: (0, i)),
            pl.BlockSpec((ROUTER_TILE, SUBLANES), lambda i: (i, 0)),
            pl.BlockSpec((e, LANES), lambda i: (0, 0)),
            pl.BlockSpec((tiles_per_rt * e, LANES), lambda i: (i, 0)),
        ],
        out_shape=[
            jax.ShapeDtypeStruct((SUBLANES, t), I32),
            jax.ShapeDtypeStruct((SUBLANES, t), I32),
            jax.ShapeDtypeStruct((SUBLANES, t), I32),
            jax.ShapeDtypeStruct((t, SUBLANES), F32),
            jax.ShapeDtypeStruct((e, LANES), F32),
            jax.ShapeDtypeStruct((n_rt * tiles_per_rt * e, LANES), F32),
        ],
        scratch_shapes=[pltpu.VMEM((e, 1), F32)],
        compiler_params=_cparams("arbitrary"),
        name="moe_router",
    )(nx, w_router.T.astype(BF16), router_bias.reshape(e, 1))

    counts = cnt[:, 0].astype(I32)
    ends = jnp.cumsum(counts)
    starts = ends - counts
    dest = pl.pallas_call(
        _dest_kernel,
        grid=(n_rt,),
        in_specs=[
            pl.BlockSpec((e, 1), lambda i: (0, 0)),
            pl.BlockSpec((SUBLANES, ROUTER_TILE), lambda i: (0, i)),
            pl.BlockSpec((SUBLANES, ROUTER_TILE), lambda i: (0, i)),
        ],
        out_specs=pl.BlockSpec((SUBLANES, ROUTER_TILE), lambda i: (0, i)),
        out_shape=jax.ShapeDtypeStruct((SUBLANES, t), I32),
        compiler_params=_cparams("parallel"),
        name="moe_dest",
    )(starts.astype(F32).reshape(e, 1), eidx, rank)
    n_tiles = t // ROW_TILE
    inv = jnp.argsort(dest[:TOP_K].reshape(-1)).astype(I32)
    tile_cnt = tcnt[:, 0].astype(I32).reshape(n_tiles, e)
    run_pos = starts[None, :] + jnp.cumsum(tile_cnt, axis=0) - tile_cnt
    run_spec = pl.BlockSpec((1, 1, e), lambda i: (i, 0, 0), memory_space=pltpu.SMEM)
    xs = pl.pallas_call(
        _dispatch_kernel,
        grid=(n_tiles,),
        in_specs=[
            pl.BlockSpec((SUBLANES, ROW_TILE), lambda i: (0, i), memory_space=pltpu.SMEM),
            run_spec, run_spec,
            pl.BlockSpec((ROW_TILE * ch, LANES), lambda i: (i, 0)),
        ],
        out_specs=pl.BlockSpec(memory_space=pl.ANY),
        out_shape=jax.ShapeDtypeStruct((n_slots * ch, LANES), F32),
        scratch_shapes=[pltpu.VMEM((2, TOP_K * ROW_TILE * ch, LANES), F32),
                        pltpu.SemaphoreType.DMA((2,))],
        compiler_params=pltpu.CompilerParams(dimension_semantics=("arbitrary",),
                                             vmem_limit_bytes=VMEM_LIMIT, has_side_effects=True),
        name="moe_dispatch",
    )(lpos, tile_cnt.reshape(n_tiles, 1, e), run_pos.reshape(n_tiles, 1, e), nx)

    n_et = n_slots // EXPERT_TILE
    pts = jnp.sort(jnp.concatenate([jnp.arange(n_et, dtype=I32) * EXPERT_TILE, starts]))
    lo = pts
    hi = jnp.concatenate([pts[1:], jnp.full((1,), n_slots, I32)])
    tile = jnp.minimum(lo // EXPERT_TILE, n_et - 1)
    expert = jnp.minimum(jnp.sum((ends[None, :] <= lo[:, None]).astype(I32), axis=1), e - 1)
    one = jnp.ones((1,), I32)
    tile_change = (tile[1:] != tile[:-1]).astype(I32)
    first = jnp.concatenate([one, tile_change])
    last = jnp.concatenate([tile_change, one])
    newexp = jnp.concatenate([one, (expert[1:] != expert[:-1]).astype(I32)])
    lo_in = lo - tile * EXPERT_TILE
    hi_in = hi - tile * EXPERT_TILE
    n_visits = n_et + e
    d_exp = w_gate.shape[-1]
    ys = pl.pallas_call(
        _expert_kernel,
        grid_spec=pltpu.PrefetchScalarGridSpec(
            num_scalar_prefetch=7,
            grid=(n_visits,),
            in_specs=[
                pl.BlockSpec((1, 1, EXPERT_TILE),
                             lambda v, ti, ex, *_: (jnp.maximum(ti[v] - 1, 0), 0, 0),
                             memory_space=pltpu.SMEM),
                pl.BlockSpec((1, 1, EXPERT_TILE), lambda v, ti, ex, *_: (ti[v], 0, 0),
                             memory_space=pltpu.SMEM),
                pl.BlockSpec((EXPERT_TILE * ch, LANES), lambda v, ti, ex, *_: (ti[v], 0)),
                pl.BlockSpec((1, 1, d, d_exp), lambda v, ti, ex, *_: (layer, ex[v], 0, 0)),
                pl.BlockSpec((1, 1, d, d_exp), lambda v, ti, ex, *_: (layer, ex[v], 0, 0)),
                pl.BlockSpec((1, 1, d_exp, d), lambda v, ti, ex, *_: (layer, ex[v], 0, 0)),
            ],
            out_specs=pl.BlockSpec(memory_space=pl.ANY),
            scratch_shapes=[pltpu.VMEM((d, d_exp), BF16), pltpu.VMEM((d, d_exp), BF16),
                            pltpu.VMEM((d_exp, d), BF16), pltpu.VMEM((EXPERT_TILE, d), F32),
                            pltpu.VMEM((2, EXPERT_TILE * ch, LANES), F32),
                            pltpu.SemaphoreType.DMA((2,))],
        ),
        out_shape=jax.ShapeDtypeStruct((n_slots * ch, LANES), F32),
        compiler_params=_cparams("arbitrary"),
        name="moe_experts",
    )(tile, expert, lo_in, hi_in, first, last, newexp, inv.reshape(n_et, 1, EXPERT_TILE),
      inv.reshape(n_et, 1, EXPERT_TILE), xs, w_gate, w_up, w_down)

    d_sh = sg.shape[-1]
    slot_specs = [pl.BlockSpec((ROW_TILE * ch, LANES), lambda i, kk=kk: (kk * n_tiles + i, 0))
                  for kk in range(TOP_K)]
    return pl.pallas_call(
        _combine_kernel,
        grid=(n_tiles,),
        in_specs=[
            pl.BlockSpec((ROW_TILE, SUBLANES), lambda i: (i, 0)),
            pl.BlockSpec((ROW_TILE * ch, LANES), lambda i: (i, 0)),
            pl.BlockSpec((ROW_TILE, d), lambda i: (i, 0)),
            pl.BlockSpec((1, 1, d), lambda i: (gate_row_of(i) * 6 + 5, 0, 0)),
            pl.BlockSpec((d, d_sh), lambda i: (0, 0)),
            pl.BlockSpec((d, d_sh), lambda i: (0, 0)),
            pl.BlockSpec((d_sh, d), lambda i: (0, 0)),
        ] + slot_specs,
        out_specs=pl.BlockSpec((ROW_TILE, d), lambda i: (i, 0)),
        out_shape=jax.ShapeDtypeStruct((t, d), F32),
        compiler_params=_cparams("parallel"),
        name="moe_combine",
    )(wcol, nx, h, modv, sg.astype(BF16), su.astype(BF16), sd.astype(BF16), *([ys] * TOP_K))


def _rope_tables(n_lat, n_ctx, head_dim):
    rows = n_lat // GRID_W
    row = jnp.repeat(jnp.arange(rows, dtype=F32), GRID_W)
    col = jnp.tile(jnp.arange(GRID_W, dtype=F32), rows)
    n_freq = head_dim // 4
    inv = ROPE_THETA ** (-jnp.arange(n_freq, dtype=F32) / n_freq)
    ang = jnp.concatenate([row[:, None] * inv, col[:, None] * inv], axis=-1)
    cos = jnp.concatenate([jnp.ones((n_ctx, head_dim // 2), F32), jnp.cos(ang)], axis=0)
    sin = jnp.concatenate([jnp.zeros((n_ctx, head_dim // 2), F32), jnp.sin(ang)], axis=0)
    return cos, sin


def _split_halves_perm(head_dim):
    return np.concatenate([np.arange(0, head_dim, 2), np.arange(1, head_dim, 2)])


def kernel(x, c, ctx, c_ctx, mod_w, mod_b, norm1_g, norm2_g, ev_w_in, ev_w_out, a_ln_g, a_ln_b, a_ws, a_bs, b_q_norm, b_k_norm, b_lam_q1, b_lam_k1, b_lam_q2, b_lam_k2, b_subln, od_w_qkv, od_w_out, c_q_norm, c_k_norm, moe_router, moe_bias, moe_w_gate, moe_w_up, moe_w_down, sh_w_gate, sh_w_up, sh_w_down):
    bsz, n_lat, d = x.shape
    n_ctx = ctx.shape[1]
    depth = mod_w.shape[0]
    assert depth == 2 and n_ctx == ROW_TILE and n_lat % ROW_TILE == 0 and bsz + 1 <= MOD_ROWS
    assert d == SUBLANES * LANES
    n_seq = n_ctx + n_lat
    tpb = n_seq // ROW_TILE
    lpb = n_lat // ROW_TILE
    t_all = bsz * n_seq
    n_tiles = t_all // ROW_TILE
    ctx_row = bsz

    cond = jnp.zeros((MOD_ROWS, d), F32).at[:bsz].set(c).at[ctx_row].set(c_ctx)
    mod = _adaln(cond, mod_w, mod_b)
    modv = [mod[l].reshape(MOD_ROWS * 6, 1, d) for l in range(depth)]

    def row_all(i):
        return jnp.where(i % tpb == 0, ctx_row, i // tpb)

    def mspec(j, row_of):
        return pl.BlockSpec((1, 1, d), lambda i: (row_of(i) * 6 + j, 0, 0))

    def full(shape):
        return pl.BlockSpec(shape, lambda *_: (0,) * len(shape))

    x_spec = pl.BlockSpec((1, ROW_TILE, d), lambda i: (i // tpb, jnp.maximum(i % tpb - 1, 0), 0))
    ctx_spec = pl.BlockSpec((1, ROW_TILE, d), lambda i: (i // tpb, 0, 0))
    ch = d // LANES
    tok_spec = pl.BlockSpec((ROW_TILE * ch, LANES), lambda i: (i, 0))

    lam_init = 0.8 - 0.6 * math.exp(-0.3 * 0)
    p64 = _split_halves_perm(B_HEAD_DIM)
    col_perm = np.concatenate(
        [np.arange(2 * A_WIDTH)]
        + [2 * A_WIDTH + blk * B_HEAD_DIM + p64 for blk in range(2 * B_WIDTH // B_HEAD_DIM)]
        + [np.arange(2 * A_WIDTH + 2 * B_WIDTH, 2 * A_WIDTH + 3 * B_WIDTH)])
    w_in = ev_w_in[0][:, col_perm].astype(BF16)
    even_in = w_in.shape[1]
    cos_b, sin_b = _rope_tables(n_lat, n_ctx, B_HEAD_DIM)
    zeros_b = jnp.zeros_like(sin_b)
    tab_c = jnp.tile(jnp.concatenate([cos_b, cos_b], axis=-1), (1, 2))
    tab_sa = jnp.tile(jnp.concatenate([-sin_b, zeros_b], axis=-1), (1, 2))
    tab_sb = jnp.tile(jnp.concatenate([zeros_b, sin_b], axis=-1), (1, 2))
    qg = jnp.tile(b_q_norm[0][p64], 2).reshape(1, LANES)
    kg = jnp.tile(b_k_norm[0][p64], 2).reshape(1, LANES)
    tab_spec = pl.BlockSpec((ROW_TILE, LANES), lambda i: (i % tpb, 0))
    row_spec = lambda w: pl.BlockSpec((ROW_TILE, w), lambda i: (i, 0))
    uv, q, k, v = pl.pallas_call(
        functools.partial(_even_in_kernel, tiles_per_batch=tpb),
        grid=(n_tiles,),
        in_specs=[x_spec, ctx_spec, mspec(0, row_all), mspec(1, row_all), full((1, d)),
                  full((d, even_in)), full((1, LANES)), full((1, LANES)),
                  tab_spec, tab_spec, tab_spec],
        out_specs=[row_spec(2 * A_WIDTH), row_spec(B_WIDTH), row_spec(B_WIDTH), row_spec(B_WIDTH)],
        out_shape=[jax.ShapeDtypeStruct((t_all, 2 * A_WIDTH), F32),
                   jax.ShapeDtypeStruct((t_all, B_WIDTH), BF16),
                   jax.ShapeDtypeStruct((t_all, B_WIDTH), BF16),
                   jax.ShapeDtypeStruct((t_all, B_WIDTH), BF16)],
        compiler_params=_cparams("parallel"),
        name="even_in",
    )(x, ctx, modv[0], modv[0], norm1_g[0].reshape(1, d), w_in, qg, kg, tab_c, tab_sa, tab_sb)

    lamv = jnp.zeros((SUBLANES, LANES), F32)
    for r, vec in enumerate((b_lam_q1[0], b_lam_k1[0], b_lam_q2[0], b_lam_k2[0])):
        lamv = lamv.at[r, :B_HEAD_DIM].set(vec)
    o = pl.pallas_call(
        functools.partial(_diff_attn_kernel, ctx_len=n_ctx, lam_init=lam_init),
        grid=(bsz, tpb),
        in_specs=[
            pl.BlockSpec((SUBLANES, LANES), lambda b, qi: (0, 0)),
            pl.BlockSpec((ROW_TILE, B_WIDTH), lambda b, qi: (b * tpb + qi, 0)),
            pl.BlockSpec((n_seq, B_WIDTH), lambda b, qi: (b, 0)),
            pl.BlockSpec((n_seq, B_WIDTH), lambda b, qi: (b, 0)),
        ],
        out_specs=pl.BlockSpec((ROW_TILE, B_WIDTH), lambda b, qi: (b * tpb + qi, 0)),
        out_shape=jax.ShapeDtypeStruct((t_all, B_WIDTH), F32),
        compiler_params=_cparams("parallel", "arbitrary"),
        name="diff_attn",
    )(lamv, q, k, v)

    bs_col = jnp.repeat(a_bs[0].T, A_GROUP_DIM, axis=1)
    sub_g = b_subln[0].reshape(1, LANES)
    h1, nx = pl.pallas_call(
        functools.partial(_even_out_kernel, lam_init=lam_init, tiles_per_batch=tpb),
        grid=(n_tiles,),
        in_specs=[row_spec(B_WIDTH), row_spec(2 * A_WIDTH), x_spec, ctx_spec,
                  mspec(2, row_all), mspec(3, row_all), mspec(4, row_all),
                  full((1, LANES)), full((1, A_WIDTH)), full((1, A_WIDTH)),
                  full((A_GROUPS, GMLP_CHUNK, GMLP_CHUNK)), full((GMLP_CHUNK, A_WIDTH)),
                  full((A_WIDTH + B_WIDTH, d)), full((1, d))],
        out_specs=[row_spec(d), tok_spec],
        out_shape=[jax.ShapeDtypeStruct((t_all, d), F32),
                   jax.ShapeDtypeStruct((t_all * ch, LANES), F32)],
        compiler_params=_cparams("parallel"),
        name="even_out",
    )(o, uv, x, ctx, modv[0], modv[0], modv[0], sub_g, a_ln_g[0].reshape(1, A_WIDTH),
      a_ln_b[0].reshape(1, A_WIDTH), a_ws[0].astype(BF16), bs_col,
      ev_w_out[0].astype(BF16), norm2_g[0].reshape(1, d))

    h2 = _moe(nx, h1, modv[0], row_all, 0, moe_router[0], moe_bias[0], moe_w_gate, moe_w_up,
              moe_w_down, sh_w_gate[0], sh_w_up[0], sh_w_down[0])

    p128 = _split_halves_perm(C_HEAD_DIM)
    n_qkv_heads = C_HEADS + 2 * C_KV_HEADS
    col_perm = np.concatenate(
        [blk * C_HEAD_DIM + p128 for blk in range(C_HEADS + C_KV_HEADS)]
        + [np.arange((C_HEADS + C_KV_HEADS) * C_HEAD_DIM, n_qkv_heads * C_HEAD_DIM)])
    w_qkv = od_w_qkv[0][:, col_perm].astype(BF16)
    cos_c, sin_c = _rope_tables(n_lat, n_ctx, C_HEAD_DIM)
    tab_c1 = jnp.concatenate([cos_c, cos_c], axis=-1)
    tab_s1 = jnp.concatenate([-sin_c, sin_c], axis=-1)
    qg1 = c_q_norm[0][p128].reshape(1, LANES)
    kg1 = c_k_norm[0][p128].reshape(1, LANES)
    nq = C_HEADS * C_HEAD_DIM
    nkv = C_KV_HEADS * C_HEAD_DIM
    q1, k1, v1 = pl.pallas_call(
        _odd_in_kernel,
        grid=(n_tiles,),
        in_specs=[row_spec(d), mspec(0, row_all), mspec(1, row_all), full((1, d)),
                  full((d, nq + 2 * nkv)), full((1, LANES)), full((1, LANES)), tab_spec, tab_spec],
        out_specs=[row_spec(nq), row_spec(nkv), row_spec(nkv)],
        out_shape=[jax.ShapeDtypeStruct((t_all, nq), BF16),
                   jax.ShapeDtypeStruct((t_all, nkv), BF16),
                   jax.ShapeDtypeStruct((t_all, nkv), BF16)],
        compiler_params=_cparams("parallel"),
        name="odd_in",
    )(h2, modv[1], modv[1], norm1_g[1].reshape(1, d), w_qkv, qg1, kg1, tab_c1, tab_s1)

    t_lat = bsz * n_lat
    grp = C_HEADS // C_KV_HEADS
    o1 = pl.pallas_call(
        _gqa_kernel,
        grid=(bsz, C_KV_HEADS, lpb // GQA_Q_TILES),
        in_specs=[
            pl.BlockSpec((ROW_TILE, grp * LANES),
                         lambda b, n, qi, j=j: (b * tpb + 1 + qi * GQA_Q_TILES + j, n))
            for j in range(GQA_Q_TILES)
        ] + [
            pl.BlockSpec((n_seq, LANES), lambda b, n, qi: (b, n)),
            pl.BlockSpec((n_seq, LANES), lambda b, n, qi: (b, n)),
        ],
        out_specs=pl.BlockSpec((GQA_Q_TILES * ROW_TILE, grp * LANES),
                               lambda b, n, qi: (b * (lpb // GQA_Q_TILES) + qi, n)),
        out_shape=jax.ShapeDtypeStruct((t_lat, nq), BF16),
        compiler_params=_cparams("parallel", "parallel", "arbitrary"),
        name="gqa_attn",
    )(*([q1] * GQA_Q_TILES), k1, v1)

    def row_lat(i):
        return i // lpb

    lat_tiles = t_lat // ROW_TILE
    hx, nx1 = pl.pallas_call(
        _odd_out_kernel,
        grid=(lat_tiles,),
        in_specs=[row_spec(nq),
                  pl.BlockSpec((ROW_TILE, d), lambda i: ((i // lpb) * tpb + 1 + i % lpb, 0)),
                  mspec(2, row_lat), mspec(3, row_lat), mspec(4, row_lat),
                  full((nq, d)), full((1, d))],
        out_specs=[row_spec(d), tok_spec],
        out_shape=[jax.ShapeDtypeStruct((t_lat, d), F32),
                   jax.ShapeDtypeStruct((t_lat * ch, LANES), F32)],
        compiler_params=_cparams("parallel"),
        name="odd_out",
    )(o1, h2, modv[1], modv[1], modv[1], od_w_out[0].astype(BF16), norm2_g[1].reshape(1, d))

    out = _moe(nx1, hx, modv[1], row_lat, 1, moe_router[1], moe_bias[1], moe_w_gate, moe_w_up,
               moe_w_down, sh_w_gate[1], sh_w_up[1], sh_w_down[1])
    return out.reshape(bsz, n_lat, d)
```

```python
import functools
import math

import numpy as np
import jax
import jax.numpy as jnp
from jax import lax
from jax.experimental import pallas as pl
from jax.experimental.pallas import tpu as pltpu

F32 = jnp.float32
BF16 = jnp.bfloat16
I32 = jnp.int32

GRID_W = 64
EPS = 1e-6
ROPE_THETA = 10000.0
A_GROUPS = 4
A_GROUP_DIM = 128
A_WIDTH = A_GROUPS * A_GROUP_DIM
GMLP_CHUNK = 128
B_HEADS = 4
B_HEAD_DIM = 64
B_WIDTH = B_HEADS * 2 * B_HEAD_DIM
C_HEADS = 8
C_KV_HEADS = 2
C_HEAD_DIM = 128
N_EXPERTS = 64
TOP_K = 6
N_GROUPS = 8
TOPK_GROUPS = 4
ROUTE_SCALE = 2.5
LOG2E = math.log2(math.e)

LANES = 128
SUBLANES = 8
ROW_TILE = 256
ROUTER_TILE = 1024
EXPERT_TILE = 512
GQA_Q_TILES = 4
PROJ_BLOCK = 256
MOD_ROWS = 24
VMEM_LIMIT = 56 * 1024 * 1024

NT_DIMS = (((1,), (1,)), ((), ()))


def _cparams(*sem):
    return pltpu.CompilerParams(dimension_semantics=sem, vmem_limit_bytes=VMEM_LIMIT)


def _rms(x, g):
    return x * lax.rsqrt(jnp.mean(x * x, axis=-1, keepdims=True) + EPS) * g


def _norm_mod(h, g, shift, scale):
    return _rms(h, g) * (1.0 + scale) + shift


def _gelu(x):
    return 0.5 * x * (1.0 + lax.erf(x * np.float32(math.sqrt(0.5))))


def _silu(x):
    return x * jax.nn.sigmoid(x)


def _bdot(a, b):
    return jnp.dot(a.astype(BF16), b.astype(BF16), preferred_element_type=F32)


def _from_token_tiles(ref, rows, d):
    ch = d // LANES
    groups = []
    for g in range(rows // SUBLANES):
        groups.append(jnp.concatenate(
            [ref[pl.ds(g * SUBLANES * ch + j, SUBLANES, stride=ch), :] for j in range(ch)], axis=-1))
    return jnp.concatenate(groups, axis=0)


def _to_token_tiles(ref, val):
    rows, d = val.shape
    ch = d // LANES
    for g in range(rows // SUBLANES):
        for j in range(ch):
            ref[pl.ds(g * SUBLANES * ch + j, SUBLANES, stride=ch), :] = (
                val[g * SUBLANES:(g + 1) * SUBLANES, j * LANES:(j + 1) * LANES])


def _adaln_kernel(c_ref, w_ref, b_ref, o_ref):
    o_ref[0] = _bdot(_silu(c_ref[...]), w_ref[0]) + b_ref[0]


def _adaln(cond, mod_w, mod_b):
    depth, d, d6 = mod_w.shape
    tn = d6 // 4
    return pl.pallas_call(
        _adaln_kernel,
        grid=(depth, d6 // tn),
        in_specs=[
            pl.BlockSpec((MOD_ROWS, d), lambda l, j: (0, 0)),
            pl.BlockSpec((1, d, tn), lambda l, j: (l, 0, j)),
            pl.BlockSpec((1, 1, tn), lambda l, j: (l, 0, j)),
        ],
        out_specs=pl.BlockSpec((1, MOD_ROWS, tn), lambda l, j: (l, 0, j)),
        out_shape=jax.ShapeDtypeStruct((depth, MOD_ROWS, d6), F32),
        compiler_params=_cparams("parallel", "parallel"),
        name="adaln",
    )(cond, mod_w, mod_b.reshape(depth, 1, d6))


def _pick_stream(x_ref, ctx_ref, tiles_per_batch):
    is_ctx = pl.program_id(0) % tiles_per_batch == 0
    return jnp.where(is_ctx, ctx_ref[0], x_ref[0])


def _even_in_kernel(x_ref, ctx_ref, sh_ref, sc_ref, g_ref, w_ref, qg_ref, kg_ref, c_ref, sa_ref,
                    sb_ref, uv_ref, q_ref, k_ref, v_ref, *, tiles_per_batch):
    h = _pick_stream(x_ref, ctx_ref, tiles_per_batch)
    n = _norm_mod(h, g_ref[...], sh_ref[0], sc_ref[0]).astype(BF16)

    def cols(c0):
        return jnp.dot(n, w_ref[:, c0:c0 + PROJ_BLOCK], preferred_element_type=F32)

    for c0 in range(0, 2 * A_WIDTH, PROJ_BLOCK):
        uv_ref[:, c0:c0 + PROJ_BLOCK] = _gelu(cols(c0))
    cos, sa, sb = c_ref[...], sa_ref[...], sb_ref[...]
    lane = lax.broadcasted_iota(I32, cos.shape, 1)
    low = lane < B_HEAD_DIM

    def head_pair(x, gain, scale):
        sq = x * x
        s_lo = jnp.sum(jnp.where(low, sq, 0.0), axis=-1, keepdims=True)
        s_hi = jnp.sum(jnp.where(low, 0.0, sq), axis=-1, keepdims=True)
        ms = jnp.where(low, s_lo, s_hi) * np.float32(1.0 / B_HEAD_DIM)
        y = x * lax.rsqrt(ms + EPS) * gain
        y = (y * cos + pltpu.roll(y, LANES - B_HEAD_DIM // 2, 1) * sa
             + pltpu.roll(y, B_HEAD_DIM // 2, 1) * sb)
        if scale is not None:
            y = y * scale
        return y.astype(BF16)

    q0 = 2 * A_WIDTH
    k0 = q0 + B_WIDTH
    v0 = k0 + B_WIDTH
    per = PROJ_BLOCK // LANES
    for c0 in range(0, B_WIDTH, PROJ_BLOCK):
        pq = cols(q0 + c0)
        pk = cols(k0 + c0)
        for j in range(per):
            sl = slice(c0 + j * LANES, c0 + (j + 1) * LANES)
            q_ref[:, sl] = head_pair(pq[:, j * LANES:(j + 1) * LANES], qg_ref[...],
                                     np.float32(B_HEAD_DIM ** -0.5))
            k_ref[:, sl] = head_pair(pk[:, j * LANES:(j + 1) * LANES], kg_ref[...], None)
        v_ref[:, c0:c0 + PROJ_BLOCK] = cols(v0 + c0).astype(BF16)


def _odd_in_kernel(h_ref, sh_ref, sc_ref, g_ref, w_ref, qg_ref, kg_ref, c_ref, s_ref,
                   q_ref, k_ref, v_ref):
    n = _norm_mod(h_ref[...], g_ref[...], sh_ref[0], sc_ref[0])
    p = jnp.dot(n.astype(BF16), w_ref[...], preferred_element_type=F32)
    cos, sin = c_ref[...], s_ref[...]

    def head(x, gain, scale):
        y = _rms(x, gain)
        y = y * cos + pltpu.roll(y, C_HEAD_DIM // 2, 1) * sin
        if scale is not None:
            y = y * scale
        return y.astype(BF16)

    nq = C_HEADS * C_HEAD_DIM
    nkv = C_KV_HEADS * C_HEAD_DIM
    for j in range(C_HEADS):
        q_ref[:, j * LANES:(j + 1) * LANES] = head(p[:, j * LANES:(j + 1) * LANES], qg_ref[...],
                                                   None)
    for j in range(C_KV_HEADS):
        k_ref[:, j * LANES:(j + 1) * LANES] = head(
            p[:, nq + j * LANES:nq + (j + 1) * LANES], kg_ref[...], None)
    v_ref[...] = p[:, nq + nkv:nq + 2 * nkv].astype(BF16)


def _diff_attn_kernel(lam_ref, q_ref, k_ref, v_ref, o_ref, *, ctx_len, lam_init):
    lv = lam_ref[...]
    lam = (jnp.exp(jnp.sum(lv[0:1] * lv[1:2], axis=-1, keepdims=True))
           - jnp.exp(jnp.sum(lv[2:3] * lv[3:4], axis=-1, keepdims=True)) + np.float32(lam_init))
    low = lax.broadcasted_iota(I32, (q_ref.shape[0], LANES), 1) < B_HEAD_DIM

    def softmax(qm, k, scale):
        s = lax.dot_general(qm, k, NT_DIMS, preferred_element_type=F32)
        p = jnp.exp(s - jnp.max(s, axis=-1, keepdims=True))
        return p * (scale / jnp.sum(p, axis=-1, keepdims=True))

    def attend(n_keys):
        for hh in range(q_ref.shape[1] // LANES):
            cs = slice(hh * LANES, (hh + 1) * LANES)
            q = q_ref[:, cs]
            zero = jnp.zeros_like(q)
            k = k_ref[0:n_keys, cs]
            a = (softmax(jnp.where(low, q, zero), k, 1.0)
                 - softmax(jnp.where(low, zero, q), k, lam))
            o_ref[:, cs] = jnp.dot(a.astype(BF16), v_ref[0:n_keys, cs], preferred_element_type=F32)

    is_ctx = pl.program_id(1) == 0

    @pl.when(is_ctx)
    def _():
        attend(ctx_len)

    @pl.when(jnp.logical_not(is_ctx))
    def _():
        attend(k_ref.shape[0])


def _gqa_kernel(*refs):
    q_refs = refs[:GQA_Q_TILES]
    k_ref, v_ref, o_ref = refs[GQA_Q_TILES:]
    rows = q_refs[0].shape[0]
    grp = q_refs[0].shape[1] // LANES
    c = np.float32(C_HEAD_DIM ** -0.5 * LOG2E)
    k = k_ref[...]
    v = v_ref[...]
    for j, q_ref in enumerate(q_refs):
        for g in range(grp):
            cs = slice(g * LANES, (g + 1) * LANES)
            s = lax.dot_general(q_ref[:, cs], k, NT_DIMS, preferred_element_type=F32)
            p = jnp.exp2((s - jnp.max(s, axis=-1, keepdims=True)) * c)
            l = jnp.sum(p, axis=-1, keepdims=True)
            o = jnp.dot(p.astype(BF16), v, preferred_element_type=F32) / l
            o_ref[j * rows:(j + 1) * rows, cs] = o.astype(o_ref.dtype)


def _even_out_kernel(o_ref, uv_ref, x_ref, ctx_ref, gate_ref, sh_ref, sc_ref, sub_ref, lng_ref,
                     lnb_ref, ws_ref, bs_ref, w_ref, g2_ref, h1_ref, nx_ref, *, lam_init,
                     tiles_per_batch):
    o = o_ref[...]
    uv = uv_ref[...]
    u = uv[:, :A_WIDTH]
    v = uv[:, A_WIDTH:]
    mu = jnp.mean(v, axis=-1, keepdims=True)
    var = jnp.mean(jnp.square(v - mu), axis=-1, keepdims=True)
    vn = ((v - mu) * lax.rsqrt(var + EPS) * lng_ref[...] + lnb_ref[...]).astype(BF16)
    rows = o.shape[0]
    parts = []
    for c in range(rows // GMLP_CHUNK):
        rs = slice(c * GMLP_CHUNK, (c + 1) * GMLP_CHUNK)
        for g in range(A_GROUPS):
            cs = slice(g * A_GROUP_DIM, (g + 1) * A_GROUP_DIM)
            mixed = jnp.dot(ws_ref[g], vn[rs, cs], preferred_element_type=F32) + bs_ref[:, cs]
            parts.append((c, g, u[rs, cs] * mixed))
    a_rows = [jnp.concatenate([p for (c2, _, p) in parts if c2 == c], axis=-1)
              for c in range(rows // GMLP_CHUNK)]
    a = jnp.concatenate(a_rows, axis=0)
    heads = []
    for hh in range(B_HEADS):
        oh = o[:, hh * LANES:(hh + 1) * LANES]
        heads.append(_rms(oh, sub_ref[...]) * np.float32(1.0 - lam_init))
    cat = jnp.concatenate([a] + heads, axis=-1).astype(BF16)
    y = jnp.dot(cat, w_ref[...], preferred_element_type=F32)
    h1 = _pick_stream(x_ref, ctx_ref, tiles_per_batch) + gate_ref[0] * y
    h1_ref[...] = h1
    _to_token_tiles(nx_ref, _norm_mod(h1, g2_ref[...], sh_ref[0], sc_ref[0]))


def _odd_out_kernel(o_ref, h_ref, gate_ref, sh_ref, sc_ref, w_ref, g2_ref, h1_ref, nx_ref):
    y = jnp.dot(o_ref[...], w_ref[...], preferred_element_type=F32)
    h1 = h_ref[...] + gate_ref[0] * y
    h1_ref[...] = h1
    _to_token_tiles(nx_ref, _norm_mod(h1, g2_ref[...], sh_ref[0], sc_ref[0]))


def _rows_to_block(rows, dtype):
    n = rows[0].shape[1]
    rio = lax.broadcasted_iota(I32, (SUBLANES, n), 0)
    out = jnp.zeros((SUBLANES, n), dtype)
    for r, row in enumerate(rows):
        out = jnp.where(rio == r, jnp.broadcast_to(row.astype(dtype), (SUBLANES, n)), out)
    return out


def _transpose_block(xt):
    n = xt.shape[1]
    eye = jnp.where(lax.broadcasted_iota(I32, (n, n), 0) == lax.broadcasted_iota(I32, (n, n), 1),
                    1.0, 0.0).astype(BF16)
    acc = jnp.zeros((n, SUBLANES), F32)
    rem = xt
    for _ in range(3):
        part = rem.astype(BF16)
        acc = acc + lax.dot_general(eye, part, NT_DIMS, preferred_element_type=F32)
        rem = rem - part.astype(F32)
    return acc


def _router_kernel(x_ref, wr_ref, b_ref, eidx_ref, rank_ref, lpos_ref, wcol_ref, cnt_ref, tcnt_ref,
                   run_ref):
    @pl.when(pl.program_id(0) == 0)
    def _():
        run_ref[...] = jnp.zeros_like(run_ref)

    per = N_EXPERTS // N_GROUPS
    d = wr_ref.shape[1]
    x = _from_token_tiles(x_ref, x_ref.shape[0] * LANES // d, d)
    logits = lax.dot_general(wr_ref[...], x.astype(BF16), NT_DIMS,
                             preferred_element_type=F32)
    scores = jax.nn.sigmoid(logits)
    sel = scores + b_ref[...]
    tm = sel.shape[1]
    neg = np.float32(-np.inf)
    jio = lax.broadcasted_iota(I32, (per, tm), 0).astype(F32)
    gio = lax.broadcasted_iota(I32, (N_GROUPS, tm), 0).astype(F32)

    def rmax(x):
        return jnp.max(x, axis=0, keepdims=True)

    def rmin(x):
        return jnp.min(x, axis=0, keepdims=True)

    sel_g = [sel[g * per:(g + 1) * per, :] for g in range(N_GROUPS)]
    sc_g = [scores[g * per:(g + 1) * per, :] for g in range(N_GROUPS)]
    gs = jnp.zeros((N_GROUPS, tm), F32)
    for g in range(N_GROUPS):
        m1 = rmax(sel_g[g])
        i1 = rmin(jnp.where(sel_g[g] == m1, jio, np.float32(per)))
        m2 = rmax(jnp.where(jio == i1, neg, sel_g[g]))
        gs = jnp.where(gio == np.float32(g), jnp.broadcast_to(m1 + m2, gs.shape), gs)
    gsel = jnp.zeros((N_GROUPS, tm), I32)
    for _ in range(TOPK_GROUPS):
        m = rmax(gs)
        idx = rmin(jnp.where(gs == m, gio, np.float32(N_GROUPS)))
        hit = gio == idx
        gsel = jnp.where(hit, 1, gsel)
        gs = jnp.where(hit, neg, gs)
    masked = [jnp.where(jnp.broadcast_to(gsel[g:g + 1, :], (per, tm)) == 1, sel_g[g], neg)
              for g in range(N_GROUPS)]
    eio = [jio + np.float32(g * per) for g in range(N_GROUPS)]
    e_rows, w_rows, hits = [], [], []
    for _ in range(TOP_K):
        m = masked[0]
        for g in range(1, N_GROUPS):
            m = jnp.maximum(m, masked[g])
        m = rmax(m)
        cand = jnp.where(masked[0] == m, eio[0], np.float32(N_EXPERTS))
        for g in range(1, N_GROUPS):
            cand = jnp.minimum(cand, jnp.where(masked[g] == m, eio[g], np.float32(N_EXPERTS)))
        idx = rmin(cand)
        hit = [eio[g] == idx for g in range(N_GROUPS)]
        wsel = jnp.where(hit[0], sc_g[0], 0.0)
        for g in range(1, N_GROUPS):
            wsel = wsel + jnp.where(hit[g], sc_g[g], 0.0)
        masked = [jnp.where(hit[g], neg, masked[g]) for g in range(N_GROUPS)]
        e_rows.append(idx)
        w_rows.append(jnp.sum(wsel, axis=0, keepdims=True))
        hits.append(hit)
    wsum = w_rows[0]
    for r in w_rows[1:]:
        wsum = wsum + r
    w_rows = [r / wsum * np.float32(ROUTE_SCALE) for r in w_rows]
    onehot = []
    for g in range(N_GROUPS):
        any_hit = hits[0][g]
        for kk in range(1, TOP_K):
            any_hit = jnp.logical_or(any_hit, hits[kk][g])
        onehot.append(jnp.where(any_hit, 1.0, 0.0))
    mt = jnp.concatenate(onehot, axis=0)
    before = (lax.broadcasted_iota(I32, (tm, tm), 0) < lax.broadcasted_iota(I32, (tm, tm), 1))
    prefix = jnp.dot(mt.astype(BF16), jnp.where(before, 1.0, 0.0).astype(BF16),
                     preferred_element_type=F32)
    def pick(table):
        rows = []
        for kk in range(TOP_K):
            acc = jnp.where(hits[kk][0], table[0:per, :], 0.0)
            for g in range(1, N_GROUPS):
                acc = acc + jnp.where(hits[kk][g], table[g * per:(g + 1) * per, :], 0.0)
            rows.append(jnp.sum(acc, axis=0, keepdims=True))
        return rows

    n_exp = mt.shape[0]
    lower = (lax.broadcasted_iota(I32, (n_exp, n_exp), 0) > lax.broadcasted_iota(I32, (n_exp, n_exp), 1))
    lower = jnp.where(lower, 1.0, 0.0).astype(BF16)
    tok = lax.broadcasted_iota(I32, (1, tm), 1)
    local = jnp.zeros_like(prefix)
    seen = jnp.zeros((n_exp, 1), F32)
    for hh in range(tm // ROW_TILE):
        in_tile = jnp.logical_and(tok >= hh * ROW_TILE, tok < (hh + 1) * ROW_TILE)
        c_tile = jnp.sum(jnp.where(in_tile, mt, 0.0), axis=1, keepdims=True)
        offs = jnp.dot(lower, jnp.broadcast_to(c_tile, (n_exp, LANES)).astype(BF16),
                       preferred_element_type=F32)[:, 0:1]
        local = jnp.where(in_tile, prefix - seen + offs, local)
        tcnt_ref[hh * n_exp:(hh + 1) * n_exp, :] = jnp.broadcast_to(c_tile, (n_exp, LANES))
        seen = seen + c_tile

    run = run_ref[...] + jnp.sum(mt, axis=1, keepdims=True)
    eidx_ref[...] = _rows_to_block(e_rows, I32)
    rank_ref[...] = _rows_to_block(pick(prefix + run_ref[...]), I32)
    lpos_ref[...] = _rows_to_block(pick(local), I32) * SUBLANES
    wcol_ref[...] = _transpose_block(_rows_to_block(w_rows, F32))
    cnt_ref[...] = jnp.broadcast_to(run, cnt_ref.shape)
    run_ref[...] = run


def _dest_kernel(start_ref, eidx_ref, rank_ref, dest_ref):
    per = N_EXPERTS // N_GROUPS
    eidx = eidx_ref[...]
    tm = eidx.shape[1]
    jio = lax.broadcasted_iota(I32, (per, tm), 0)
    rows = []
    for kk in range(TOP_K):
        e = jnp.broadcast_to(eidx[kk:kk + 1, :], (per, tm))
        acc = jnp.zeros((per, tm), F32)
        for g in range(N_GROUPS):
            st = jnp.broadcast_to(start_ref[g * per:(g + 1) * per, :], (per, tm))
            acc = acc + jnp.where(jio + g * per == e, st, 0.0)
        rows.append(jnp.sum(acc, axis=0, keepdims=True))
    dest_ref[...] = _rows_to_block(rows, I32) + rank_ref[...]


def _row_copy(src, s_row, dst, d_row, sem):
    def tile_start(row):
        start = row * SUBLANES
        return start if isinstance(row, int) else pl.multiple_of(start, SUBLANES)

    s0 = tile_start(s_row)
    d0 = tile_start(d_row)
    return pltpu.make_async_copy(src.at[pl.ds(s0, SUBLANES)], dst.at[pl.ds(d0, SUBLANES)], sem)


def _dispatch_kernel(lpos_ref, cnt_ref, xpos_ref, x_ref, xs_ref, stage, sem):
    i = pl.program_id(0)
    rows = x_ref.shape[0] // SUBLANES
    n_exp = cnt_ref.shape[2]
    unroll = 4
    small = [1 << b for b in range(4, -1, -1)]
    big = [1 << b for b in range(rows.bit_length() - 1, 4, -1)]

    def span(start, n_rows):
        return pl.ds(pl.multiple_of(start, SUBLANES), n_rows * SUBLANES)

    def fill_and_send(slot):
        def place(t4, carry):
            for u in range(unroll):
                t = t4 * unroll + u
                tok = x_ref[span(t * SUBLANES, 1), :]
                for kk in range(TOP_K):
                    stage[slot, span(lpos_ref[kk, t], 1), :] = tok
            return carry

        lax.fori_loop(0, rows // unroll, place, 0)

        def chunks(sizes, n, src, dst):
            for j, size in enumerate(sizes):
                take = (n & size) != 0

                @pl.when(take)
                def _(src=src, dst=dst, size=size, j=j):
                    pltpu.make_async_copy(stage.at[slot, span(src, size)], xs_ref.at[span(dst, size)],
                                          sem.at[slot]).start(priority=j % 2)

                step = jnp.where(take, size * SUBLANES, 0)
                src = src + step
                dst = dst + step

        def send(e, src):
            n = cnt_ref[0, 0, e]
            dst = xpos_ref[0, 0, e] * SUBLANES
            n_big = n & ~(2 * small[0] - 1)

            @pl.when(n_big != 0)
            def _():
                chunks(big, n, src, dst)

            chunks(small, n, src + n_big * SUBLANES, dst + n_big * SUBLANES)
            return src + n * SUBLANES

        lax.fori_loop(0, n_exp, send, 0)

    def wait_stage(slot):
        pltpu.make_async_copy(stage.at[slot], stage.at[slot], sem.at[slot]).wait()

    for slot in range(2):
        @pl.when(i % 2 == slot)
        def _(slot=slot):
            @pl.when(i >= 2)
            def _():
                wait_stage(slot)

            fill_and_send(slot)

    @pl.when(i == pl.num_programs(0) - 1)
    def _():
        for slot in range(2):
            @pl.when(jnp.logical_or(i >= 1, i % 2 == slot))
            def _(slot=slot):
                wait_stage(slot)


def _expert_kernel(tile_s, exp_s, lo_s, hi_s, first_s, last_s, new_s, inv_prev_ref, inv_ref,
                   xs_ref, wg_ref, wu_ref, wd_ref, ys_ref, wg_b, wu_b, wd_b, acc, stage, sem):
    v = pl.program_id(0)
    tile = tile_s[v]
    rows, d = acc.shape

    @pl.when(new_s[v] == 1)
    def _():
        wg_b[...] = wg_ref[0, 0].astype(BF16)
        wu_b[...] = wu_ref[0, 0].astype(BF16)
        wd_b[...] = wd_ref[0, 0].astype(BF16)

    @pl.when(first_s[v] == 1)
    def _():
        acc[...] = jnp.zeros_like(acc)

    lo = lo_s[v]
    hi = hi_s[v]
    nonempty = hi > lo
    flush_prev = jnp.logical_and(first_s[v] == 1, tile >= 1)

    def compute():
        x = _from_token_tiles(xs_ref, rows, d).astype(BF16)
        g = jnp.dot(x, wg_b[...], preferred_element_type=F32)
        u = jnp.dot(x, wu_b[...], preferred_element_type=F32)
        y = jnp.dot((_silu(g) * u).astype(BF16), wd_b[...], preferred_element_type=F32)
        row = lax.broadcasted_iota(I32, (rows, 1), 0)
        mine = jnp.logical_and(row >= lo, row < hi)
        acc[...] = jnp.where(mine, y, acc[...])

    def start_scatter(idx_ref, slot, unrolled):
        def start(r, par):
            _row_copy(stage.at[slot], r, ys_ref, idx_ref[0, 0, r], sem.at[slot]).start(priority=par)

        if unrolled:
            for r in range(rows):
                start(r, r % 2)
        else:
            def body(r2, carry):
                for par in range(2):
                    start(r2 * 2 + par, par)
                return carry
            lax.fori_loop(0, rows // 2, body, 0)

    def wait_scatter(slot):
        pltpu.make_async_copy(stage.at[slot], stage.at[slot], sem.at[slot]).wait()

    for slot in range(2):
        prev_here = jnp.logical_and(flush_prev, (tile + 1) % 2 == slot)

        @pl.when(jnp.logical_and(prev_here, nonempty))
        def _(slot=slot):
            start_scatter(inv_prev_ref, slot, True)
            compute()

        @pl.when(jnp.logical_and(prev_here, jnp.logical_not(nonempty)))
        def _(slot=slot):
            start_scatter(inv_prev_ref, slot, False)

    @pl.when(jnp.logical_and(nonempty, jnp.logical_not(flush_prev)))
    def _():
        compute()

    is_final = v == pl.num_programs(0) - 1
    for slot in range(2):
        @pl.when(jnp.logical_and(last_s[v] == 1, tile % 2 == slot))
        def _(slot=slot):
            @pl.when(tile >= 2)
            def _():
                wait_scatter(slot)

            _to_token_tiles(stage.at[slot], acc[...])

            @pl.when(is_final)
            def _():
                start_scatter(inv_ref, slot, False)

    @pl.when(is_final)
    def _():
        for slot in range(2):
            @pl.when(jnp.logical_or(tile >= 1, tile % 2 == slot))
            def _(slot=slot):
                wait_scatter(slot)


def _combine_kernel(wcol_ref, x_ref, h_ref, gate_ref, sg_ref, su_ref, sd_ref, *refs):
    y_refs, o_ref = refs[:TOP_K], refs[TOP_K]
    rows, d = h_ref.shape
    x = _from_token_tiles(x_ref, rows, d).astype(BF16)
    g = jnp.dot(x, sg_ref[...], preferred_element_type=F32)
    u = jnp.dot(x, su_ref[...], preferred_element_type=F32)
    acc = jnp.dot((_silu(g) * u).astype(BF16), sd_ref[...], preferred_element_type=F32)
    wcol = wcol_ref[...]
    for kk in range(TOP_K):
        acc = acc + _from_token_tiles(y_refs[kk], rows, d) * wcol[:, kk:kk + 1]
    o_ref[...] = h_ref[...] + gate_ref[0] * acc


def _moe(nx, h, modv, gate_row_of, layer, w_router, router_bias, w_gate, w_up, w_down, sg, su, sd):
    t, d = h.shape
    ch = d // LANES
    n_slots = t * TOP_K
    n_rt = t // ROUTER_TILE
    e = N_EXPERTS
    tiles_per_rt = ROUTER_TILE // ROW_TILE
    eidx, rank, lpos, wcol, cnt, tcnt = pl.pallas_call(
        _router_kernel,
        grid=(n_rt,),
        in_specs=[
            pl.BlockSpec((ROUTER_TILE * ch, LANES), lambda i: (i, 0)),
            pl.BlockSpec((e, d), lambda i: (0, 0)),
            pl.BlockSpec((e, 1), lambda i: (0, 0)),
        ],
        out_specs=[
            pl.BlockSpec((SUBLANES, ROUTER_TILE), lambda i: (0, i)),
            pl.BlockSpec((SUBLANES, ROUTER_TILE), lambda i: (0, i)),
            pl.BlockSpec((SUBLANES, ROUTER_TILE), lambda i: (0, i)),
            pl.BlockSpec((ROUTER_TILE, SUBLANES), lambda i: (i, 0)),
            pl.BlockSpec((e, LANES), lambda i: (0, 0)),
            pl.BlockSpec((tiles_per_rt * e, LANES), lambda i: (i, 0)),
        ],
        out_shape=[
            jax.ShapeDtypeStruct((SUBLANES, t), I32),
            jax.ShapeDtypeStruct((SUBLANES, t), I32),
            jax.ShapeDtypeStruct((SUBLANES, t), I32),
            jax.ShapeDtypeStruct((t, SUBLANES), F32),
            jax.ShapeDtypeStruct((e, LANES), F32),
            jax.ShapeDtypeStruct((n_rt * tiles_per_rt * e, LANES), F32),
        ],
        scratch_shapes=[pltpu.VMEM((e, 1), F32)],
        compiler_params=_cparams("arbitrary"),
        name="moe_router",
    )(nx, w_router.T.astype(BF16), router_bias.reshape(e, 1))

    counts = cnt[:, 0].astype(I32)
    ends = jnp.cumsum(counts)
    starts = ends - counts
    dest = pl.pallas_call(
        _dest_kernel,
        grid=(n_rt,),
        in_specs=[
            pl.BlockSpec((e, 1), lambda i: (0, 0)),
            pl.BlockSpec((SUBLANES, ROUTER_TILE), lambda i: (0, i)),
            pl.BlockSpec((SUBLANES, ROUTER_TILE), lambda i: (0, i)),
        ],
        out_specs=pl.BlockSpec((SUBLANES, ROUTER_TILE), lambda i: (0, i)),
        out_shape=jax.ShapeDtypeStruct((SUBLANES, t), I32),
        compiler_params=_cparams("parallel"),
        name="moe_dest",
    )(starts.astype(F32).reshape(e, 1), eidx, rank)
    n_tiles = t // ROW_TILE
    inv = jnp.argsort(dest[:TOP_K].reshape(-1)).astype(I32)
    tile_cnt = tcnt[:, 0].astype(I32).reshape(n_tiles, e)
    run_pos = starts[None, :] + jnp.cumsum(tile_cnt, axis=0) - tile_cnt
    run_spec = pl.BlockSpec((1, 1, e), lambda i: (i, 0, 0), memory_space=pltpu.SMEM)
    xs = pl.pallas_call(
        _dispatch_kernel,
        grid=(n_tiles,),
        in_specs=[
            pl.BlockSpec((SUBLANES, ROW_TILE), lambda i: (0, i), memory_space=pltpu.SMEM),
            run_spec, run_spec,
            pl.BlockSpec((ROW_TILE * ch, LANES), lambda i: (i, 0)),
        ],
        out_specs=pl.BlockSpec(memory_space=pl.ANY),
        out_shape=jax.ShapeDtypeStruct((n_slots * ch, LANES), F32),
        scratch_shapes=[pltpu.VMEM((2, TOP_K * ROW_TILE * ch, LANES), F32),
                        pltpu.SemaphoreType.DMA((2,))],
        compiler_params=pltpu.CompilerParams(dimension_semantics=("arbitrary",),
                                             vmem_limit_bytes=VMEM_LIMIT, has_side_effects=True),
        name="moe_dispatch",
    )(lpos, tile_cnt.reshape(n_tiles, 1, e), run_pos.reshape(n_tiles, 1, e), nx)

    n_et = n_slots // EXPERT_TILE
    pts = jnp.sort(jnp.concatenate([jnp.arange(n_et, dtype=I32) * EXPERT_TILE, starts]))
    lo = pts
    hi = jnp.concatenate([pts[1:], jnp.full((1,), n_slots, I32)])
    tile = jnp.minimum(lo // EXPERT_TILE, n_et - 1)
    expert = jnp.minimum(jnp.sum((ends[None, :] <= lo[:, None]).astype(I32), axis=1), e - 1)
    one = jnp.ones((1,), I32)
    tile_change = (tile[1:] != tile[:-1]).astype(I32)
    first = jnp.concatenate([one, tile_change])
    last = jnp.concatenate([tile_change, one])
    newexp = jnp.concatenate([one, (expert[1:] != expert[:-1]).astype(I32)])
    lo_in = lo - tile * EXPERT_TILE
    hi_in = hi - tile * EXPERT_TILE
    n_visits = n_et + e
    d_exp = w_gate.shape[-1]
    ys = pl.pallas_call(
        _expert_kernel,
        grid_spec=pltpu.PrefetchScalarGridSpec(
            num_scalar_prefetch=7,
            grid=(n_visits,),
            in_specs=[
                pl.BlockSpec((1, 1, EXPERT_TILE),
                             lambda v, ti, ex, *_: (jnp.maximum(ti[v] - 1, 0), 0, 0),
                             memory_space=pltpu.SMEM),
                pl.BlockSpec((1, 1, EXPERT_TILE), lambda v, ti, ex, *_: (ti[v], 0, 0),
                             memory_space=pltpu.SMEM),
                pl.BlockSpec((EXPERT_TILE * ch, LANES), lambda v, ti, ex, *_: (ti[v], 0)),
                pl.BlockSpec((1, 1, d, d_exp), lambda v, ti, ex, *_: (layer, ex[v], 0, 0)),
                pl.BlockSpec((1, 1, d, d_exp), lambda v, ti, ex, *_: (layer, ex[v], 0, 0)),
                pl.BlockSpec((1, 1, d_exp, d), lambda v, ti, ex, *_: (layer, ex[v], 0, 0)),
            ],
            out_specs=pl.BlockSpec(memory_space=pl.ANY),
            scratch_shapes=[pltpu.VMEM((d, d_exp), BF16), pltpu.VMEM((d, d_exp), BF16),
                            pltpu.VMEM((d_exp, d), BF16), pltpu.VMEM((EXPERT_TILE, d), F32),
                            pltpu.VMEM((2, EXPERT_TILE * ch, LANES), F32),
                            pltpu.SemaphoreType.DMA((2,))],
        ),
        out_shape=jax.ShapeDtypeStruct((n_slots * ch, LANES), F32),
        compiler_params=_cparams("arbitrary"),
        name="moe_experts",
    )(tile, expert, lo_in, hi_in, first, last, newexp, inv.reshape(n_et, 1, EXPERT_TILE),
      inv.reshape(n_et, 1, EXPERT_TILE), xs, w_gate, w_up, w_down)

    d_sh = sg.shape[-1]
    slot_specs = [pl.BlockSpec((ROW_TILE * ch, LANES), lambda i, kk=kk: (kk * n_tiles + i, 0))
                  for kk in range(TOP_K)]
    return pl.pallas_call(
        _combine_kernel,
        grid=(n_tiles,),
        in_specs=[
            pl.BlockSpec((ROW_TILE, SUBLANES), lambda i: (i, 0)),
            pl.BlockSpec((ROW_TILE * ch, LANES), lambda i: (i, 0)),
            pl.BlockSpec((ROW_TILE, d), lambda i: (i, 0)),
            pl.BlockSpec((1, 1, d), lambda i: (gate_row_of(i) * 6 + 5, 0, 0)),
            pl.BlockSpec((d, d_sh), lambda i: (0, 0)),
            pl.BlockSpec((d, d_sh), lambda i: (0, 0)),
            pl.BlockSpec((d_sh, d), lambda i: (0, 0)),
        ] + slot_specs,
        out_specs=pl.BlockSpec((ROW_TILE, d), lambda i: (i, 0)),
        out_shape=jax.ShapeDtypeStruct((t, d), F32),
        compiler_params=_cparams("parallel"),
        name="moe_combine",
    )(wcol, nx, h, modv, sg.astype(BF16), su.astype(BF16), sd.astype(BF16), *([ys] * TOP_K))


def _rope_tables(n_lat, n_ctx, head_dim):
    rows = n_lat // GRID_W
    row = jnp.repeat(jnp.arange(rows, dtype=F32), GRID_W)
    col = jnp.tile(jnp.arange(GRID_W, dtype=F32), rows)
    n_freq = head_dim // 4
    inv = ROPE_THETA ** (-jnp.arange(n_freq, dtype=F32) / n_freq)
    ang = jnp.concatenate([row[:, None] * inv, col[:, None] * inv], axis=-1)
    cos = jnp.concatenate([jnp.ones((n_ctx, head_dim // 2), F32), jnp.cos(ang)], axis=0)
    sin = jnp.concatenate([jnp.zeros((n_ctx, head_dim // 2), F32), jnp.sin(ang)], axis=0)
    return cos, sin


def _split_halves_perm(head_dim):
    return np.concatenate([np.arange(0, head_dim, 2), np.arange(1, head_dim, 2)])


def kernel(x, c, ctx, c_ctx, mod_w, mod_b, norm1_g, norm2_g, ev_w_in, ev_w_out, a_ln_g, a_ln_b, a_ws, a_bs, b_q_norm, b_k_norm, b_lam_q1, b_lam_k1, b_lam_q2, b_lam_k2, b_subln, od_w_qkv, od_w_out, c_q_norm, c_k_norm, moe_router, moe_bias, moe_w_gate, moe_w_up, moe_w_down, sh_w_gate, sh_w_up, sh_w_down):
    bsz, n_lat, d = x.shape
    n_ctx = ctx.shape[1]
    depth = mod_w.shape[0]
    assert depth == 2 and n_ctx == ROW_TILE and n_lat % ROW_TILE == 0 and bsz + 1 <= MOD_ROWS
    assert d == SUBLANES * LANES
    n_seq = n_ctx + n_lat
    tpb = n_seq // ROW_TILE
    lpb = n_lat // ROW_TILE
    t_all = bsz * n_seq
    n_tiles = t_all // ROW_TILE
    ctx_row = bsz

    cond = jnp.zeros((MOD_ROWS, d), F32).at[:bsz].set(c).at[ctx_row].set(c_ctx)
    mod = _adaln(cond, mod_w, mod_b)
    modv = [mod[l].reshape(MOD_ROWS * 6, 1, d) for l in range(depth)]

    def row_all(i):
        return jnp.where(i % tpb == 0, ctx_row, i // tpb)

    def mspec(j, row_of):
        return pl.BlockSpec((1, 1, d), lambda i: (row_of(i) * 6 + j, 0, 0))

    def full(shape):
        return pl.BlockSpec(shape, lambda *_: (0,) * len(shape))

    x_spec = pl.BlockSpec((1, ROW_TILE, d), lambda i: (i // tpb, jnp.maximum(i % tpb - 1, 0), 0))
    ctx_spec = pl.BlockSpec((1, ROW_TILE, d), lambda i: (i // tpb, 0, 0))
    ch = d // LANES
    tok_spec = pl.BlockSpec((ROW_TILE * ch, LANES), lambda i: (i, 0))

    lam_init = 0.8 - 0.6 * math.exp(-0.3 * 0)
    p64 = _split_halves_perm(B_HEAD_DIM)
    col_perm = np.concatenate(
        [np.arange(2 * A_WIDTH)]
        + [2 * A_WIDTH + blk * B_HEAD_DIM + p64 for blk in range(2 * B_WIDTH // B_HEAD_DIM)]
        + [np.arange(2 * A_WIDTH + 2 * B_WIDTH, 2 * A_WIDTH + 3 * B_WIDTH)])
    w_in = ev_w_in[0][:, col_perm].astype(BF16)
    even_in = w_in.shape[1]
    cos_b, sin_b = _rope_tables(n_lat, n_ctx, B_HEAD_DIM)
    zeros_b = jnp.zeros_like(sin_b)
    tab_c = jnp.tile(jnp.concatenate([cos_b, cos_b], axis=-1), (1, 2))
    tab_sa = jnp.tile(jnp.concatenate([-sin_b, zeros_b], axis=-1), (1, 2))
    tab_sb = jnp.tile(jnp.concatenate([zeros_b, sin_b], axis=-1), (1, 2))
    qg = jnp.tile(b_q_norm[0][p64], 2).reshape(1, LANES)
    kg = jnp.tile(b_k_norm[0][p64], 2).reshape(1, LANES)
    tab_spec = pl.BlockSpec((ROW_TILE, LANES), lambda i: (i % tpb, 0))
    row_spec = lambda w: pl.BlockSpec((ROW_TILE, w), lambda i: (i, 0))
    uv, q, k, v = pl.pallas_call(
        functools.partial(_even_in_kernel, tiles_per_batch=tpb),
        grid=(n_tiles,),
        in_specs=[x_spec, ctx_spec, mspec(0, row_all), mspec(1, row_all), full((1, d)),
                  full((d, even_in)), full((1, LANES)), full((1, LANES)),
                  tab_spec, tab_spec, tab_spec],
        out_specs=[row_spec(2 * A_WIDTH), row_spec(B_WIDTH), row_spec(B_WIDTH), row_spec(B_WIDTH)],
        out_shape=[jax.ShapeDtypeStruct((t_all, 2 * A_WIDTH), F32),
                   jax.ShapeDtypeStruct((t_all, B_WIDTH), BF16),
                   jax.ShapeDtypeStruct((t_all, B_WIDTH), BF16),
                   jax.ShapeDtypeStruct((t_all, B_WIDTH), BF16)],
        compiler_params=_cparams("parallel"),
        name="even_in",
    )(x, ctx, modv[0], modv[0], norm1_g[0].reshape(1, d), w_in, qg, kg, tab_c, tab_sa, tab_sb)

    lamv = jnp.zeros((SUBLANES, LANES), F32)
    for r, vec in enumerate((b_lam_q1[0], b_lam_k1[0], b_lam_q2[0], b_lam_k2[0])):
        lamv = lamv.at[r, :B_HEAD_DIM].set(vec)
    o = pl.pallas_call(
        functools.partial(_diff_attn_kernel, ctx_len=n_ctx, lam_init=lam_init),
        grid=(bsz, tpb),
        in_specs=[
            pl.BlockSpec((SUBLANES, LANES), lambda b, qi: (0, 0)),
            pl.BlockSpec((ROW_TILE, B_WIDTH), lambda b, qi: (b * tpb + qi, 0)),
            pl.BlockSpec((n_seq, B_WIDTH), lambda b, qi: (b, 0)),
            pl.BlockSpec((n_seq, B_WIDTH), lambda b, qi: (b, 0)),
        ],
        out_specs=pl.BlockSpec((ROW_TILE, B_WIDTH), lambda b, qi: (b * tpb + qi, 0)),
        out_shape=jax.ShapeDtypeStruct((t_all, B_WIDTH), F32),
        compiler_params=_cparams("parallel", "arbitrary"),
        name="diff_attn",
    )(lamv, q, k, v)

    bs_col = jnp.repeat(a_bs[0].T, A_GROUP_DIM, axis=1)
    sub_g = b_subln[0].reshape(1, LANES)
    h1, nx = pl.pallas_call(
        functools.partial(_even_out_kernel, lam_init=lam_init, tiles_per_batch=tpb),
        grid=(n_tiles,),
        in_specs=[row_spec(B_WIDTH), row_spec(2 * A_WIDTH), x_spec, ctx_spec,
                  mspec(2, row_all), mspec(3, row_all), mspec(4, row_all),
                  full((1, LANES)), full((1, A_WIDTH)), full((1, A_WIDTH)),
                  full((A_GROUPS, GMLP_CHUNK, GMLP_CHUNK)), full((GMLP_CHUNK, A_WIDTH)),
                  full((A_WIDTH + B_WIDTH, d)), full((1, d))],
        out_specs=[row_spec(d), tok_spec],
        out_shape=[jax.ShapeDtypeStruct((t_all, d), F32),
                   jax.ShapeDtypeStruct((t_all * ch, LANES), F32)],
        compiler_params=_cparams("parallel"),
        name="even_out",
    )(o, uv, x, ctx, modv[0], modv[0], modv[0], sub_g, a_ln_g[0].reshape(1, A_WIDTH),
      a_ln_b[0].reshape(1, A_WIDTH), a_ws[0].astype(BF16), bs_col,
      ev_w_out[0].astype(BF16), norm2_g[0].reshape(1, d))

    h2 = _moe(nx, h1, modv[0], row_all, 0, moe_router[0], moe_bias[0], moe_w_gate, moe_w_up,
              moe_w_down, sh_w_gate[0], sh_w_up[0], sh_w_down[0])

    p128 = _split_halves_perm(C_HEAD_DIM)
    n_qkv_heads = C_HEADS + 2 * C_KV_HEADS
    col_perm = np.concatenate(
        [blk * C_HEAD_DIM + p128 for blk in range(C_HEADS + C_KV_HEADS)]
        + [np.arange((C_HEADS + C_KV_HEADS) * C_HEAD_DIM, n_qkv_heads * C_HEAD_DIM)])
    w_qkv = od_w_qkv[0][:, col_perm].astype(BF16)
    cos_c, sin_c = _rope_tables(n_lat, n_ctx, C_HEAD_DIM)
    tab_c1 = jnp.concatenate([cos_c, cos_c], axis=-1)
    tab_s1 = jnp.concatenate([-sin_c, sin_c], axis=-1)
    qg1 = c_q_norm[0][p128].reshape(1, LANES)
    kg1 = c_k_norm[0][p128].reshape(1, LANES)
    nq = C_HEADS * C_HEAD_DIM
    nkv = C_KV_HEADS * C_HEAD_DIM
    q1, k1, v1 = pl.pallas_call(
        _odd_in_kernel,
        grid=(n_tiles,),
        in_specs=[row_spec(d), mspec(0, row_all), mspec(1, row_all), full((1, d)),
                  full((d, nq + 2 * nkv)), full((1, LANES)), full((1, LANES)), tab_spec, tab_spec],
        out_specs=[row_spec(nq), row_spec(nkv), row_spec(nkv)],
        out_shape=[jax.ShapeDtypeStruct((t_all, nq), BF16),
                   jax.ShapeDtypeStruct((t_all, nkv), BF16),
                   jax.ShapeDtypeStruct((t_all, nkv), BF16)],
        compiler_params=_cparams("parallel"),
        name="odd_in",
    )(h2, modv[1], modv[1], norm1_g[1].reshape(1, d), w_qkv, qg1, kg1, tab_c1, tab_s1)

    t_lat = bsz * n_lat
    grp = C_HEADS // C_KV_HEADS
    o1 = pl.pallas_call(
        _gqa_kernel,
        grid=(bsz, C_KV_HEADS, lpb // GQA_Q_TILES),
        in_specs=[
            pl.BlockSpec((ROW_TILE, grp * LANES),
                         lambda b, n, qi, j=j: (b * tpb + 1 + qi * GQA_Q_TILES + j, n))
            for j in range(GQA_Q_TILES)
        ] + [
            pl.BlockSpec((n_seq, LANES), lambda b, n, qi: (b, n)),
            pl.BlockSpec((n_seq, LANES), lambda b, n, qi: (b, n)),
        ],
        out_specs=pl.BlockSpec((GQA_Q_TILES * ROW_TILE, grp * LANES),
                               lambda b, n, qi: (b * (lpb // GQA_Q_TILES) + qi, n)),
        out_shape=jax.ShapeDtypeStruct((t_lat, nq), BF16),
        compiler_params=_cparams("parallel", "parallel", "arbitrary"),
        name="gqa_attn",
    )(*([q1] * GQA_Q_TILES), k1, v1)

    def row_lat(i):
        return i // lpb

    lat_tiles = t_lat // ROW_TILE
    hx, nx1 = pl.pallas_call(
        _odd_out_kernel,
        grid=(lat_tiles,),
        in_specs=[row_spec(nq),
                  pl.BlockSpec((ROW_TILE, d), lambda i: ((i // lpb) * tpb + 1 + i % lpb, 0)),
                  mspec(2, row_lat), mspec(3, row_lat), mspec(4, row_lat),
                  full((nq, d)), full((1, d))],
        out_specs=[row_spec(d), tok_spec],
        out_shape=[jax.ShapeDtypeStruct((t_lat, d), F32),
                   jax.ShapeDtypeStruct((t_lat * ch, LANES), F32)],
        compiler_params=_cparams("parallel"),
        name="odd_out",
    )(o1, h2, modv[1], modv[1], modv[1], od_w_out[0].astype(BF16), norm2_g[1].reshape(1, d))

    out = _moe(nx1, hx, modv[1], row_lat, 1, moe_router[1], moe_bias[1], moe_w_gate, moe_w_up,
               moe_w_down, sh_w_gate[1], sh_w_up[1], sh_w_down[1])
    return out.reshape(bsz, n_lat, d)
```
